```python
import math
import jax, jax.numpy as jnp
from jax import lax
import numpy as np

D_MODEL = 2048
BATCH = 4
SEQ = 8192
DEPTH = 1
DEC_BATCH = 8
DEC_SEQ = 16
PAST_LEN = 2048

CHUNK = 64
EPS = 1e-6
SSM_GROUPS = 64
SSM_GROUP_CH = 16
SSM_WIDTH = SSM_GROUPS * SSM_GROUP_CH
SSM_STATE = 64
MLA_HEADS = 16
Q_LORA = 768
KV_LORA = 512
NOPE_DIM = 128
ROPE_DIM = 64
V_DIM = 128
ROPE_THETA = 10000.0
Q_BLOCK = 128
ATTN_SCALE = 1.0 / math.sqrt(NOPE_DIM + ROPE_DIM)
OFF_SSM = 0
OFF_Q = OFF_SSM + SSM_WIDTH
OFF_KV = OFF_Q + Q_LORA
OFF_KR = OFF_KV + KV_LORA
OFF_GATE = OFF_KR + ROPE_DIM
IN_WIDTH = OFF_GATE + 2 * D_MODEL
PEER_HEADS = 8
N_KEYS = 128
N_EXPERTS = N_KEYS * N_KEYS
PEER_KEY_DIM = 256
PEER_HALF = PEER_KEY_DIM // 2
PEER_TOPK = 16
PEER_BLOCK = 128

kernel_name = 'hybrid_s5_mla_peer_streaming_step'


def _rmsnorm(x, g):
    xf = x.astype(jnp.float32)
    y = xf * lax.rsqrt(jnp.mean(xf * xf, axis=-1, keepdims=True) + EPS) * g.astype(jnp.float32)
    return y.astype(x.dtype)


def _modulate(h, shift, scale):
    return h * (1.0 + scale) + shift


def _rope_tables(pos):
    half = ROPE_DIM // 2
    inv = jnp.power(ROPE_THETA, -jnp.arange(half, dtype=jnp.float32) / half)
    ang = pos.astype(jnp.float32)[:, None] * inv
    return jnp.cos(ang), jnp.sin(ang)


def _apply_rope(x, cos, sin):
    half = ROPE_DIM // 2
    xf = x.astype(jnp.float32)
    x1, x2 = xf[..., :half], xf[..., half:]
    return jnp.concatenate([x1 * cos - x2 * sin, x1 * sin + x2 * cos], axis=-1).astype(x.dtype)


def _ssm_combine(e1, e2):
    a1r, a1i, b1r, b1i = e1
    a2r, a2i, b2r, b2i = e2
    return (a2r * a1r - a2i * a1i, a2r * a1i + a2i * a1r,
            a2r * b1r - a2i * b1i + b2r, a2r * b1i + a2i * b1r + b2i)


def _s5_scan(u, s0_re, s0_im, a_re, a_im, log_dt, b_re, b_im, c_re, c_im, d_skip):
    f32 = jnp.float32
    Bn, L, _ = u.shape
    uf = u.astype(f32).reshape(Bn, L, SSM_GROUPS, SSM_GROUP_CH)
    a_re = a_re.astype(f32)
    a_im = a_im.astype(f32)
    dt = jnp.exp(log_dt.astype(f32))[:, None]
    mag = jnp.exp(dt * a_re)
    ab_re = mag * jnp.cos(dt * a_im)
    ab_im = mag * jnp.sin(dt * a_im)
    den = a_re * a_re + a_im * a_im
    nr = ab_re - 1.0
    f_re = (nr * a_re + ab_im * a_im) / den
    f_im = (ab_im * a_re - nr * a_im) / den
    b_re = b_re.astype(f32)
    b_im = b_im.astype(f32)
    bb_re = f_re[..., None] * b_re - f_im[..., None] * b_im
    bb_im = f_re[..., None] * b_im + f_im[..., None] * b_re
    c_re = c_re.astype(f32)
    c_im = c_im.astype(f32)
    d_skip = d_skip.astype(f32)
    blk = min(CHUNK, L)
    nb = L // blk
    ub = uf.reshape(Bn, nb, blk, SSM_GROUPS, SSM_GROUP_CH).transpose(1, 0, 2, 3, 4)

    def step(carry, u_blk):
        sr, si = carry
        xr = jnp.einsum('blgi,gpi->blgp', u_blk, bb_re)
        xi = jnp.einsum('blgi,gpi->blgp', u_blk, bb_im)
        xr = xr.at[:, 0].add(ab_re * sr - ab_im * si)
        xi = xi.at[:, 0].add(ab_re * si + ab_im * sr)
        ar_seq = jnp.broadcast_to(ab_re, xr.shape)
        ai_seq = jnp.broadcast_to(ab_im, xr.shape)
        _, _, hr, hi = lax.associative_scan(_ssm_combine, (ar_seq, ai_seq, xr, xi), axis=1)
        y = (jnp.einsum('blgp,gip->blgi', hr, c_re) - jnp.einsum('blgp,gip->blgi', hi, c_im)
             + d_skip * u_blk)
        return (hr[:, -1], hi[:, -1]), y

    (sr, si), ys = lax.scan(step, (s0_re.astype(f32), s0_im.astype(f32)), ub)
    y = ys.transpose(1, 0, 2, 3, 4).reshape(Bn, L, SSM_WIDTH)
    return y.astype(u.dtype), sr, si


def _mla_attend(q_nope, q_rope, ckv, kr, q_pos, k_pos, w_uk):
    Bn, L, H, _ = q_nope.shape
    blk = min(Q_BLOCK, L)
    nb = L // blk
    qn = q_nope.reshape(Bn, nb, blk, H, NOPE_DIM).transpose(1, 0, 2, 3, 4)
    qr = q_rope.reshape(Bn, nb, blk, H, ROPE_DIM).transpose(1, 0, 2, 3, 4)
    qp = q_pos.reshape(nb, blk)
    k_chunk = k_pos // CHUNK

    def one(args):
        qn_b, qr_b, qp_b = args
        q_abs = jnp.einsum('bqhn,chn->bqhc', qn_b, w_uk)
        s = (jnp.einsum('bqhc,bkc->bhqk', q_abs, ckv)
             + jnp.einsum('bqhr,bkr->bhqk', qr_b, kr)).astype(jnp.float32) * ATTN_SCALE
        mask = k_chunk[None, :] <= (qp_b // CHUNK)[:, None]
        s = jnp.where(mask[None, None], s, -1e30)
        p = jax.nn.softmax(s, axis=-1).astype(ckv.dtype)
        return jnp.einsum('bhqk,bkc->bqhc', p, ckv)

    o = lax.map(one, (qn, qr, qp))
    return o.transpose(1, 0, 2, 3, 4).reshape(Bn, L, H, KV_LORA)


def _peer(x, wq, k1, k2, u_tab, v_tab):
    Bn, L, D = x.shape
    T = Bn * L
    tb = min(PEER_BLOCK, T)
    nb = -(-T // tb)
    xf = jnp.pad(x.reshape(T, D), ((0, nb * tb - T), (0, 0))).reshape(nb, tb, D)

    def one(xb):
        q = (xb @ wq).reshape(tb, PEER_HEADS, PEER_KEY_DIM)
        s1 = jnp.einsum('thd,hnd->thn', q[..., :PEER_HALF], k1).astype(jnp.float32)
        s2 = jnp.einsum('thd,hnd->thn', q[..., PEER_HALF:], k2).astype(jnp.float32)
        v1, i1 = lax.top_k(s1, PEER_TOPK)
        v2, i2 = lax.top_k(s2, PEER_TOPK)
        cand = (v1[..., :, None] + v2[..., None, :]).reshape(tb, PEER_HEADS, PEER_TOPK * PEER_TOPK)
        cidx = (i1[..., :, None] * N_KEYS + i2[..., None, :]).reshape(tb, PEER_HEADS, PEER_TOPK * PEER_TOPK)
        top_s, pos = lax.top_k(cand, PEER_TOPK)
        eidx = jnp.take_along_axis(cidx, pos, axis=-1)
        g = jax.nn.softmax(top_s, axis=-1).astype(xb.dtype)
        u_sel = jnp.take(u_tab, eidx, axis=0)
        v_sel = jnp.take(v_tab, eidx, axis=0)
        hid = jax.nn.gelu(jnp.einsum('thkd,td->thk', u_sel, xb))
        return jnp.einsum('thk,thkd->td', g * hid, v_sel)

    out = lax.map(one, xf).reshape(nb * tb, D)[:T]
    return out.reshape(Bn, L, D)


def _layer(x, c, past_ckv, past_kr, s0_re, s0_im, p):
    Bn, L, _ = x.shape
    past = 0 if past_ckv is None else past_ckv.shape[1]
    q_pos = past + jnp.arange(L, dtype=jnp.int32)
    k_pos = jnp.arange(past + L, dtype=jnp.int32)
    mod = jax.nn.silu(c) @ p['w_ada'] + p['b_ada']
    sh1, sc1, gt1, sh2, sc2, gt2 = jnp.split(mod[:, None, :], 6, axis=-1)
    h = _modulate(_rmsnorm(x, p['g_norm1']), sh1, sc1)
    z = h @ p['w_in']
    u = z[..., OFF_SSM:OFF_Q]
    q_down = z[..., OFF_Q:OFF_KV]
    kv_down = z[..., OFF_KV:OFF_KR]
    kr_raw = z[..., OFF_KR:OFF_GATE]
    g_a = z[..., OFF_GATE:OFF_GATE + D_MODEL]
    g_b = z[..., OFF_GATE + D_MODEL:]
    y_s, s_re, s_im = _s5_scan(u, s0_re, s0_im, p['ssm_a_re'], p['ssm_a_im'], p['ssm_log_dt'],
                               p['ssm_b_re'], p['ssm_b_im'], p['ssm_c_re'], p['ssm_c_im'], p['ssm_d'])
    y_s = jax.nn.gelu(y_s) @ p['w_glu']
    y_a = y_s[..., :D_MODEL] * jax.nn.sigmoid(y_s[..., D_MODEL:])
    cos, sin = _rope_tables(q_pos)
    q = (_rmsnorm(q_down, p['g_q']) @ p['w_qu']).reshape(Bn, L, MLA_HEADS, NOPE_DIM + ROPE_DIM)
    q_nope = q[..., :NOPE_DIM]
    q_rope = _apply_rope(q[..., NOPE_DIM:], cos[:, None, :], sin[:, None, :])
    ckv_new = _rmsnorm(kv_down, p['g_kv'])
    kr_new = _apply_rope(kr_raw, cos, sin)
    if past_ckv is None:
        ckv_all, kr_all = ckv_new, kr_new
    else:
        ckv_all = jnp.concatenate([past_ckv.astype(ckv_new.dtype), ckv_new], axis=1)
        kr_all = jnp.concatenate([past_kr.astype(kr_new.dtype), kr_new], axis=1)
    o_lat = _mla_attend(q_nope, q_rope, ckv_all, kr_all, q_pos, k_pos, p['w_uk'])
    o = jnp.einsum('blhc,chv->blhv', o_lat, p['w_uv']).reshape(Bn, L, MLA_HEADS * V_DIM)
    y_b = o @ p['w_o']
    merged = jax.nn.sigmoid(g_a) * y_a + jax.nn.sigmoid(g_b) * y_b
    x = x + gt1 * (merged @ p['w_out'])
    h2 = _modulate(_rmsnorm(x, p['g_norm2']), sh2, sc2)
    x = x + gt2 * _peer(h2, p['peer_wq'], p['peer_k1'], p['peer_k2'], p['peer_u'], p['peer_v'])
    return x, ckv_new, kr_new, s_re, s_im


def setup_inputs(seed: int = 0) -> dict:
    key = jax.random.key(seed)
    ks = iter(jax.random.split(key, 40))
    f32 = jnp.float32

    def nrm(shape, scale):
        return jax.random.normal(next(ks), shape, f32) * scale

    n_idx = jnp.arange(SSM_STATE, dtype=f32)
    return {
        'x_prompt': nrm((BATCH, SEQ, D_MODEL), 1.0),
        'x_sample': nrm((DEC_BATCH, DEC_SEQ, D_MODEL), 1.0),
        'c_prompt': nrm((BATCH, D_MODEL), 1.0),
        'c_sample': nrm((DEC_BATCH, D_MODEL), 1.0),
        'cache_ckv': nrm((DEPTH, DEC_BATCH, PAST_LEN, KV_LORA), 1.0),
        'cache_krope': nrm((DEPTH, DEC_BATCH, PAST_LEN, ROPE_DIM), 1.0),
        'state_ssm_re': nrm((DEPTH, DEC_BATCH, SSM_GROUPS, SSM_STATE), 0.5),
        'state_ssm_im': nrm((DEPTH, DEC_BATCH, SSM_GROUPS, SSM_STATE), 0.5),
        'w_ada': nrm((DEPTH, D_MODEL, 6 * D_MODEL), 0.5 * D_MODEL ** -0.5),
        'b_ada': nrm((DEPTH, 6 * D_MODEL), 0.02),
        'g_norm1': 1.0 + nrm((DEPTH, D_MODEL), 0.02),
        'g_norm2': 1.0 + nrm((DEPTH, D_MODEL), 0.02),
        'w_in': nrm((DEPTH, D_MODEL, IN_WIDTH), D_MODEL ** -0.5),
        'g_q': 1.0 + nrm((DEPTH, Q_LORA), 0.02),
        'w_qu': nrm((DEPTH, Q_LORA, MLA_HEADS * (NOPE_DIM + ROPE_DIM)), Q_LORA ** -0.5),
        'g_kv': 1.0 + nrm((DEPTH, KV_LORA), 0.02),
        'w_uk': nrm((DEPTH, KV_LORA, MLA_HEADS, NOPE_DIM), KV_LORA ** -0.5),
        'w_uv': nrm((DEPTH, KV_LORA, MLA_HEADS, V_DIM), KV_LORA ** -0.5),
        'w_o': nrm((DEPTH, MLA_HEADS * V_DIM, D_MODEL), (MLA_HEADS * V_DIM) ** -0.5),
        'ssm_a_re': -0.5 + nrm((DEPTH, SSM_GROUPS, SSM_STATE), 0.01),
        'ssm_a_im': math.pi * n_idx + nrm((DEPTH, SSM_GROUPS, SSM_STATE), 0.01),
        'ssm_log_dt': jax.random.uniform(next(ks), (DEPTH, SSM_GROUPS), f32, math.log(1e-3), math.log(1e-1)),
        'ssm_b_re': nrm((DEPTH, SSM_GROUPS, SSM_STATE, SSM_GROUP_CH), (2 * SSM_GROUP_CH) ** -0.5),
        'ssm_b_im': nrm((DEPTH, SSM_GROUPS, SSM_STATE, SSM_GROUP_CH), (2 * SSM_GROUP_CH) ** -0.5),
        'ssm_c_re': nrm((DEPTH, SSM_GROUPS, SSM_GROUP_CH, SSM_STATE), (2 * SSM_STATE) ** -0.5),
        'ssm_c_im': nrm((DEPTH, SSM_GROUPS, SSM_GROUP_CH, SSM_STATE), (2 * SSM_STATE) ** -0.5),
        'ssm_d': nrm((DEPTH, SSM_GROUPS, SSM_GROUP_CH), 0.5),
        'w_glu': nrm((DEPTH, SSM_WIDTH, 2 * D_MODEL), SSM_WIDTH ** -0.5),
        'w_out': nrm((DEPTH, D_MODEL, D_MODEL), D_MODEL ** -0.5),
        'peer_wq': nrm((DEPTH, D_MODEL, PEER_HEADS * PEER_KEY_DIM), D_MODEL ** -0.5),
        'peer_k1': nrm((DEPTH, PEER_HEADS, N_KEYS, PEER_HALF), PEER_HALF ** -0.5),
        'peer_k2': nrm((DEPTH, PEER_HEADS, N_KEYS, PEER_HALF), PEER_HALF ** -0.5),
        'peer_u': nrm((DEPTH, N_EXPERTS, D_MODEL), D_MODEL ** -0.5),
        'peer_v': nrm((DEPTH, N_EXPERTS, D_MODEL), PEER_HEADS ** -0.5),
        'g_final': 1.0 + nrm((D_MODEL,), 0.02),
    }


def reference(x_prompt, x_sample, c_prompt, c_sample, cache_ckv, cache_krope, state_ssm_re, state_ssm_im,
              w_ada, b_ada, g_norm1, g_norm2, w_in, g_q, w_qu, g_kv, w_uk, w_uv, w_o,
              ssm_a_re, ssm_a_im, ssm_log_dt, ssm_b_re, ssm_b_im, ssm_c_re, ssm_c_im, ssm_d,
              w_glu, w_out, peer_wq, peer_k1, peer_k2, peer_u, peer_v, g_final):
    xp, xs = x_prompt, x_sample
    zeros = jnp.zeros((x_prompt.shape[0], SSM_GROUPS, SSM_STATE), jnp.float32)
    ckv_p, kr_p, sre_p, sim_p = [], [], [], []
    ckv_s, kr_s, sre_s, sim_s = [], [], [], []
    for l in range(DEPTH):
        p = {
            'w_ada': w_ada[l], 'b_ada': b_ada[l], 'g_norm1': g_norm1[l], 'g_norm2': g_norm2[l],
            'w_in': w_in[l], 'g_q': g_q[l], 'w_qu': w_qu[l], 'g_kv': g_kv[l],
            'w_uk': w_uk[l], 'w_uv': w_uv[l], 'w_o': w_o[l],
            'ssm_a_re': ssm_a_re[l], 'ssm_a_im': ssm_a_im[l], 'ssm_log_dt': ssm_log_dt[l],
            'ssm_b_re': ssm_b_re[l], 'ssm_b_im': ssm_b_im[l], 'ssm_c_re': ssm_c_re[l],
            'ssm_c_im': ssm_c_im[l], 'ssm_d': ssm_d[l], 'w_glu': w_glu[l], 'w_out': w_out[l],
            'peer_wq': peer_wq[l], 'peer_k1': peer_k1[l], 'peer_k2': peer_k2[l],
            'peer_u': peer_u[l], 'peer_v': peer_v[l],
        }
        xp, a0, a1, a2, a3 = _layer(xp, c_prompt, None, None, zeros, zeros, p)
        ckv_p.append(a0); kr_p.append(a1); sre_p.append(a2); sim_p.append(a3)
        xs, b0, b1, b2, b3 = _layer(xs, c_sample, cache_ckv[l], cache_krope[l],
                                    state_ssm_re[l], state_ssm_im[l], p)
        ckv_s.append(b0); kr_s.append(b1); sre_s.append(b2); sim_s.append(b3)
    y_prompt = _rmsnorm(xp, g_final)
    y_sample = _rmsnorm(xs, g_final)
    return (y_prompt, y_sample,
            jnp.stack(ckv_p), jnp.stack(kr_p), jnp.stack(sre_p), jnp.stack(sim_p),
            jnp.stack(ckv_s), jnp.stack(kr_s), jnp.stack(sre_s), jnp.stack(sim_s))
```

```python
import functools
import math

import jax
import jax.numpy as jnp
from jax import lax
from jax.experimental import pallas as pl
from jax.experimental.pallas import tpu as pltpu

F32 = jnp.float32
BF16 = jnp.bfloat16

EPS = 1e-6
CHUNK = 64
ROPE_THETA = 10000.0
LANES = 128
SUBLANES = 8
MXU_DIM = 256
VMEM_LIMIT_BYTES = 56 * 1024 * 1024
MASKED = -1e30
REMOVED = -3e38


def _params(*semantics):
    return pltpu.CompilerParams(dimension_semantics=semantics, vmem_limit_bytes=VMEM_LIMIT_BYTES)


def _tile(n, pref):
    if n <= pref:
        return n
    t = pref
    while n % t:
        t //= 2
    assert t >= SUBLANES, (n, pref)
    return t


def _sigmoid(x):
    return 1.0 / (1.0 + jnp.exp(-x))


def _gelu(x):
    return 0.5 * x * (1.0 + jnp.tanh(math.sqrt(2.0 / math.pi) * (x + 0.044715 * (x * x * x))))


def _rms(x, g):
    return x * lax.rsqrt(jnp.mean(x * x, axis=-1, keepdims=True) + EPS) * g


def _ada_kernel(c_ref, w_ref, b_ref, o_ref):
    c = c_ref[...]
    a = (c * _sigmoid(c)).astype(BF16)
    o_ref[...] = jnp.dot(a, w_ref[...].astype(BF16), preferred_element_type=F32) + b_ref[...]


def _ada(c, w, b):
    rows, d = c.shape
    n = w.shape[1]
    tn = _tile(n, 1024)
    return pl.pallas_call(
        _ada_kernel,
        grid=(n // tn,),
        in_specs=[
            pl.BlockSpec((rows, d), lambda j: (0, 0)),
            pl.BlockSpec((d, tn), lambda j: (0, j)),
            pl.BlockSpec((1, tn), lambda j: (0, j)),
        ],
        out_specs=pl.BlockSpec((rows, tn), lambda j: (0, j)),
        out_shape=jax.ShapeDtypeStruct((rows, n), F32),
        compiler_params=_params("parallel"),
        name="ada",
    )(c, w, b.reshape(1, n))


def _normmod_kernel(x_ref, g_ref, sc_ref, sh_ref, o_ref):
    h = _rms(x_ref[0], g_ref[...]) * (1.0 + sc_ref[0]) + sh_ref[0]
    o_ref[0] = h.astype(BF16)


def _normmod(x, g, sc, sh):
    b, l, d = x.shape
    tm = _tile(l, 512)
    return pl.pallas_call(
        _normmod_kernel,
        grid=(b, l // tm),
        in_specs=[
            pl.BlockSpec((1, tm, d), lambda i, j: (i, j, 0)),
            pl.BlockSpec((1, d), lambda i, j: (0, 0)),
            pl.BlockSpec((1, 1, d), lambda i, j: (i, 0, 0)),
            pl.BlockSpec((1, 1, d), lambda i, j: (i, 0, 0)),
        ],
        out_specs=pl.BlockSpec((1, tm, d), lambda i, j: (i, j, 0)),
        out_shape=jax.ShapeDtypeStruct((b, l, d), BF16),
        compiler_params=_params("parallel", "parallel"),
        name="normmod",
    )(x, g.reshape(1, d), sc, sh)


def _mm_kernel(a_ref, w_ref, o_ref, *, act):
    z = jnp.dot(a_ref[...], w_ref[...], preferred_element_type=F32)
    if act == "sigmoid":
        z = _sigmoid(z)
    o_ref[...] = z.astype(o_ref.dtype)


def _mm(a, w, out_dtype, act=None, tn_pref=512):
    t, k = a.shape
    n = w.shape[1]
    tm = _tile(t, 512)
    tn = _tile(n, tn_pref)
    return pl.pallas_call(
        functools.partial(_mm_kernel, act=act),
        grid=(t // tm, n // tn),
        in_specs=[
            pl.BlockSpec((tm, k), lambda i, j: (i, 0)),
            pl.BlockSpec((k, tn), lambda i, j: (0, j)),
        ],
        out_specs=pl.BlockSpec((tm, tn), lambda i, j: (i, j)),
        out_shape=jax.ShapeDtypeStruct((t, n), out_dtype),
        compiler_params=_params("parallel", "parallel"),
        name="mm",
    )(a, w)


def _mm_split_kernel(a_ref, w_ref, o_ref, *, parts):
    z = jnp.dot(a_ref[...], w_ref[...], preferred_element_type=F32)
    for p in range(parts):
        o_ref[p] = z[:, p * LANES:(p + 1) * LANES].astype(o_ref.dtype)


def _mm_split(a, w, out_dtype):
    t, k = a.shape
    n = w.shape[1]
    tm = _tile(t, 512)
    tn = _tile(n, MXU_DIM)
    parts = tn // LANES
    return pl.pallas_call(
        functools.partial(_mm_split_kernel, parts=parts),
        grid=(t // tm, n // tn),
        in_specs=[
            pl.BlockSpec((tm, k), lambda i, j: (i, 0)),
            pl.BlockSpec((k, tn), lambda i, j: (0, j)),
        ],
        out_specs=pl.BlockSpec((parts, tm, LANES), lambda i, j: (j, i, 0)),
        out_shape=jax.ShapeDtypeStruct((n // LANES, t, LANES), out_dtype),
        compiler_params=_params("parallel", "parallel"),
        name="mm_split",
    )(a, w)


def _mla_proj_kernel(h_ref, wm_ref, gq_ref, gkv_ref, wq_ref, cos_ref, sin_ref,
                     qn_ref, qr_ref, ckv_ref, ckvb_ref, kr_ref, krp_ref,
                     *, heads, q_lora, kv_lora, rope, scale):
    z = jnp.dot(h_ref[...], wm_ref[...], preferred_element_type=F32)
    cos = cos_ref[...]
    sin = sin_ref[...]
    off = q_lora + kv_lora
    krp = z[:, off:off + LANES] * cos + z[:, off + LANES:off + 2 * LANES] * sin
    kr_ref[...] = krp[:, :rope]
    krp_ref[...] = krp.astype(BF16)
    ckv = _rms(z[:, q_lora:off], gkv_ref[...])
    ckv_ref[...] = ckv
    ckvb_ref[...] = ckv.astype(BF16)
    qd = _rms(z[:, :q_lora], gq_ref[...]).astype(BF16)
    zq = jnp.dot(qd, wq_ref[...], preferred_element_type=F32)
    hn = heads * LANES
    for hh in range(heads):
        lo = hh * LANES
        qn_ref[hh] = (zq[:, lo:lo + LANES] * scale).astype(BF16)
        qr = zq[:, hn + lo:hn + lo + LANES] * cos + zq[:, 2 * hn + lo:2 * hn + lo + LANES] * sin
        qr_ref[hh] = (qr * scale).astype(BF16)


def _mla_proj(h, wm, gq, gkv, wq, cos, sin, seq, *, heads, q_lora, kv_lora, rope, scale):
    t, d = h.shape
    tm = _tile(t, 256)
    if seq % tm == 0:
        nrep = seq // tm
        tab_map = lambda i: (i % nrep, 0)
    else:
        assert tm % seq == 0
        cos = jnp.tile(cos, (tm // seq, 1))
        sin = jnp.tile(sin, (tm // seq, 1))
        tab_map = lambda i: (0, 0)
    nm = wm.shape[1]
    nq = wq.shape[1]
    row = lambda i: (i, 0)
    const = lambda i: (0, 0)
    return pl.pallas_call(
        functools.partial(_mla_proj_kernel, heads=heads, q_lora=q_lora, kv_lora=kv_lora, rope=rope, scale=scale),
        grid=(t // tm,),
        in_specs=[
            pl.BlockSpec((tm, d), row),
            pl.BlockSpec((d, nm), const),
            pl.BlockSpec((1, q_lora), const),
            pl.BlockSpec((1, kv_lora), const),
            pl.BlockSpec((q_lora, nq), const),
            pl.BlockSpec((tm, LANES), tab_map),
            pl.BlockSpec((tm, LANES), tab_map),
        ],
        out_specs=[
            pl.BlockSpec((heads, tm, LANES), lambda i: (0, i, 0)),
            pl.BlockSpec((heads, tm, LANES), lambda i: (0, i, 0)),
            pl.BlockSpec((tm, kv_lora), row),
            pl.BlockSpec((tm, kv_lora), row),
            pl.BlockSpec((tm, rope), row),
            pl.BlockSpec((tm, LANES), row),
        ],
        out_shape=[
            jax.ShapeDtypeStruct((heads, t, LANES), BF16),
            jax.ShapeDtypeStruct((heads, t, LANES), BF16),
            jax.ShapeDtypeStruct((t, kv_lora), F32),
            jax.ShapeDtypeStruct((t, kv_lora), BF16),
            jax.ShapeDtypeStruct((t, rope), F32),
            jax.ShapeDtypeStruct((t, LANES), BF16),
        ],
        compiler_params=_params("parallel"),
        name="mla_proj",
    )(h, wm, gq.reshape(1, q_lora), gkv.reshape(1, kv_lora), wq, cos, sin)


def _kvup_kernel(c_ref, w_ref, k_ref, v_ref, *, heads):
    z = jnp.dot(c_ref[...], w_ref[...], preferred_element_type=F32)
    for hh in range(heads):
        k_ref[hh] = z[:, hh * LANES:(hh + 1) * LANES].astype(BF16)
        v_ref[hh] = z[:, (heads + hh) * LANES:(heads + hh + 1) * LANES].astype(BF16)


def _kvup(ckv, w, heads):
    t, c = ckv.shape
    tm = _tile(t, 512)
    if t % tm:
        tm = t
    return pl.pallas_call(
        functools.partial(_kvup_kernel, heads=heads),
        grid=(t // tm,),
        in_specs=[
            pl.BlockSpec((tm, c), lambda i: (i, 0)),
            pl.BlockSpec(w.shape, lambda i: (0, 0)),
        ],
        out_specs=[
            pl.BlockSpec((heads, tm, LANES), lambda i: (0, i, 0)),
            pl.BlockSpec((heads, tm, LANES), lambda i: (0, i, 0)),
        ],
        out_shape=[jax.ShapeDtypeStruct((heads, t, LANES), BF16)] * 2,
        compiler_params=_params("parallel"),
        name="kvup",
    )(ckv, w)


def _last_key(iq, tq, q_pos0, n_keys):
    last_q = q_pos0 + iq * tq + tq - 1
    return jnp.minimum(n_keys - 1, (last_q // CHUNK) * CHUNK + CHUNK - 1)


def _flash_kernel(qn_ref, qr_ref, k_ref, kr_ref, v_ref, o_ref, qc_scr, m_scr, l_scr, acc_scr,
                  *, hb, tq, tk, q_pos0, n_keys, nk):
    iq = pl.program_id(2)
    ik = pl.program_id(3)
    q_first = q_pos0 + iq * tq
    last_key = _last_key(iq, tq, q_pos0, n_keys)
    full_key = jnp.minimum(n_keys - 1, (q_first // CHUNK) * CHUNK + CHUNK - 1)
    needed = ik * tk <= last_key
    full = (ik + 1) * tk - 1 <= full_key

    @pl.when(ik == 0)
    def _init():
        m_scr[...] = jnp.full(m_scr.shape, MASKED, F32)
        l_scr[...] = jnp.zeros(l_scr.shape, F32)
        acc_scr[...] = jnp.zeros(acc_scr.shape, F32)
        for hh in range(hb):
            qc_scr[hh] = jnp.concatenate([qn_ref[hh], qr_ref[hh]], axis=1)

    def step(masked):
        kr = kr_ref[...]
        if masked:
            qp = q_first + lax.broadcasted_iota(jnp.int32, (tq, tk), 0)
            kp = ik * tk + lax.broadcasted_iota(jnp.int32, (tq, tk), 1)
            allowed = jnp.logical_and(kp // CHUNK <= qp // CHUNK, kp < n_keys)

        def head(hh, carry):
            kc = jnp.concatenate([k_ref[hh], kr], axis=1)
            s = lax.dot_general(qc_scr[hh], kc, (((1,), (1,)), ((), ())), preferred_element_type=F32)
            if masked:
                s = jnp.where(allowed, s, MASKED)
            m_prev = m_scr[hh]
            m_new = jnp.maximum(m_prev, jnp.max(s, axis=1, keepdims=True))
            alpha = jnp.exp(m_prev - m_new)
            p = jnp.exp(s - pltpu.repeat(m_new, tk // LANES, axis=1))
            l_scr[hh] = alpha * l_scr[hh] + jnp.sum(p, axis=1, keepdims=True)
            acc_scr[hh] = alpha * acc_scr[hh] + jnp.dot(p.astype(BF16), v_ref[hh], preferred_element_type=F32)
            m_scr[hh] = m_new
            return carry

        lax.fori_loop(0, hb, head, 0)

    @pl.when(jnp.logical_and(needed, full))
    def _full():
        step(False)

    @pl.when(jnp.logical_and(needed, jnp.logical_not(full)))
    def _diag():
        step(True)

    @pl.when(ik == nk - 1)
    def _done():
        for hh in range(hb):
            o_ref[hh] = (acc_scr[hh] / l_scr[hh]).astype(BF16)


def _flash(qn, qr, k, krp, v, *, batch, lq, lk, q_pos0, n_keys, tq, tk, hb):
    heads = qn.shape[0]
    nq = lq // tq
    nk = lk // tk
    assert lq % tq == 0 and lk % tk == 0 and heads % hb == 0 and tk % LANES == 0

    def kblk(iq, ik):
        return jnp.minimum(ik, _last_key(iq, tq, q_pos0, n_keys) // tk)

    qmap = lambda b, h, iq, ik: (h, b * nq + iq, 0)
    kmap = lambda b, h, iq, ik: (h, b * nk + kblk(iq, ik), 0)
    return pl.pallas_call(
        functools.partial(_flash_kernel, hb=hb, tq=tq, tk=tk, q_pos0=q_pos0, n_keys=n_keys, nk=nk),
        grid=(batch, heads // hb, nq, nk),
        in_specs=[
            pl.BlockSpec((hb, tq, LANES), qmap),
            pl.BlockSpec((hb, tq, LANES), qmap),
            pl.BlockSpec((hb, tk, LANES), kmap),
            pl.BlockSpec((tk, LANES), lambda b, h, iq, ik: (b * nk + kblk(iq, ik), 0)),
            pl.BlockSpec((hb, tk, LANES), kmap),
        ],
        out_specs=pl.BlockSpec((hb, tq, LANES), qmap),
        out_shape=jax.ShapeDtypeStruct((heads, batch * lq, LANES), BF16),
        scratch_shapes=[
            pltpu.VMEM((hb, tq, 2 * LANES), BF16),
            pltpu.VMEM((hb, tq, LANES), F32),
            pltpu.VMEM((hb, tq, LANES), F32),
            pltpu.VMEM((hb, tq, LANES), F32),
        ],
        compiler_params=_params("parallel", "parallel", "parallel", "arbitrary"),
        name="flash",
    )(qn, qr, k, krp, v)


def _s5_kernel(u_ref, wb_ref, scn_ref, wc_ref, d_ref, s0_ref, y_ref, st_ref, xs_ref, car_ref,
               *, ts, nkt, kw, sw, nt):
    it = pl.program_id(1)

    @pl.when(it == 0)
    def _load_state():
        car_ref[...] = s0_ref[0]

    lane_groups = sw // LANES
    for kt in range(nkt):
        ukt = u_ref[0, :, kt * kw:(kt + 1) * kw]
        xs_ref[...] = jnp.dot(ukt, wb_ref[kt], preferred_element_type=F32)
        for lg in range(lane_groups):
            lo = lg * LANES
            re_sl = slice(lo, lo + LANES)
            im_sl = slice(sw + lo, sw + lo + LANES)
            st_sl = slice(kt * sw + lo, kt * sw + lo + LANES)
            a1r, a1i, a2r, a2i, a4r, a4i, pr, pi = [scn_ref[kt, c, :, re_sl] for c in range(8)]

            def blk(k, carry, re_sl=re_sl, im_sl=im_sl, a1r=a1r, a1i=a1i, a2r=a2r, a2i=a2i,
                    a4r=a4r, a4i=a4i, pr=pr, pi=pi):
                cr, ci = carry
                rows = pl.ds(pl.multiple_of(k * SUBLANES, SUBLANES), SUBLANES)
                hr = xs_ref[rows, re_sl]
                hi = xs_ref[rows, im_sl]
                for ar, ai, dist in ((a1r, a1i, 1), (a2r, a2i, 2), (a4r, a4i, 4)):
                    sr = pltpu.roll(hr, dist, 0)
                    si = pltpu.roll(hi, dist, 0)
                    hr, hi = hr + ar * sr - ai * si, hi + ar * si + ai * sr
                crb = jnp.broadcast_to(cr, (SUBLANES, LANES))
                cib = jnp.broadcast_to(ci, (SUBLANES, LANES))
                hr, hi = hr + pr * crb - pi * cib, hi + pr * cib + pi * crb
                xs_ref[rows, re_sl] = hr
                xs_ref[rows, im_sl] = hi
                return hr[SUBLANES - 1:SUBLANES], hi[SUBLANES - 1:SUBLANES]

            cr, ci = lax.fori_loop(0, ts // SUBLANES, blk, (car_ref[0:1, st_sl], car_ref[1:2, st_sl]))
            car_ref[0:1, st_sl] = cr
            car_ref[1:2, st_sl] = ci
        y = jnp.dot(xs_ref[...].astype(BF16), wc_ref[kt], preferred_element_type=F32)
        y = y + d_ref[:, kt * kw:(kt + 1) * kw] * ukt.astype(F32)
        y_ref[0, :, kt * kw:(kt + 1) * kw] = _gelu(y).astype(BF16)

    @pl.when(it == nt - 1)
    def _store_state():
        st_ref[0] = car_ref[...]


def _s5(u, wb, scn, wc, dsk, s0):
    b, l, w = u.shape
    nkt, kw, sw2 = wb.shape
    sw = sw2 // 2
    ns = s0.shape[2]
    ts = _tile(l, 512)
    nt = l // ts
    const3 = lambda i, j: (0, 0, 0)
    return pl.pallas_call(
        functools.partial(_s5_kernel, ts=ts, nkt=nkt, kw=kw, sw=sw, nt=nt),
        grid=(b, nt),
        in_specs=[
            pl.BlockSpec((1, ts, w), lambda i, j: (i, j, 0)),
            pl.BlockSpec(wb.shape, const3),
            pl.BlockSpec(scn.shape, lambda i, j: (0, 0, 0, 0)),
            pl.BlockSpec(wc.shape, const3),
            pl.BlockSpec((1, w), lambda i, j: (0, 0)),
            pl.BlockSpec((1, 2, ns), lambda i, j: (i, 0, 0)),
        ],
        out_specs=[
            pl.BlockSpec((1, ts, w), lambda i, j: (i, j, 0)),
            pl.BlockSpec((1, 2, ns), lambda i, j: (i, 0, 0)),
        ],
        out_shape=[
            jax.ShapeDtypeStruct((b, l, w), BF16),
            jax.ShapeDtypeStruct((b, 2, ns), F32),
        ],
        scratch_shapes=[
            pltpu.VMEM((ts, sw2), F32),
            pltpu.VMEM((2, ns), F32),
        ],
        compiler_params=_params("parallel", "arbitrary"),
        name="s5",
    )(u, wb, scn, wc, dsk, s0)


def _s5_tables(a_re, a_im, log_dt, b_re, b_im, c_re, c_im, d_skip):
    g, p = a_re.shape
    ch = b_re.shape[2]
    gpt = MXU_DIM // ch
    nkt = g // gpt
    dt = jnp.exp(log_dt.astype(F32))[:, None]
    mag = jnp.exp(dt * a_re)
    ab_re = mag * jnp.cos(dt * a_im)
    ab_im = mag * jnp.sin(dt * a_im)
    den = a_re * a_re + a_im * a_im
    nr = ab_re - 1.0
    f_re = (nr * a_re + ab_im * a_im) / den
    f_im = (ab_im * a_re - nr * a_im) / den
    bb_re = f_re[..., None] * b_re - f_im[..., None] * b_im
    bb_im = f_re[..., None] * b_im + f_im[..., None] * b_re
    eye = jnp.eye(gpt, dtype=F32)

    def in_blockdiag(bb):
        return jnp.einsum("kgpi,gh->kgihp", bb.reshape(nkt, gpt, p, ch), eye).reshape(nkt, gpt * ch, gpt * p)

    def out_blockdiag(cc):
        return jnp.einsum("kgjp,gh->kgphj", cc.reshape(nkt, gpt, ch, p), eye).reshape(nkt, gpt * p, gpt * ch)

    wb = jnp.concatenate([in_blockdiag(bb_re), in_blockdiag(bb_im)], axis=2).astype(BF16)
    wc = jnp.concatenate([out_blockdiag(c_re), out_blockdiag(-c_im)], axis=1).astype(BF16)

    def cmul(x, y):
        return x[0] * y[0] - x[1] * y[1], x[0] * y[1] + x[1] * y[0]

    lam1 = (ab_re.reshape(nkt, gpt * p), ab_im.reshape(nkt, gpt * p))
    lam2 = cmul(lam1, lam1)
    lam4 = cmul(lam2, lam2)
    rows = jnp.arange(SUBLANES)[None, :, None]

    def shifted(lam, dist):
        return [jnp.where(rows >= dist, c[:, None, :], 0.0) for c in lam]

    pw = [lam1]
    for _ in range(SUBLANES - 1):
        pw.append(cmul(pw[-1], lam1))
    p_re = jnp.stack([c[0] for c in pw], axis=1)
    p_im = jnp.stack([c[1] for c in pw], axis=1)
    scn = jnp.stack(shifted(lam1, 1) + shifted(lam2, 2) + shifted(lam4, 4) + [p_re, p_im], axis=1)
    return wb, scn.astype(F32), wc, d_skip.reshape(1, g * ch).astype(F32)


def _merge_kernel(ys_ref, o_ref, wga_ref, wgb_ref, wo_ref, ga_ref, gb_ref, out_ref, *, heads):
    ys = ys_ref[...]
    ya = jnp.dot(ys, wga_ref[...], preferred_element_type=F32)
    ya = ya * _sigmoid(jnp.dot(ys, wgb_ref[...], preferred_element_type=F32))
    oc = jnp.concatenate([o_ref[hh] for hh in range(heads)], axis=1)
    yb = jnp.dot(oc, wo_ref[...], preferred_element_type=F32)
    out_ref[...] = (ga_ref[...].astype(F32) * ya + gb_ref[...].astype(F32) * yb).astype(BF16)


def _merge(ys, o, w_glu, w_o, gates):
    t, sw = ys.shape
    heads = o.shape[0]
    d = w_o.shape[1]
    tm = _tile(t, 512)
    tn = _tile(d, 512)
    nj = d // tn
    return pl.pallas_call(
        functools.partial(_merge_kernel, heads=heads),
        grid=(t // tm, nj),
        in_specs=[
            pl.BlockSpec((tm, sw), lambda i, j: (i, 0)),
            pl.BlockSpec((heads, tm, LANES), lambda i, j: (0, i, 0)),
            pl.BlockSpec((sw, tn), lambda i, j: (0, j)),
            pl.BlockSpec((sw, tn), lambda i, j: (0, nj + j)),
            pl.BlockSpec((heads * LANES, tn), lambda i, j: (0, j)),
            pl.BlockSpec((tm, tn), lambda i, j: (i, j)),
            pl.BlockSpec((tm, tn), lambda i, j: (i, nj + j)),
        ],
        out_specs=pl.BlockSpec((tm, tn), lambda i, j: (i, j)),
        out_shape=jax.ShapeDtypeStruct((t, d), BF16),
        compiler_params=_params("parallel", "parallel"),
        name="merge",
    )(ys, o, w_glu, w_glu, w_o, gates, gates)


def _resid_kernel(m_ref, w_ref, x_ref, gt_ref, g_ref, sc_ref, sh_ref, x1_ref, h2_ref):
    x1 = x_ref[0] + gt_ref[0] * jnp.dot(m_ref[0], w_ref[...], preferred_element_type=F32)
    x1_ref[0] = x1
    h2_ref[0] = (_rms(x1, g_ref[...]) * (1.0 + sc_ref[0]) + sh_ref[0]).astype(BF16)


def _resid(merged, w_out, x, gt, g2, sc, sh):
    b, l, d = x.shape
    tm = _tile(l, 256)
    tok = lambda i, j: (i, j, 0)
    per_b = lambda i, j: (i, 0, 0)
    return pl.pallas_call(
        _resid_kernel,
        grid=(b, l // tm),
        in_specs=[
            pl.BlockSpec((1, tm, d), tok),
            pl.BlockSpec((d, d), lambda i, j: (0, 0)),
            pl.BlockSpec((1, tm, d), tok),
            pl.BlockSpec((1, 1, d), per_b),
            pl.BlockSpec((1, d), lambda i, j: (0, 0)),
            pl.BlockSpec((1, 1, d), per_b),
            pl.BlockSpec((1, 1, d), per_b),
        ],
        out_specs=[pl.BlockSpec((1, tm, d), tok), pl.BlockSpec((1, tm, d), tok)],
        out_shape=[jax.ShapeDtypeStruct((b, l, d), F32), jax.ShapeDtypeStruct((b, l, d), BF16)],
        compiler_params=_params("parallel", "parallel"),
        name="resid",
    )(merged.reshape(b, l, d), w_out, x, gt, g2.reshape(1, d), sc, sh)


def _peer_topk_kernel(q_ref, k1_ref, k2_ref, flat_ref, invalid_ref, r2_ref, c1_ref, a1_ref, a2_ref,
                      *, nh, nkeys, topk):
    tb = q_ref.shape[1]
    iota_k = lax.broadcasted_iota(jnp.int32, (nkeys, tb), 0).astype(F32)
    iota_t = lax.broadcasted_iota(jnp.int32, (topk, tb), 0).astype(F32)
    flat = flat_ref[...]
    invalid = invalid_ref[...]
    nt_dims = (((1,), (1,)), ((), ()))

    def extract(s):
        work = s
        rank = jnp.full((nkeys, tb), float(topk), F32)
        vals = []
        for a in range(topk):
            m = jnp.max(work, axis=0, keepdims=True)
            idx = jnp.min(jnp.where(work == m, iota_k, float(nkeys)), axis=0, keepdims=True)
            sel = iota_k == idx
            rank = jnp.where(sel, float(a), rank)
            work = jnp.where(sel, REMOVED, work)
            vals.append(m)
        return vals, rank

    def stack(vals):
        out = jnp.zeros((topk, tb), F32)
        for a in range(topk):
            out = jnp.where(iota_t == float(a), vals[a], out)
        return out

    def head(hh, carry):
        s1 = lax.dot_general(k1_ref[hh], q_ref[2 * hh], nt_dims, preferred_element_type=F32)
        s2 = lax.dot_general(k2_ref[hh], q_ref[2 * hh + 1], nt_dims, preferred_element_type=F32)
        v1, rank1 = extract(s1)
        v2, rank2 = extract(s2)
        vs1 = stack(v1)
        vs2 = stack(v2)
        blocks = [v1[0] + vs2]
        for a in range(1, SUBLANES):
            blocks.append(v1[a] + vs2[:SUBLANES])
        blocks.append(vs1[SUBLANES:] + v2[0])
        cand = jnp.concatenate(blocks, axis=0) + invalid
        counts = jnp.zeros((topk, tb), F32)
        top = None
        zsum = None
        for kk in range(topk):
            m = jnp.max(cand, axis=0, keepdims=True)
            f = jnp.min(jnp.where(cand == m, flat, 1e9), axis=0, keepdims=True)
            cand = jnp.where(flat == f, REMOVED, cand)
            counts = counts + jnp.where(iota_t == jnp.floor(f * (1.0 / topk)), 1.0, 0.0)
            if kk == 0:
                top = m
                zsum = jnp.ones_like(m)
            else:
                zsum = zsum + jnp.exp(m - top)
        c1 = jnp.zeros((nkeys, tb), F32)
        for a in range(topk):
            c1 = jnp.where(rank1 == float(a), counts[a:a + 1], c1)
        r2_ref[hh] = rank2
        c1_ref[hh] = c1
        a1_ref[hh] = jnp.exp(s1 - v1[0]) * (1.0 / zsum)
        a2_ref[hh] = jnp.exp(s2 - v2[0])
        return carry

    lax.fori_loop(0, nh, head, 0)


def _peer_topk(q, k1, k2, topk):
    nh, nkeys, half = k1.shape
    t = q.shape[1]
    tb = LANES
    assert topk == 2 * SUBLANES and half == LANES and t % tb == 0
    rows = jnp.arange(topk + (SUBLANES - 1) * SUBLANES + SUBLANES)
    a_idx = jnp.where(rows < topk, 0, jnp.where(rows < topk + (SUBLANES - 1) * SUBLANES,
                                                1 + (rows - topk) // SUBLANES, SUBLANES + (rows - topk - (SUBLANES - 1) * SUBLANES)))
    b_idx = jnp.where(rows < topk, rows, jnp.where(rows < topk + (SUBLANES - 1) * SUBLANES, (rows - topk) % SUBLANES, 0))
    flat = jnp.broadcast_to((a_idx * topk + b_idx).astype(F32)[:, None], (rows.shape[0], tb))
    invalid = jnp.broadcast_to(jnp.where((a_idx + 1) * (b_idx + 1) <= topk, 0.0, REMOVED).astype(F32)[:, None],
                               (rows.shape[0], tb))
    out = jax.ShapeDtypeStruct((nh, nkeys, t), F32)
    ospec = pl.BlockSpec((nh, nkeys, tb), lambda i: (0, 0, i))
    return pl.pallas_call(
        functools.partial(_peer_topk_kernel, nh=nh, nkeys=nkeys, topk=topk),
        grid=(t // tb,),
        in_specs=[
            pl.BlockSpec((2 * nh, tb, LANES), lambda i: (0, i, 0)),
            pl.BlockSpec(k1.shape, lambda i: (0, 0, 0)),
            pl.BlockSpec(k2.shape, lambda i: (0, 0, 0)),
            pl.BlockSpec(flat.shape, lambda i: (0, 0)),
            pl.BlockSpec(invalid.shape, lambda i: (0, 0)),
        ],
        out_specs=[ospec] * 4,
        out_shape=[out] * 4,
        compiler_params=_params("parallel"),
        name="peer_topk",
    )(q, k1, k2, flat, invalid)


def _peer_mix_kernel(h_ref, u_ref, vt_ref, r2_ref, c1_ref, a1_ref, a2_ref, o_ref, ht_scr, w_scr, acc_scr,
                     *, nh, nkeys, ni, ne):
    e = pl.program_id(1)
    tb = h_ref.shape[0]

    @pl.when(e == 0)
    def _init():
        acc_scr[...] = jnp.zeros(acc_scr.shape, F32)

    ht_scr[...] = lax.dot_general(u_ref[...], h_ref[...], (((1,), (1,)), ((), ())), preferred_element_type=F32)
    first_keys = pl.ds(pl.multiple_of(e * ni, SUBLANES), ni)
    for il in range(ni):
        rows = slice(il * nkeys, (il + 1) * nkeys)
        for lg in range(tb // LANES):
            sl = slice(lg * LANES, (lg + 1) * LANES)
            gate = jnp.zeros((nkeys, LANES), F32)
            for hh in range(nh):
                partners = c1_ref[hh, first_keys, sl][il:il + 1]
                first = a1_ref[hh, first_keys, sl][il:il + 1]
                gate = gate + jnp.where(r2_ref[hh, :, sl] < partners, a2_ref[hh, :, sl], 0.0) * first
            w_scr[rows, sl] = (gate * _gelu(ht_scr[rows, sl])).astype(BF16)
    acc_scr[...] += jnp.dot(vt_ref[...], w_scr[...], preferred_element_type=F32)

    @pl.when(e == ne - 1)
    def _done():
        o_ref[...] = acc_scr[...].T


def _peer_mix(h2, u_tab, vt_tab, r2, c1, a1, a2):
    t, d = h2.shape
    nh, nkeys, _ = r2.shape
    n_exp = u_tab.shape[0]
    tb = _tile(t, 512)
    ni = SUBLANES
    eb = ni * nkeys
    ne = n_exp // eb
    aux = pl.BlockSpec((nh, nkeys, tb), lambda i, e: (0, 0, i), pipeline_mode=pl.Buffered(1))
    return pl.pallas_call(
        functools.partial(_peer_mix_kernel, nh=nh, nkeys=nkeys, ni=ni, ne=ne),
        grid=(t // tb, ne),
        in_specs=[
            pl.BlockSpec((tb, d), lambda i, e: (i, 0)),
            pl.BlockSpec((eb, d), lambda i, e: (e, 0)),
            pl.BlockSpec((d, eb), lambda i, e: (0, e)),
            aux, aux, aux, aux,
        ],
        out_specs=pl.BlockSpec((tb, d), lambda i, e: (i, 0)),
        out_shape=jax.ShapeDtypeStruct((t, d), F32),
        scratch_shapes=[
            pltpu.VMEM((eb, tb), F32),
            pltpu.VMEM((eb, tb), BF16),
            pltpu.VMEM((d, tb), F32),
        ],
        compiler_params=_params("parallel", "arbitrary"),
        name="peer_mix",
    )(h2, u_tab, vt_tab, r2, c1, a1, a2)


def _final_kernel(x_ref, p_ref, gt_ref, g_ref, y_ref, *, norm):
    y = x_ref[0] + gt_ref[0] * p_ref[0]
    y_ref[0] = _rms(y, g_ref[...]) if norm else y


def _final(x1, peer, gt, g, norm):
    b, l, d = x1.shape
    tm = _tile(l, 512)
    tok = lambda i, j: (i, j, 0)
    return pl.pallas_call(
        functools.partial(_final_kernel, norm=norm),
        grid=(b, l // tm),
        in_specs=[
            pl.BlockSpec((1, tm, d), tok),
            pl.BlockSpec((1, tm, d), tok),
            pl.BlockSpec((1, 1, d), lambda i, j: (i, 0, 0)),
            pl.BlockSpec((1, d), lambda i, j: (0, 0)),
        ],
        out_specs=pl.BlockSpec((1, tm, d), tok),
        out_shape=jax.ShapeDtypeStruct((b, l, d), F32),
        compiler_params=_params("parallel", "parallel"),
        name="final",
    )(x1, peer.reshape(b, l, d), gt, g.reshape(1, d))


def _rope_tables(pos, rope):
    half = rope // 2
    inv = jnp.power(ROPE_THETA, -jnp.arange(half, dtype=F32) / half)
    ang = pos.astype(F32)[:, None] * inv
    reps = LANES // half
    return jnp.tile(jnp.cos(ang), (1, reps)), jnp.tile(jnp.sin(ang), (1, reps))


def _prep_weights(p, dims):
    d, sw, q_lora, kv_lora, rope, heads, nope = (dims[k] for k in ("d", "sw", "q_lora", "kv_lora", "rope", "heads", "nope"))
    half = rope // 2
    w_in = p["w_in"]
    off_q = sw
    off_kv = off_q + q_lora
    off_kr = off_kv + kv_lora
    off_g = off_kr + rope
    kr_w = w_in[:, off_kr:off_g]
    kr_rot = jnp.concatenate([-kr_w[:, half:], kr_w[:, :half]], axis=1)
    pad = jnp.zeros((d, LANES - rope), F32)
    w_mla = jnp.concatenate([w_in[:, off_q:off_kr], kr_w, pad, kr_rot, pad], axis=1).astype(BF16)
    wq = p["w_qu"].reshape(q_lora, heads, nope + rope)
    wq_n = wq[:, :, :nope].reshape(q_lora, heads * nope)
    wq_r = wq[:, :, nope:]
    wq_rot = jnp.concatenate([-wq_r[:, :, half:], wq_r[:, :, :half]], axis=2)
    hpad = jnp.zeros((q_lora, heads, LANES - rope), F32)
    wq_a = jnp.concatenate([wq_r, hpad], axis=2).reshape(q_lora, heads * LANES)
    wq_b = jnp.concatenate([wq_rot, hpad], axis=2).reshape(q_lora, heads * LANES)
    return {
        "w_gates": w_in[:, off_g:].astype(BF16),
        "w_u": w_in[:, :sw].astype(BF16),
        "w_mla": w_mla,
        "w_q": jnp.concatenate([wq_n, wq_a, wq_b], axis=1).astype(BF16),
        "w_kv": jnp.concatenate([p["w_uk"].reshape(kv_lora, heads * nope),
                                 p["w_uv"].reshape(kv_lora, heads * dims["v_dim"])], axis=1).astype(BF16),
        "w_o": p["w_o"].astype(BF16),
        "w_glu": p["w_glu"].astype(BF16),
        "w_out": p["w_out"].astype(BF16),
        "peer_wq": p["peer_wq"].astype(BF16),
        "peer_k1": p["peer_k1"].astype(BF16),
        "peer_k2": p["peer_k2"].astype(BF16),
        "peer_u": p["peer_u"].astype(BF16),
        "peer_vt": p["peer_v"].T.astype(BF16),
        "s5": _s5_tables(p["ssm_a_re"], p["ssm_a_im"], p["ssm_log_dt"], p["ssm_b_re"], p["ssm_b_im"],
                         p["ssm_c_re"], p["ssm_c_im"], p["ssm_d"]),
    }


def _layer(x, mod, past_ckv, past_kr, s0, p, w, dims):
    b, l, d = x.shape
    t = b * l
    heads, rope, kv_lora, topk = dims["heads"], dims["rope"], dims["kv_lora"], dims["topk"]
    sh1, sc1, gt1, sh2, sc2, gt2 = mod
    past = 0 if past_ckv is None else past_ckv.shape[1]

    h = _normmod(x, p["g_norm1"], sc1, sh1).reshape(t, d)
    gates = _mm(h, w["w_gates"], BF16, act="sigmoid")
    u = _mm(h, w["w_u"], BF16)
    cos, sin = _rope_tables(past + jnp.arange(l, dtype=jnp.int32), rope)
    qn, qr, ckv, ckv_b, kr, krp = _mla_proj(
        h, w["w_mla"], p["g_q"], p["g_kv"], w["w_q"], cos, sin, l,
        heads=heads, q_lora=dims["q_lora"], kv_lora=kv_lora, rope=rope, scale=dims["scale"])

    ys, s_fin = _s5(u.reshape(b, l, -1), *w["s5"], s0)

    if past_ckv is None:
        lk, keys_c, keys_r = l, ckv_b, krp
        tq = tk = _tile(l, 512)
    else:
        n_keys = past + l
        lk = -(-n_keys // LANES) * LANES
        keys_c = jnp.concatenate([past_ckv.astype(BF16), ckv_b.reshape(b, l, kv_lora)], axis=1)
        keys_c = jnp.pad(keys_c, ((0, 0), (0, lk - n_keys), (0, 0))).reshape(b * lk, kv_lora)
        past_r = jnp.pad(past_kr.astype(BF16), ((0, 0), (0, 0), (0, LANES - rope)))
        keys_r = jnp.concatenate([past_r, krp.reshape(b, l, LANES)], axis=1)
        keys_r = jnp.pad(keys_r, ((0, 0), (0, lk - n_keys), (0, 0))).reshape(b * lk, LANES)
        tq, tk = l, lk
    kh, vh = _kvup(keys_c, w["w_kv"], heads)
    o = _flash(qn, qr, kh, keys_r, vh, batch=b, lq=l, lk=lk, q_pos0=past, n_keys=past + l,
               tq=tq, tk=tk, hb=min(heads, 8))

    merged = _merge(ys.reshape(t, -1), o, w["w_glu"], w["w_o"], gates)
    x1, h2 = _resid(merged, w["w_out"], x, gt1, p["g_norm2"], sc2, sh2)

    h2 = h2.reshape(t, d)
    q = _mm_split(h2, w["peer_wq"], BF16)
    r2, c1, a1, a2 = _peer_topk(q, w["peer_k1"], w["peer_k2"], topk)
    peer = _peer_mix(h2, w["peer_u"], w["peer_vt"], r2, c1, a1, a2)
    return x1, peer, gt2, ckv.reshape(b, l, kv_lora), kr.reshape(b, l, rope), s_fin


def kernel(x_prompt, x_sample, c_prompt, c_sample, cache_ckv, cache_krope, state_ssm_re, state_ssm_im, w_ada, b_ada, g_norm1, g_norm2, w_in, g_q, w_qu, g_kv, w_uk, w_uv, w_o, ssm_a_re, ssm_a_im, ssm_log_dt, ssm_b_re, ssm_b_im, ssm_c_re, ssm_c_im, ssm_d, w_glu, w_out, peer_wq, peer_k1, peer_k2, peer_u, peer_v, g_final):
    depth = w_in.shape[0]
    bp, lp, d = x_prompt.shape
    bs, ls, _ = x_sample.shape
    groups, states = ssm_a_re.shape[1:]
    heads, nope = w_uk.shape[2:]
    rope = cache_krope.shape[-1]
    dims = {
        "d": d, "sw": groups * ssm_b_re.shape[3], "q_lora": g_q.shape[1], "kv_lora": g_kv.shape[1],
        "rope": rope, "heads": heads, "nope": nope, "v_dim": w_uv.shape[3],
        "scale": 1.0 / math.sqrt(nope + rope), "topk": 16,
    }
    assert nope == LANES and dims["v_dim"] == LANES and rope <= LANES

    xp, xs = x_prompt, x_sample
    nb = bp + bs
    rows = -(-nb // 16) * 16
    c_all = jnp.pad(jnp.concatenate([c_prompt, c_sample], axis=0), ((0, rows - nb), (0, 0)))
    zeros = jnp.zeros((bp, 2, groups * states), F32)
    outs_p, outs_s = [], []
    for layer in range(depth):
        p = {
            "g_norm1": g_norm1[layer], "g_norm2": g_norm2[layer], "w_in": w_in[layer], "g_q": g_q[layer],
            "w_qu": w_qu[layer], "g_kv": g_kv[layer], "w_uk": w_uk[layer], "w_uv": w_uv[layer], "w_o": w_o[layer],
            "ssm_a_re": ssm_a_re[layer], "ssm_a_im": ssm_a_im[layer], "ssm_log_dt": ssm_log_dt[layer],
            "ssm_b_re": ssm_b_re[layer], "ssm_b_im": ssm_b_im[layer], "ssm_c_re": ssm_c_re[layer],
            "ssm_c_im": ssm_c_im[layer], "ssm_d": ssm_d[layer], "w_glu": w_glu[layer], "w_out": w_out[layer],
            "peer_wq": peer_wq[layer], "peer_k1": peer_k1[layer], "peer_k2": peer_k2[layer],
            "peer_u": peer_u[layer], "peer_v": peer_v[layer],
        }
        w = _prep_weights(p, dims)
        mod = _ada(c_all, w_ada[layer], b_ada[layer])
        mod_p = [m.reshape(bp, 1, d) for m in jnp.split(mod[:bp], 6, axis=-1)]
        mod_s = [m.reshape(bs, 1, d) for m in jnp.split(mod[bp:nb], 6, axis=-1)]
        s0_s = jnp.stack([state_ssm_re[layer].reshape(bs, -1), state_ssm_im[layer].reshape(bs, -1)], axis=1)
        res_p = _layer(xp, mod_p, None, None, zeros, p, w, dims)
        res_s = _layer(xs, mod_s, cache_ckv[layer], cache_krope[layer], s0_s, p, w, dims)
        last = layer == depth - 1
        for res, outs, g in ((res_p, outs_p, bp), (res_s, outs_s, bs)):
            x1, peer, gt2, ckv, kr, s_fin = res
            outs.append((x1, peer, gt2, ckv, kr, s_fin[:, 0].reshape(g, groups, states), s_fin[:, 1].reshape(g, groups, states)))
        xp = _final(outs_p[-1][0], outs_p[-1][1], outs_p[-1][2], g_final, last)
        xs = _final(outs_s[-1][0], outs_s[-1][1], outs_s[-1][2], g_final, last)
    stack = lambda outs, k: jnp.stack([o[k] for o in outs])
    return (xp, xs,
            stack(outs_p, 3), stack(outs_p, 4), stack(outs_p, 5), stack(outs_p, 6),
            stack(outs_s, 3), stack(outs_s, 4), stack(outs_s, 5), stack(outs_s, 6))
```

```python
import functools
import math

import jax
import jax.numpy as jnp
from jax import lax
from jax.experimental import pallas as pl
from jax.experimental.pallas import tpu as pltpu

F32 = jnp.float32
BF16 = jnp.bfloat16

EPS = 1e-6
CHUNK = 64
ROPE_THETA = 10000.0
LANES = 128
SUBLANES = 8
MXU_DIM = 256
VMEM_LIMIT_BYTES = 56 * 1024 * 1024
MASKED = -1e30
REMOVED = -3e38


def _params(*semantics):
    return pltpu.CompilerParams(dimension_semantics=semantics, vmem_limit_bytes=VMEM_LIMIT_BYTES)


def _tile(n, pref):
    if n <= pref:
        return n
    t = pref
    while n % t:
        t //= 2
    assert t >= SUBLANES, (n, pref)
    return t


def _sigmoid(x):
    return 1.0 / (1.0 + jnp.exp(-x))


def _gelu(x):
    return 0.5 * x * (1.0 + jnp.tanh(math.sqrt(2.0 / math.pi) * (x + 0.044715 * (x * x * x))))


def _rms(x, g):
    return x * lax.rsqrt(jnp.mean(x * x, axis=-1, keepdims=True) + EPS) * g


def _ada_kernel(c_ref, w_ref, b_ref, o_ref):
    c = c_ref[...]
    a = (c * _sigmoid(c)).astype(BF16)
    o_ref[...] = jnp.dot(a, w_ref[...].astype(BF16), preferred_element_type=F32) + b_ref[...]


def _ada(c, w, b):
    rows, d = c.shape
    n = w.shape[1]
    tn = _tile(n, 1024)
    return pl.pallas_call(
        _ada_kernel,
        grid=(n // tn,),
        in_specs=[
            pl.BlockSpec((rows, d), lambda j: (0, 0)),
            pl.BlockSpec((d, tn), lambda j: (0, j)),
            pl.BlockSpec((1, tn), lambda j: (0, j)),
        ],
        out_specs=pl.BlockSpec((rows, tn), lambda j: (0, j)),
        out_shape=jax.ShapeDtypeStruct((rows, n), F32),
        compiler_params=_params("parallel"),
        name="ada",
    )(c, w, b.reshape(1, n))


def _normmod_kernel(x_ref, g_ref, sc_ref, sh_ref, o_ref):
    h = _rms(x_ref[0], g_ref[...]) * (1.0 + sc_ref[0]) + sh_ref[0]
    o_ref[0] = h.astype(BF16)


def _normmod(x, g, sc, sh):
    b, l, d = x.shape
    tm = _tile(l, 512)
    return pl.pallas_call(
        _normmod_kernel,
        grid=(b, l // tm),
        in_specs=[
            pl.BlockSpec((1, tm, d), lambda i, j: (i, j, 0)),
            pl.BlockSpec((1, d), lambda i, j: (0, 0)),
            pl.BlockSpec((1, 1, d), lambda i, j: (i, 0, 0)),
            pl.BlockSpec((1, 1, d), lambda i, j: (i, 0, 0)),
        ],
        out_specs=pl.BlockSpec((1, tm, d), lambda i, j: (i, j, 0)),
        out_shape=jax.ShapeDtypeStruct((b, l, d), BF16),
        compiler_params=_params("parallel", "parallel"),
        name="normmod",
    )(x, g.reshape(1, d), sc, sh)


def _mm_kernel(a_ref, w_ref, o_ref, *, act):
    z = jnp.dot(a_ref[...], w_ref[...], preferred_element_type=F32)
    if act == "sigmoid":
        z = _sigmoid(z)
    o_ref[...] = z.astype(o_ref.dtype)


def _mm(a, w, out_dtype, act=None, tn_pref=512):
    t, k = a.shape
    n = w.shape[1]
    tm = _tile(t, 512)
    tn = _tile(n, tn_pref)
    return pl.pallas_call(
        functools.partial(_mm_kernel, act=act),
        grid=(t // tm, n // tn),
        in_specs=[
            pl.BlockSpec((tm, k), lambda i, j: (i, 0)),
            pl.BlockSpec((k, tn), lambda i, j: (0, j)),
        ],
        out_specs=pl.BlockSpec((tm, tn), lambda i, j: (i, j)),
        out_shape=jax.ShapeDtypeStruct((t, n), out_dtype),
        compiler_params=_params("parallel", "parallel"),
        name="mm",
    )(a, w)


def _mm_split_kernel(a_ref, w_ref, o_ref, *, parts):
    z = jnp.dot(a_ref[...], w_ref[...], preferred_element_type=F32)
    for p in range(parts):
        o_ref[p] = z[:, p * LANES:(p + 1) * LANES].astype(o_ref.dtype)


def _mm_split(a, w, out_dtype):
    t, k = a.shape
    n = w.shape[1]
    tm = _tile(t, 512)
    tn = _tile(n, MXU_DIM)
    parts = tn // LANES
    return pl.pallas_call(
        functools.partial(_mm_split_kernel, parts=parts),
        grid=(t // tm, n // tn),
        in_specs=[
            pl.BlockSpec((tm, k), lambda i, j: (i, 0)),
            pl.BlockSpec((k, tn), lambda i, j: (0, j)),
        ],
        out_specs=pl.BlockSpec((parts, tm, LANES), lambda i, j: (j, i, 0)),
        out_shape=jax.ShapeDtypeStruct((n // LANES, t, LANES), out_dtype),
        compiler_params=_params("parallel", "parallel"),
        name="mm_split",
    )(a, w)


def _mla_proj_kernel(h_ref, wm_ref, gq_ref, gkv_ref, wq_ref, cos_ref, sin_ref,
                     qn_ref, qr_ref, ckv_ref, ckvb_ref, kr_ref, krp_ref,
                     *, heads, q_lora, kv_lora, rope, scale):
    z = jnp.dot(h_ref[...], wm_ref[...], preferred_element_type=F32)
    cos = cos_ref[...]
    sin = sin_ref[...]
    off = q_lora + kv_lora
    krp = z[:, off:off + LANES] * cos + z[:, off + LANES:off + 2 * LANES] * sin
    kr_ref[...] = krp[:, :rope]
    krp_ref[...] = krp.astype(BF16)
    ckv = _rms(z[:, q_lora:off], gkv_ref[...])
    ckv_ref[...] = ckv
    ckvb_ref[...] = ckv.astype(BF16)
    qd = _rms(z[:, :q_lora], gq_ref[...]).astype(BF16)
    zq = jnp.dot(qd, wq_ref[...], preferred_element_type=F32)
    hn = heads * LANES
    for hh in range(heads):
        lo = hh * LANES
        qn_ref[hh] = (zq[:, lo:lo + LANES] * scale).astype(BF16)
        qr = zq[:, hn + lo:hn + lo + LANES] * cos + zq[:, 2 * hn + lo:2 * hn + lo + LANES] * sin
        qr_ref[hh] = (qr * scale).astype(BF16)


def _mla_proj(h, wm, gq, gkv, wq, cos, sin, seq, *, heads, q_lora, kv_lora, rope, scale):
    t, d = h.shape
    tm = _tile(t, 256)
    if seq % tm == 0:
        nrep = seq // tm
        tab_map = lambda i: (i % nrep, 0)
    else:
        assert tm % seq == 0
        cos = jnp.tile(cos, (tm // seq, 1))
        sin = jnp.tile(sin, (tm // seq, 1))
        tab_map = lambda i: (0, 0)
    nm = wm.shape[1]
    nq = wq.shape[1]
    row = lambda i: (i, 0)
    const = lambda i: (0, 0)
    return pl.pallas_call(
        functools.partial(_mla_proj_kernel, heads=heads, q_lora=q_lora, kv_lora=kv_lora, rope=rope, scale=scale),
        grid=(t // tm,),
        in_specs=[
            pl.BlockSpec((tm, d), row),
            pl.BlockSpec((d, nm), const),
            pl.BlockSpec((1, q_lora), const),
            pl.BlockSpec((1, kv_lora), const),
            pl.BlockSpec((q_lora, nq), const),
            pl.BlockSpec((tm, LANES), tab_map),
            pl.BlockSpec((tm, LANES), tab_map),
        ],
        out_specs=[
            pl.BlockSpec((heads, tm, LANES), lambda i: (0, i, 0)),
            pl.BlockSpec((heads, tm, LANES), lambda i: (0, i, 0)),
            pl.BlockSpec((tm, kv_lora), row),
            pl.BlockSpec((tm, kv_lora), row),
            pl.BlockSpec((tm, rope), row),
            pl.BlockSpec((tm, LANES), row),
        ],
        out_shape=[
            jax.ShapeDtypeStruct((heads, t, LANES), BF16),
            jax.ShapeDtypeStruct((heads, t, LANES), BF16),
            jax.ShapeDtypeStruct((t, kv_lora), F32),
            jax.ShapeDtypeStruct((t, kv_lora), BF16),
            jax.ShapeDtypeStruct((t, rope), F32),
            jax.ShapeDtypeStruct((t, LANES), BF16),
        ],
        compiler_params=_params("parallel"),
        name="mla_proj",
    )(h, wm, gq.reshape(1, q_lora), gkv.reshape(1, kv_lora), wq, cos, sin)


def _kvup_kernel(c_ref, w_ref, k_ref, v_ref, *, heads):
    z = jnp.dot(c_ref[...], w_ref[...], preferred_element_type=F32)
    for hh in range(heads):
        k_ref[hh] = z[:, hh * LANES:(hh + 1) * LANES].astype(BF16)
        v_ref[hh] = z[:, (heads + hh) * LANES:(heads + hh + 1) * LANES].astype(BF16)


def _kvup(ckv, w, heads):
    t, c = ckv.shape
    tm = _tile(t, 512)
    if t % tm:
        tm = t
    return pl.pallas_call(
        functools.partial(_kvup_kernel, heads=heads),
        grid=(t // tm,),
        in_specs=[
            pl.BlockSpec((tm, c), lambda i: (i, 0)),
            pl.BlockSpec(w.shape, lambda i: (0, 0)),
        ],
        out_specs=[
            pl.BlockSpec((heads, tm, LANES), lambda i: (0, i, 0)),
            pl.BlockSpec((heads, tm, LANES), lambda i: (0, i, 0)),
        ],
        out_shape=[jax.ShapeDtypeStruct((heads, t, LANES), BF16)] * 2,
        compiler_params=_params("parallel"),
        name="kvup",
    )(ckv, w)


def _flash_kernel(iq_tab, ik_tab, flag_tab, qn_ref, qr_ref, k_ref, kr_ref, v_ref, o_ref,
                  qc_scr, m_scr, acc_scr, *, hb, tq, tk, q_pos0, n_keys):
    pair = pl.program_id(2)
    iq = iq_tab[pair]
    ik = ik_tab[pair]
    flags = flag_tab[pair]
    first = (flags & 1) != 0
    last = (flags & 2) != 0
    full = (flags & 4) != 0

    @pl.when(first)
    def _init():
        m_scr[...] = jnp.full(m_scr.shape, MASKED, F32)
        acc_scr[...] = jnp.zeros(acc_scr.shape, F32)
        for hh in range(hb):
            qc_scr[hh] = jnp.concatenate([qn_ref[hh], qr_ref[hh]], axis=1)

    def step(masked):
        kr = kr_ref[...]
        ones = jnp.ones((tk, LANES), BF16)
        if masked:
            qp = q_pos0 + iq * tq + lax.broadcasted_iota(jnp.int32, (tq, tk), 0)
            kp = ik * tk + lax.broadcasted_iota(jnp.int32, (tq, tk), 1)
            allowed = jnp.logical_and(kp // CHUNK <= qp // CHUNK, kp < n_keys)

        def head(hh, carry):
            kc = jnp.concatenate([k_ref[hh], kr], axis=1)
            s = lax.dot_general(qc_scr[hh], kc, (((1,), (1,)), ((), ())), preferred_element_type=F32)
            if masked:
                s = jnp.where(allowed, s, MASKED)
            m_prev = m_scr[hh]
            m_new = jnp.maximum(m_prev, jnp.max(s, axis=1, keepdims=True))
            alpha = jnp.exp2(m_prev - m_new)
            p = jnp.exp2(s - pltpu.repeat(m_new, tk // LANES, axis=1)).astype(BF16)
            vc = jnp.concatenate([v_ref[hh], ones], axis=1)
            acc_scr[hh] = pltpu.repeat(alpha, 2, axis=1) * acc_scr[hh] + jnp.dot(p, vc, preferred_element_type=F32)
            m_scr[hh] = m_new
            return carry

        lax.fori_loop(0, hb, head, 0, unroll=min(hb, 8))

    @pl.when(full)
    def _full():
        step(False)

    @pl.when(jnp.logical_not(full))
    def _diag():
        step(True)

    @pl.when(last)
    def _done():
        for hh in range(hb):
            acc = acc_scr[hh]
            o_ref[hh] = (acc[:, :LANES] / acc[:, LANES:]).astype(BF16)


def _flash_pairs(lq, lk, tq, tk, q_pos0, n_keys):
    iqs, iks, flags = [], [], []
    for iq in range(lq // tq):
        q_first = q_pos0 + iq * tq
        last_key = min(n_keys - 1, ((q_first + tq - 1) // CHUNK) * CHUNK + CHUNK - 1)
        full_key = min(n_keys - 1, (q_first // CHUNK) * CHUNK + CHUNK - 1)
        n_blocks = last_key // tk + 1
        for ik in range(n_blocks):
            full = (ik + 1) * tk - 1 <= full_key
            iqs.append(iq)
            iks.append(ik)
            flags.append((1 if ik == 0 else 0) | (2 if ik == n_blocks - 1 else 0) | (4 if full else 0))
    as_i32 = lambda xs: jnp.asarray(xs, jnp.int32)
    return as_i32(iqs), as_i32(iks), as_i32(flags)


def _flash(qn, qr, k, krp, v, *, batch, lq, lk, q_pos0, n_keys, tq, tk, hb):
    heads = qn.shape[0]
    nq = lq // tq
    nk = lk // tk
    assert lq % tq == 0 and lk % tk == 0 and heads % hb == 0 and tk % LANES == 0
    iq_tab, ik_tab, flag_tab = _flash_pairs(lq, lk, tq, tk, q_pos0, n_keys)
    qmap = lambda b, h, p, iqt, ikt, ft: (h, b * nq + iqt[p], 0)
    kmap = lambda b, h, p, iqt, ikt, ft: (h, b * nk + ikt[p], 0)
    return pl.pallas_call(
        functools.partial(_flash_kernel, hb=hb, tq=tq, tk=tk, q_pos0=q_pos0, n_keys=n_keys),
        grid_spec=pltpu.PrefetchScalarGridSpec(
            num_scalar_prefetch=3,
            grid=(batch, heads // hb, iq_tab.shape[0]),
            in_specs=[
                pl.BlockSpec((hb, tq, LANES), qmap),
                pl.BlockSpec((hb, tq, LANES), qmap),
                pl.BlockSpec((hb, tk, LANES), kmap),
                pl.BlockSpec((tk, LANES), lambda b, h, p, iqt, ikt, ft: (b * nk + ikt[p], 0)),
                pl.BlockSpec((hb, tk, LANES), kmap),
            ],
            out_specs=pl.BlockSpec((hb, tq, LANES), qmap),
            scratch_shapes=[
                pltpu.VMEM((hb, tq, 2 * LANES), BF16),
                pltpu.VMEM((hb, tq, LANES), F32),
                pltpu.VMEM((hb, tq, 2 * LANES), F32),
            ],
        ),
        out_shape=jax.ShapeDtypeStruct((heads, batch * lq, LANES), BF16),
        compiler_params=_params("parallel", "parallel", "arbitrary"),
        name="flash",
    )(iq_tab, ik_tab, flag_tab, qn, qr, k, krp, v)


def _s5_kernel(u_ref, wb_ref, scn_ref, wc_ref, d_ref, s0_ref, y_ref, st_ref, xs_ref, car_ref,
               *, ts, nkt, kw, sw, nt):
    it = pl.program_id(1)

    @pl.when(it == 0)
    def _load_state():
        car_ref[...] = s0_ref[0]

    for kt in range(nkt):
        ukt = u_ref[0, :, kt * kw:(kt + 1) * kw]
        xs_ref[...] = jnp.dot(ukt, wb_ref[kt], preferred_element_type=F32)
        st_sl = slice(kt * sw, (kt + 1) * sw)

        def blk(k, carry, kt=kt):
            cr, ci = carry
            rows = pl.ds(pl.multiple_of(k * SUBLANES, SUBLANES), SUBLANES)
            hr = xs_ref[rows, :sw]
            hi = xs_ref[rows, sw:]
            for c0, dist in ((0, 1), (2, 2), (4, 4)):
                ar = scn_ref[kt, c0]
                ai = scn_ref[kt, c0 + 1]
                sr = pltpu.roll(hr, dist, 0)
                si = pltpu.roll(hi, dist, 0)
                hr, hi = hr + ar * sr - ai * si, hi + ar * si + ai * sr
            pr = scn_ref[kt, 6]
            pi = scn_ref[kt, 7]
            crb = jnp.broadcast_to(cr, (SUBLANES, sw))
            cib = jnp.broadcast_to(ci, (SUBLANES, sw))
            hr, hi = hr + pr * crb - pi * cib, hi + pr * cib + pi * crb
            xs_ref[rows, :sw] = hr
            xs_ref[rows, sw:] = hi
            return hr[SUBLANES - 1:SUBLANES], hi[SUBLANES - 1:SUBLANES]

        cr, ci = lax.fori_loop(0, ts // SUBLANES, blk, (car_ref[0:1, st_sl], car_ref[1:2, st_sl]),
                               unroll=2 if ts >= 2 * SUBLANES else 1)
        car_ref[0:1, st_sl] = cr
        car_ref[1:2, st_sl] = ci
        y = jnp.dot(xs_ref[...].astype(BF16), wc_ref[kt], preferred_element_type=F32)
        y = y + d_ref[:, kt * kw:(kt + 1) * kw] * ukt.astype(F32)
        y_ref[0, :, kt * kw:(kt + 1) * kw] = _gelu(y).astype(BF16)

    @pl.when(it == nt - 1)
    def _store_state():
        st_ref[0] = car_ref[...]


def _s5(u, wb, scn, wc, dsk, s0):
    b, l, w = u.shape
    nkt, kw, sw2 = wb.shape
    sw = sw2 // 2
    ns = s0.shape[2]
    ts = _tile(l, 512)
    nt = l // ts
    const3 = lambda i, j: (0, 0, 0)
    return pl.pallas_call(
        functools.partial(_s5_kernel, ts=ts, nkt=nkt, kw=kw, sw=sw, nt=nt),
        grid=(b, nt),
        in_specs=[
            pl.BlockSpec((1, ts, w), lambda i, j: (i, j, 0)),
            pl.BlockSpec(wb.shape, const3),
            pl.BlockSpec(scn.shape, lambda i, j: (0, 0, 0, 0)),
            pl.BlockSpec(wc.shape, const3),
            pl.BlockSpec((1, w), lambda i, j: (0, 0)),
            pl.BlockSpec((1, 2, ns), lambda i, j: (i, 0, 0)),
        ],
        out_specs=[
            pl.BlockSpec((1, ts, w), lambda i, j: (i, j, 0)),
            pl.BlockSpec((1, 2, ns), lambda i, j: (i, 0, 0)),
        ],
        out_shape=[
            jax.ShapeDtypeStruct((b, l, w), BF16),
            jax.ShapeDtypeStruct((b, 2, ns), F32),
        ],
        scratch_shapes=[
            pltpu.VMEM((ts, sw2), F32),
            pltpu.VMEM((2, ns), F32),
        ],
        compiler_params=_params("parallel", "arbitrary"),
        name="s5",
    )(u, wb, scn, wc, dsk, s0)


def _s5_tables(a_re, a_im, log_dt, b_re, b_im, c_re, c_im, d_skip):
    g, p = a_re.shape
    ch = b_re.shape[2]
    gpt = MXU_DIM // ch
    nkt = g // gpt
    dt = jnp.exp(log_dt.astype(F32))[:, None]
    mag = jnp.exp(dt * a_re)
    ab_re = mag * jnp.cos(dt * a_im)
    ab_im = mag * jnp.sin(dt * a_im)
    den = a_re * a_re + a_im * a_im
    nr = ab_re - 1.0
    f_re = (nr * a_re + ab_im * a_im) / den
    f_im = (ab_im * a_re - nr * a_im) / den
    bb_re = f_re[..., None] * b_re - f_im[..., None] * b_im
    bb_im = f_re[..., None] * b_im + f_im[..., None] * b_re
    eye = jnp.eye(gpt, dtype=F32)

    def in_blockdiag(bb):
        return jnp.einsum("kgpi,gh->kgihp", bb.reshape(nkt, gpt, p, ch), eye).reshape(nkt, gpt * ch, gpt * p)

    def out_blockdiag(cc):
        return jnp.einsum("kgjp,gh->kgphj", cc.reshape(nkt, gpt, ch, p), eye).reshape(nkt, gpt * p, gpt * ch)

    wb = jnp.concatenate([in_blockdiag(bb_re), in_blockdiag(bb_im)], axis=2).astype(BF16)
    wc = jnp.concatenate([out_blockdiag(c_re), out_blockdiag(-c_im)], axis=1).astype(BF16)

    def cmul(x, y):
        return x[0] * y[0] - x[1] * y[1], x[0] * y[1] + x[1] * y[0]

    lam1 = (ab_re.reshape(nkt, gpt * p), ab_im.reshape(nkt, gpt * p))
    lam2 = cmul(lam1, lam1)
    lam4 = cmul(lam2, lam2)
    rows = jnp.arange(SUBLANES)[None, :, None]

    def shifted(lam, dist):
        return [jnp.where(rows >= dist, c[:, None, :], 0.0) for c in lam]

    pw = [lam1]
    for _ in range(SUBLANES - 1):
        pw.append(cmul(pw[-1], lam1))
    p_re = jnp.stack([c[0] for c in pw], axis=1)
    p_im = jnp.stack([c[1] for c in pw], axis=1)
    scn = jnp.stack(shifted(lam1, 1) + shifted(lam2, 2) + shifted(lam4, 4) + [p_re, p_im], axis=1)
    return wb, scn.astype(F32), wc, d_skip.reshape(1, g * ch).astype(F32)


def _merge_kernel(ys_ref, o_ref, wga_ref, wgb_ref, wo_ref, ga_ref, gb_ref, out_ref, *, heads):
    ys = ys_ref[...]
    ya = jnp.dot(ys, wga_ref[...], preferred_element_type=F32)
    ya = ya * _sigmoid(jnp.dot(ys, wgb_ref[...], preferred_element_type=F32))
    oc = jnp.concatenate([o_ref[hh] for hh in range(heads)], axis=1)
    yb = jnp.dot(oc, wo_ref[...], preferred_element_type=F32)
    out_ref[...] = (ga_ref[...].astype(F32) * ya + gb_ref[...].astype(F32) * yb).astype(BF16)


def _merge(ys, o, w_glu, w_o, gates):
    t, sw = ys.shape
    heads = o.shape[0]
    d = w_o.shape[1]
    tm = _tile(t, 512)
    tn = _tile(d, 512)
    nj = d // tn
    return pl.pallas_call(
        functools.partial(_merge_kernel, heads=heads),
        grid=(t // tm, nj),
        in_specs=[
            pl.BlockSpec((tm, sw), lambda i, j: (i, 0)),
            pl.BlockSpec((heads, tm, LANES), lambda i, j: (0, i, 0)),
            pl.BlockSpec((sw, tn), lambda i, j: (0, j)),
            pl.BlockSpec((sw, tn), lambda i, j: (0, nj + j)),
            pl.BlockSpec((heads * LANES, tn), lambda i, j: (0, j)),
            pl.BlockSpec((tm, tn), lambda i, j: (i, j)),
            pl.BlockSpec((tm, tn), lambda i, j: (i, nj + j)),
        ],
        out_specs=pl.BlockSpec((tm, tn), lambda i, j: (i, j)),
        out_shape=jax.ShapeDtypeStruct((t, d), BF16),
        compiler_params=_params("parallel", "parallel"),
        name="merge",
    )(ys, o, w_glu, w_glu, w_o, gates, gates)


def _resid_kernel(m_ref, w_ref, x_ref, gt_ref, g_ref, sc_ref, sh_ref, x1_ref, h2_ref):
    x1 = x_ref[0] + gt_ref[0] * jnp.dot(m_ref[0], w_ref[...], preferred_element_type=F32)
    x1_ref[0] = x1
    h2_ref[0] = (_rms(x1, g_ref[...]) * (1.0 + sc_ref[0]) + sh_ref[0]).astype(BF16)


def _resid(merged, w_out, x, gt, g2, sc, sh):
    b, l, d = x.shape
    tm = _tile(l, 256)
    tok = lambda i, j: (i, j, 0)
    per_b = lambda i, j: (i, 0, 0)
    return pl.pallas_call(
        _resid_kernel,
        grid=(b, l // tm),
        in_specs=[
            pl.BlockSpec((1, tm, d), tok),
            pl.BlockSpec((d, d), lambda i, j: (0, 0)),
            pl.BlockSpec((1, tm, d), tok),
            pl.BlockSpec((1, 1, d), per_b),
            pl.BlockSpec((1, d), lambda i, j: (0, 0)),
            pl.BlockSpec((1, 1, d), per_b),
            pl.BlockSpec((1, 1, d), per_b),
        ],
        out_specs=[pl.BlockSpec((1, tm, d), tok), pl.BlockSpec((1, tm, d), tok)],
        out_shape=[jax.ShapeDtypeStruct((b, l, d), F32), jax.ShapeDtypeStruct((b, l, d), BF16)],
        compiler_params=_params("parallel", "parallel"),
        name="resid",
    )(merged.reshape(b, l, d), w_out, x, gt, g2.reshape(1, d), sc, sh)


def _peer_topk_kernel(q_ref, k1_ref, k2_ref, flat_ref, invalid_ref, r2_ref, c1_ref, a1_ref, a2_ref,
                      *, nh, nkeys, topk):
    tb = q_ref.shape[1]
    iota_k = lax.broadcasted_iota(jnp.int32, (nkeys, tb), 0).astype(F32)
    iota_t = lax.broadcasted_iota(jnp.int32, (topk, tb), 0).astype(F32)
    flat = flat_ref[...]
    invalid = invalid_ref[...]
    nt_dims = (((1,), (1,)), ((), ()))

    def extract(s):
        work = s
        rank = jnp.full((nkeys, tb), float(topk), F32)
        vals = []
        for a in range(topk):
            m = jnp.max(work, axis=0, keepdims=True)
            idx = jnp.min(jnp.where(work == m, iota_k, float(nkeys)), axis=0, keepdims=True)
            sel = iota_k == idx
            rank = jnp.where(sel, float(a), rank)
            work = jnp.where(sel, REMOVED, work)
            vals.append(m)
        return vals, rank

    def stack(vals):
        out = jnp.zeros((topk, tb), F32)
        for a in range(topk):
            out = jnp.where(iota_t == float(a), vals[a], out)
        return out

    def head(hh, carry):
        s1 = lax.dot_general(k1_ref[hh], q_ref[2 * hh], nt_dims, preferred_element_type=F32)
        s2 = lax.dot_general(k2_ref[hh], q_ref[2 * hh + 1], nt_dims, preferred_element_type=F32)
        v1, rank1 = extract(s1)
        v2, rank2 = extract(s2)
        vs1 = stack(v1)
        vs2 = stack(v2)
        blocks = [v1[0] + vs2]
        for a in range(1, SUBLANES):
            blocks.append(v1[a] + vs2[:SUBLANES])
        blocks.append(vs1[SUBLANES:] + v2[0])
        cand = jnp.concatenate(blocks, axis=0) + invalid
        counts = jnp.zeros((topk, tb), F32)
        top = None
        zsum = None
        for kk in range(topk):
            m = jnp.max(cand, axis=0, keepdims=True)
            f = jnp.min(jnp.where(cand == m, flat, 1e9), axis=0, keepdims=True)
            cand = jnp.where(flat == f, REMOVED, cand)
            counts = counts + jnp.where(iota_t == jnp.floor(f * (1.0 / topk)), 1.0, 0.0)
            if kk == 0:
                top = m
                zsum = jnp.ones_like(m)
            else:
                zsum = zsum + jnp.exp(m - top)
        c1 = jnp.zeros((nkeys, tb), F32)
        for a in range(topk):
            c1 = jnp.where(rank1 == float(a), counts[a:a + 1], c1)
        r2_ref[hh] = rank2
        c1_ref[hh] = c1
        a1_ref[hh] = jnp.exp(s1 - v1[0]) * (1.0 / zsum)
        a2_ref[hh] = jnp.exp(s2 - v2[0])
        return carry

    lax.fori_loop(0, nh, head, 0, unroll=2)


def _peer_topk(q, k1, k2, topk):
    nh, nkeys, half = k1.shape
    t = q.shape[1]
    tb = LANES
    assert topk == 2 * SUBLANES and half == LANES and t % tb == 0
    rows = jnp.arange(topk + (SUBLANES - 1) * SUBLANES + SUBLANES)
    a_idx = jnp.where(rows < topk, 0, jnp.where(rows < topk + (SUBLANES - 1) * SUBLANES,
                                                1 + (rows - topk) // SUBLANES, SUBLANES + (rows - topk - (SUBLANES - 1) * SUBLANES)))
    b_idx = jnp.where(rows < topk, rows, jnp.where(rows < topk + (SUBLANES - 1) * SUBLANES, (rows - topk) % SUBLANES, 0))
    flat = jnp.broadcast_to((a_idx * topk + b_idx).astype(F32)[:, None], (rows.shape[0], tb))
    invalid = jnp.broadcast_to(jnp.where((a_idx + 1) * (b_idx + 1) <= topk, 0.0, REMOVED).astype(F32)[:, None],
                               (rows.shape[0], tb))
    out = lambda dt: jax.ShapeDtypeStruct((nh, nkeys, t), dt)
    ospec = pl.BlockSpec((nh, nkeys, tb), lambda i: (0, 0, i))
    return pl.pallas_call(
        functools.partial(_peer_topk_kernel, nh=nh, nkeys=nkeys, topk=topk),
        grid=(t // tb,),
        in_specs=[
            pl.BlockSpec((2 * nh, tb, LANES), lambda i: (0, i, 0)),
            pl.BlockSpec(k1.shape, lambda i: (0, 0, 0)),
            pl.BlockSpec(k2.shape, lambda i: (0, 0, 0)),
            pl.BlockSpec(flat.shape, lambda i: (0, 0)),
            pl.BlockSpec(invalid.shape, lambda i: (0, 0)),
        ],
        out_specs=[ospec] * 4,
        out_shape=[out(F32)] * 4,
        compiler_params=_params("parallel"),
        name="peer_topk",
    )(q, k1, k2, flat, invalid)


def _peer_mix_kernel(h_ref, u_ref, vt_ref, r2_ref, c1_ref, a1_ref, a2_ref, o_ref, ht_scr, w_scr, acc_scr,
                     *, nh, nkeys, ni, ne):
    s = pl.program_id(1)
    tb = h_ref.shape[0]
    d = vt_ref.shape[0]
    cur = s % 2

    @pl.when(s == 0)
    def _init():
        acc_scr[...] = jnp.zeros(acc_scr.shape, F32)
        w_scr[1] = jnp.zeros(w_scr.shape[1:], BF16)

    e = jnp.minimum(s, ne - 1)
    first_keys = pl.ds(pl.multiple_of(e * ni, SUBLANES), ni)
    h = h_ref[...]
    n_chunks = 4
    e_rows = ni * nkeys // n_chunks
    d_rows = d // n_chunks
    for c in range(n_chunks):
        crows = slice(c * e_rows, (c + 1) * e_rows)
        ht_scr[crows, :] = lax.dot_general(u_ref[crows, :], h, (((1,), (1,)), ((), ())),
                                           preferred_element_type=F32)
        orows = slice(c * d_rows, (c + 1) * d_rows)
        acc_scr[orows, :] += jnp.dot(vt_ref[orows, :], w_scr[1 - cur], preferred_element_type=F32)
        for il in range(c * e_rows // nkeys, (c + 1) * e_rows // nkeys):
            rows = slice(il * nkeys, (il + 1) * nkeys)
            for lg in range(tb // LANES):
                sl = slice(lg * LANES, (lg + 1) * LANES)
                gate = jnp.zeros((nkeys, LANES), F32)
                for hh in range(nh):
                    partners = c1_ref[hh, first_keys, sl][il:il + 1]
                    first = a1_ref[hh, first_keys, sl][il:il + 1]
                    gate = gate + jnp.where(r2_ref[hh, :, sl] < partners, a2_ref[hh, :, sl], 0.0) * first
                w_scr[cur, rows, sl] = (gate * _gelu(ht_scr[rows, sl])).astype(BF16)

    @pl.when(s == ne)
    def _done():
        o_ref[...] = acc_scr[...].T


def _peer_mix(h2, u_tab, vt_tab, r2, c1, a1, a2):
    t, d = h2.shape
    nh, nkeys, _ = r2.shape
    n_exp = u_tab.shape[0]
    tb = _tile(t, 512)
    ni = SUBLANES
    eb = ni * nkeys
    ne = n_exp // eb
    aux = pl.BlockSpec((nh, nkeys, tb), lambda i, e: (0, 0, i), pipeline_mode=pl.Buffered(1))
    return pl.pallas_call(
        functools.partial(_peer_mix_kernel, nh=nh, nkeys=nkeys, ni=ni, ne=ne),
        grid=(t // tb, ne + 1),
        in_specs=[
            pl.BlockSpec((tb, d), lambda i, s: (i, 0)),
            pl.BlockSpec((eb, d), lambda i, s: (jnp.minimum(s, ne - 1), 0)),
            pl.BlockSpec((d, eb), lambda i, s: (0, jnp.maximum(s - 1, 0))),
            aux, aux, aux, aux,
        ],
        out_specs=pl.BlockSpec((tb, d), lambda i, s: (i, 0)),
        out_shape=jax.ShapeDtypeStruct((t, d), F32),
        scratch_shapes=[
            pltpu.VMEM((eb, tb), F32),
            pltpu.VMEM((2, eb, tb), BF16),
            pltpu.VMEM((d, tb), F32),
        ],
        compiler_params=_params("parallel", "arbitrary"),
        name="peer_mix",
    )(h2, u_tab, vt_tab, r2, c1, a1, a2)


def _final_kernel(x_ref, p_ref, gt_ref, g_ref, y_ref, *, norm):
    y = x_ref[0] + gt_ref[0] * p_ref[0]
    y_ref[0] = _rms(y, g_ref[...]) if norm else y


def _final(x1, peer, gt, g, norm):
    b, l, d = x1.shape
    tm = _tile(l, 512)
    tok = lambda i, j: (i, j, 0)
    return pl.pallas_call(
        functools.partial(_final_kernel, norm=norm),
        grid=(b, l // tm),
        in_specs=[
            pl.BlockSpec((1, tm, d), tok),
            pl.BlockSpec((1, tm, d), tok),
            pl.BlockSpec((1, 1, d), lambda i, j: (i, 0, 0)),
            pl.BlockSpec((1, d), lambda i, j: (0, 0)),
        ],
        out_specs=pl.BlockSpec((1, tm, d), tok),
        out_shape=jax.ShapeDtypeStruct((b, l, d), F32),
        compiler_params=_params("parallel", "parallel"),
        name="final",
    )(x1, peer.reshape(b, l, d), gt, g.reshape(1, d))


def _rope_tables(pos, rope):
    half = rope // 2
    inv = jnp.power(ROPE_THETA, -jnp.arange(half, dtype=F32) / half)
    ang = pos.astype(F32)[:, None] * inv
    reps = LANES // half
    return jnp.tile(jnp.cos(ang), (1, reps)), jnp.tile(jnp.sin(ang), (1, reps))


def _prep_weights(p, dims):
    d, sw, q_lora, kv_lora, rope, heads, nope = (dims[k] for k in ("d", "sw", "q_lora", "kv_lora", "rope", "heads", "nope"))
    half = rope // 2
    w_in = p["w_in"]
    off_q = sw
    off_kv = off_q + q_lora
    off_kr = off_kv + kv_lora
    off_g = off_kr + rope
    kr_w = w_in[:, off_kr:off_g]
    kr_rot = jnp.concatenate([-kr_w[:, half:], kr_w[:, :half]], axis=1)
    pad = jnp.zeros((d, LANES - rope), F32)
    w_mla = jnp.concatenate([w_in[:, off_q:off_kr], kr_w, pad, kr_rot, pad], axis=1).astype(BF16)
    wq = p["w_qu"].reshape(q_lora, heads, nope + rope)
    wq_n = wq[:, :, :nope].reshape(q_lora, heads * nope)
    wq_r = wq[:, :, nope:]
    wq_rot = jnp.concatenate([-wq_r[:, :, half:], wq_r[:, :, :half]], axis=2)
    hpad = jnp.zeros((q_lora, heads, LANES - rope), F32)
    wq_a = jnp.concatenate([wq_r, hpad], axis=2).reshape(q_lora, heads * LANES)
    wq_b = jnp.concatenate([wq_rot, hpad], axis=2).reshape(q_lora, heads * LANES)
    return {
        "w_gates": w_in[:, off_g:].astype(BF16),
        "w_u": w_in[:, :sw].astype(BF16),
        "w_mla": w_mla,
        "w_q": jnp.concatenate([wq_n, wq_a, wq_b], axis=1).astype(BF16),
        "w_kv": jnp.concatenate([p["w_uk"].reshape(kv_lora, heads * nope),
                                 p["w_uv"].reshape(kv_lora, heads * dims["v_dim"])], axis=1).astype(BF16),
        "w_o": p["w_o"].astype(BF16),
        "w_glu": p["w_glu"].astype(BF16),
        "w_out": p["w_out"].astype(BF16),
        "peer_wq": p["peer_wq"].astype(BF16),
        "peer_k1": p["peer_k1"].astype(BF16),
        "peer_k2": p["peer_k2"].astype(BF16),
        "peer_u": p["peer_u"].astype(BF16),
        "peer_vt": p["peer_v"].T.astype(BF16),
        "s5": _s5_tables(p["ssm_a_re"], p["ssm_a_im"], p["ssm_log_dt"], p["ssm_b_re"], p["ssm_b_im"],
                         p["ssm_c_re"], p["ssm_c_im"], p["ssm_d"]),
    }


def _layer(x, mod, past_ckv, past_kr, s0, p, w, dims):
    b, l, d = x.shape
    t = b * l
    heads, rope, kv_lora, topk = dims["heads"], dims["rope"], dims["kv_lora"], dims["topk"]
    sh1, sc1, gt1, sh2, sc2, gt2 = mod
    past = 0 if past_ckv is None else past_ckv.shape[1]

    h = _normmod(x, p["g_norm1"], sc1, sh1).reshape(t, d)
    gates = _mm(h, w["w_gates"], BF16, act="sigmoid")
    u = _mm(h, w["w_u"], BF16)
    cos, sin = _rope_tables(past + jnp.arange(l, dtype=jnp.int32), rope)
    qn, qr, ckv, ckv_b, kr, krp = _mla_proj(
        h, w["w_mla"], p["g_q"], p["g_kv"], w["w_q"], cos, sin, l,
        heads=heads, q_lora=dims["q_lora"], kv_lora=kv_lora, rope=rope, scale=dims["scale"])

    ys, s_fin = _s5(u.reshape(b, l, -1), *w["s5"], s0)

    if past_ckv is None:
        lk, keys_c, keys_r = l, ckv_b, krp
        tq = tk = _tile(l, 512)
    else:
        n_keys = past + l
        lk = -(-n_keys // LANES) * LANES
        keys_c = jnp.concatenate([past_ckv.astype(BF16), ckv_b.reshape(b, l, kv_lora)], axis=1)
        keys_c = jnp.pad(keys_c, ((0, 0), (0, lk - n_keys), (0, 0))).reshape(b * lk, kv_lora)
        past_r = jnp.pad(past_kr.astype(BF16), ((0, 0), (0, 0), (0, LANES - rope)))
        keys_r = jnp.concatenate([past_r, krp.reshape(b, l, LANES)], axis=1)
        keys_r = jnp.pad(keys_r, ((0, 0), (0, lk - n_keys), (0, 0))).reshape(b * lk, LANES)
        tq, tk = l, lk
    kh, vh = _kvup(keys_c, w["w_kv"], heads)
    o = _flash(qn, qr, kh, keys_r, vh, batch=b, lq=l, lk=lk, q_pos0=past, n_keys=past + l,
               tq=tq, tk=tk, hb=heads)

    merged = _merge(ys.reshape(t, -1), o, w["w_glu"], w["w_o"], gates)
    x1, h2 = _resid(merged, w["w_out"], x, gt1, p["g_norm2"], sc2, sh2)

    h2 = h2.reshape(t, d)
    q = _mm_split(h2, w["peer_wq"], BF16)
    r2, c1, a1, a2 = _peer_topk(q, w["peer_k1"], w["peer_k2"], topk)
    peer = _peer_mix(h2, w["peer_u"], w["peer_vt"], r2, c1, a1, a2)
    return x1, peer, gt2, ckv.reshape(b, l, kv_lora), kr.reshape(b, l, rope), s_fin


def kernel(x_prompt, x_sample, c_prompt, c_sample, cache_ckv, cache_krope, state_ssm_re, state_ssm_im, w_ada, b_ada, g_norm1, g_norm2, w_in, g_q, w_qu, g_kv, w_uk, w_uv, w_o, ssm_a_re, ssm_a_im, ssm_log_dt, ssm_b_re, ssm_b_im, ssm_c_re, ssm_c_im, ssm_d, w_glu, w_out, peer_wq, peer_k1, peer_k2, peer_u, peer_v, g_final):
    depth = w_in.shape[0]
    bp, lp, d = x_prompt.shape
    bs, ls, _ = x_sample.shape
    groups, states = ssm_a_re.shape[1:]
    heads, nope = w_uk.shape[2:]
    rope = cache_krope.shape[-1]
    dims = {
        "d": d, "sw": groups * ssm_b_re.shape[3], "q_lora": g_q.shape[1], "kv_lora": g_kv.shape[1],
        "rope": rope, "heads": heads, "nope": nope, "v_dim": w_uv.shape[3],
        "scale": math.log2(math.e) / math.sqrt(nope + rope), "topk": 16,
    }
    assert nope == LANES and dims["v_dim"] == LANES and rope <= LANES

    xp, xs = x_prompt, x_sample
    nb = bp + bs
    rows = -(-nb // 16) * 16
    c_all = jnp.pad(jnp.concatenate([c_prompt, c_sample], axis=0), ((0, rows - nb), (0, 0)))
    zeros = jnp.zeros((bp, 2, groups * states), F32)
    outs_p, outs_s = [], []
    for layer in range(depth):
        p = {
            "g_norm1": g_norm1[layer], "g_norm2": g_norm2[layer], "w_in": w_in[layer], "g_q": g_q[layer],
            "w_qu": w_qu[layer], "g_kv": g_kv[layer], "w_uk": w_uk[layer], "w_uv": w_uv[layer], "w_o": w_o[layer],
            "ssm_a_re": ssm_a_re[layer], "ssm_a_im": ssm_a_im[layer], "ssm_log_dt": ssm_log_dt[layer],
            "ssm_b_re": ssm_b_re[layer], "ssm_b_im": ssm_b_im[layer], "ssm_c_re": ssm_c_re[layer],
            "ssm_c_im": ssm_c_im[layer], "ssm_d": ssm_d[layer], "w_glu": w_glu[layer], "w_out": w_out[layer],
            "peer_wq": peer_wq[layer], "peer_k1": peer_k1[layer], "peer_k2": peer_k2[layer],
            "peer_u": peer_u[layer], "peer_v": peer_v[layer],
        }
        w = _prep_weights(p, dims)
        mod = _ada(c_all, w_ada[layer], b_ada[layer])
        mod_p = [m.reshape(bp, 1, d) for m in jnp.split(mod[:bp], 6, axis=-1)]
        mod_s = [m.reshape(bs, 1, d) for m in jnp.split(mod[bp:nb], 6, axis=-1)]
        s0_s = jnp.stack([state_ssm_re[layer].reshape(bs, -1), state_ssm_im[layer].reshape(bs, -1)], axis=1)
        res_p = _layer(xp, mod_p, None, None, zeros, p, w, dims)
        res_s = _layer(xs, mod_s, cache_ckv[layer], cache_krope[layer], s0_s, p, w, dims)
        last = layer == depth - 1
        for res, outs, g in ((res_p, outs_p, bp), (res_s, outs_s, bs)):
            x1, peer, gt2, ckv, kr, s_fin = res
            outs.append((x1, peer, gt2, ckv, kr, s_fin[:, 0].reshape(g, groups, states), s_fin[:, 1].reshape(g, groups, states)))
        xp = _final(outs_p[-1][0], outs_p[-1][1], outs_p[-1][2], g_final, last)
        xs = _final(outs_s[-1][0], outs_s[-1][1], outs_s[-1][2], g_final, last)
    stack = lambda outs, k: jnp.stack([o[k] for o in outs])
    return (xp, xs,
            stack(outs_p, 3), stack(outs_p, 4), stack(outs_p, 5), stack(outs_p, 6),
            stack(outs_s, 3), stack(outs_s, 4), stack(outs_s, 5), stack(outs_s, 6))
```

```python
import functools
import math

import jax
import jax.numpy as jnp
from jax import lax
from jax.experimental import pallas as pl
from jax.experimental.pallas import tpu as pltpu

F32 = jnp.float32
BF16 = jnp.bfloat16

EPS = 1e-6
CHUNK = 64
ROPE_THETA = 10000.0
LANES = 128
SUBLANES = 8
MXU_DIM = 256
VMEM_LIMIT_BYTES = 56 * 1024 * 1024
MASKED = -1e30
REMOVED = -3e38
EXCLUDED = -1e38


def _params(*semantics):
    return pltpu.CompilerParams(dimension_semantics=semantics, vmem_limit_bytes=VMEM_LIMIT_BYTES)


def _tile(n, pref):
    if n <= pref:
        return n
    t = pref
    while n % t:
        t //= 2
    assert t >= SUBLANES, (n, pref)
    return t


def _sigmoid(x):
    return 1.0 / (1.0 + jnp.exp(-x))


def _gelu(x):
    return 0.5 * x * (1.0 + jnp.tanh(math.sqrt(2.0 / math.pi) * (x + 0.044715 * (x * x * x))))


def _rms(x, g):
    return x * lax.rsqrt(jnp.mean(x * x, axis=-1, keepdims=True) + EPS) * g


def _ada_kernel(c_ref, w_ref, b_ref, o_ref):
    c = c_ref[...]
    a = (c * _sigmoid(c)).astype(BF16)
    o_ref[...] = jnp.dot(a, w_ref[...].astype(BF16), preferred_element_type=F32) + b_ref[...]


def _ada(c, w, b):
    rows, d = c.shape
    n = w.shape[1]
    tn = _tile(n, 1024)
    return pl.pallas_call(
        _ada_kernel,
        grid=(n // tn,),
        in_specs=[
            pl.BlockSpec((rows, d), lambda j: (0, 0)),
            pl.BlockSpec((d, tn), lambda j: (0, j)),
            pl.BlockSpec((1, tn), lambda j: (0, j)),
        ],
        out_specs=pl.BlockSpec((rows, tn), lambda j: (0, j)),
        out_shape=jax.ShapeDtypeStruct((rows, n), F32),
        compiler_params=_params("parallel"),
        name="ada",
    )(c, w, b.reshape(1, n))


def _normmod_kernel(x_ref, g_ref, sc_ref, sh_ref, o_ref):
    h = _rms(x_ref[0], g_ref[...]) * (1.0 + sc_ref[0]) + sh_ref[0]
    o_ref[0] = h.astype(BF16)


def _normmod(x, g, sc, sh):
    b, l, d = x.shape
    tm = _tile(l, 512)
    return pl.pallas_call(
        _normmod_kernel,
        grid=(b, l // tm),
        in_specs=[
            pl.BlockSpec((1, tm, d), lambda i, j: (i, j, 0)),
            pl.BlockSpec((1, d), lambda i, j: (0, 0)),
            pl.BlockSpec((1, 1, d), lambda i, j: (i, 0, 0)),
            pl.BlockSpec((1, 1, d), lambda i, j: (i, 0, 0)),
        ],
        out_specs=pl.BlockSpec((1, tm, d), lambda i, j: (i, j, 0)),
        out_shape=jax.ShapeDtypeStruct((b, l, d), BF16),
        compiler_params=_params("parallel", "parallel"),
        name="normmod",
    )(x, g.reshape(1, d), sc, sh)


def _mm_kernel(a_ref, w_ref, o_ref, *, act):
    z = jnp.dot(a_ref[...], w_ref[...], preferred_element_type=F32)
    if act == "sigmoid":
        z = _sigmoid(z)
    o_ref[...] = z.astype(o_ref.dtype)


def _mm(a, w, out_dtype, act=None, tn_pref=512):
    t, k = a.shape
    n = w.shape[1]
    tm = _tile(t, 512)
    tn = _tile(n, tn_pref)
    return pl.pallas_call(
        functools.partial(_mm_kernel, act=act),
        grid=(t // tm, n // tn),
        in_specs=[
            pl.BlockSpec((tm, k), lambda i, j: (i, 0)),
            pl.BlockSpec((k, tn), lambda i, j: (0, j)),
        ],
        out_specs=pl.BlockSpec((tm, tn), lambda i, j: (i, j)),
        out_shape=jax.ShapeDtypeStruct((t, n), out_dtype),
        compiler_params=_params("parallel", "parallel"),
        name="mm",
    )(a, w)


def _mm_split_kernel(a_ref, w_ref, o_ref, *, parts):
    z = jnp.dot(a_ref[...], w_ref[...], preferred_element_type=F32)
    for p in range(parts):
        o_ref[p] = z[:, p * LANES:(p + 1) * LANES].astype(o_ref.dtype)


def _mm_split(a, w, out_dtype):
    t, k = a.shape
    n = w.shape[1]
    tm = _tile(t, 512)
    tn = _tile(n, MXU_DIM)
    parts = tn // LANES
    return pl.pallas_call(
        functools.partial(_mm_split_kernel, parts=parts),
        grid=(t // tm, n // tn),
        in_specs=[
            pl.BlockSpec((tm, k), lambda i, j: (i, 0)),
            pl.BlockSpec((k, tn), lambda i, j: (0, j)),
        ],
        out_specs=pl.BlockSpec((parts, tm, LANES), lambda i, j: (j, i, 0)),
        out_shape=jax.ShapeDtypeStruct((n // LANES, t, LANES), out_dtype),
        compiler_params=_params("parallel", "parallel"),
        name="mm_split",
    )(a, w)


def _mla_proj_kernel(h_ref, wm_ref, gq_ref, gkv_ref, wq_ref, cos_ref, sin_ref,
                     qn_ref, qr_ref, ckv_ref, ckvb_ref, kr_ref, krp_ref,
                     *, heads, q_lora, kv_lora, rope, scale):
    z = jnp.dot(h_ref[...], wm_ref[...], preferred_element_type=F32)
    cos = cos_ref[...]
    sin = sin_ref[...]
    off = q_lora + kv_lora
    krp = z[:, off:off + LANES] * cos + z[:, off + LANES:off + 2 * LANES] * sin
    kr_ref[...] = krp[:, :rope]
    krp_ref[...] = krp.astype(BF16)
    ckv = _rms(z[:, q_lora:off], gkv_ref[...])
    ckv_ref[...] = ckv
    ckvb_ref[...] = ckv.astype(BF16)
    qd = _rms(z[:, :q_lora], gq_ref[...]).astype(BF16)
    zq = jnp.dot(qd, wq_ref[...], preferred_element_type=F32)
    hn = heads * LANES
    for hh in range(heads):
        lo = hh * LANES
        qn_ref[hh] = (zq[:, lo:lo + LANES] * scale).astype(BF16)
        qr = zq[:, hn + lo:hn + lo + LANES] * cos + zq[:, 2 * hn + lo:2 * hn + lo + LANES] * sin
        qr_ref[hh] = (qr * scale).astype(BF16)


def _mla_proj(h, wm, gq, gkv, wq, cos, sin, seq, *, heads, q_lora, kv_lora, rope, scale):
    t, d = h.shape
    tm = _tile(t, 256)
    if seq % tm == 0:
        nrep = seq // tm
        tab_map = lambda i: (i % nrep, 0)
    else:
        assert tm % seq == 0
        cos = jnp.tile(cos, (tm // seq, 1))
        sin = jnp.tile(sin, (tm // seq, 1))
        tab_map = lambda i: (0, 0)
    nm = wm.shape[1]
    nq = wq.shape[1]
    row = lambda i: (i, 0)
    const = lambda i: (0, 0)
    return pl.pallas_call(
        functools.partial(_mla_proj_kernel, heads=heads, q_lora=q_lora, kv_lora=kv_lora, rope=rope, scale=scale),
        grid=(t // tm,),
        in_specs=[
            pl.BlockSpec((tm, d), row),
            pl.BlockSpec((d, nm), const),
            pl.BlockSpec((1, q_lora), const),
            pl.BlockSpec((1, kv_lora), const),
            pl.BlockSpec((q_lora, nq), const),
            pl.BlockSpec((tm, LANES), tab_map),
            pl.BlockSpec((tm, LANES), tab_map),
        ],
        out_specs=[
            pl.BlockSpec((heads, tm, LANES), lambda i: (0, i, 0)),
            pl.BlockSpec((heads, tm, LANES), lambda i: (0, i, 0)),
            pl.BlockSpec((tm, kv_lora), row),
            pl.BlockSpec((tm, kv_lora), row),
            pl.BlockSpec((tm, rope), row),
            pl.BlockSpec((tm, LANES), row),
        ],
        out_shape=[
            jax.ShapeDtypeStruct((heads, t, LANES), BF16),
            jax.ShapeDtypeStruct((heads, t, LANES), BF16),
            jax.ShapeDtypeStruct((t, kv_lora), F32),
            jax.ShapeDtypeStruct((t, kv_lora), BF16),
            jax.ShapeDtypeStruct((t, rope), F32),
            jax.ShapeDtypeStruct((t, LANES), BF16),
        ],
        compiler_params=_params("parallel"),
        name="mla_proj",
    )(h, wm, gq.reshape(1, q_lora), gkv.reshape(1, kv_lora), wq, cos, sin)


def _kvup_kernel(c_ref, w_ref, k_ref, v_ref, *, heads):
    z = jnp.dot(c_ref[...], w_ref[...], preferred_element_type=F32)
    for hh in range(heads):
        k_ref[hh] = z[:, hh * LANES:(hh + 1) * LANES].astype(BF16)
        v_ref[hh] = z[:, (heads + hh) * LANES:(heads + hh + 1) * LANES].astype(BF16)


def _kvup(ckv, w, heads):
    t, c = ckv.shape
    tm = _tile(t, 512)
    if t % tm:
        tm = t
    return pl.pallas_call(
        functools.partial(_kvup_kernel, heads=heads),
        grid=(t // tm,),
        in_specs=[
            pl.BlockSpec((tm, c), lambda i: (i, 0)),
            pl.BlockSpec(w.shape, lambda i: (0, 0)),
        ],
        out_specs=[
            pl.BlockSpec((heads, tm, LANES), lambda i: (0, i, 0)),
            pl.BlockSpec((heads, tm, LANES), lambda i: (0, i, 0)),
        ],
        out_shape=[jax.ShapeDtypeStruct((heads, t, LANES), BF16)] * 2,
        compiler_params=_params("parallel"),
        name="kvup",
    )(ckv, w)


def _flash_kernel(iq_tab, ik_tab, flag_tab, qn_ref, qr_ref, k_ref, kr_ref, v_ref, o_ref,
                  qc_scr, m_scr, acc_scr, *, hb, tq, tk, q_pos0, n_keys):
    pair = pl.program_id(2)
    iq = iq_tab[pair]
    ik = ik_tab[pair]
    flags = flag_tab[pair]
    first = (flags & 1) != 0
    last = (flags & 2) != 0
    full = (flags & 4) != 0

    @pl.when(first)
    def _init():
        m_scr[...] = jnp.full(m_scr.shape, MASKED, F32)
        acc_scr[...] = jnp.zeros(acc_scr.shape, F32)
        for hh in range(hb):
            qc_scr[hh] = jnp.concatenate([qn_ref[hh], qr_ref[hh]], axis=1)

    def step(masked):
        kr = kr_ref[...]
        ones = jnp.ones((tk, LANES), BF16)
        if masked:
            qp = q_pos0 + iq * tq + lax.broadcasted_iota(jnp.int32, (tq, tk), 0)
            kp = ik * tk + lax.broadcasted_iota(jnp.int32, (tq, tk), 1)
            allowed = jnp.logical_and(kp // CHUNK <= qp // CHUNK, kp < n_keys)

        def head(hh, carry):
            kc = jnp.concatenate([k_ref[hh], kr], axis=1)
            s = lax.dot_general(qc_scr[hh], kc, (((1,), (1,)), ((), ())), preferred_element_type=F32)
            if masked:
                s = jnp.where(allowed, s, MASKED)
            m_prev = m_scr[hh]
            m_new = jnp.maximum(m_prev, jnp.max(s, axis=1, keepdims=True))
            alpha = jnp.exp2(m_prev - m_new)
            p = jnp.exp2(s - pltpu.repeat(m_new, tk // LANES, axis=1)).astype(BF16)
            vc = jnp.concatenate([v_ref[hh], ones], axis=1)
            acc_scr[hh] = pltpu.repeat(alpha, 2, axis=1) * acc_scr[hh] + jnp.dot(p, vc, preferred_element_type=F32)
            m_scr[hh] = m_new
            return carry

        lax.fori_loop(0, hb, head, 0, unroll=min(hb, 8))

    @pl.when(full)
    def _full():
        step(False)

    @pl.when(jnp.logical_not(full))
    def _diag():
        step(True)

    @pl.when(last)
    def _done():
        for hh in range(hb):
            acc = acc_scr[hh]
            o_ref[hh] = (acc[:, :LANES] / acc[:, LANES:]).astype(BF16)


def _flash_pairs(lq, lk, tq, tk, q_pos0, n_keys):
    iqs, iks, flags = [], [], []
    for iq in range(lq // tq):
        q_first = q_pos0 + iq * tq
        last_key = min(n_keys - 1, ((q_first + tq - 1) // CHUNK) * CHUNK + CHUNK - 1)
        full_key = min(n_keys - 1, (q_first // CHUNK) * CHUNK + CHUNK - 1)
        n_blocks = last_key // tk + 1
        for ik in range(n_blocks):
            full = (ik + 1) * tk - 1 <= full_key
            iqs.append(iq)
            iks.append(ik)
            flags.append((1 if ik == 0 else 0) | (2 if ik == n_blocks - 1 else 0) | (4 if full else 0))
    as_i32 = lambda xs: jnp.asarray(xs, jnp.int32)
    return as_i32(iqs), as_i32(iks), as_i32(flags)


def _flash(qn, qr, k, krp, v, *, batch, lq, lk, q_pos0, n_keys, tq, tk, hb):
    heads = qn.shape[0]
    nq = lq // tq
    nk = lk // tk
    assert lq % tq == 0 and lk % tk == 0 and heads % hb == 0 and tk % LANES == 0
    iq_tab, ik_tab, flag_tab = _flash_pairs(lq, lk, tq, tk, q_pos0, n_keys)
    qmap = lambda b, h, p, iqt, ikt, ft: (h, b * nq + iqt[p], 0)
    kmap = lambda b, h, p, iqt, ikt, ft: (h, b * nk + ikt[p], 0)
    return pl.pallas_call(
        functools.partial(_flash_kernel, hb=hb, tq=tq, tk=tk, q_pos0=q_pos0, n_keys=n_keys),
        grid_spec=pltpu.PrefetchScalarGridSpec(
            num_scalar_prefetch=3,
            grid=(batch, heads // hb, iq_tab.shape[0]),
            in_specs=[
                pl.BlockSpec((hb, tq, LANES), qmap),
                pl.BlockSpec((hb, tq, LANES), qmap),
                pl.BlockSpec((hb, tk, LANES), kmap),
                pl.BlockSpec((tk, LANES), lambda b, h, p, iqt, ikt, ft: (b * nk + ikt[p], 0)),
                pl.BlockSpec((hb, tk, LANES), kmap),
            ],
            out_specs=pl.BlockSpec((hb, tq, LANES), qmap),
            scratch_shapes=[
                pltpu.VMEM((hb, tq, 2 * LANES), BF16),
                pltpu.VMEM((hb, tq, LANES), F32),
                pltpu.VMEM((hb, tq, 2 * LANES), F32),
            ],
        ),
        out_shape=jax.ShapeDtypeStruct((heads, batch * lq, LANES), BF16),
        compiler_params=_params("parallel", "parallel", "arbitrary"),
        name="flash",
    )(iq_tab, ik_tab, flag_tab, qn, qr, k, krp, v)


def _s5_kernel(u_ref, wb_ref, scn_ref, wc_ref, d_ref, s0_ref, y_ref, st_ref, xs_ref, car_ref,
               *, ts, nkt, kw, sw, nt):
    it = pl.program_id(1)

    @pl.when(it == 0)
    def _load_state():
        car_ref[...] = s0_ref[0]

    for kt in range(nkt):
        ukt = u_ref[0, :, kt * kw:(kt + 1) * kw]
        xs_ref[...] = jnp.dot(ukt, wb_ref[kt], preferred_element_type=F32)
        st_sl = slice(kt * sw, (kt + 1) * sw)

        def blk(k, carry, kt=kt):
            cr, ci = carry
            rows = pl.ds(pl.multiple_of(k * SUBLANES, SUBLANES), SUBLANES)
            hr = xs_ref[rows, :sw]
            hi = xs_ref[rows, sw:]
            for c0, dist in ((0, 1), (2, 2), (4, 4)):
                ar = scn_ref[kt, c0]
                ai = scn_ref[kt, c0 + 1]
                sr = pltpu.roll(hr, dist, 0)
                si = pltpu.roll(hi, dist, 0)
                hr, hi = hr + ar * sr - ai * si, hi + ar * si + ai * sr
            pr = scn_ref[kt, 6]
            pi = scn_ref[kt, 7]
            crb = jnp.broadcast_to(cr, (SUBLANES, sw))
            cib = jnp.broadcast_to(ci, (SUBLANES, sw))
            hr, hi = hr + pr * crb - pi * cib, hi + pr * cib + pi * crb
            xs_ref[rows, :sw] = hr
            xs_ref[rows, sw:] = hi
            return hr[SUBLANES - 1:SUBLANES], hi[SUBLANES - 1:SUBLANES]

        cr, ci = lax.fori_loop(0, ts // SUBLANES, blk, (car_ref[0:1, st_sl], car_ref[1:2, st_sl]),
                               unroll=2 if ts >= 2 * SUBLANES else 1)
        car_ref[0:1, st_sl] = cr
        car_ref[1:2, st_sl] = ci
        y = jnp.dot(xs_ref[...].astype(BF16), wc_ref[kt], preferred_element_type=F32)
        y = y + d_ref[:, kt * kw:(kt + 1) * kw] * ukt.astype(F32)
        y_ref[0, :, kt * kw:(kt + 1) * kw] = _gelu(y).astype(BF16)

    @pl.when(it == nt - 1)
    def _store_state():
        st_ref[0] = car_ref[...]


def _s5(u, wb, scn, wc, dsk, s0):
    b, l, w = u.shape
    nkt, kw, sw2 = wb.shape
    sw = sw2 // 2
    ns = s0.shape[2]
    ts = _tile(l, 512)
    nt = l // ts
    const3 = lambda i, j: (0, 0, 0)
    return pl.pallas_call(
        functools.partial(_s5_kernel, ts=ts, nkt=nkt, kw=kw, sw=sw, nt=nt),
        grid=(b, nt),
        in_specs=[
            pl.BlockSpec((1, ts, w), lambda i, j: (i, j, 0)),
            pl.BlockSpec(wb.shape, const3),
            pl.BlockSpec(scn.shape, lambda i, j: (0, 0, 0, 0)),
            pl.BlockSpec(wc.shape, const3),
            pl.BlockSpec((1, w), lambda i, j: (0, 0)),
            pl.BlockSpec((1, 2, ns), lambda i, j: (i, 0, 0)),
        ],
        out_specs=[
            pl.BlockSpec((1, ts, w), lambda i, j: (i, j, 0)),
            pl.BlockSpec((1, 2, ns), lambda i, j: (i, 0, 0)),
        ],
        out_shape=[
            jax.ShapeDtypeStruct((b, l, w), BF16),
            jax.ShapeDtypeStruct((b, 2, ns), F32),
        ],
        scratch_shapes=[
            pltpu.VMEM((ts, sw2), F32),
            pltpu.VMEM((2, ns), F32),
        ],
        compiler_params=_params("parallel", "arbitrary"),
        name="s5",
    )(u, wb, scn, wc, dsk, s0)


def _s5_tables(a_re, a_im, log_dt, b_re, b_im, c_re, c_im, d_skip):
    g, p = a_re.shape
    ch = b_re.shape[2]
    gpt = MXU_DIM // ch
    nkt = g // gpt
    dt = jnp.exp(log_dt.astype(F32))[:, None]
    mag = jnp.exp(dt * a_re)
    ab_re = mag * jnp.cos(dt * a_im)
    ab_im = mag * jnp.sin(dt * a_im)
    den = a_re * a_re + a_im * a_im
    nr = ab_re - 1.0
    f_re = (nr * a_re + ab_im * a_im) / den
    f_im = (ab_im * a_re - nr * a_im) / den
    bb_re = f_re[..., None] * b_re - f_im[..., None] * b_im
    bb_im = f_re[..., None] * b_im + f_im[..., None] * b_re
    eye = jnp.eye(gpt, dtype=F32)

    def in_blockdiag(bb):
        return jnp.einsum("kgpi,gh->kgihp", bb.reshape(nkt, gpt, p, ch), eye).reshape(nkt, gpt * ch, gpt * p)

    def out_blockdiag(cc):
        return jnp.einsum("kgjp,gh->kgphj", cc.reshape(nkt, gpt, ch, p), eye).reshape(nkt, gpt * p, gpt * ch)

    wb = jnp.concatenate([in_blockdiag(bb_re), in_blockdiag(bb_im)], axis=2).astype(BF16)
    wc = jnp.concatenate([out_blockdiag(c_re), out_blockdiag(-c_im)], axis=1).astype(BF16)

    def cmul(x, y):
        return x[0] * y[0] - x[1] * y[1], x[0] * y[1] + x[1] * y[0]

    lam1 = (ab_re.reshape(nkt, gpt * p), ab_im.reshape(nkt, gpt * p))
    lam2 = cmul(lam1, lam1)
    lam4 = cmul(lam2, lam2)
    rows = jnp.arange(SUBLANES)[None, :, None]

    def shifted(lam, dist):
        return [jnp.where(rows >= dist, c[:, None, :], 0.0) for c in lam]

    pw = [lam1]
    for _ in range(SUBLANES - 1):
        pw.append(cmul(pw[-1], lam1))
    p_re = jnp.stack([c[0] for c in pw], axis=1)
    p_im = jnp.stack([c[1] for c in pw], axis=1)
    scn = jnp.stack(shifted(lam1, 1) + shifted(lam2, 2) + shifted(lam4, 4) + [p_re, p_im], axis=1)
    return wb, scn.astype(F32), wc, d_skip.reshape(1, g * ch).astype(F32)


def _merge_kernel(ys_ref, o_ref, wga_ref, wgb_ref, wo_ref, ga_ref, gb_ref, out_ref, *, heads):
    ys = ys_ref[...]
    ya = jnp.dot(ys, wga_ref[...], preferred_element_type=F32)
    ya = ya * _sigmoid(jnp.dot(ys, wgb_ref[...], preferred_element_type=F32))
    oc = jnp.concatenate([o_ref[hh] for hh in range(heads)], axis=1)
    yb = jnp.dot(oc, wo_ref[...], preferred_element_type=F32)
    out_ref[...] = (ga_ref[...].astype(F32) * ya + gb_ref[...].astype(F32) * yb).astype(BF16)


def _merge(ys, o, w_glu, w_o, gates):
    t, sw = ys.shape
    heads = o.shape[0]
    d = w_o.shape[1]
    tm = _tile(t, 512)
    tn = _tile(d, 512)
    nj = d // tn
    return pl.pallas_call(
        functools.partial(_merge_kernel, heads=heads),
        grid=(t // tm, nj),
        in_specs=[
            pl.BlockSpec((tm, sw), lambda i, j: (i, 0)),
            pl.BlockSpec((heads, tm, LANES), lambda i, j: (0, i, 0)),
            pl.BlockSpec((sw, tn), lambda i, j: (0, j)),
            pl.BlockSpec((sw, tn), lambda i, j: (0, nj + j)),
            pl.BlockSpec((heads * LANES, tn), lambda i, j: (0, j)),
            pl.BlockSpec((tm, tn), lambda i, j: (i, j)),
            pl.BlockSpec((tm, tn), lambda i, j: (i, nj + j)),
        ],
        out_specs=pl.BlockSpec((tm, tn), lambda i, j: (i, j)),
        out_shape=jax.ShapeDtypeStruct((t, d), BF16),
        compiler_params=_params("parallel", "parallel"),
        name="merge",
    )(ys, o, w_glu, w_glu, w_o, gates, gates)


def _resid_kernel(m_ref, w_ref, x_ref, gt_ref, g_ref, sc_ref, sh_ref, x1_ref, h2_ref):
    x1 = x_ref[0] + gt_ref[0] * jnp.dot(m_ref[0], w_ref[...], preferred_element_type=F32)
    x1_ref[0] = x1
    h2_ref[0] = (_rms(x1, g_ref[...]) * (1.0 + sc_ref[0]) + sh_ref[0]).astype(BF16)


def _resid(merged, w_out, x, gt, g2, sc, sh):
    b, l, d = x.shape
    tm = _tile(l, 256)
    tok = lambda i, j: (i, j, 0)
    per_b = lambda i, j: (i, 0, 0)
    return pl.pallas_call(
        _resid_kernel,
        grid=(b, l // tm),
        in_specs=[
            pl.BlockSpec((1, tm, d), tok),
            pl.BlockSpec((d, d), lambda i, j: (0, 0)),
            pl.BlockSpec((1, tm, d), tok),
            pl.BlockSpec((1, 1, d), per_b),
            pl.BlockSpec((1, d), lambda i, j: (0, 0)),
            pl.BlockSpec((1, 1, d), per_b),
            pl.BlockSpec((1, 1, d), per_b),
        ],
        out_specs=[pl.BlockSpec((1, tm, d), tok), pl.BlockSpec((1, tm, d), tok)],
        out_shape=[jax.ShapeDtypeStruct((b, l, d), F32), jax.ShapeDtypeStruct((b, l, d), BF16)],
        compiler_params=_params("parallel", "parallel"),
        name="resid",
    )(merged.reshape(b, l, d), w_out, x, gt, g2.reshape(1, d), sc, sh)


def _peer_topk_kernel(q_ref, k1_ref, k2_ref, flat_ref, invalid_ref, r2_ref, c1_ref, a1_ref, a2_ref,
                      *, nh, nkeys, topk):
    tb = q_ref.shape[1]
    iota_k = lax.broadcasted_iota(jnp.int32, (nkeys, tb), 0).astype(F32)
    iota_t = lax.broadcasted_iota(jnp.int32, (topk, tb), 0).astype(F32)
    flat = flat_ref[...]
    invalid = invalid_ref[...]
    nt_dims = (((1,), (1,)), ((), ()))

    n_cand = flat.shape[0]

    def extract(s, exact_ties):
        work = s
        rank = jnp.full((nkeys, tb), float(topk), F32)
        vals = []
        for a in range(topk):
            m = jnp.max(work, axis=0, keepdims=True)
            if exact_ties:
                idx = jnp.min(jnp.where(work == m, iota_k, float(nkeys)), axis=0, keepdims=True)
                sel = iota_k == idx
            else:
                sel = work == m
            rank = jnp.where(sel, float(a), rank)
            work = jnp.where(sel, REMOVED, work)
            vals.append(m)
        ranked = jnp.sum(jnp.where(rank < float(topk), 1.0, 0.0), axis=0, keepdims=True)
        return vals, rank, ranked

    def stack(vals):
        out = jnp.zeros((topk, tb), F32)
        for a in range(topk):
            out = jnp.where(iota_t == float(a), vals[a], out)
        return out

    def candidates(v1, v2):
        vs1 = stack(v1)
        vs2 = stack(v2)
        blocks = [v1[0] + vs2]
        for a in range(1, SUBLANES):
            blocks.append(v1[a] + vs2[:SUBLANES])
        blocks.append(vs1[SUBLANES:] + v2[0])
        return jnp.concatenate(blocks, axis=0) + invalid

    def route(s1, s2, exact_ties):
        v1, rank1, n1 = extract(s1, exact_ties)
        v2, rank2, n2 = extract(s2, exact_ties)
        cand = candidates(v1, v2)
        counts = jnp.zeros((topk, tb), F32)
        top = None
        zsum = None
        for kk in range(topk):
            m = jnp.max(cand, axis=0, keepdims=True)
            if exact_ties:
                f = jnp.min(jnp.where(cand == m, flat, 1e9), axis=0, keepdims=True)
                cand = jnp.where(flat == f, REMOVED, cand)
                counts = counts + jnp.where(iota_t == jnp.floor(f * (1.0 / topk)), 1.0, 0.0)
            else:
                cand = jnp.where(cand == m, REMOVED, cand)
            if kk == 0:
                top = m
                zsum = jnp.ones_like(m)
            else:
                zsum = zsum + jnp.exp(m - top)
        n3 = None
        if not exact_ties:
            gone = jnp.where(cand == REMOVED, 1.0, 0.0)
            n3 = jnp.sum(gone, axis=0, keepdims=True)
            per_rank = [jnp.sum(gone[:topk], axis=0, keepdims=True)]
            for a in range(1, SUBLANES):
                lo = topk + (a - 1) * SUBLANES
                per_rank.append(jnp.sum(gone[lo:lo + SUBLANES], axis=0, keepdims=True))
            counts = jnp.concatenate([stack(per_rank + [per_rank[0]] * (topk - SUBLANES))[:SUBLANES],
                                      gone[n_cand - SUBLANES:]], axis=0)
        c1 = jnp.zeros((nkeys, tb), F32)
        for a in range(topk):
            c1 = jnp.where(rank1 == float(a), counts[a:a + 1], c1)
        a1 = jnp.exp(s1 - v1[0]) * (1.0 / zsum)
        a2 = jnp.exp(s2 - v2[0])
        clean = None
        if not exact_ties:
            want = float(topk)
            bad = jnp.where(n1 != want, 1.0, 0.0) + jnp.where(n2 != want, 1.0, 0.0) + jnp.where(n3 != want, 1.0, 0.0)
            clean = jnp.max(bad) == 0.0
        return (rank2, c1, a1, a2), clean

    def store(hh, tables):
        r2_ref[hh], c1_ref[hh], a1_ref[hh], a2_ref[hh] = tables

    def pair(pp, carry):
        heads = (2 * pp, 2 * pp + 1)
        scores = []
        for hh in heads:
            scores.append((lax.dot_general(k1_ref[hh], q_ref[2 * hh], nt_dims, preferred_element_type=F32),
                           lax.dot_general(k2_ref[hh], q_ref[2 * hh + 1], nt_dims, preferred_element_type=F32)))
        quick = [route(s1, s2, False) for s1, s2 in scores]
        clean = jnp.logical_and(quick[0][1], quick[1][1])

        @pl.when(clean)
        def _no_ties():
            for hh, (tables, _) in zip(heads, quick):
                store(hh, tables)

        @pl.when(jnp.logical_not(clean))
        def _ties():
            for hh, (s1, s2) in zip(heads, scores):
                store(hh, route(s1, s2, True)[0])

        return carry

    lax.fori_loop(0, nh // 2, pair, 0)


def _peer_topk(q, k1, k2, topk):
    nh, nkeys, half = k1.shape
    t = q.shape[1]
    tb = LANES
    assert topk == 2 * SUBLANES and half == LANES and t % tb == 0
    rows = jnp.arange(topk + (SUBLANES - 1) * SUBLANES + SUBLANES)
    a_idx = jnp.where(rows < topk, 0, jnp.where(rows < topk + (SUBLANES - 1) * SUBLANES,
                                                1 + (rows - topk) // SUBLANES, SUBLANES + (rows - topk - (SUBLANES - 1) * SUBLANES)))
    b_idx = jnp.where(rows < topk, rows, jnp.where(rows < topk + (SUBLANES - 1) * SUBLANES, (rows - topk) % SUBLANES, 0))
    flat = jnp.broadcast_to((a_idx * topk + b_idx).astype(F32)[:, None], (rows.shape[0], tb))
    invalid = jnp.broadcast_to(jnp.where((a_idx + 1) * (b_idx + 1) <= topk, 0.0, EXCLUDED).astype(F32)[:, None],
                               (rows.shape[0], tb))
    out = lambda dt: jax.ShapeDtypeStruct((nh, nkeys, t), dt)
    ospec = pl.BlockSpec((nh, nkeys, tb), lambda i: (0, 0, i))
    return pl.pallas_call(
        functools.partial(_peer_topk_kernel, nh=nh, nkeys=nkeys, topk=topk),
        grid=(t // tb,),
        in_specs=[
            pl.BlockSpec((2 * nh, tb, LANES), lambda i: (0, i, 0)),
            pl.BlockSpec(k1.shape, lambda i: (0, 0, 0)),
            pl.BlockSpec(k2.shape, lambda i: (0, 0, 0)),
            pl.BlockSpec(flat.shape, lambda i: (0, 0)),
            pl.BlockSpec(invalid.shape, lambda i: (0, 0)),
        ],
        out_specs=[ospec] * 4,
        out_shape=[out(F32)] * 4,
        compiler_params=_params("parallel"),
        name="peer_topk",
    )(q, k1, k2, flat, invalid)


def _peer_mix_kernel(h_ref, u_ref, vt_ref, r2_ref, c1_ref, a1_ref, a2_ref, o_ref, ht_scr, w_scr, acc_scr,
                     *, nh, nkeys, ni, ne):
    e = pl.program_id(1)
    tb = h_ref.shape[0]

    @pl.when(e == 0)
    def _init():
        acc_scr[...] = jnp.zeros(acc_scr.shape, F32)

    first_keys = pl.ds(pl.multiple_of(e * ni, SUBLANES), ni)
    h = h_ref[...]
    n_chunks = 4
    e_rows = ni * nkeys // n_chunks
    for c in range(n_chunks):
        crows = slice(c * e_rows, (c + 1) * e_rows)
        ht_scr[crows, :] = lax.dot_general(u_ref[crows, :], h, (((1,), (1,)), ((), ())),
                                           preferred_element_type=F32)
        for il in range(c * e_rows // nkeys, (c + 1) * e_rows // nkeys):
            rows = slice(il * nkeys, (il + 1) * nkeys)
            for lg in range(tb // LANES):
                sl = slice(lg * LANES, (lg + 1) * LANES)
                gate = jnp.zeros((nkeys, LANES), F32)
                for hh in range(nh):
                    partners = c1_ref[hh, first_keys, sl][il:il + 1]
                    first = a1_ref[hh, first_keys, sl][il:il + 1]
                    gate = gate + jnp.where(r2_ref[hh, :, sl] < partners, a2_ref[hh, :, sl], 0.0) * first
                w_scr[rows, sl] = (gate * _gelu(ht_scr[rows, sl])).astype(BF16)
    acc_scr[...] += jnp.dot(vt_ref[...], w_scr[...], preferred_element_type=F32)

    @pl.when(e == ne - 1)
    def _done():
        o_ref[...] = acc_scr[...].T


def _peer_mix(h2, u_tab, vt_tab, r2, c1, a1, a2):
    t, d = h2.shape
    nh, nkeys, _ = r2.shape
    n_exp = u_tab.shape[0]
    tb = _tile(t, 512)
    ni = SUBLANES
    eb = ni * nkeys
    ne = n_exp // eb
    aux = pl.BlockSpec((nh, nkeys, tb), lambda i, e: (0, 0, i), pipeline_mode=pl.Buffered(1))
    return pl.pallas_call(
        functools.partial(_peer_mix_kernel, nh=nh, nkeys=nkeys, ni=ni, ne=ne),
        grid=(t // tb, ne),
        in_specs=[
            pl.BlockSpec((tb, d), lambda i, e: (i, 0)),
            pl.BlockSpec((eb, d), lambda i, e: (e, 0)),
            pl.BlockSpec((d, eb), lambda i, e: (0, e)),
            aux, aux, aux, aux,
        ],
        out_specs=pl.BlockSpec((tb, d), lambda i, e: (i, 0)),
        out_shape=jax.ShapeDtypeStruct((t, d), F32),
        scratch_shapes=[
            pltpu.VMEM((eb, tb), F32),
            pltpu.VMEM((eb, tb), BF16),
            pltpu.VMEM((d, tb), F32),
        ],
        compiler_params=_params("parallel", "arbitrary"),
        name="peer_mix",
    )(h2, u_tab, vt_tab, r2, c1, a1, a2)


def _final_kernel(x_ref, p_ref, gt_ref, g_ref, y_ref, *, norm):
    y = x_ref[0] + gt_ref[0] * p_ref[0]
    y_ref[0] = _rms(y, g_ref[...]) if norm else y


def _final(x1, peer, gt, g, norm):
    b, l, d = x1.shape
    tm = _tile(l, 512)
    tok = lambda i, j: (i, j, 0)
    return pl.pallas_call(
        functools.partial(_final_kernel, norm=norm),
        grid=(b, l // tm),
        in_specs=[
            pl.BlockSpec((1, tm, d), tok),
            pl.BlockSpec((1, tm, d), tok),
            pl.BlockSpec((1, 1, d), lambda i, j: (i, 0, 0)),
            pl.BlockSpec((1, d), lambda i, j: (0, 0)),
        ],
        out_specs=pl.BlockSpec((1, tm, d), tok),
        out_shape=jax.ShapeDtypeStruct((b, l, d), F32),
        compiler_params=_params("parallel", "parallel"),
        name="final",
    )(x1, peer.reshape(b, l, d), gt, g.reshape(1, d))


def _rope_tables(pos, rope):
    half = rope // 2
    inv = jnp.power(ROPE_THETA, -jnp.arange(half, dtype=F32) / half)
    ang = pos.astype(F32)[:, None] * inv
    reps = LANES // half
    return jnp.tile(jnp.cos(ang), (1, reps)), jnp.tile(jnp.sin(ang), (1, reps))


def _prep_weights(p, dims):
    d, sw, q_lora, kv_lora, rope, heads, nope = (dims[k] for k in ("d", "sw", "q_lora", "kv_lora", "rope", "heads", "nope"))
    half = rope // 2
    w_in = p["w_in"]
    off_q = sw
    off_kv = off_q + q_lora
    off_kr = off_kv + kv_lora
    off_g = off_kr + rope
    kr_w = w_in[:, off_kr:off_g]
    kr_rot = jnp.concatenate([-kr_w[:, half:], kr_w[:, :half]], axis=1)
    pad = jnp.zeros((d, LANES - rope), F32)
    w_mla = jnp.concatenate([w_in[:, off_q:off_kr], kr_w, pad, kr_rot, pad], axis=1).astype(BF16)
    wq = p["w_qu"].reshape(q_lora, heads, nope + rope)
    wq_n = wq[:, :, :nope].reshape(q_lora, heads * nope)
    wq_r = wq[:, :, nope:]
    wq_rot = jnp.concatenate([-wq_r[:, :, half:], wq_r[:, :, :half]], axis=2)
    hpad = jnp.zeros((q_lora, heads, LANES - rope), F32)
    wq_a = jnp.concatenate([wq_r, hpad], axis=2).reshape(q_lora, heads * LANES)
    wq_b = jnp.concatenate([wq_rot, hpad], axis=2).reshape(q_lora, heads * LANES)
    return {
        "w_gates": w_in[:, off_g:].astype(BF16),
        "w_u": w_in[:, :sw].astype(BF16),
        "w_mla": w_mla,
        "w_q": jnp.concatenate([wq_n, wq_a, wq_b], axis=1).astype(BF16),
        "w_kv": jnp.concatenate([p["w_uk"].reshape(kv_lora, heads * nope),
                                 p["w_uv"].reshape(kv_lora, heads * dims["v_dim"])], axis=1).astype(BF16),
        "w_o": p["w_o"].astype(BF16),
        "w_glu": p["w_glu"].astype(BF16),
        "w_out": p["w_out"].astype(BF16),
        "peer_wq": p["peer_wq"].astype(BF16),
        "peer_k1": p["peer_k1"].astype(BF16),
        "peer_k2": p["peer_k2"].astype(BF16),
        "peer_u": p["peer_u"].astype(BF16),
        "peer_vt": p["peer_v"].T.astype(BF16),
        "s5": _s5_tables(p["ssm_a_re"], p["ssm_a_im"], p["ssm_log_dt"], p["ssm_b_re"], p["ssm_b_im"],
                         p["ssm_c_re"], p["ssm_c_im"], p["ssm_d"]),
    }


def _layer(x, mod, past_ckv, past_kr, s0, p, w, dims):
    b, l, d = x.shape
    t = b * l
    heads, rope, kv_lora, topk = dims["heads"], dims["rope"], dims["kv_lora"], dims["topk"]
    sh1, sc1, gt1, sh2, sc2, gt2 = mod
    past = 0 if past_ckv is None else past_ckv.shape[1]

    h = _normmod(x, p["g_norm1"], sc1, sh1).reshape(t, d)
    gates = _mm(h, w["w_gates"], BF16, act="sigmoid")
    u = _mm(h, w["w_u"], BF16)
    cos, sin = _rope_tables(past + jnp.arange(l, dtype=jnp.int32), rope)
    qn, qr, ckv, ckv_b, kr, krp = _mla_proj(
        h, w["w_mla"], p["g_q"], p["g_kv"], w["w_q"], cos, sin, l,
        heads=heads, q_lora=dims["q_lora"], kv_lora=kv_lora, rope=rope, scale=dims["scale"])

    ys, s_fin = _s5(u.reshape(b, l, -1), *w["s5"], s0)

    if past_ckv is None:
        lk, keys_c, keys_r = l, ckv_b, krp
        tq = tk = _tile(l, 512)
    else:
        n_keys = past + l
        lk = -(-n_keys // LANES) * LANES
        keys_c = jnp.concatenate([past_ckv.astype(BF16), ckv_b.reshape(b, l, kv_lora)], axis=1)
        keys_c = jnp.pad(keys_c, ((0, 0), (0, lk - n_keys), (0, 0))).reshape(b * lk, kv_lora)
        past_r = jnp.pad(past_kr.astype(BF16), ((0, 0), (0, 0), (0, LANES - rope)))
        keys_r = jnp.concatenate([past_r, krp.reshape(b, l, LANES)], axis=1)
        keys_r = jnp.pad(keys_r, ((0, 0), (0, lk - n_keys), (0, 0))).reshape(b * lk, LANES)
        tq, tk = l, lk
    kh, vh = _kvup(keys_c, w["w_kv"], heads)
    o = _flash(qn, qr, kh, keys_r, vh, batch=b, lq=l, lk=lk, q_pos0=past, n_keys=past + l,
               tq=tq, tk=tk, hb=heads)

    merged = _merge(ys.reshape(t, -1), o, w["w_glu"], w["w_o"], gates)
    x1, h2 = _resid(merged, w["w_out"], x, gt1, p["g_norm2"], sc2, sh2)

    h2 = h2.reshape(t, d)
    q = _mm_split(h2, w["peer_wq"], BF16)
    r2, c1, a1, a2 = _peer_topk(q, w["peer_k1"], w["peer_k2"], topk)
    peer = _peer_mix(h2, w["peer_u"], w["peer_vt"], r2, c1, a1, a2)
    return x1, peer, gt2, ckv.reshape(b, l, kv_lora), kr.reshape(b, l, rope), s_fin


def kernel(x_prompt, x_sample, c_prompt, c_sample, cache_ckv, cache_krope, state_ssm_re, state_ssm_im, w_ada, b_ada, g_norm1, g_norm2, w_in, g_q, w_qu, g_kv, w_uk, w_uv, w_o, ssm_a_re, ssm_a_im, ssm_log_dt, ssm_b_re, ssm_b_im, ssm_c_re, ssm_c_im, ssm_d, w_glu, w_out, peer_wq, peer_k1, peer_k2, peer_u, peer_v, g_final):
    depth = w_in.shape[0]
    bp, lp, d = x_prompt.shape
    bs, ls, _ = x_sample.shape
    groups, states = ssm_a_re.shape[1:]
    heads, nope = w_uk.shape[2:]
    rope = cache_krope.shape[-1]
    dims = {
        "d": d, "sw": groups * ssm_b_re.shape[3], "q_lora": g_q.shape[1], "kv_lora": g_kv.shape[1],
        "rope": rope, "heads": heads, "nope": nope, "v_dim": w_uv.shape[3],
        "scale": math.log2(math.e) / math.sqrt(nope + rope), "topk": 16,
    }
    assert nope == LANES and dims["v_dim"] == LANES and rope <= LANES

    xp, xs = x_prompt, x_sample
    nb = bp + bs
    rows = -(-nb // 16) * 16
    c_all = jnp.pad(jnp.concatenate([c_prompt, c_sample], axis=0), ((0, rows - nb), (0, 0)))
    zeros = jnp.zeros((bp, 2, groups * states), F32)
    outs_p, outs_s = [], []
    for layer in range(depth):
        p = {
            "g_norm1": g_norm1[layer], "g_norm2": g_norm2[layer], "w_in": w_in[layer], "g_q": g_q[layer],
            "w_qu": w_qu[layer], "g_kv": g_kv[layer], "w_uk": w_uk[layer], "w_uv": w_uv[layer], "w_o": w_o[layer],
            "ssm_a_re": ssm_a_re[layer], "ssm_a_im": ssm_a_im[layer], "ssm_log_dt": ssm_log_dt[layer],
            "ssm_b_re": ssm_b_re[layer], "ssm_b_im": ssm_b_im[layer], "ssm_c_re": ssm_c_re[layer],
            "ssm_c_im": ssm_c_im[layer], "ssm_d": ssm_d[layer], "w_glu": w_glu[layer], "w_out": w_out[layer],
            "peer_wq": peer_wq[layer], "peer_k1": peer_k1[layer], "peer_k2": peer_k2[layer],
            "peer_u": peer_u[layer], "peer_v": peer_v[layer],
        }
        w = _prep_weights(p, dims)
        mod = _ada(c_all, w_ada[layer], b_ada[layer])
        mod_p = [m.reshape(bp, 1, d) for m in jnp.split(mod[:bp], 6, axis=-1)]
        mod_s = [m.reshape(bs, 1, d) for m in jnp.split(mod[bp:nb], 6, axis=-1)]
        s0_s = jnp.stack([state_ssm_re[layer].reshape(bs, -1), state_ssm_im[layer].reshape(bs, -1)], axis=1)
        res_p = _layer(xp, mod_p, None, None, zeros, p, w, dims)
        res_s = _layer(xs, mod_s, cache_ckv[layer], cache_krope[layer], s0_s, p, w, dims)
        last = layer == depth - 1
        for res, outs, g in ((res_p, outs_p, bp), (res_s, outs_s, bs)):
            x1, peer, gt2, ckv, kr, s_fin = res
            outs.append((x1, peer, gt2, ckv, kr, s_fin[:, 0].reshape(g, groups, states), s_fin[:, 1].reshape(g, groups, states)))
        xp = _final(outs_p[-1][0], outs_p[-1][1], outs_p[-1][2], g_final, last)
        xs = _final(outs_s[-1][0], outs_s[-1][1], outs_s[-1][2], g_final, last)
    stack = lambda outs, k: jnp.stack([o[k] for o in outs])
    return (xp, xs,
            stack(outs_p, 3), stack(outs_p, 4), stack(outs_p, 5), stack(outs_p, 6),
            stack(outs_s, 3), stack(outs_s, 4), stack(outs_s, 5), stack(outs_s, 6))
```

```python
import functools
import math

import jax
import jax.numpy as jnp
from jax import lax
from jax.experimental import pallas as pl
from jax.experimental.pallas import tpu as pltpu

F32 = jnp.float32
BF16 = jnp.bfloat16

EPS = 1e-6
CHUNK = 64
ROPE_THETA = 10000.0
LANES = 128
SUBLANES = 8
MXU_DIM = 256
VMEM_LIMIT_BYTES = 56 * 1024 * 1024
MASKED = -1e30
REMOVED = -3e38
EXCLUDED = -1e38


def _params(*semantics):
    return pltpu.CompilerParams(dimension_semantics=semantics, vmem_limit_bytes=VMEM_LIMIT_BYTES)


def _tile(n, pref):
    if n <= pref:
        return n
    t = pref
    while n % t:
        t //= 2
    assert t >= SUBLANES, (n, pref)
    return t


def _sigmoid(x):
    return 1.0 / (1.0 + jnp.exp(-x))


def _gelu(x):
    return 0.5 * x * (1.0 + jnp.tanh(math.sqrt(2.0 / math.pi) * (x + 0.044715 * (x * x * x))))


def _rms(x, g):
    return x * lax.rsqrt(jnp.mean(x * x, axis=-1, keepdims=True) + EPS) * g


def _ada_kernel(c_ref, w_ref, b_ref, o_ref):
    c = c_ref[...]
    a = (c * _sigmoid(c)).astype(BF16)
    o_ref[...] = jnp.dot(a, w_ref[...].astype(BF16), preferred_element_type=F32) + b_ref[...]


def _ada(c, w, b):
    rows, d = c.shape
    n = w.shape[1]
    tn = _tile(n, 1024)
    return pl.pallas_call(
        _ada_kernel,
        grid=(n // tn,),
        in_specs=[
            pl.BlockSpec((rows, d), lambda j: (0, 0)),
            pl.BlockSpec((d, tn), lambda j: (0, j)),
            pl.BlockSpec((1, tn), lambda j: (0, j)),
        ],
        out_specs=pl.BlockSpec((rows, tn), lambda j: (0, j)),
        out_shape=jax.ShapeDtypeStruct((rows, n), F32),
        compiler_params=_params("parallel"),
        name="ada",
    )(c, w, b.reshape(1, n))


def _normmod_kernel(x_ref, g_ref, sc_ref, sh_ref, o_ref):
    h = _rms(x_ref[0], g_ref[...]) * (1.0 + sc_ref[0]) + sh_ref[0]
    o_ref[0] = h.astype(BF16)


def _normmod(x, g, sc, sh):
    b, l, d = x.shape
    tm = _tile(l, 512)
    return pl.pallas_call(
        _normmod_kernel,
        grid=(b, l // tm),
        in_specs=[
            pl.BlockSpec((1, tm, d), lambda i, j: (i, j, 0)),
            pl.BlockSpec((1, d), lambda i, j: (0, 0)),
            pl.BlockSpec((1, 1, d), lambda i, j: (i, 0, 0)),
            pl.BlockSpec((1, 1, d), lambda i, j: (i, 0, 0)),
        ],
        out_specs=pl.BlockSpec((1, tm, d), lambda i, j: (i, j, 0)),
        out_shape=jax.ShapeDtypeStruct((b, l, d), BF16),
        compiler_params=_params("parallel", "parallel"),
        name="normmod",
    )(x, g.reshape(1, d), sc, sh)


def _mm_kernel(a_ref, w_ref, o_ref, *, act):
    z = jnp.dot(a_ref[...], w_ref[...], preferred_element_type=F32)
    if act == "sigmoid":
        z = _sigmoid(z)
    o_ref[...] = z.astype(o_ref.dtype)


def _mm(a, w, out_dtype, act=None, tn_pref=512):
    t, k = a.shape
    n = w.shape[1]
    tm = _tile(t, 512)
    tn = _tile(n, tn_pref)
    return pl.pallas_call(
        functools.partial(_mm_kernel, act=act),
        grid=(t // tm, n // tn),
        in_specs=[
            pl.BlockSpec((tm, k), lambda i, j: (i, 0)),
            pl.BlockSpec((k, tn), lambda i, j: (0, j)),
        ],
        out_specs=pl.BlockSpec((tm, tn), lambda i, j: (i, j)),
        out_shape=jax.ShapeDtypeStruct((t, n), out_dtype),
        compiler_params=_params("parallel", "parallel"),
        name="mm",
    )(a, w)


def _mm_split_kernel(a_ref, w_ref, o_ref, *, parts):
    z = jnp.dot(a_ref[...], w_ref[...], preferred_element_type=F32)
    for p in range(parts):
        o_ref[p] = z[:, p * LANES:(p + 1) * LANES].astype(o_ref.dtype)


def _mm_split(a, w, out_dtype):
    t, k = a.shape
    n = w.shape[1]
    tm = _tile(t, 512)
    tn = _tile(n, MXU_DIM)
    parts = tn // LANES
    return pl.pallas_call(
        functools.partial(_mm_split_kernel, parts=parts),
        grid=(t // tm, n // tn),
        in_specs=[
            pl.BlockSpec((tm, k), lambda i, j: (i, 0)),
            pl.BlockSpec((k, tn), lambda i, j: (0, j)),
        ],
        out_specs=pl.BlockSpec((parts, tm, LANES), lambda i, j: (j, i, 0)),
        out_shape=jax.ShapeDtypeStruct((n // LANES, t, LANES), out_dtype),
        compiler_params=_params("parallel", "parallel"),
        name="mm_split",
    )(a, w)


def _mla_proj_kernel(h_ref, wm_ref, gq_ref, gkv_ref, wq_ref, cos_ref, sin_ref,
                     qn_ref, qr_ref, ckv_ref, ckvb_ref, kr_ref, krp_ref,
                     *, heads, q_lora, kv_lora, rope, scale):
    z = jnp.dot(h_ref[...], wm_ref[...], preferred_element_type=F32)
    cos = cos_ref[...]
    sin = sin_ref[...]
    off = q_lora + kv_lora
    krp = z[:, off:off + LANES] * cos + z[:, off + LANES:off + 2 * LANES] * sin
    kr_ref[...] = krp[:, :rope]
    krp_ref[...] = krp.astype(BF16)
    ckv = _rms(z[:, q_lora:off], gkv_ref[...])
    ckv_ref[...] = ckv
    ckvb_ref[...] = ckv.astype(BF16)
    qd = _rms(z[:, :q_lora], gq_ref[...]).astype(BF16)
    zq = jnp.dot(qd, wq_ref[...], preferred_element_type=F32)
    hn = heads * LANES
    for hh in range(heads):
        lo = hh * LANES
        qn_ref[hh] = (zq[:, lo:lo + LANES] * scale).astype(BF16)
        qr = zq[:, hn + lo:hn + lo + LANES] * cos + zq[:, 2 * hn + lo:2 * hn + lo + LANES] * sin
        qr_ref[hh] = (qr * scale).astype(BF16)


def _mla_proj(h, wm, gq, gkv, wq, cos, sin, seq, *, heads, q_lora, kv_lora, rope, scale):
    t, d = h.shape
    tm = _tile(t, 256)
    if seq % tm == 0:
        nrep = seq // tm
        tab_map = lambda i: (i % nrep, 0)
    else:
        assert tm % seq == 0
        cos = jnp.tile(cos, (tm // seq, 1))
        sin = jnp.tile(sin, (tm // seq, 1))
        tab_map = lambda i: (0, 0)
    nm = wm.shape[1]
    nq = wq.shape[1]
    row = lambda i: (i, 0)
    const = lambda i: (0, 0)
    return pl.pallas_call(
        functools.partial(_mla_proj_kernel, heads=heads, q_lora=q_lora, kv_lora=kv_lora, rope=rope, scale=scale),
        grid=(t // tm,),
        in_specs=[
            pl.BlockSpec((tm, d), row),
            pl.BlockSpec((d, nm), const),
            pl.BlockSpec((1, q_lora), const),
            pl.BlockSpec((1, kv_lora), const),
            pl.BlockSpec((q_lora, nq), const),
            pl.BlockSpec((tm, LANES), tab_map),
            pl.BlockSpec((tm, LANES), tab_map),
        ],
        out_specs=[
            pl.BlockSpec((heads, tm, LANES), lambda i: (0, i, 0)),
            pl.BlockSpec((heads, tm, LANES), lambda i: (0, i, 0)),
            pl.BlockSpec((tm, kv_lora), row),
            pl.BlockSpec((tm, kv_lora), row),
            pl.BlockSpec((tm, rope), row),
            pl.BlockSpec((tm, LANES), row),
        ],
        out_shape=[
            jax.ShapeDtypeStruct((heads, t, LANES), BF16),
            jax.ShapeDtypeStruct((heads, t, LANES), BF16),
            jax.ShapeDtypeStruct((t, kv_lora), F32),
            jax.ShapeDtypeStruct((t, kv_lora), BF16),
            jax.ShapeDtypeStruct((t, rope), F32),
            jax.ShapeDtypeStruct((t, LANES), BF16),
        ],
        compiler_params=_params("parallel"),
        name="mla_proj",
    )(h, wm, gq.reshape(1, q_lora), gkv.reshape(1, kv_lora), wq, cos, sin)


def _kvup_kernel(c_ref, w_ref, k_ref, v_ref, *, heads):
    z = jnp.dot(c_ref[...], w_ref[...], preferred_element_type=F32)
    for hh in range(heads):
        k_ref[hh] = z[:, hh * LANES:(hh + 1) * LANES].astype(BF16)
        v_ref[hh] = z[:, (heads + hh) * LANES:(heads + hh + 1) * LANES].astype(BF16)


def _kvup(ckv, w, heads):
    t, c = ckv.shape
    tm = _tile(t, 512)
    if t % tm:
        tm = t
    return pl.pallas_call(
        functools.partial(_kvup_kernel, heads=heads),
        grid=(t // tm,),
        in_specs=[
            pl.BlockSpec((tm, c), lambda i: (i, 0)),
            pl.BlockSpec(w.shape, lambda i: (0, 0)),
        ],
        out_specs=[
            pl.BlockSpec((heads, tm, LANES), lambda i: (0, i, 0)),
            pl.BlockSpec((heads, tm, LANES), lambda i: (0, i, 0)),
        ],
        out_shape=[jax.ShapeDtypeStruct((heads, t, LANES), BF16)] * 2,
        compiler_params=_params("parallel"),
        name="kvup",
    )(ckv, w)


def _flash_kernel(iq_tab, ik_tab, flag_tab, qn_ref, qr_ref, k_ref, kr_ref, v_ref, o_ref,
                  qc_scr, m_scr, acc_scr, *, hb, tq, tk, q_pos0, n_keys):
    pair = pl.program_id(2)
    iq = iq_tab[pair]
    ik = ik_tab[pair]
    flags = flag_tab[pair]
    first = (flags & 1) != 0
    last = (flags & 2) != 0
    full = (flags & 4) != 0

    @pl.when(first)
    def _init():
        m_scr[...] = jnp.full(m_scr.shape, MASKED, F32)
        acc_scr[...] = jnp.zeros(acc_scr.shape, F32)
        for hh in range(hb):
            qc_scr[hh] = jnp.concatenate([qn_ref[hh], qr_ref[hh]], axis=1)

    def step(masked):
        kr = kr_ref[...]
        ones = jnp.ones((tk, LANES), BF16)
        if masked:
            qp = q_pos0 + iq * tq + lax.broadcasted_iota(jnp.int32, (tq, tk), 0)
            kp = ik * tk + lax.broadcasted_iota(jnp.int32, (tq, tk), 1)
            allowed = jnp.logical_and(kp // CHUNK <= qp // CHUNK, kp < n_keys)

        def head(hh, carry):
            kc = jnp.concatenate([k_ref[hh], kr], axis=1)
            s = lax.dot_general(qc_scr[hh], kc, (((1,), (1,)), ((), ())), preferred_element_type=F32)
            if masked:
                s = jnp.where(allowed, s, MASKED)
            m_prev = m_scr[hh]
            m_new = jnp.maximum(m_prev, jnp.max(s, axis=1, keepdims=True))
            alpha = jnp.exp2(m_prev - m_new)
            p = jnp.exp2(s - pltpu.repeat(m_new, tk // LANES, axis=1)).astype(BF16)
            vc = jnp.concatenate([v_ref[hh], ones], axis=1)
            acc_scr[hh] = pltpu.repeat(alpha, 2, axis=1) * acc_scr[hh] + jnp.dot(p, vc, preferred_element_type=F32)
            m_scr[hh] = m_new
            return carry

        lax.fori_loop(0, hb, head, 0, unroll=min(hb, 8))

    @pl.when(full)
    def _full():
        step(False)

    @pl.when(jnp.logical_not(full))
    def _diag():
        step(True)

    @pl.when(last)
    def _done():
        for hh in range(hb):
            acc = acc_scr[hh]
            o_ref[hh] = (acc[:, :LANES] / acc[:, LANES:]).astype(BF16)


def _flash_pairs(lq, lk, tq, tk, q_pos0, n_keys):
    iqs, iks, flags = [], [], []
    for iq in range(lq // tq):
        q_first = q_pos0 + iq * tq
        last_key = min(n_keys - 1, ((q_first + tq - 1) // CHUNK) * CHUNK + CHUNK - 1)
        full_key = min(n_keys - 1, (q_first // CHUNK) * CHUNK + CHUNK - 1)
        n_blocks = last_key // tk + 1
        for ik in range(n_blocks):
            full = (ik + 1) * tk - 1 <= full_key
            iqs.append(iq)
            iks.append(ik)
            flags.append((1 if ik == 0 else 0) | (2 if ik == n_blocks - 1 else 0) | (4 if full else 0))
    as_i32 = lambda xs: jnp.asarray(xs, jnp.int32)
    return as_i32(iqs), as_i32(iks), as_i32(flags)


def _flash(qn, qr, k, krp, v, *, batch, lq, lk, q_pos0, n_keys, tq, tk, hb):
    heads = qn.shape[0]
    nq = lq // tq
    nk = lk // tk
    assert lq % tq == 0 and lk % tk == 0 and heads % hb == 0 and tk % LANES == 0
    iq_tab, ik_tab, flag_tab = _flash_pairs(lq, lk, tq, tk, q_pos0, n_keys)
    qmap = lambda b, h, p, iqt, ikt, ft: (h, b * nq + iqt[p], 0)
    kmap = lambda b, h, p, iqt, ikt, ft: (h, b * nk + ikt[p], 0)
    return pl.pallas_call(
        functools.partial(_flash_kernel, hb=hb, tq=tq, tk=tk, q_pos0=q_pos0, n_keys=n_keys),
        grid_spec=pltpu.PrefetchScalarGridSpec(
            num_scalar_prefetch=3,
            grid=(batch, heads // hb, iq_tab.shape[0]),
            in_specs=[
                pl.BlockSpec((hb, tq, LANES), qmap),
                pl.BlockSpec((hb, tq, LANES), qmap),
                pl.BlockSpec((hb, tk, LANES), kmap),
                pl.BlockSpec((tk, LANES), lambda b, h, p, iqt, ikt, ft: (b * nk + ikt[p], 0)),
                pl.BlockSpec((hb, tk, LANES), kmap),
            ],
            out_specs=pl.BlockSpec((hb, tq, LANES), qmap),
            scratch_shapes=[
                pltpu.VMEM((hb, tq, 2 * LANES), BF16),
                pltpu.VMEM((hb, tq, LANES), F32),
                pltpu.VMEM((hb, tq, 2 * LANES), F32),
            ],
        ),
        out_shape=jax.ShapeDtypeStruct((heads, batch * lq, LANES), BF16),
        compiler_params=_params("parallel", "parallel", "arbitrary"),
        name="flash",
    )(iq_tab, ik_tab, flag_tab, qn, qr, k, krp, v)


def _s5_kernel(u_ref, wb_ref, scn_ref, wc_ref, d_ref, s0_ref, y_ref, st_ref, xs_ref, car_ref,
               *, ts, nkt, kw, sw, nt):
    it = pl.program_id(1)

    @pl.when(it == 0)
    def _load_state():
        car_ref[...] = s0_ref[0]

    for kt in range(nkt):
        ukt = u_ref[0, :, kt * kw:(kt + 1) * kw]
        xs_ref[...] = jnp.dot(ukt, wb_ref[kt], preferred_element_type=F32)
        st_sl = slice(kt * sw, (kt + 1) * sw)

        def blk(k, carry, kt=kt):
            cr, ci = carry
            rows = pl.ds(pl.multiple_of(k * SUBLANES, SUBLANES), SUBLANES)
            hr = xs_ref[rows, :sw]
            hi = xs_ref[rows, sw:]
            for c0, dist in ((0, 1), (2, 2), (4, 4)):
                ar = scn_ref[kt, c0]
                ai = scn_ref[kt, c0 + 1]
                sr = pltpu.roll(hr, dist, 0)
                si = pltpu.roll(hi, dist, 0)
                hr, hi = hr + ar * sr - ai * si, hi + ar * si + ai * sr
            pr = scn_ref[kt, 6]
            pi = scn_ref[kt, 7]
            crb = jnp.broadcast_to(cr, (SUBLANES, sw))
            cib = jnp.broadcast_to(ci, (SUBLANES, sw))
            hr, hi = hr + pr * crb - pi * cib, hi + pr * cib + pi * crb
            xs_ref[rows, :sw] = hr
            xs_ref[rows, sw:] = hi
            return hr[SUBLANES - 1:SUBLANES], hi[SUBLANES - 1:SUBLANES]

        cr, ci = lax.fori_loop(0, ts // SUBLANES, blk, (car_ref[0:1, st_sl], car_ref[1:2, st_sl]),
                               unroll=2 if ts >= 2 * SUBLANES else 1)
        car_ref[0:1, st_sl] = cr
        car_ref[1:2, st_sl] = ci
        y = jnp.dot(xs_ref[...].astype(BF16), wc_ref[kt], preferred_element_type=F32)
        y = y + d_ref[:, kt * kw:(kt + 1) * kw] * ukt.astype(F32)
        y_ref[0, :, kt * kw:(kt + 1) * kw] = _gelu(y).astype(BF16)

    @pl.when(it == nt - 1)
    def _store_state():
        st_ref[0] = car_ref[...]


def _s5(u, wb, scn, wc, dsk, s0):
    b, l, w = u.shape
    nkt, kw, sw2 = wb.shape
    sw = sw2 // 2
    ns = s0.shape[2]
    ts = _tile(l, 512)
    nt = l // ts
    const3 = lambda i, j: (0, 0, 0)
    return pl.pallas_call(
        functools.partial(_s5_kernel, ts=ts, nkt=nkt, kw=kw, sw=sw, nt=nt),
        grid=(b, nt),
        in_specs=[
            pl.BlockSpec((1, ts, w), lambda i, j: (i, j, 0)),
            pl.BlockSpec(wb.shape, const3),
            pl.BlockSpec(scn.shape, lambda i, j: (0, 0, 0, 0)),
            pl.BlockSpec(wc.shape, const3),
            pl.BlockSpec((1, w), lambda i, j: (0, 0)),
            pl.BlockSpec((1, 2, ns), lambda i, j: (i, 0, 0)),
        ],
        out_specs=[
            pl.BlockSpec((1, ts, w), lambda i, j: (i, j, 0)),
            pl.BlockSpec((1, 2, ns), lambda i, j: (i, 0, 0)),
        ],
        out_shape=[
            jax.ShapeDtypeStruct((b, l, w), BF16),
            jax.ShapeDtypeStruct((b, 2, ns), F32),
        ],
        scratch_shapes=[
            pltpu.VMEM((ts, sw2), F32),
            pltpu.VMEM((2, ns), F32),
        ],
        compiler_params=_params("parallel", "arbitrary"),
        name="s5",
    )(u, wb, scn, wc, dsk, s0)


def _s5_tables(a_re, a_im, log_dt, b_re, b_im, c_re, c_im, d_skip):
    g, p = a_re.shape
    ch = b_re.shape[2]
    gpt = MXU_DIM // ch
    nkt = g // gpt
    dt = jnp.exp(log_dt.astype(F32))[:, None]
    mag = jnp.exp(dt * a_re)
    ab_re = mag * jnp.cos(dt * a_im)
    ab_im = mag * jnp.sin(dt * a_im)
    den = a_re * a_re + a_im * a_im
    nr = ab_re - 1.0
    f_re = (nr * a_re + ab_im * a_im) / den
    f_im = (ab_im * a_re - nr * a_im) / den
    bb_re = f_re[..., None] * b_re - f_im[..., None] * b_im
    bb_im = f_re[..., None] * b_im + f_im[..., None] * b_re
    eye = jnp.eye(gpt, dtype=F32)

    def in_blockdiag(bb):
        return jnp.einsum("kgpi,gh->kgihp", bb.reshape(nkt, gpt, p, ch), eye).reshape(nkt, gpt * ch, gpt * p)

    def out_blockdiag(cc):
        return jnp.einsum("kgjp,gh->kgphj", cc.reshape(nkt, gpt, ch, p), eye).reshape(nkt, gpt * p, gpt * ch)

    wb = jnp.concatenate([in_blockdiag(bb_re), in_blockdiag(bb_im)], axis=2).astype(BF16)
    wc = jnp.concatenate([out_blockdiag(c_re), out_blockdiag(-c_im)], axis=1).astype(BF16)

    def cmul(x, y):
        return x[0] * y[0] - x[1] * y[1], x[0] * y[1] + x[1] * y[0]

    lam1 = (ab_re.reshape(nkt, gpt * p), ab_im.reshape(nkt, gpt * p))
    lam2 = cmul(lam1, lam1)
    lam4 = cmul(lam2, lam2)
    rows = jnp.arange(SUBLANES)[None, :, None]

    def shifted(lam, dist):
        return [jnp.where(rows >= dist, c[:, None, :], 0.0) for c in lam]

    pw = [lam1]
    for _ in range(SUBLANES - 1):
        pw.append(cmul(pw[-1], lam1))
    p_re = jnp.stack([c[0] for c in pw], axis=1)
    p_im = jnp.stack([c[1] for c in pw], axis=1)
    scn = jnp.stack(shifted(lam1, 1) + shifted(lam2, 2) + shifted(lam4, 4) + [p_re, p_im], axis=1)
    return wb, scn.astype(F32), wc, d_skip.reshape(1, g * ch).astype(F32)


def _merge_kernel(ys_ref, o_ref, wga_ref, wgb_ref, wo_ref, ga_ref, gb_ref, out_ref, *, heads):
    ys = ys_ref[...]
    ya = jnp.dot(ys, wga_ref[...], preferred_element_type=F32)
    ya = ya * _sigmoid(jnp.dot(ys, wgb_ref[...], preferred_element_type=F32))
    oc = jnp.concatenate([o_ref[hh] for hh in range(heads)], axis=1)
    yb = jnp.dot(oc, wo_ref[...], preferred_element_type=F32)
    out_ref[...] = (ga_ref[...].astype(F32) * ya + gb_ref[...].astype(F32) * yb).astype(BF16)


def _merge(ys, o, w_glu, w_o, gates):
    t, sw = ys.shape
    heads = o.shape[0]
    d = w_o.shape[1]
    tm = _tile(t, 512)
    tn = _tile(d, 512)
    nj = d // tn
    return pl.pallas_call(
        functools.partial(_merge_kernel, heads=heads),
        grid=(t // tm, nj),
        in_specs=[
            pl.BlockSpec((tm, sw), lambda i, j: (i, 0)),
            pl.BlockSpec((heads, tm, LANES), lambda i, j: (0, i, 0)),
            pl.BlockSpec((sw, tn), lambda i, j: (0, j)),
            pl.BlockSpec((sw, tn), lambda i, j: (0, nj + j)),
            pl.BlockSpec((heads * LANES, tn), lambda i, j: (0, j)),
            pl.BlockSpec((tm, tn), lambda i, j: (i, j)),
            pl.BlockSpec((tm, tn), lambda i, j: (i, nj + j)),
        ],
        out_specs=pl.BlockSpec((tm, tn), lambda i, j: (i, j)),
        out_shape=jax.ShapeDtypeStruct((t, d), BF16),
        compiler_params=_params("parallel", "parallel"),
        name="merge",
    )(ys, o, w_glu, w_glu, w_o, gates, gates)


def _resid_kernel(m_ref, w_ref, x_ref, gt_ref, g_ref, sc_ref, sh_ref, x1_ref, h2_ref):
    x1 = x_ref[0] + gt_ref[0] * jnp.dot(m_ref[0], w_ref[...], preferred_element_type=F32)
    x1_ref[0] = x1
    h2_ref[0] = (_rms(x1, g_ref[...]) * (1.0 + sc_ref[0]) + sh_ref[0]).astype(BF16)


def _resid(merged, w_out, x, gt, g2, sc, sh):
    b, l, d = x.shape
    tm = _tile(l, 256)
    tok = lambda i, j: (i, j, 0)
    per_b = lambda i, j: (i, 0, 0)
    return pl.pallas_call(
        _resid_kernel,
        grid=(b, l // tm),
        in_specs=[
            pl.BlockSpec((1, tm, d), tok),
            pl.BlockSpec((d, d), lambda i, j: (0, 0)),
            pl.BlockSpec((1, tm, d), tok),
            pl.BlockSpec((1, 1, d), per_b),
            pl.BlockSpec((1, d), lambda i, j: (0, 0)),
            pl.BlockSpec((1, 1, d), per_b),
            pl.BlockSpec((1, 1, d), per_b),
        ],
        out_specs=[pl.BlockSpec((1, tm, d), tok), pl.BlockSpec((1, tm, d), tok)],
        out_shape=[jax.ShapeDtypeStruct((b, l, d), F32), jax.ShapeDtypeStruct((b, l, d), BF16)],
        compiler_params=_params("parallel", "parallel"),
        name="resid",
    )(merged.reshape(b, l, d), w_out, x, gt, g2.reshape(1, d), sc, sh)


def _peer_topk_kernel(q_ref, k1_ref, k2_ref, flat_ref, invalid_ref, r2_ref, c1_ref, a1_ref, a2_ref,
                      *, nh, nkeys, topk):
    tb = q_ref.shape[1]
    iota_k = lax.broadcasted_iota(jnp.int32, (nkeys, tb), 0).astype(F32)
    iota_t = lax.broadcasted_iota(jnp.int32, (topk, tb), 0).astype(F32)
    flat = flat_ref[...]
    invalid = invalid_ref[...]
    nt_dims = (((1,), (1,)), ((), ()))

    n_cand = flat.shape[0]

    def extract(s, exact_ties):
        work = s
        rank = jnp.full((nkeys, tb), float(topk), F32)
        vals = []
        for a in range(topk):
            m = jnp.max(work, axis=0, keepdims=True)
            if exact_ties:
                idx = jnp.min(jnp.where(work == m, iota_k, float(nkeys)), axis=0, keepdims=True)
                sel = iota_k == idx
            else:
                sel = work == m
            rank = jnp.where(sel, float(a), rank)
            work = jnp.where(sel, REMOVED, work)
            vals.append(m)
        ranked = jnp.sum(jnp.where(rank < float(topk), 1.0, 0.0), axis=0, keepdims=True)
        return vals, rank, ranked

    def stack(vals):
        out = jnp.zeros((topk, tb), F32)
        for a in range(topk):
            out = jnp.where(iota_t == float(a), vals[a], out)
        return out

    def candidates(v1, v2):
        vs1 = stack(v1)
        vs2 = stack(v2)
        blocks = [v1[0] + vs2]
        for a in range(1, SUBLANES):
            blocks.append(v1[a] + vs2[:SUBLANES])
        blocks.append(vs1[SUBLANES:] + v2[0])
        return jnp.concatenate(blocks, axis=0) + invalid

    def route(s1, s2, exact_ties):
        v1, rank1, n1 = extract(s1, exact_ties)
        v2, rank2, n2 = extract(s2, exact_ties)
        cand = candidates(v1, v2)
        counts = jnp.zeros((topk, tb), F32)
        top = None
        zsum = None
        for kk in range(topk):
            m = jnp.max(cand, axis=0, keepdims=True)
            if exact_ties:
                f = jnp.min(jnp.where(cand == m, flat, 1e9), axis=0, keepdims=True)
                cand = jnp.where(flat == f, REMOVED, cand)
                counts = counts + jnp.where(iota_t == jnp.floor(f * (1.0 / topk)), 1.0, 0.0)
            else:
                cand = jnp.where(cand == m, REMOVED, cand)
            if kk == 0:
                top = m
                zsum = jnp.ones_like(m)
            else:
                zsum = zsum + jnp.exp(m - top)
        n3 = None
        if not exact_ties:
            gone = jnp.where(cand == REMOVED, 1.0, 0.0)
            n3 = jnp.sum(gone, axis=0, keepdims=True)
            per_rank = [jnp.sum(gone[:topk], axis=0, keepdims=True)]
            for a in range(1, SUBLANES):
                lo = topk + (a - 1) * SUBLANES
                per_rank.append(jnp.sum(gone[lo:lo + SUBLANES], axis=0, keepdims=True))
            counts = jnp.concatenate([stack(per_rank + [per_rank[0]] * (topk - SUBLANES))[:SUBLANES],
                                      gone[n_cand - SUBLANES:]], axis=0)
        c1 = jnp.zeros((nkeys, tb), F32)
        for a in range(topk):
            c1 = jnp.where(rank1 == float(a), counts[a:a + 1], c1)
        a1 = jnp.exp(s1 - v1[0]) * (1.0 / zsum)
        a2 = jnp.exp(s2 - v2[0])
        clean = None
        if not exact_ties:
            want = float(topk)
            bad = jnp.where(n1 != want, 1.0, 0.0) + jnp.where(n2 != want, 1.0, 0.0) + jnp.where(n3 != want, 1.0, 0.0)
            clean = jnp.max(bad) == 0.0
        return (rank2, c1, a1, a2), clean

    def store(hh, tables):
        r2_ref[hh], c1_ref[hh], a1_ref[hh], a2_ref[hh] = tables

    def pair(pp, carry):
        heads = (2 * pp, 2 * pp + 1)
        scores = []
        for hh in heads:
            scores.append((lax.dot_general(k1_ref[hh], q_ref[2 * hh], nt_dims, preferred_element_type=F32),
                           lax.dot_general(k2_ref[hh], q_ref[2 * hh + 1], nt_dims, preferred_element_type=F32)))
        quick = [route(s1, s2, False) for s1, s2 in scores]
        clean = jnp.logical_and(quick[0][1], quick[1][1])

        @pl.when(clean)
        def _no_ties():
            for hh, (tables, _) in zip(heads, quick):
                store(hh, tables)

        @pl.when(jnp.logical_not(clean))
        def _ties():
            for hh, (s1, s2) in zip(heads, scores):
                store(hh, route(s1, s2, True)[0])

        return carry

    lax.fori_loop(0, nh // 2, pair, 0)


def _peer_topk(q, k1, k2, topk):
    nh, nkeys, half = k1.shape
    t = q.shape[1]
    tb = LANES
    assert topk == 2 * SUBLANES and half == LANES and t % tb == 0
    rows = jnp.arange(topk + (SUBLANES - 1) * SUBLANES + SUBLANES)
    a_idx = jnp.where(rows < topk, 0, jnp.where(rows < topk + (SUBLANES - 1) * SUBLANES,
                                                1 + (rows - topk) // SUBLANES, SUBLANES + (rows - topk - (SUBLANES - 1) * SUBLANES)))
    b_idx = jnp.where(rows < topk, rows, jnp.where(rows < topk + (SUBLANES - 1) * SUBLANES, (rows - topk) % SUBLANES, 0))
    flat = jnp.broadcast_to((a_idx * topk + b_idx).astype(F32)[:, None], (rows.shape[0], tb))
    invalid = jnp.broadcast_to(jnp.where((a_idx + 1) * (b_idx + 1) <= topk, 0.0, EXCLUDED).astype(F32)[:, None],
                               (rows.shape[0], tb))
    out = lambda dt: jax.ShapeDtypeStruct((nh, nkeys, t), dt)
    ospec = pl.BlockSpec((nh, nkeys, tb), lambda i: (0, 0, i))
    return pl.pallas_call(
        functools.partial(_peer_topk_kernel, nh=nh, nkeys=nkeys, topk=topk),
        grid=(t // tb,),
        in_specs=[
            pl.BlockSpec((2 * nh, tb, LANES), lambda i: (0, i, 0)),
            pl.BlockSpec(k1.shape, lambda i: (0, 0, 0)),
            pl.BlockSpec(k2.shape, lambda i: (0, 0, 0)),
            pl.BlockSpec(flat.shape, lambda i: (0, 0)),
            pl.BlockSpec(invalid.shape, lambda i: (0, 0)),
        ],
        out_specs=[ospec] * 4,
        out_shape=[out(F32)] * 4,
        compiler_params=_params("parallel"),
        name="peer_topk",
    )(q, k1, k2, flat, invalid)


def _peer_mix_kernel(h_ref, u_ref, vt_ref, r2_ref, c1_ref, a1_ref, a2_ref, *rest, nh, nkeys, ni, ne, norm):
    if len(rest) == 6:
        o_ref, ht_scr, w_scr, acc_scr, r2b_scr, a2b_scr = rest
        x_ref = gt_ref = g_ref = None
    else:
        x_ref, gt_ref, g_ref, o_ref, ht_scr, w_scr, acc_scr, r2b_scr, a2b_scr = rest
    e = pl.program_id(1)
    tb = h_ref.shape[0]

    @pl.when(e == 0)
    def _init():
        acc_scr[...] = jnp.zeros(acc_scr.shape, F32)
        for hh in range(nh):
            r2b_scr[hh] = r2_ref[hh].astype(BF16)
            a2b_scr[hh] = a2_ref[hh].astype(BF16)

    ht_scr[...] = lax.dot_general(u_ref[...], h_ref[...], (((1,), (1,)), ((), ())), preferred_element_type=F32)
    first_keys = pl.ds(pl.multiple_of(e * ni, SUBLANES), ni)
    zero = jnp.zeros((nkeys, LANES), BF16)
    for il in range(ni):
        rows = slice(il * nkeys, (il + 1) * nkeys)
        for lg in range(tb // LANES):
            sl = slice(lg * LANES, (lg + 1) * LANES)
            gate = zero
            for hh in range(nh):
                partners = jnp.broadcast_to(c1_ref[hh, first_keys, sl][il:il + 1].astype(BF16), (nkeys, LANES))
                first = jnp.broadcast_to(a1_ref[hh, first_keys, sl][il:il + 1].astype(BF16), (nkeys, LANES))
                gate = gate + jnp.where(r2b_scr[hh, :, sl] < partners, a2b_scr[hh, :, sl], zero) * first
            w_scr[rows, sl] = gate * _gelu(ht_scr[rows, sl]).astype(BF16)
    acc_scr[...] += jnp.dot(vt_ref[...], w_scr[...], preferred_element_type=F32)

    @pl.when(e == ne - 1)
    def _done():
        mix = acc_scr[...].T
        if x_ref is None:
            o_ref[...] = mix
        else:
            y = x_ref[0] + gt_ref[0] * mix
            o_ref[0] = _rms(y, g_ref[...]) if norm else y


def _peer_mix(h2, u_tab, vt_tab, r2, c1, a1, a2, x1, gt, g, norm):
    b, l, _ = x1.shape
    t, d = h2.shape
    nh, nkeys, _ = r2.shape
    n_exp = u_tab.shape[0]
    tb = _tile(t, 512)
    ni = SUBLANES
    eb = ni * nkeys
    ne = n_exp // eb
    aux = pl.BlockSpec((nh, nkeys, tb), lambda i, e: (0, 0, i), pipeline_mode=pl.Buffered(1))
    in_specs = [
        pl.BlockSpec((tb, d), lambda i, e: (i, 0)),
        pl.BlockSpec((eb, d), lambda i, e: (e, 0)),
        pl.BlockSpec((d, eb), lambda i, e: (0, e)),
        aux, aux, aux, aux,
    ]
    args = [h2, u_tab, vt_tab, r2, c1, a1, a2]
    fused = l % tb == 0
    if fused:
        per_seq = l // tb
        in_specs += [
            pl.BlockSpec((1, tb, d), lambda i, e: (i // per_seq, i % per_seq, 0), pipeline_mode=pl.Buffered(1)),
            pl.BlockSpec((1, 1, d), lambda i, e: (i // per_seq, 0, 0)),
            pl.BlockSpec((1, d), lambda i, e: (0, 0)),
        ]
        args += [x1, gt, g.reshape(1, d)]
        out_spec = pl.BlockSpec((1, tb, d), lambda i, e: (i // per_seq, i % per_seq, 0))
        out_shape = jax.ShapeDtypeStruct((b, l, d), F32)
    else:
        out_spec = pl.BlockSpec((tb, d), lambda i, e: (i, 0))
        out_shape = jax.ShapeDtypeStruct((t, d), F32)
    out = pl.pallas_call(
        functools.partial(_peer_mix_kernel, nh=nh, nkeys=nkeys, ni=ni, ne=ne, norm=norm),
        grid=(t // tb, ne),
        in_specs=in_specs,
        out_specs=out_spec,
        out_shape=out_shape,
        scratch_shapes=[
            pltpu.VMEM((eb, tb), F32),
            pltpu.VMEM((eb, tb), BF16),
            pltpu.VMEM((d, tb), F32),
            pltpu.VMEM((nh, nkeys, tb), BF16),
            pltpu.VMEM((nh, nkeys, tb), BF16),
        ],
        compiler_params=_params("parallel", "arbitrary"),
        name="peer_mix",
    )(*args)
    return out if fused else _final(x1, out, gt, g, norm)


def _final_kernel(x_ref, p_ref, gt_ref, g_ref, y_ref, *, norm):
    y = x_ref[0] + gt_ref[0] * p_ref[0]
    y_ref[0] = _rms(y, g_ref[...]) if norm else y


def _final(x1, peer, gt, g, norm):
    b, l, d = x1.shape
    tm = _tile(l, 512)
    tok = lambda i, j: (i, j, 0)
    return pl.pallas_call(
        functools.partial(_final_kernel, norm=norm),
        grid=(b, l // tm),
        in_specs=[
            pl.BlockSpec((1, tm, d), tok),
            pl.BlockSpec((1, tm, d), tok),
            pl.BlockSpec((1, 1, d), lambda i, j: (i, 0, 0)),
            pl.BlockSpec((1, d), lambda i, j: (0, 0)),
        ],
        out_specs=pl.BlockSpec((1, tm, d), tok),
        out_shape=jax.ShapeDtypeStruct((b, l, d), F32),
        compiler_params=_params("parallel", "parallel"),
        name="final",
    )(x1, peer.reshape(b, l, d), gt, g.reshape(1, d))


def _rope_tables(pos, rope):
    half = rope // 2
    inv = jnp.power(ROPE_THETA, -jnp.arange(half, dtype=F32) / half)
    ang = pos.astype(F32)[:, None] * inv
    reps = LANES // half
    return jnp.tile(jnp.cos(ang), (1, reps)), jnp.tile(jnp.sin(ang), (1, reps))


def _prep_weights(p, dims):
    d, sw, q_lora, kv_lora, rope, heads, nope = (dims[k] for k in ("d", "sw", "q_lora", "kv_lora", "rope", "heads", "nope"))
    half = rope // 2
    w_in = p["w_in"]
    off_q = sw
    off_kv = off_q + q_lora
    off_kr = off_kv + kv_lora
    off_g = off_kr + rope
    kr_w = w_in[:, off_kr:off_g]
    kr_rot = jnp.concatenate([-kr_w[:, half:], kr_w[:, :half]], axis=1)
    pad = jnp.zeros((d, LANES - rope), F32)
    w_mla = jnp.concatenate([w_in[:, off_q:off_kr], kr_w, pad, kr_rot, pad], axis=1).astype(BF16)
    wq = p["w_qu"].reshape(q_lora, heads, nope + rope)
    wq_n = wq[:, :, :nope].reshape(q_lora, heads * nope)
    wq_r = wq[:, :, nope:]
    wq_rot = jnp.concatenate([-wq_r[:, :, half:], wq_r[:, :, :half]], axis=2)
    hpad = jnp.zeros((q_lora, heads, LANES - rope), F32)
    wq_a = jnp.concatenate([wq_r, hpad], axis=2).reshape(q_lora, heads * LANES)
    wq_b = jnp.concatenate([wq_rot, hpad], axis=2).reshape(q_lora, heads * LANES)
    return {
        "w_gates": w_in[:, off_g:].astype(BF16),
        "w_u": w_in[:, :sw].astype(BF16),
        "w_mla": w_mla,
        "w_q": jnp.concatenate([wq_n, wq_a, wq_b], axis=1).astype(BF16),
        "w_kv": jnp.concatenate([p["w_uk"].reshape(kv_lora, heads * nope),
                                 p["w_uv"].reshape(kv_lora, heads * dims["v_dim"])], axis=1).astype(BF16),
        "w_o": p["w_o"].astype(BF16),
        "w_glu": p["w_glu"].astype(BF16),
        "w_out": p["w_out"].astype(BF16),
        "peer_wq": p["peer_wq"].astype(BF16),
        "peer_k1": p["peer_k1"].astype(BF16),
        "peer_k2": p["peer_k2"].astype(BF16),
        "peer_u": p["peer_u"].astype(BF16),
        "peer_vt": p["peer_v"].T.astype(BF16),
        "s5": _s5_tables(p["ssm_a_re"], p["ssm_a_im"], p["ssm_log_dt"], p["ssm_b_re"], p["ssm_b_im"],
                         p["ssm_c_re"], p["ssm_c_im"], p["ssm_d"]),
    }


def _layer(x, mod, past_ckv, past_kr, s0, p, w, dims, g_final, last):
    b, l, d = x.shape
    t = b * l
    heads, rope, kv_lora, topk = dims["heads"], dims["rope"], dims["kv_lora"], dims["topk"]
    sh1, sc1, gt1, sh2, sc2, gt2 = mod
    past = 0 if past_ckv is None else past_ckv.shape[1]

    h = _normmod(x, p["g_norm1"], sc1, sh1).reshape(t, d)
    gates = _mm(h, w["w_gates"], BF16, act="sigmoid")
    u = _mm(h, w["w_u"], BF16)
    cos, sin = _rope_tables(past + jnp.arange(l, dtype=jnp.int32), rope)
    qn, qr, ckv, ckv_b, kr, krp = _mla_proj(
        h, w["w_mla"], p["g_q"], p["g_kv"], w["w_q"], cos, sin, l,
        heads=heads, q_lora=dims["q_lora"], kv_lora=kv_lora, rope=rope, scale=dims["scale"])

    ys, s_fin = _s5(u.reshape(b, l, -1), *w["s5"], s0)

    if past_ckv is None:
        lk, keys_c, keys_r = l, ckv_b, krp
        tq = tk = _tile(l, 512)
    else:
        n_keys = past + l
        lk = -(-n_keys // LANES) * LANES
        keys_c = jnp.concatenate([past_ckv.astype(BF16), ckv_b.reshape(b, l, kv_lora)], axis=1)
        keys_c = jnp.pad(keys_c, ((0, 0), (0, lk - n_keys), (0, 0))).reshape(b * lk, kv_lora)
        past_r = jnp.pad(past_kr.astype(BF16), ((0, 0), (0, 0), (0, LANES - rope)))
        keys_r = jnp.concatenate([past_r, krp.reshape(b, l, LANES)], axis=1)
        keys_r = jnp.pad(keys_r, ((0, 0), (0, lk - n_keys), (0, 0))).reshape(b * lk, LANES)
        tq, tk = l, lk
    kh, vh = _kvup(keys_c, w["w_kv"], heads)
    o = _flash(qn, qr, kh, keys_r, vh, batch=b, lq=l, lk=lk, q_pos0=past, n_keys=past + l,
               tq=tq, tk=tk, hb=heads)

    merged = _merge(ys.reshape(t, -1), o, w["w_glu"], w["w_o"], gates)
    x1, h2 = _resid(merged, w["w_out"], x, gt1, p["g_norm2"], sc2, sh2)

    h2 = h2.reshape(t, d)
    q = _mm_split(h2, w["peer_wq"], BF16)
    r2, c1, a1, a2 = _peer_topk(q, w["peer_k1"], w["peer_k2"], topk)
    x2 = _peer_mix(h2, w["peer_u"], w["peer_vt"], r2, c1, a1, a2, x1, gt2, g_final, last)
    return x2, ckv.reshape(b, l, kv_lora), kr.reshape(b, l, rope), s_fin


def kernel(x_prompt, x_sample, c_prompt, c_sample, cache_ckv, cache_krope, state_ssm_re, state_ssm_im, w_ada, b_ada, g_norm1, g_norm2, w_in, g_q, w_qu, g_kv, w_uk, w_uv, w_o, ssm_a_re, ssm_a_im, ssm_log_dt, ssm_b_re, ssm_b_im, ssm_c_re, ssm_c_im, ssm_d, w_glu, w_out, peer_wq, peer_k1, peer_k2, peer_u, peer_v, g_final):
    depth = w_in.shape[0]
    bp, lp, d = x_prompt.shape
    bs, ls, _ = x_sample.shape
    groups, states = ssm_a_re.shape[1:]
    heads, nope = w_uk.shape[2:]
    rope = cache_krope.shape[-1]
    dims = {
        "d": d, "sw": groups * ssm_b_re.shape[3], "q_lora": g_q.shape[1], "kv_lora": g_kv.shape[1],
        "rope": rope, "heads": heads, "nope": nope, "v_dim": w_uv.shape[3],
        "scale": math.log2(math.e) / math.sqrt(nope + rope), "topk": 16,
    }
    assert nope == LANES and dims["v_dim"] == LANES and rope <= LANES

    xp, xs = x_prompt, x_sample
    nb = bp + bs
    rows = -(-nb // 16) * 16
    c_all = jnp.pad(jnp.concatenate([c_prompt, c_sample], axis=0), ((0, rows - nb), (0, 0)))
    zeros = jnp.zeros((bp, 2, groups * states), F32)
    outs_p, outs_s = [], []
    for layer in range(depth):
        p = {
            "g_norm1": g_norm1[layer], "g_norm2": g_norm2[layer], "w_in": w_in[layer], "g_q": g_q[layer],
            "w_qu": w_qu[layer], "g_kv": g_kv[layer], "w_uk": w_uk[layer], "w_uv": w_uv[layer], "w_o": w_o[layer],
            "ssm_a_re": ssm_a_re[layer], "ssm_a_im": ssm_a_im[layer], "ssm_log_dt": ssm_log_dt[layer],
            "ssm_b_re": ssm_b_re[layer], "ssm_b_im": ssm_b_im[layer], "ssm_c_re": ssm_c_re[layer],
            "ssm_c_im": ssm_c_im[layer], "ssm_d": ssm_d[layer], "w_glu": w_glu[layer], "w_out": w_out[layer],
            "peer_wq": peer_wq[layer], "peer_k1": peer_k1[layer], "peer_k2": peer_k2[layer],
            "peer_u": peer_u[layer], "peer_v": peer_v[layer],
        }
        w = _prep_weights(p, dims)
        mod = _ada(c_all, w_ada[layer], b_ada[layer])
        mod_p = [m.reshape(bp, 1, d) for m in jnp.split(mod[:bp], 6, axis=-1)]
        mod_s = [m.reshape(bs, 1, d) for m in jnp.split(mod[bp:nb], 6, axis=-1)]
        s0_s = jnp.stack([state_ssm_re[layer].reshape(bs, -1), state_ssm_im[layer].reshape(bs, -1)], axis=1)
        last = layer == depth - 1
        xp, *res_p = _layer(xp, mod_p, None, None, zeros, p, w, dims, g_final, last)
        xs, *res_s = _layer(xs, mod_s, cache_ckv[layer], cache_krope[layer], s0_s, p, w, dims, g_final, last)
        for res, outs, g in ((res_p, outs_p, bp), (res_s, outs_s, bs)):
            ckv, kr, s_fin = res
            outs.append((ckv, kr, s_fin[:, 0].reshape(g, groups, states), s_fin[:, 1].reshape(g, groups, states)))
    stack = lambda outs, k: jnp.stack([o[k] for o in outs])
    return (xp, xs,
            stack(outs_p, 0), stack(outs_p, 1), stack(outs_p, 2), stack(outs_p, 3),
            stack(outs_s, 0), stack(outs_s, 1), stack(outs_s, 2), stack(outs_s, 3))
```

```python
import functools
import math

import jax
import jax.numpy as jnp
from jax import lax
from jax.experimental import pallas as pl
from jax.experimental.pallas import tpu as pltpu

F32 = jnp.float32
BF16 = jnp.bfloat16

EPS = 1e-6
CHUNK = 64
ROPE_THETA = 10000.0
LANES = 128
SUBLANES = 8
MXU_DIM = 256
VMEM_LIMIT_BYTES = 56 * 1024 * 1024
MASKED = -1e30
REMOVED = -3e38
EXCLUDED = -1e38


def _params(*semantics):
    return pltpu.CompilerParams(dimension_semantics=semantics, vmem_limit_bytes=VMEM_LIMIT_BYTES)


def _tile(n, pref):
    if n <= pref:
        return n
    t = pref
    while n % t:
        t //= 2
    assert t >= SUBLANES, (n, pref)
    return t


def _sigmoid(x):
    return 1.0 / (1.0 + jnp.exp(-x))


def _gelu(x):
    return 0.5 * x * (1.0 + jnp.tanh(math.sqrt(2.0 / math.pi) * (x + 0.044715 * (x * x * x))))


def _rms(x, g):
    return x * lax.rsqrt(jnp.mean(x * x, axis=-1, keepdims=True) + EPS) * g


def _ada_kernel(c_ref, w_ref, b_ref, o_ref):
    c = c_ref[...]
    a = (c * _sigmoid(c)).astype(BF16)
    o_ref[...] = jnp.dot(a, w_ref[...].astype(BF16), preferred_element_type=F32) + b_ref[...]


def _ada(c, w, b):
    rows, d = c.shape
    n = w.shape[1]
    tn = _tile(n, 1024)
    return pl.pallas_call(
        _ada_kernel,
        grid=(n // tn,),
        in_specs=[
            pl.BlockSpec((rows, d), lambda j: (0, 0)),
            pl.BlockSpec((d, tn), lambda j: (0, j)),
            pl.BlockSpec((1, tn), lambda j: (0, j)),
        ],
        out_specs=pl.BlockSpec((rows, tn), lambda j: (0, j)),
        out_shape=jax.ShapeDtypeStruct((rows, n), F32),
        compiler_params=_params("parallel"),
        name="ada",
    )(c, w, b.reshape(1, n))


def _normmod_kernel(x_ref, g_ref, sc_ref, sh_ref, o_ref):
    h = _rms(x_ref[0], g_ref[...]) * (1.0 + sc_ref[0]) + sh_ref[0]
    o_ref[0] = h.astype(BF16)


def _normmod(x, g, sc, sh):
    b, l, d = x.shape
    tm = _tile(l, 512)
    return pl.pallas_call(
        _normmod_kernel,
        grid=(b, l // tm),
        in_specs=[
            pl.BlockSpec((1, tm, d), lambda i, j: (i, j, 0)),
            pl.BlockSpec((1, d), lambda i, j: (0, 0)),
            pl.BlockSpec((1, 1, d), lambda i, j: (i, 0, 0)),
            pl.BlockSpec((1, 1, d), lambda i, j: (i, 0, 0)),
        ],
        out_specs=pl.BlockSpec((1, tm, d), lambda i, j: (i, j, 0)),
        out_shape=jax.ShapeDtypeStruct((b, l, d), BF16),
        compiler_params=_params("parallel", "parallel"),
        name="normmod",
    )(x, g.reshape(1, d), sc, sh)


def _mm_kernel(a_ref, w_ref, o_ref, *, act):
    z = jnp.dot(a_ref[...], w_ref[...], preferred_element_type=F32)
    if act == "sigmoid":
        z = _sigmoid(z)
    o_ref[...] = z.astype(o_ref.dtype)


def _mm(a, w, out_dtype, act=None, tn_pref=512):
    t, k = a.shape
    n = w.shape[1]
    tm = _tile(t, 512)
    tn = _tile(n, tn_pref)
    return pl.pallas_call(
        functools.partial(_mm_kernel, act=act),
        grid=(t // tm, n // tn),
        in_specs=[
            pl.BlockSpec((tm, k), lambda i, j: (i, 0)),
            pl.BlockSpec((k, tn), lambda i, j: (0, j)),
        ],
        out_specs=pl.BlockSpec((tm, tn), lambda i, j: (i, j)),
        out_shape=jax.ShapeDtypeStruct((t, n), out_dtype),
        compiler_params=_params("parallel", "parallel"),
        name="mm",
    )(a, w)


def _mm_split_kernel(a_ref, w_ref, o_ref, *, parts):
    z = jnp.dot(a_ref[...], w_ref[...], preferred_element_type=F32)
    for p in range(parts):
        o_ref[p] = z[:, p * LANES:(p + 1) * LANES].astype(o_ref.dtype)


def _mm_split(a, w, out_dtype):
    t, k = a.shape
    n = w.shape[1]
    tm = _tile(t, 512)
    tn = _tile(n, MXU_DIM)
    parts = tn // LANES
    return pl.pallas_call(
        functools.partial(_mm_split_kernel, parts=parts),
        grid=(t // tm, n // tn),
        in_specs=[
            pl.BlockSpec((tm, k), lambda i, j: (i, 0)),
            pl.BlockSpec((k, tn), lambda i, j: (0, j)),
        ],
        out_specs=pl.BlockSpec((parts, tm, LANES), lambda i, j: (j, i, 0)),
        out_shape=jax.ShapeDtypeStruct((n // LANES, t, LANES), out_dtype),
        compiler_params=_params("parallel", "parallel"),
        name="mm_split",
    )(a, w)


def _mla_proj_kernel(h_ref, wm_ref, gq_ref, gkv_ref, wq_ref, cos_ref, sin_ref,
                     qn_ref, qr_ref, ckv_ref, ckvb_ref, kr_ref, krp_ref,
                     *, heads, q_lora, kv_lora, rope, scale):
    z = jnp.dot(h_ref[...], wm_ref[...], preferred_element_type=F32)
    cos = cos_ref[...]
    sin = sin_ref[...]
    off = q_lora + kv_lora
    krp = z[:, off:off + LANES] * cos + z[:, off + LANES:off + 2 * LANES] * sin
    kr_ref[...] = krp[:, :rope]
    krp_ref[...] = krp.astype(BF16)
    ckv = _rms(z[:, q_lora:off], gkv_ref[...])
    ckv_ref[...] = ckv
    ckvb_ref[...] = ckv.astype(BF16)
    qd = _rms(z[:, :q_lora], gq_ref[...]).astype(BF16)
    zq = jnp.dot(qd, wq_ref[...], preferred_element_type=F32)
    hn = heads * LANES
    for hh in range(heads):
        lo = hh * LANES
        qn_ref[hh] = (zq[:, lo:lo + LANES] * scale).astype(BF16)
        qr = zq[:, hn + lo:hn + lo + LANES] * cos + zq[:, 2 * hn + lo:2 * hn + lo + LANES] * sin
        qr_ref[hh] = (qr * scale).astype(BF16)


def _mla_proj(h, wm, gq, gkv, wq, cos, sin, seq, *, heads, q_lora, kv_lora, rope, scale):
    t, d = h.shape
    tm = _tile(t, 256)
    if seq % tm == 0:
        nrep = seq // tm
        tab_map = lambda i: (i % nrep, 0)
    else:
        assert tm % seq == 0
        cos = jnp.tile(cos, (tm // seq, 1))
        sin = jnp.tile(sin, (tm // seq, 1))
        tab_map = lambda i: (0, 0)
    nm = wm.shape[1]
    nq = wq.shape[1]
    row = lambda i: (i, 0)
    const = lambda i: (0, 0)
    return pl.pallas_call(
        functools.partial(_mla_proj_kernel, heads=heads, q_lora=q_lora, kv_lora=kv_lora, rope=rope, scale=scale),
        grid=(t // tm,),
        in_specs=[
            pl.BlockSpec((tm, d), row),
            pl.BlockSpec((d, nm), const),
            pl.BlockSpec((1, q_lora), const),
            pl.BlockSpec((1, kv_lora), const),
            pl.BlockSpec((q_lora, nq), const),
            pl.BlockSpec((tm, LANES), tab_map),
            pl.BlockSpec((tm, LANES), tab_map),
        ],
        out_specs=[
            pl.BlockSpec((heads, tm, LANES), lambda i: (0, i, 0)),
            pl.BlockSpec((heads, tm, LANES), lambda i: (0, i, 0)),
            pl.BlockSpec((tm, kv_lora), row),
            pl.BlockSpec((tm, kv_lora), row),
            pl.BlockSpec((tm, rope), row),
            pl.BlockSpec((tm, LANES), row),
        ],
        out_shape=[
            jax.ShapeDtypeStruct((heads, t, LANES), BF16),
            jax.ShapeDtypeStruct((heads, t, LANES), BF16),
            jax.ShapeDtypeStruct((t, kv_lora), F32),
            jax.ShapeDtypeStruct((t, kv_lora), BF16),
            jax.ShapeDtypeStruct((t, rope), F32),
            jax.ShapeDtypeStruct((t, LANES), BF16),
        ],
        compiler_params=_params("parallel"),
        name="mla_proj",
    )(h, wm, gq.reshape(1, q_lora), gkv.reshape(1, kv_lora), wq, cos, sin)


def _kvup_kernel(c_ref, w_ref, k_ref, v_ref, *, heads):
    z = jnp.dot(c_ref[...], w_ref[...], preferred_element_type=F32)
    for hh in range(heads):
        k_ref[hh] = z[:, hh * LANES:(hh + 1) * LANES].astype(BF16)
        v_ref[hh] = z[:, (heads + hh) * LANES:(heads + hh + 1) * LANES].astype(BF16)


def _kvup(ckv, w, heads):
    t, c = ckv.shape
    tm = _tile(t, 512)
    if t % tm:
        tm = t
    return pl.pallas_call(
        functools.partial(_kvup_kernel, heads=heads),
        grid=(t // tm,),
        in_specs=[
            pl.BlockSpec((tm, c), lambda i: (i, 0)),
            pl.BlockSpec(w.shape, lambda i: (0, 0)),
        ],
        out_specs=[
            pl.BlockSpec((heads, tm, LANES), lambda i: (0, i, 0)),
            pl.BlockSpec((heads, tm, LANES), lambda i: (0, i, 0)),
        ],
        out_shape=[jax.ShapeDtypeStruct((heads, t, LANES), BF16)] * 2,
        compiler_params=_params("parallel"),
        name="kvup",
    )(ckv, w)


def _flash_kernel(iq_tab, ik_tab, flag_tab, qn_ref, qr_ref, k_ref, kr_ref, v_ref, o_ref,
                  qc_scr, m_scr, acc_scr, *, hb, tq, tk, q_pos0, n_keys):
    pair = pl.program_id(2)
    iq = iq_tab[pair]
    ik = ik_tab[pair]
    flags = flag_tab[pair]
    first = (flags & 1) != 0
    last = (flags & 2) != 0
    full = (flags & 4) != 0

    @pl.when(first)
    def _init():
        m_scr[...] = jnp.full(m_scr.shape, MASKED, F32)
        acc_scr[...] = jnp.zeros(acc_scr.shape, F32)
        for hh in range(hb):
            qc_scr[hh] = jnp.concatenate([qn_ref[hh], qr_ref[hh]], axis=1)

    def step(masked):
        kr = kr_ref[...]
        ones = jnp.ones((tk, LANES), BF16)
        if masked:
            qp = q_pos0 + iq * tq + lax.broadcasted_iota(jnp.int32, (tq, tk), 0)
            kp = ik * tk + lax.broadcasted_iota(jnp.int32, (tq, tk), 1)
            allowed = jnp.logical_and(kp // CHUNK <= qp // CHUNK, kp < n_keys)

        def head(hh, carry):
            kc = jnp.concatenate([k_ref[hh], kr], axis=1)
            s = lax.dot_general(qc_scr[hh], kc, (((1,), (1,)), ((), ())), preferred_element_type=F32)
            if masked:
                s = jnp.where(allowed, s, MASKED)
            m_prev = m_scr[hh]
            m_new = jnp.maximum(m_prev, jnp.max(s, axis=1, keepdims=True))
            alpha = jnp.exp2(m_prev - m_new)
            p = jnp.exp2(s - pltpu.repeat(m_new, tk // LANES, axis=1)).astype(BF16)
            vc = jnp.concatenate([v_ref[hh], ones], axis=1)
            acc_scr[hh] = pltpu.repeat(alpha, 2, axis=1) * acc_scr[hh] + jnp.dot(p, vc, preferred_element_type=F32)
            m_scr[hh] = m_new
            return carry

        lax.fori_loop(0, hb, head, 0, unroll=min(hb, 8))

    @pl.when(full)
    def _full():
        step(False)

    @pl.when(jnp.logical_not(full))
    def _diag():
        step(True)

    @pl.when(last)
    def _done():
        for hh in range(hb):
            acc = acc_scr[hh]
            o_ref[hh] = (acc[:, :LANES] / acc[:, LANES:]).astype(BF16)


def _flash_pairs(lq, lk, tq, tk, q_pos0, n_keys):
    iqs, iks, flags = [], [], []
    for iq in range(lq // tq):
        q_first = q_pos0 + iq * tq
        last_key = min(n_keys - 1, ((q_first + tq - 1) // CHUNK) * CHUNK + CHUNK - 1)
        full_key = min(n_keys - 1, (q_first // CHUNK) * CHUNK + CHUNK - 1)
        n_blocks = last_key // tk + 1
        for ik in range(n_blocks):
            full = (ik + 1) * tk - 1 <= full_key
            iqs.append(iq)
            iks.append(ik)
            flags.append((1 if ik == 0 else 0) | (2 if ik == n_blocks - 1 else 0) | (4 if full else 0))
    as_i32 = lambda xs: jnp.asarray(xs, jnp.int32)
    return as_i32(iqs), as_i32(iks), as_i32(flags)


def _flash(qn, qr, k, krp, v, *, batch, lq, lk, q_pos0, n_keys, tq, tk, hb):
    heads = qn.shape[0]
    nq = lq // tq
    nk = lk // tk
    assert lq % tq == 0 and lk % tk == 0 and heads % hb == 0 and tk % LANES == 0
    iq_tab, ik_tab, flag_tab = _flash_pairs(lq, lk, tq, tk, q_pos0, n_keys)
    qmap = lambda b, h, p, iqt, ikt, ft: (h, b * nq + iqt[p], 0)
    kmap = lambda b, h, p, iqt, ikt, ft: (h, b * nk + ikt[p], 0)
    return pl.pallas_call(
        functools.partial(_flash_kernel, hb=hb, tq=tq, tk=tk, q_pos0=q_pos0, n_keys=n_keys),
        grid_spec=pltpu.PrefetchScalarGridSpec(
            num_scalar_prefetch=3,
            grid=(batch, heads // hb, iq_tab.shape[0]),
            in_specs=[
                pl.BlockSpec((hb, tq, LANES), qmap),
                pl.BlockSpec((hb, tq, LANES), qmap),
                pl.BlockSpec((hb, tk, LANES), kmap),
                pl.BlockSpec((tk, LANES), lambda b, h, p, iqt, ikt, ft: (b * nk + ikt[p], 0)),
                pl.BlockSpec((hb, tk, LANES), kmap),
            ],
            out_specs=pl.BlockSpec((hb, tq, LANES), qmap),
            scratch_shapes=[
                pltpu.VMEM((hb, tq, 2 * LANES), BF16),
                pltpu.VMEM((hb, tq, LANES), F32),
                pltpu.VMEM((hb, tq, 2 * LANES), F32),
            ],
        ),
        out_shape=jax.ShapeDtypeStruct((heads, batch * lq, LANES), BF16),
        compiler_params=_params("parallel", "parallel", "arbitrary"),
        name="flash",
    )(iq_tab, ik_tab, flag_tab, qn, qr, k, krp, v)


def _s5_kernel(u_ref, wb_ref, scn_ref, wc_ref, d_ref, s0_ref, y_ref, st_ref, xs_ref, car_ref,
               *, ts, nkt, kw, sw, nt):
    it = pl.program_id(1)

    @pl.when(it == 0)
    def _load_state():
        car_ref[...] = s0_ref[0]

    for kt in range(nkt):
        ukt = u_ref[0, :, kt * kw:(kt + 1) * kw]
        xs_ref[...] = jnp.dot(ukt, wb_ref[kt], preferred_element_type=F32)
        st_sl = slice(kt * sw, (kt + 1) * sw)

        def blk(k, carry, kt=kt):
            cr, ci = carry
            rows = pl.ds(pl.multiple_of(k * SUBLANES, SUBLANES), SUBLANES)
            hr = xs_ref[rows, :sw]
            hi = xs_ref[rows, sw:]
            for c0, dist in ((0, 1), (2, 2), (4, 4)):
                ar = scn_ref[kt, c0]
                ai = scn_ref[kt, c0 + 1]
                sr = pltpu.roll(hr, dist, 0)
                si = pltpu.roll(hi, dist, 0)
                hr, hi = hr + ar * sr - ai * si, hi + ar * si + ai * sr
            pr = scn_ref[kt, 6]
            pi = scn_ref[kt, 7]
            crb = jnp.broadcast_to(cr, (SUBLANES, sw))
            cib = jnp.broadcast_to(ci, (SUBLANES, sw))
            hr, hi = hr + pr * crb - pi * cib, hi + pr * cib + pi * crb
            xs_ref[rows, :sw] = hr
            xs_ref[rows, sw:] = hi
            return hr[SUBLANES - 1:SUBLANES], hi[SUBLANES - 1:SUBLANES]

        cr, ci = lax.fori_loop(0, ts // SUBLANES, blk, (car_ref[0:1, st_sl], car_ref[1:2, st_sl]),
                               unroll=2 if ts >= 2 * SUBLANES else 1)
        car_ref[0:1, st_sl] = cr
        car_ref[1:2, st_sl] = ci
        y = jnp.dot(xs_ref[...].astype(BF16), wc_ref[kt], preferred_element_type=F32)
        y = y + d_ref[:, kt * kw:(kt + 1) * kw] * ukt.astype(F32)
        y_ref[0, :, kt * kw:(kt + 1) * kw] = _gelu(y).astype(BF16)

    @pl.when(it == nt - 1)
    def _store_state():
        st_ref[0] = car_ref[...]


def _s5(u, wb, scn, wc, dsk, s0):
    b, l, w = u.shape
    nkt, kw, sw2 = wb.shape
    sw = sw2 // 2
    ns = s0.shape[2]
    ts = _tile(l, 512)
    nt = l // ts
    const3 = lambda i, j: (0, 0, 0)
    return pl.pallas_call(
        functools.partial(_s5_kernel, ts=ts, nkt=nkt, kw=kw, sw=sw, nt=nt),
        grid=(b, nt),
        in_specs=[
            pl.BlockSpec((1, ts, w), lambda i, j: (i, j, 0)),
            pl.BlockSpec(wb.shape, const3),
            pl.BlockSpec(scn.shape, lambda i, j: (0, 0, 0, 0)),
            pl.BlockSpec(wc.shape, const3),
            pl.BlockSpec((1, w), lambda i, j: (0, 0)),
            pl.BlockSpec((1, 2, ns), lambda i, j: (i, 0, 0)),
        ],
        out_specs=[
            pl.BlockSpec((1, ts, w), lambda i, j: (i, j, 0)),
            pl.BlockSpec((1, 2, ns), lambda i, j: (i, 0, 0)),
        ],
        out_shape=[
            jax.ShapeDtypeStruct((b, l, w), BF16),
            jax.ShapeDtypeStruct((b, 2, ns), F32),
        ],
        scratch_shapes=[
            pltpu.VMEM((ts, sw2), F32),
            pltpu.VMEM((2, ns), F32),
        ],
        compiler_params=_params("parallel", "arbitrary"),
        name="s5",
    )(u, wb, scn, wc, dsk, s0)


def _s5_tables(a_re, a_im, log_dt, b_re, b_im, c_re, c_im, d_skip):
    g, p = a_re.shape
    ch = b_re.shape[2]
    gpt = MXU_DIM // ch
    nkt = g // gpt
    dt = jnp.exp(log_dt.astype(F32))[:, None]
    mag = jnp.exp(dt * a_re)
    ab_re = mag * jnp.cos(dt * a_im)
    ab_im = mag * jnp.sin(dt * a_im)
    den = a_re * a_re + a_im * a_im
    nr = ab_re - 1.0
    f_re = (nr * a_re + ab_im * a_im) / den
    f_im = (ab_im * a_re - nr * a_im) / den
    bb_re = f_re[..., None] * b_re - f_im[..., None] * b_im
    bb_im = f_re[..., None] * b_im + f_im[..., None] * b_re
    eye = jnp.eye(gpt, dtype=F32)

    def in_blockdiag(bb):
        return jnp.einsum("kgpi,gh->kgihp", bb.reshape(nkt, gpt, p, ch), eye).reshape(nkt, gpt * ch, gpt * p)

    def out_blockdiag(cc):
        return jnp.einsum("kgjp,gh->kgphj", cc.reshape(nkt, gpt, ch, p), eye).reshape(nkt, gpt * p, gpt * ch)

    wb = jnp.concatenate([in_blockdiag(bb_re), in_blockdiag(bb_im)], axis=2).astype(BF16)
    wc = jnp.concatenate([out_blockdiag(c_re), out_blockdiag(-c_im)], axis=1).astype(BF16)

    def cmul(x, y):
        return x[0] * y[0] - x[1] * y[1], x[0] * y[1] + x[1] * y[0]

    lam1 = (ab_re.reshape(nkt, gpt * p), ab_im.reshape(nkt, gpt * p))
    lam2 = cmul(lam1, lam1)
    lam4 = cmul(lam2, lam2)
    rows = jnp.arange(SUBLANES)[None, :, None]

    def shifted(lam, dist):
        return [jnp.where(rows >= dist, c[:, None, :], 0.0) for c in lam]

    pw = [lam1]
    for _ in range(SUBLANES - 1):
        pw.append(cmul(pw[-1], lam1))
    p_re = jnp.stack([c[0] for c in pw], axis=1)
    p_im = jnp.stack([c[1] for c in pw], axis=1)
    scn = jnp.stack(shifted(lam1, 1) + shifted(lam2, 2) + shifted(lam4, 4) + [p_re, p_im], axis=1)
    return wb, scn.astype(F32), wc, d_skip.reshape(1, g * ch).astype(F32)


def _merge_kernel(ys_ref, o_ref, wga_ref, wgb_ref, wo_ref, ga_ref, gb_ref, out_ref, *, heads):
    ys = ys_ref[...]
    ya = jnp.dot(ys, wga_ref[...], preferred_element_type=F32)
    ya = ya * _sigmoid(jnp.dot(ys, wgb_ref[...], preferred_element_type=F32))
    oc = jnp.concatenate([o_ref[hh] for hh in range(heads)], axis=1)
    yb = jnp.dot(oc, wo_ref[...], preferred_element_type=F32)
    out_ref[...] = (ga_ref[...].astype(F32) * ya + gb_ref[...].astype(F32) * yb).astype(BF16)


def _merge(ys, o, w_glu, w_o, gates):
    t, sw = ys.shape
    heads = o.shape[0]
    d = w_o.shape[1]
    tm = _tile(t, 512)
    tn = _tile(d, 512)
    nj = d // tn
    return pl.pallas_call(
        functools.partial(_merge_kernel, heads=heads),
        grid=(t // tm, nj),
        in_specs=[
            pl.BlockSpec((tm, sw), lambda i, j: (i, 0)),
            pl.BlockSpec((heads, tm, LANES), lambda i, j: (0, i, 0)),
            pl.BlockSpec((sw, tn), lambda i, j: (0, j)),
            pl.BlockSpec((sw, tn), lambda i, j: (0, nj + j)),
            pl.BlockSpec((heads * LANES, tn), lambda i, j: (0, j)),
            pl.BlockSpec((tm, tn), lambda i, j: (i, j)),
            pl.BlockSpec((tm, tn), lambda i, j: (i, nj + j)),
        ],
        out_specs=pl.BlockSpec((tm, tn), lambda i, j: (i, j)),
        out_shape=jax.ShapeDtypeStruct((t, d), BF16),
        compiler_params=_params("parallel", "parallel"),
        name="merge",
    )(ys, o, w_glu, w_glu, w_o, gates, gates)


def _resid_kernel(m_ref, w_ref, x_ref, gt_ref, g_ref, sc_ref, sh_ref, x1_ref, h2_ref):
    x1 = x_ref[0] + gt_ref[0] * jnp.dot(m_ref[0], w_ref[...], preferred_element_type=F32)
    x1_ref[0] = x1
    h2_ref[0] = (_rms(x1, g_ref[...]) * (1.0 + sc_ref[0]) + sh_ref[0]).astype(BF16)


def _resid(merged, w_out, x, gt, g2, sc, sh):
    b, l, d = x.shape
    tm = _tile(l, 256)
    tok = lambda i, j: (i, j, 0)
    per_b = lambda i, j: (i, 0, 0)
    return pl.pallas_call(
        _resid_kernel,
        grid=(b, l // tm),
        in_specs=[
            pl.BlockSpec((1, tm, d), tok),
            pl.BlockSpec((d, d), lambda i, j: (0, 0)),
            pl.BlockSpec((1, tm, d), tok),
            pl.BlockSpec((1, 1, d), per_b),
            pl.BlockSpec((1, d), lambda i, j: (0, 0)),
            pl.BlockSpec((1, 1, d), per_b),
            pl.BlockSpec((1, 1, d), per_b),
        ],
        out_specs=[pl.BlockSpec((1, tm, d), tok), pl.BlockSpec((1, tm, d), tok)],
        out_shape=[jax.ShapeDtypeStruct((b, l, d), F32), jax.ShapeDtypeStruct((b, l, d), BF16)],
        compiler_params=_params("parallel", "parallel"),
        name="resid",
    )(merged.reshape(b, l, d), w_out, x, gt, g2.reshape(1, d), sc, sh)


def _peer_topk_kernel(q_ref, k1_ref, k2_ref, flat_ref, invalid_ref, r2_ref, c1_ref, a1_ref, a2_ref,
                      *, nh, nkeys, topk):
    tb = q_ref.shape[1]
    iota_k = lax.broadcasted_iota(jnp.int32, (nkeys, tb), 0).astype(F32)
    iota_t = lax.broadcasted_iota(jnp.int32, (topk, tb), 0).astype(F32)
    flat = flat_ref[...]
    invalid = invalid_ref[...]
    nt_dims = (((1,), (1,)), ((), ()))

    n_cand = flat.shape[0]

    def extract(s, exact_ties):
        work = s
        rank = jnp.full((nkeys, tb), float(topk), F32)
        vals = []
        for a in range(topk):
            m = jnp.max(work, axis=0, keepdims=True)
            if exact_ties:
                idx = jnp.min(jnp.where(work == m, iota_k, float(nkeys)), axis=0, keepdims=True)
                sel = iota_k == idx
            else:
                sel = work == m
            rank = jnp.where(sel, float(a), rank)
            work = jnp.where(sel, REMOVED, work)
            vals.append(m)
        ranked = jnp.sum(jnp.where(rank < float(topk), 1.0, 0.0), axis=0, keepdims=True)
        return vals, rank, ranked

    def stack(vals):
        out = jnp.zeros((topk, tb), F32)
        for a in range(topk):
            out = jnp.where(iota_t == float(a), vals[a], out)
        return out

    def candidates(v1, v2):
        vs1 = stack(v1)
        vs2 = stack(v2)
        blocks = [v1[0] + vs2]
        for a in range(1, SUBLANES):
            blocks.append(v1[a] + vs2[:SUBLANES])
        blocks.append(vs1[SUBLANES:] + v2[0])
        return jnp.concatenate(blocks, axis=0) + invalid

    def route(s1, s2, exact_ties):
        v1, rank1, n1 = extract(s1, exact_ties)
        v2, rank2, n2 = extract(s2, exact_ties)
        cand = candidates(v1, v2)
        counts = jnp.zeros((topk, tb), F32)
        top = None
        zsum = None
        for kk in range(topk):
            m = jnp.max(cand, axis=0, keepdims=True)
            if exact_ties:
                f = jnp.min(jnp.where(cand == m, flat, 1e9), axis=0, keepdims=True)
                cand = jnp.where(flat == f, REMOVED, cand)
                counts = counts + jnp.where(iota_t == jnp.floor(f * (1.0 / topk)), 1.0, 0.0)
            else:
                cand = jnp.where(cand == m, REMOVED, cand)
            if kk == 0:
                top = m
                zsum = jnp.ones_like(m)
            else:
                zsum = zsum + jnp.exp(m - top)
        n3 = None
        if not exact_ties:
            gone = jnp.where(cand == REMOVED, 1.0, 0.0)
            n3 = jnp.sum(gone, axis=0, keepdims=True)
            per_rank = [jnp.sum(gone[:topk], axis=0, keepdims=True)]
            for a in range(1, SUBLANES):
                lo = topk + (a - 1) * SUBLANES
                per_rank.append(jnp.sum(gone[lo:lo + SUBLANES], axis=0, keepdims=True))
            counts = jnp.concatenate([stack(per_rank + [per_rank[0]] * (topk - SUBLANES))[:SUBLANES],
                                      gone[n_cand - SUBLANES:]], axis=0)
        c1 = jnp.zeros((nkeys, tb), F32)
        for a in range(topk):
            c1 = jnp.where(rank1 == float(a), counts[a:a + 1], c1)
        a1 = jnp.exp(s1 - v1[0]) * (1.0 / zsum)
        a2 = jnp.exp(s2 - v2[0])
        clean = None
        if not exact_ties:
            want = float(topk)
            bad = jnp.where(n1 != want, 1.0, 0.0) + jnp.where(n2 != want, 1.0, 0.0) + jnp.where(n3 != want, 1.0, 0.0)
            clean = jnp.max(bad) == 0.0
        return (rank2, c1, a1, a2), clean

    def store(hh, tables):
        r2_ref[hh], c1_ref[hh], a1_ref[hh], a2_ref[hh] = tables

    def pair(pp, carry):
        heads = (2 * pp, 2 * pp + 1)
        scores = []
        for hh in heads:
            scores.append((lax.dot_general(k1_ref[hh], q_ref[2 * hh], nt_dims, preferred_element_type=F32),
                           lax.dot_general(k2_ref[hh], q_ref[2 * hh + 1], nt_dims, preferred_element_type=F32)))
        quick = [route(s1, s2, False) for s1, s2 in scores]
        clean = jnp.logical_and(quick[0][1], quick[1][1])

        @pl.when(clean)
        def _no_ties():
            for hh, (tables, _) in zip(heads, quick):
                store(hh, tables)

        @pl.when(jnp.logical_not(clean))
        def _ties():
            for hh, (s1, s2) in zip(heads, scores):
                store(hh, route(s1, s2, True)[0])

        return carry

    lax.fori_loop(0, nh // 2, pair, 0)


def _peer_topk(q, k1, k2, topk):
    nh, nkeys, half = k1.shape
    t = q.shape[1]
    tb = LANES
    assert topk == 2 * SUBLANES and half == LANES and t % tb == 0
    rows = jnp.arange(topk + (SUBLANES - 1) * SUBLANES + SUBLANES)
    a_idx = jnp.where(rows < topk, 0, jnp.where(rows < topk + (SUBLANES - 1) * SUBLANES,
                                                1 + (rows - topk) // SUBLANES, SUBLANES + (rows - topk - (SUBLANES - 1) * SUBLANES)))
    b_idx = jnp.where(rows < topk, rows, jnp.where(rows < topk + (SUBLANES - 1) * SUBLANES, (rows - topk) % SUBLANES, 0))
    flat = jnp.broadcast_to((a_idx * topk + b_idx).astype(F32)[:, None], (rows.shape[0], tb))
    invalid = jnp.broadcast_to(jnp.where((a_idx + 1) * (b_idx + 1) <= topk, 0.0, EXCLUDED).astype(F32)[:, None],
                               (rows.shape[0], tb))
    out = lambda dt: jax.ShapeDtypeStruct((nh, nkeys, t), dt)
    ospec = pl.BlockSpec((nh, nkeys, tb), lambda i: (0, 0, i))
    return pl.pallas_call(
        functools.partial(_peer_topk_kernel, nh=nh, nkeys=nkeys, topk=topk),
        grid=(t // tb,),
        in_specs=[
            pl.BlockSpec((2 * nh, tb, LANES), lambda i: (0, i, 0)),
            pl.BlockSpec(k1.shape, lambda i: (0, 0, 0)),
            pl.BlockSpec(k2.shape, lambda i: (0, 0, 0)),
            pl.BlockSpec(flat.shape, lambda i: (0, 0)),
            pl.BlockSpec(invalid.shape, lambda i: (0, 0)),
        ],
        out_specs=[ospec] * 4,
        out_shape=[out(F32)] * 4,
        compiler_params=_params("parallel"),
        name="peer_topk",
    )(q, k1, k2, flat, invalid)


def _peer_mix_kernel(h_ref, u_ref, vt_ref, r2_ref, c1_ref, a1_ref, a2_ref, *rest, nh, nkeys, ni, ne, norm):
    if len(rest) == 4:
        o_ref, ht_scr, w_scr, acc_scr = rest
        x_ref = gt_ref = g_ref = None
    else:
        x_ref, gt_ref, g_ref, o_ref, ht_scr, w_scr, acc_scr = rest
    e = pl.program_id(1)
    tb = h_ref.shape[0]

    @pl.when(e == 0)
    def _init():
        acc_scr[...] = jnp.zeros(acc_scr.shape, F32)

    ht_scr[...] = lax.dot_general(u_ref[...], h_ref[...], (((1,), (1,)), ((), ())), preferred_element_type=F32)
    first_keys = pl.ds(pl.multiple_of(e * ni, SUBLANES), ni)
    for il in range(ni):
        rows = slice(il * nkeys, (il + 1) * nkeys)
        for lg in range(tb // LANES):
            sl = slice(lg * LANES, (lg + 1) * LANES)
            gate = jnp.zeros((nkeys, LANES), F32)
            for hh in range(nh):
                partners = c1_ref[hh, first_keys, sl][il:il + 1]
                first = a1_ref[hh, first_keys, sl][il:il + 1]
                gate = gate + jnp.where(r2_ref[hh, :, sl] < partners, a2_ref[hh, :, sl], 0.0) * first
            w_scr[rows, sl] = (gate * _gelu(ht_scr[rows, sl])).astype(BF16)
    acc_scr[...] += jnp.dot(vt_ref[...], w_scr[...], preferred_element_type=F32)

    @pl.when(e == ne - 1)
    def _done():
        mix = acc_scr[...].T
        if x_ref is None:
            o_ref[...] = mix
        else:
            y = x_ref[0] + gt_ref[0] * mix
            o_ref[0] = _rms(y, g_ref[...]) if norm else y


def _peer_mix(h2, u_tab, vt_tab, r2, c1, a1, a2, x1, gt, g, norm):
    b, l, _ = x1.shape
    t, d = h2.shape
    nh, nkeys, _ = r2.shape
    n_exp = u_tab.shape[0]
    tb = _tile(t, 512)
    ni = SUBLANES
    eb = ni * nkeys
    ne = n_exp // eb
    aux = pl.BlockSpec((nh, nkeys, tb), lambda i, e: (0, 0, i), pipeline_mode=pl.Buffered(1))
    in_specs = [
        pl.BlockSpec((tb, d), lambda i, e: (i, 0)),
        pl.BlockSpec((eb, d), lambda i, e: (e, 0)),
        pl.BlockSpec((d, eb), lambda i, e: (0, e)),
        aux, aux, aux, aux,
    ]
    args = [h2, u_tab, vt_tab, r2, c1, a1, a2]
    fused = l % tb == 0
    if fused:
        per_seq = l // tb
        in_specs += [
            pl.BlockSpec((1, tb, d), lambda i, e: (i // per_seq, i % per_seq, 0), pipeline_mode=pl.Buffered(1)),
            pl.BlockSpec((1, 1, d), lambda i, e: (i // per_seq, 0, 0)),
            pl.BlockSpec((1, d), lambda i, e: (0, 0)),
        ]
        args += [x1, gt, g.reshape(1, d)]
        out_spec = pl.BlockSpec((1, tb, d), lambda i, e: (i // per_seq, i % per_seq, 0))
        out_shape = jax.ShapeDtypeStruct((b, l, d), F32)
    else:
        out_spec = pl.BlockSpec((tb, d), lambda i, e: (i, 0))
        out_shape = jax.ShapeDtypeStruct((t, d), F32)
    out = pl.pallas_call(
        functools.partial(_peer_mix_kernel, nh=nh, nkeys=nkeys, ni=ni, ne=ne, norm=norm),
        grid=(t // tb, ne),
        in_specs=in_specs,
        out_specs=out_spec,
        out_shape=out_shape,
        scratch_shapes=[
            pltpu.VMEM((eb, tb), F32),
            pltpu.VMEM((eb, tb), BF16),
            pltpu.VMEM((d, tb), F32),
        ],
        compiler_params=_params("parallel", "arbitrary"),
        name="peer_mix",
    )(*args)
    return out if fused else _final(x1, out, gt, g, norm)


def _final_kernel(x_ref, p_ref, gt_ref, g_ref, y_ref, *, norm):
    y = x_ref[0] + gt_ref[0] * p_ref[0]
    y_ref[0] = _rms(y, g_ref[...]) if norm else y


def _final(x1, peer, gt, g, norm):
    b, l, d = x1.shape
    tm = _tile(l, 512)
    tok = lambda i, j: (i, j, 0)
    return pl.pallas_call(
        functools.partial(_final_kernel, norm=norm),
        grid=(b, l // tm),
        in_specs=[
            pl.BlockSpec((1, tm, d), tok),
            pl.BlockSpec((1, tm, d), tok),
            pl.BlockSpec((1, 1, d), lambda i, j: (i, 0, 0)),
            pl.BlockSpec((1, d), lambda i, j: (0, 0)),
        ],
        out_specs=pl.BlockSpec((1, tm, d), tok),
        out_shape=jax.ShapeDtypeStruct((b, l, d), F32),
        compiler_params=_params("parallel", "parallel"),
        name="final",
    )(x1, peer.reshape(b, l, d), gt, g.reshape(1, d))


def _rope_tables(pos, rope):
    half = rope // 2
    inv = jnp.power(ROPE_THETA, -jnp.arange(half, dtype=F32) / half)
    ang = pos.astype(F32)[:, None] * inv
    reps = LANES // half
    return jnp.tile(jnp.cos(ang), (1, reps)), jnp.tile(jnp.sin(ang), (1, reps))


def _prep_weights(p, dims):
    d, sw, q_lora, kv_lora, rope, heads, nope = (dims[k] for k in ("d", "sw", "q_lora", "kv_lora", "rope", "heads", "nope"))
    half = rope // 2
    w_in = p["w_in"]
    off_q = sw
    off_kv = off_q + q_lora
    off_kr = off_kv + kv_lora
    off_g = off_kr + rope
    kr_w = w_in[:, off_kr:off_g]
    kr_rot = jnp.concatenate([-kr_w[:, half:], kr_w[:, :half]], axis=1)
    pad = jnp.zeros((d, LANES - rope), F32)
    w_mla = jnp.concatenate([w_in[:, off_q:off_kr], kr_w, pad, kr_rot, pad], axis=1).astype(BF16)
    wq = p["w_qu"].reshape(q_lora, heads, nope + rope)
    wq_n = wq[:, :, :nope].reshape(q_lora, heads * nope)
    wq_r = wq[:, :, nope:]
    wq_rot = jnp.concatenate([-wq_r[:, :, half:], wq_r[:, :, :half]], axis=2)
    hpad = jnp.zeros((q_lora, heads, LANES - rope), F32)
    wq_a = jnp.concatenate([wq_r, hpad], axis=2).reshape(q_lora, heads * LANES)
    wq_b = jnp.concatenate([wq_rot, hpad], axis=2).reshape(q_lora, heads * LANES)
    return {
        "w_gates": w_in[:, off_g:].astype(BF16),
        "w_u": w_in[:, :sw].astype(BF16),
        "w_mla": w_mla,
        "w_q": jnp.concatenate([wq_n, wq_a, wq_b], axis=1).astype(BF16),
        "w_kv": jnp.concatenate([p["w_uk"].reshape(kv_lora, heads * nope),
                                 p["w_uv"].reshape(kv_lora, heads * dims["v_dim"])], axis=1).astype(BF16),
        "w_o": p["w_o"].astype(BF16),
        "w_glu": p["w_glu"].astype(BF16),
        "w_out": p["w_out"].astype(BF16),
        "peer_wq": p["peer_wq"].astype(BF16),
        "peer_k1": p["peer_k1"].astype(BF16),
        "peer_k2": p["peer_k2"].astype(BF16),
        "peer_u": p["peer_u"].astype(BF16),
        "peer_vt": p["peer_v"].T.astype(BF16),
        "s5": _s5_tables(p["ssm_a_re"], p["ssm_a_im"], p["ssm_log_dt"], p["ssm_b_re"], p["ssm_b_im"],
                         p["ssm_c_re"], p["ssm_c_im"], p["ssm_d"]),
    }


def _layer(x, mod, past_ckv, past_kr, s0, p, w, dims, g_final, last):
    b, l, d = x.shape
    t = b * l
    heads, rope, kv_lora, topk = dims["heads"], dims["rope"], dims["kv_lora"], dims["topk"]
    sh1, sc1, gt1, sh2, sc2, gt2 = mod
    past = 0 if past_ckv is None else past_ckv.shape[1]

    h = _normmod(x, p["g_norm1"], sc1, sh1).reshape(t, d)
    gates = _mm(h, w["w_gates"], BF16, act="sigmoid")
    u = _mm(h, w["w_u"], BF16)
    cos, sin = _rope_tables(past + jnp.arange(l, dtype=jnp.int32), rope)
    qn, qr, ckv, ckv_b, kr, krp = _mla_proj(
        h, w["w_mla"], p["g_q"], p["g_kv"], w["w_q"], cos, sin, l,
        heads=heads, q_lora=dims["q_lora"], kv_lora=kv_lora, rope=rope, scale=dims["scale"])

    ys, s_fin = _s5(u.reshape(b, l, -1), *w["s5"], s0)

    if past_ckv is None:
        lk, keys_c, keys_r = l, ckv_b, krp
        tq = tk = _tile(l, 512)
    else:
        n_keys = past + l
        lk = -(-n_keys // LANES) * LANES
        keys_c = jnp.concatenate([past_ckv.astype(BF16), ckv_b.reshape(b, l, kv_lora)], axis=1)
        keys_c = jnp.pad(keys_c, ((0, 0), (0, lk - n_keys), (0, 0))).reshape(b * lk, kv_lora)
        past_r = jnp.pad(past_kr.astype(BF16), ((0, 0), (0, 0), (0, LANES - rope)))
        keys_r = jnp.concatenate([past_r, krp.reshape(b, l, LANES)], axis=1)
        keys_r = jnp.pad(keys_r, ((0, 0), (0, lk - n_keys), (0, 0))).reshape(b * lk, LANES)
        tq, tk = l, lk
    kh, vh = _kvup(keys_c, w["w_kv"], heads)
    o = _flash(qn, qr, kh, keys_r, vh, batch=b, lq=l, lk=lk, q_pos0=past, n_keys=past + l,
               tq=tq, tk=tk, hb=heads)

    merged = _merge(ys.reshape(t, -1), o, w["w_glu"], w["w_o"], gates)
    x1, h2 = _resid(merged, w["w_out"], x, gt1, p["g_norm2"], sc2, sh2)

    h2 = h2.reshape(t, d)
    q = _mm_split(h2, w["peer_wq"], BF16)
    r2, c1, a1, a2 = _peer_topk(q, w["peer_k1"], w["peer_k2"], topk)
    x2 = _peer_mix(h2, w["peer_u"], w["peer_vt"], r2, c1, a1, a2, x1, gt2, g_final, last)
    return x2, ckv.reshape(b, l, kv_lora), kr.reshape(b, l, rope), s_fin


def kernel(x_prompt, x_sample, c_prompt, c_sample, cache_ckv, cache_krope, state_ssm_re, state_ssm_im, w_ada, b_ada, g_norm1, g_norm2, w_in, g_q, w_qu, g_kv, w_uk, w_uv, w_o, ssm_a_re, ssm_a_im, ssm_log_dt, ssm_b_re, ssm_b_im, ssm_c_re, ssm_c_im, ssm_d, w_glu, w_out, peer_wq, peer_k1, peer_k2, peer_u, peer_v, g_final):
    depth = w_in.shape[0]
    bp, lp, d = x_prompt.shape
    bs, ls, _ = x_sample.shape
    groups, states = ssm_a_re.shape[1:]
    heads, nope = w_uk.shape[2:]
    rope = cache_krope.shape[-1]
    dims = {
        "d": d, "sw": groups * ssm_b_re.shape[3], "q_lora": g_q.shape[1], "kv_lora": g_kv.shape[1],
        "rope": rope, "heads": heads, "nope": nope, "v_dim": w_uv.shape[3],
        "scale": math.log2(math.e) / math.sqrt(nope + rope), "topk": 16,
    }
    assert nope == LANES and dims["v_dim"] == LANES and rope <= LANES

    xp, xs = x_prompt, x_sample
    nb = bp + bs
    rows = -(-nb // 16) * 16
    c_all = jnp.pad(jnp.concatenate([c_prompt, c_sample], axis=0), ((0, rows - nb), (0, 0)))
    zeros = jnp.zeros((bp, 2, groups * states), F32)
    outs_p, outs_s = [], []
    for layer in range(depth):
        p = {
            "g_norm1": g_norm1[layer], "g_norm2": g_norm2[layer], "w_in": w_in[layer], "g_q": g_q[layer],
            "w_qu": w_qu[layer], "g_kv": g_kv[layer], "w_uk": w_uk[layer], "w_uv": w_uv[layer], "w_o": w_o[layer],
            "ssm_a_re": ssm_a_re[layer], "ssm_a_im": ssm_a_im[layer], "ssm_log_dt": ssm_log_dt[layer],
            "ssm_b_re": ssm_b_re[layer], "ssm_b_im": ssm_b_im[layer], "ssm_c_re": ssm_c_re[layer],
            "ssm_c_im": ssm_c_im[layer], "ssm_d": ssm_d[layer], "w_glu": w_glu[layer], "w_out": w_out[layer],
            "peer_wq": peer_wq[layer], "peer_k1": peer_k1[layer], "peer_k2": peer_k2[layer],
            "peer_u": peer_u[layer], "peer_v": peer_v[layer],
        }
        w = _prep_weights(p, dims)
        mod = _ada(c_all, w_ada[layer], b_ada[layer])
        mod_p = [m.reshape(bp, 1, d) for m in jnp.split(mod[:bp], 6, axis=-1)]
        mod_s = [m.reshape(bs, 1, d) for m in jnp.split(mod[bp:nb], 6, axis=-1)]
        s0_s = jnp.stack([state_ssm_re[layer].reshape(bs, -1), state_ssm_im[layer].reshape(bs, -1)], axis=1)
        last = layer == depth - 1
        xp, *res_p = _layer(xp, mod_p, None, None, zeros, p, w, dims, g_final, last)
        xs, *res_s = _layer(xs, mod_s, cache_ckv[layer], cache_krope[layer], s0_s, p, w, dims, g_final, last)
        for res, outs, g in ((res_p, outs_p, bp), (res_s, outs_s, bs)):
            ckv, kr, s_fin = res
            outs.append((ckv, kr, s_fin[:, 0].reshape(g, groups, states), s_fin[:, 1].reshape(g, groups, states)))
    stack = lambda outs, k: jnp.stack([o[k] for o in outs])
    return (xp, xs,
            stack(outs_p, 0), stack(outs_p, 1), stack(outs_p, 2), stack(outs_p, 3),
            stack(outs_s, 0), stack(outs_s, 1), stack(outs_s, 2), stack(outs_s, 3))
```

```python
import functools
import math

import jax
import jax.numpy as jnp
from jax import lax
from jax.experimental import pallas as pl
from jax.experimental.pallas import tpu as pltpu

F32 = jnp.float32
BF16 = jnp.bfloat16

EPS = 1e-6
CHUNK = 64
ROPE_THETA = 10000.0
LANES = 128
SUBLANES = 8
MXU_DIM = 256
VMEM_LIMIT_BYTES = 56 * 1024 * 1024
MASKED = -1e30
REMOVED = -3e38
EXCLUDED = -1e38


def _params(*semantics):
    return pltpu.CompilerParams(dimension_semantics=semantics, vmem_limit_bytes=VMEM_LIMIT_BYTES)


def _tile(n, pref):
    if n <= pref:
        return n
    t = pref
    while n % t:
        t //= 2
    assert t >= SUBLANES, (n, pref)
    return t


def _sigmoid(x):
    return 1.0 / (1.0 + jnp.exp(-x))


def _gelu(x):
    return 0.5 * x * (1.0 + jnp.tanh(math.sqrt(2.0 / math.pi) * (x + 0.044715 * (x * x * x))))


def _rms(x, g):
    return x * lax.rsqrt(jnp.mean(x * x, axis=-1, keepdims=True) + EPS) * g


def _ada_kernel(c_ref, w_ref, b_ref, o_ref):
    c = c_ref[...]
    a = (c * _sigmoid(c)).astype(BF16)
    o_ref[...] = jnp.dot(a, w_ref[...].astype(BF16), preferred_element_type=F32) + b_ref[...]


def _ada(c, w, b):
    rows, d = c.shape
    n = w.shape[1]
    tn = _tile(n, 1024)
    return pl.pallas_call(
        _ada_kernel,
        grid=(n // tn,),
        in_specs=[
            pl.BlockSpec((rows, d), lambda j: (0, 0)),
            pl.BlockSpec((d, tn), lambda j: (0, j)),
            pl.BlockSpec((1, tn), lambda j: (0, j)),
        ],
        out_specs=pl.BlockSpec((rows, tn), lambda j: (0, j)),
        out_shape=jax.ShapeDtypeStruct((rows, n), F32),
        compiler_params=_params("parallel"),
        name="ada",
    )(c, w, b.reshape(1, n))


def _normmod_kernel(x_ref, g_ref, sc_ref, sh_ref, o_ref):
    h = _rms(x_ref[0], g_ref[...]) * (1.0 + sc_ref[0]) + sh_ref[0]
    o_ref[0] = h.astype(BF16)


def _normmod(x, g, sc, sh):
    b, l, d = x.shape
    tm = _tile(l, 512)
    return pl.pallas_call(
        _normmod_kernel,
        grid=(b, l // tm),
        in_specs=[
            pl.BlockSpec((1, tm, d), lambda i, j: (i, j, 0)),
            pl.BlockSpec((1, d), lambda i, j: (0, 0)),
            pl.BlockSpec((1, 1, d), lambda i, j: (i, 0, 0)),
            pl.BlockSpec((1, 1, d), lambda i, j: (i, 0, 0)),
        ],
        out_specs=pl.BlockSpec((1, tm, d), lambda i, j: (i, j, 0)),
        out_shape=jax.ShapeDtypeStruct((b, l, d), BF16),
        compiler_params=_params("parallel", "parallel"),
        name="normmod",
    )(x, g.reshape(1, d), sc, sh)


def _mm_kernel(a_ref, w_ref, o_ref, *, act):
    z = jnp.dot(a_ref[...], w_ref[...], preferred_element_type=F32)
    if act == "sigmoid":
        z = _sigmoid(z)
    o_ref[...] = z.astype(o_ref.dtype)


def _mm(a, w, out_dtype, act=None, tn_pref=1024):
    t, k = a.shape
    n = w.shape[1]
    tm = _tile(t, 512)
    tn = _tile(n, tn_pref)
    return pl.pallas_call(
        functools.partial(_mm_kernel, act=act),
        grid=(t // tm, n // tn),
        in_specs=[
            pl.BlockSpec((tm, k), lambda i, j: (i, 0)),
            pl.BlockSpec((k, tn), lambda i, j: (0, j)),
        ],
        out_specs=pl.BlockSpec((tm, tn), lambda i, j: (i, j)),
        out_shape=jax.ShapeDtypeStruct((t, n), out_dtype),
        compiler_params=_params("parallel", "parallel"),
        name="mm",
    )(a, w)


def _mm_split_kernel(a_ref, w_ref, o_ref, *, parts):
    z = jnp.dot(a_ref[...], w_ref[...], preferred_element_type=F32)
    for p in range(parts):
        o_ref[p] = z[:, p * LANES:(p + 1) * LANES].astype(o_ref.dtype)


def _mm_split(a, w, out_dtype):
    t, k = a.shape
    n = w.shape[1]
    tm = _tile(t, 512)
    tn = _tile(n, 1024)
    parts = tn // LANES
    return pl.pallas_call(
        functools.partial(_mm_split_kernel, parts=parts),
        grid=(t // tm, n // tn),
        in_specs=[
            pl.BlockSpec((tm, k), lambda i, j: (i, 0)),
            pl.BlockSpec((k, tn), lambda i, j: (0, j)),
        ],
        out_specs=pl.BlockSpec((parts, tm, LANES), lambda i, j: (j, i, 0)),
        out_shape=jax.ShapeDtypeStruct((n // LANES, t, LANES), out_dtype),
        compiler_params=_params("parallel", "parallel"),
        name="mm_split",
    )(a, w)


def _mla_proj_kernel(h_ref, wm_ref, gq_ref, gkv_ref, wq_ref, cos_ref, sin_ref,
                     qn_ref, qr_ref, ckv_ref, ckvb_ref, kr_ref, krp_ref,
                     *, heads, q_lora, kv_lora, rope, scale):
    z = jnp.dot(h_ref[...], wm_ref[...], preferred_element_type=F32)
    cos = cos_ref[...]
    sin = sin_ref[...]
    off = q_lora + kv_lora
    krp = z[:, off:off + LANES] * cos + z[:, off + LANES:off + 2 * LANES] * sin
    kr_ref[...] = krp[:, :rope]
    krp_ref[...] = krp.astype(BF16)
    ckv = _rms(z[:, q_lora:off], gkv_ref[...])
    ckv_ref[...] = ckv
    ckvb_ref[...] = ckv.astype(BF16)
    qd = _rms(z[:, :q_lora], gq_ref[...]).astype(BF16)
    zq = jnp.dot(qd, wq_ref[...], preferred_element_type=F32)
    hn = heads * LANES
    for hh in range(heads):
        lo = hh * LANES
        qn_ref[hh] = (zq[:, lo:lo + LANES] * scale).astype(BF16)
        qr = zq[:, hn + lo:hn + lo + LANES] * cos + zq[:, 2 * hn + lo:2 * hn + lo + LANES] * sin
        qr_ref[hh] = (qr * scale).astype(BF16)


def _mla_proj(h, wm, gq, gkv, wq, cos, sin, seq, *, heads, q_lora, kv_lora, rope, scale):
    t, d = h.shape
    tm = _tile(t, 256)
    if seq % tm == 0:
        nrep = seq // tm
        tab_map = lambda i: (i % nrep, 0)
    else:
        assert tm % seq == 0
        cos = jnp.tile(cos, (tm // seq, 1))
        sin = jnp.tile(sin, (tm // seq, 1))
        tab_map = lambda i: (0, 0)
    nm = wm.shape[1]
    nq = wq.shape[1]
    row = lambda i: (i, 0)
    const = lambda i: (0, 0)
    return pl.pallas_call(
        functools.partial(_mla_proj_kernel, heads=heads, q_lora=q_lora, kv_lora=kv_lora, rope=rope, scale=scale),
        grid=(t // tm,),
        in_specs=[
            pl.BlockSpec((tm, d), row),
            pl.BlockSpec((d, nm), const),
            pl.BlockSpec((1, q_lora), const),
            pl.BlockSpec((1, kv_lora), const),
            pl.BlockSpec((q_lora, nq), const),
            pl.BlockSpec((tm, LANES), tab_map),
            pl.BlockSpec((tm, LANES), tab_map),
        ],
        out_specs=[
            pl.BlockSpec((heads, tm, LANES), lambda i: (0, i, 0)),
            pl.BlockSpec((heads, tm, LANES), lambda i: (0, i, 0)),
            pl.BlockSpec((tm, kv_lora), row),
            pl.BlockSpec((tm, kv_lora), row),
            pl.BlockSpec((tm, rope), row),
            pl.BlockSpec((tm, LANES), row),
        ],
        out_shape=[
            jax.ShapeDtypeStruct((heads, t, LANES), BF16),
            jax.ShapeDtypeStruct((heads, t, LANES), BF16),
            jax.ShapeDtypeStruct((t, kv_lora), F32),
            jax.ShapeDtypeStruct((t, kv_lora), BF16),
            jax.ShapeDtypeStruct((t, rope), F32),
            jax.ShapeDtypeStruct((t, LANES), BF16),
        ],
        compiler_params=_params("parallel"),
        name="mla_proj",
    )(h, wm, gq.reshape(1, q_lora), gkv.reshape(1, kv_lora), wq, cos, sin)


def _kvup_kernel(c_ref, w_ref, k_ref, v_ref, *, heads):
    z = jnp.dot(c_ref[...], w_ref[...], preferred_element_type=F32)
    for hh in range(heads):
        k_ref[hh] = z[:, hh * LANES:(hh + 1) * LANES].astype(BF16)
        v_ref[hh] = z[:, (heads + hh) * LANES:(heads + hh + 1) * LANES].astype(BF16)


def _kvup(ckv, w, heads):
    t, c = ckv.shape
    tm = _tile(t, 512)
    if t % tm:
        tm = t
    return pl.pallas_call(
        functools.partial(_kvup_kernel, heads=heads),
        grid=(t // tm,),
        in_specs=[
            pl.BlockSpec((tm, c), lambda i: (i, 0)),
            pl.BlockSpec(w.shape, lambda i: (0, 0)),
        ],
        out_specs=[
            pl.BlockSpec((heads, tm, LANES), lambda i: (0, i, 0)),
            pl.BlockSpec((heads, tm, LANES), lambda i: (0, i, 0)),
        ],
        out_shape=[jax.ShapeDtypeStruct((heads, t, LANES), BF16)] * 2,
        compiler_params=_params("parallel"),
        name="kvup",
    )(ckv, w)


def _flash_kernel(iq_tab, ik_tab, flag_tab, qn_ref, qr_ref, k_ref, kr_ref, v_ref, o_ref,
                  qc_scr, m_scr, acc_scr, *, hb, tq, tk, q_pos0, n_keys):
    pair = pl.program_id(2)
    iq = iq_tab[pair]
    ik = ik_tab[pair]
    flags = flag_tab[pair]
    first = (flags & 1) != 0
    last = (flags & 2) != 0
    full = (flags & 4) != 0

    @pl.when(first)
    def _init():
        m_scr[...] = jnp.full(m_scr.shape, MASKED, F32)
        acc_scr[...] = jnp.zeros(acc_scr.shape, F32)
        for hh in range(hb):
            qc_scr[hh] = jnp.concatenate([qn_ref[hh], qr_ref[hh]], axis=1)

    def step(masked):
        kr = kr_ref[...]
        ones = jnp.ones((tk, LANES), BF16)
        if masked:
            qp = q_pos0 + iq * tq + lax.broadcasted_iota(jnp.int32, (tq, tk), 0)
            kp = ik * tk + lax.broadcasted_iota(jnp.int32, (tq, tk), 1)
            allowed = jnp.logical_and(kp // CHUNK <= qp // CHUNK, kp < n_keys)

        def head(hh, carry):
            kc = jnp.concatenate([k_ref[hh], kr], axis=1)
            s = lax.dot_general(qc_scr[hh], kc, (((1,), (1,)), ((), ())), preferred_element_type=F32)
            if masked:
                s = jnp.where(allowed, s, MASKED)
            m_prev = m_scr[hh]
            m_new = jnp.maximum(m_prev, jnp.max(s, axis=1, keepdims=True))
            alpha = jnp.exp2(m_prev - m_new)
            p = jnp.exp2(s - pltpu.repeat(m_new, tk // LANES, axis=1)).astype(BF16)
            vc = jnp.concatenate([v_ref[hh], ones], axis=1)
            acc_scr[hh] = pltpu.repeat(alpha, 2, axis=1) * acc_scr[hh] + jnp.dot(p, vc, preferred_element_type=F32)
            m_scr[hh] = m_new
            return carry

        lax.fori_loop(0, hb, head, 0, unroll=True)

    @pl.when(full)
    def _full():
        step(False)

    @pl.when(jnp.logical_not(full))
    def _diag():
        step(True)

    @pl.when(last)
    def _done():
        for hh in range(hb):
            acc = acc_scr[hh]
            o_ref[hh] = (acc[:, :LANES] / acc[:, LANES:]).astype(BF16)


def _flash_pairs(lq, lk, tq, tk, q_pos0, n_keys):
    iqs, iks, flags = [], [], []
    for iq in range(lq // tq):
        q_first = q_pos0 + iq * tq
        last_key = min(n_keys - 1, ((q_first + tq - 1) // CHUNK) * CHUNK + CHUNK - 1)
        full_key = min(n_keys - 1, (q_first // CHUNK) * CHUNK + CHUNK - 1)
        n_blocks = last_key // tk + 1
        for ik in range(n_blocks):
            full = (ik + 1) * tk - 1 <= full_key
            iqs.append(iq)
            iks.append(ik)
            flags.append((1 if ik == 0 else 0) | (2 if ik == n_blocks - 1 else 0) | (4 if full else 0))
    as_i32 = lambda xs: jnp.asarray(xs, jnp.int32)
    return as_i32(iqs), as_i32(iks), as_i32(flags)


def _flash(qn, qr, k, krp, v, *, batch, lq, lk, q_pos0, n_keys, tq, tk, hb):
    heads = qn.shape[0]
    nq = lq // tq
    nk = lk // tk
    assert lq % tq == 0 and lk % tk == 0 and heads % hb == 0 and tk % LANES == 0
    iq_tab, ik_tab, flag_tab = _flash_pairs(lq, lk, tq, tk, q_pos0, n_keys)
    qmap = lambda b, h, p, iqt, ikt, ft: (h, b * nq + iqt[p], 0)
    kmap = lambda b, h, p, iqt, ikt, ft: (h, b * nk + ikt[p], 0)
    return pl.pallas_call(
        functools.partial(_flash_kernel, hb=hb, tq=tq, tk=tk, q_pos0=q_pos0, n_keys=n_keys),
        grid_spec=pltpu.PrefetchScalarGridSpec(
            num_scalar_prefetch=3,
            grid=(batch, heads // hb, iq_tab.shape[0]),
            in_specs=[
                pl.BlockSpec((hb, tq, LANES), qmap),
                pl.BlockSpec((hb, tq, LANES), qmap),
                pl.BlockSpec((hb, tk, LANES), kmap),
                pl.BlockSpec((tk, LANES), lambda b, h, p, iqt, ikt, ft: (b * nk + ikt[p], 0)),
                pl.BlockSpec((hb, tk, LANES), kmap),
            ],
            out_specs=pl.BlockSpec((hb, tq, LANES), qmap),
            scratch_shapes=[
                pltpu.VMEM((hb, tq, 2 * LANES), BF16),
                pltpu.VMEM((hb, tq, LANES), F32),
                pltpu.VMEM((hb, tq, 2 * LANES), F32),
            ],
        ),
        out_shape=jax.ShapeDtypeStruct((heads, batch * lq, LANES), BF16),
        compiler_params=_params("parallel", "parallel", "arbitrary"),
        name="flash",
    )(iq_tab, ik_tab, flag_tab, qn, qr, k, krp, v)


def _s5_kernel(u_ref, wb_ref, scn_ref, wc_ref, d_ref, s0_ref, y_ref, st_ref, xs_ref, car_ref,
               *, ts, nkt, kw, sw, nt):
    it = pl.program_id(1)

    @pl.when(it == 0)
    def _load_state():
        car_ref[...] = s0_ref[0]

    for kt in range(nkt):
        ukt = u_ref[0, :, kt * kw:(kt + 1) * kw]
        xs_ref[...] = jnp.dot(ukt, wb_ref[kt], preferred_element_type=F32)
        st_sl = slice(kt * sw, (kt + 1) * sw)

        def blk(k, carry, kt=kt):
            cr, ci = carry
            rows = pl.ds(pl.multiple_of(k * SUBLANES, SUBLANES), SUBLANES)
            hr = xs_ref[rows, :sw]
            hi = xs_ref[rows, sw:]
            for c0, dist in ((0, 1), (2, 2), (4, 4)):
                ar = scn_ref[kt, c0]
                ai = scn_ref[kt, c0 + 1]
                sr = pltpu.roll(hr, dist, 0)
                si = pltpu.roll(hi, dist, 0)
                hr, hi = hr + ar * sr - ai * si, hi + ar * si + ai * sr
            pr = scn_ref[kt, 6]
            pi = scn_ref[kt, 7]
            crb = jnp.broadcast_to(cr, (SUBLANES, sw))
            cib = jnp.broadcast_to(ci, (SUBLANES, sw))
            hr, hi = hr + pr * crb - pi * cib, hi + pr * cib + pi * crb
            xs_ref[rows, :sw] = hr
            xs_ref[rows, sw:] = hi
            return hr[SUBLANES - 1:SUBLANES], hi[SUBLANES - 1:SUBLANES]

        cr, ci = lax.fori_loop(0, ts // SUBLANES, blk, (car_ref[0:1, st_sl], car_ref[1:2, st_sl]),
                               unroll=2 if ts >= 2 * SUBLANES else 1)
        car_ref[0:1, st_sl] = cr
        car_ref[1:2, st_sl] = ci
        y = jnp.dot(xs_ref[...].astype(BF16), wc_ref[kt], preferred_element_type=F32)
        y = y + d_ref[:, kt * kw:(kt + 1) * kw] * ukt.astype(F32)
        y_ref[0, :, kt * kw:(kt + 1) * kw] = _gelu(y).astype(BF16)

    @pl.when(it == nt - 1)
    def _store_state():
        st_ref[0] = car_ref[...]


def _s5(u, wb, scn, wc, dsk, s0):
    b, l, w = u.shape
    nkt, kw, sw2 = wb.shape
    sw = sw2 // 2
    ns = s0.shape[2]
    ts = _tile(l, 512)
    nt = l // ts
    const3 = lambda i, j: (0, 0, 0)
    return pl.pallas_call(
        functools.partial(_s5_kernel, ts=ts, nkt=nkt, kw=kw, sw=sw, nt=nt),
        grid=(b, nt),
        in_specs=[
            pl.BlockSpec((1, ts, w), lambda i, j: (i, j, 0)),
            pl.BlockSpec(wb.shape, const3),
            pl.BlockSpec(scn.shape, lambda i, j: (0, 0, 0, 0)),
            pl.BlockSpec(wc.shape, const3),
            pl.BlockSpec((1, w), lambda i, j: (0, 0)),
            pl.BlockSpec((1, 2, ns), lambda i, j: (i, 0, 0)),
        ],
        out_specs=[
            pl.BlockSpec((1, ts, w), lambda i, j: (i, j, 0)),
            pl.BlockSpec((1, 2, ns), lambda i, j: (i, 0, 0)),
        ],
        out_shape=[
            jax.ShapeDtypeStruct((b, l, w), BF16),
            jax.ShapeDtypeStruct((b, 2, ns), F32),
        ],
        scratch_shapes=[
            pltpu.VMEM((ts, sw2), F32),
            pltpu.VMEM((2, ns), F32),
        ],
        compiler_params=_params("parallel", "arbitrary"),
        name="s5",
    )(u, wb, scn, wc, dsk, s0)


def _s5_tables(a_re, a_im, log_dt, b_re, b_im, c_re, c_im, d_skip):
    g, p = a_re.shape
    ch = b_re.shape[2]
    gpt = MXU_DIM // ch
    nkt = g // gpt
    dt = jnp.exp(log_dt.astype(F32))[:, None]
    mag = jnp.exp(dt * a_re)
    ab_re = mag * jnp.cos(dt * a_im)
    ab_im = mag * jnp.sin(dt * a_im)
    den = a_re * a_re + a_im * a_im
    nr = ab_re - 1.0
    f_re = (nr * a_re + ab_im * a_im) / den
    f_im = (ab_im * a_re - nr * a_im) / den
    bb_re = f_re[..., None] * b_re - f_im[..., None] * b_im
    bb_im = f_re[..., None] * b_im + f_im[..., None] * b_re
    eye = jnp.eye(gpt, dtype=F32)

    def in_blockdiag(bb):
        return jnp.einsum("kgpi,gh->kgihp", bb.reshape(nkt, gpt, p, ch), eye).reshape(nkt, gpt * ch, gpt * p)

    def out_blockdiag(cc):
        return jnp.einsum("kgjp,gh->kgphj", cc.reshape(nkt, gpt, ch, p), eye).reshape(nkt, gpt * p, gpt * ch)

    wb = jnp.concatenate([in_blockdiag(bb_re), in_blockdiag(bb_im)], axis=2).astype(BF16)
    wc = jnp.concatenate([out_blockdiag(c_re), out_blockdiag(-c_im)], axis=1).astype(BF16)

    def cmul(x, y):
        return x[0] * y[0] - x[1] * y[1], x[0] * y[1] + x[1] * y[0]

    lam1 = (ab_re.reshape(nkt, gpt * p), ab_im.reshape(nkt, gpt * p))
    lam2 = cmul(lam1, lam1)
    lam4 = cmul(lam2, lam2)
    rows = jnp.arange(SUBLANES)[None, :, None]

    def shifted(lam, dist):
        return [jnp.where(rows >= dist, c[:, None, :], 0.0) for c in lam]

    pw = [lam1]
    for _ in range(SUBLANES - 1):
        pw.append(cmul(pw[-1], lam1))
    p_re = jnp.stack([c[0] for c in pw], axis=1)
    p_im = jnp.stack([c[1] for c in pw], axis=1)
    scn = jnp.stack(shifted(lam1, 1) + shifted(lam2, 2) + shifted(lam4, 4) + [p_re, p_im], axis=1)
    return wb, scn.astype(F32), wc, d_skip.reshape(1, g * ch).astype(F32)


def _merge_kernel(ys_ref, o_ref, wga_ref, wgb_ref, wo_ref, ga_ref, gb_ref, out_ref, *, heads):
    ys = ys_ref[...]
    ya = jnp.dot(ys, wga_ref[...], preferred_element_type=F32)
    ya = ya * _sigmoid(jnp.dot(ys, wgb_ref[...], preferred_element_type=F32))
    oc = jnp.concatenate([o_ref[hh] for hh in range(heads)], axis=1)
    yb = jnp.dot(oc, wo_ref[...], preferred_element_type=F32)
    out_ref[...] = (ga_ref[...].astype(F32) * ya + gb_ref[...].astype(F32) * yb).astype(BF16)


def _merge(ys, o, w_glu, w_o, gates):
    t, sw = ys.shape
    heads = o.shape[0]
    d = w_o.shape[1]
    tm = _tile(t, 512)
    tn = _tile(d, 512)
    nj = d // tn
    return pl.pallas_call(
        functools.partial(_merge_kernel, heads=heads),
        grid=(t // tm, nj),
        in_specs=[
            pl.BlockSpec((tm, sw), lambda i, j: (i, 0)),
            pl.BlockSpec((heads, tm, LANES), lambda i, j: (0, i, 0)),
            pl.BlockSpec((sw, tn), lambda i, j: (0, j)),
            pl.BlockSpec((sw, tn), lambda i, j: (0, nj + j)),
            pl.BlockSpec((heads * LANES, tn), lambda i, j: (0, j)),
            pl.BlockSpec((tm, tn), lambda i, j: (i, j)),
            pl.BlockSpec((tm, tn), lambda i, j: (i, nj + j)),
        ],
        out_specs=pl.BlockSpec((tm, tn), lambda i, j: (i, j)),
        out_shape=jax.ShapeDtypeStruct((t, d), BF16),
        compiler_params=_params("parallel", "parallel"),
        name="merge",
    )(ys, o, w_glu, w_glu, w_o, gates, gates)


def _resid_kernel(m_ref, w_ref, x_ref, gt_ref, g_ref, sc_ref, sh_ref, x1_ref, h2_ref):
    x1 = x_ref[0] + gt_ref[0] * jnp.dot(m_ref[0], w_ref[...], preferred_element_type=F32)
    x1_ref[0] = x1
    h2_ref[0] = (_rms(x1, g_ref[...]) * (1.0 + sc_ref[0]) + sh_ref[0]).astype(BF16)


def _resid(merged, w_out, x, gt, g2, sc, sh):
    b, l, d = x.shape
    tm = _tile(l, 256)
    tok = lambda i, j: (i, j, 0)
    per_b = lambda i, j: (i, 0, 0)
    return pl.pallas_call(
        _resid_kernel,
        grid=(b, l // tm),
        in_specs=[
            pl.BlockSpec((1, tm, d), tok),
            pl.BlockSpec((d, d), lambda i, j: (0, 0)),
            pl.BlockSpec((1, tm, d), tok),
            pl.BlockSpec((1, 1, d), per_b),
            pl.BlockSpec((1, d), lambda i, j: (0, 0)),
            pl.BlockSpec((1, 1, d), per_b),
            pl.BlockSpec((1, 1, d), per_b),
        ],
        out_specs=[pl.BlockSpec((1, tm, d), tok), pl.BlockSpec((1, tm, d), tok)],
        out_shape=[jax.ShapeDtypeStruct((b, l, d), F32), jax.ShapeDtypeStruct((b, l, d), BF16)],
        compiler_params=_params("parallel", "parallel"),
        name="resid",
    )(merged.reshape(b, l, d), w_out, x, gt, g2.reshape(1, d), sc, sh)


def _peer_topk_kernel(q_ref, k1_ref, k2_ref, flat_ref, invalid_ref, r2_ref, c1_ref, a1_ref, a2_ref,
                      *, nh, nkeys, topk):
    tb = q_ref.shape[1]
    iota_k = lax.broadcasted_iota(jnp.int32, (nkeys, tb), 0).astype(F32)
    iota_t = lax.broadcasted_iota(jnp.int32, (topk, tb), 0).astype(F32)
    flat = flat_ref[...]
    invalid = invalid_ref[...]
    nt_dims = (((1,), (1,)), ((), ()))

    n_cand = flat.shape[0]

    def extract(s, exact_ties):
        work = s
        rank = jnp.full((nkeys, tb), float(topk), F32)
        vals = []
        for a in range(topk):
            m = jnp.max(work, axis=0, keepdims=True)
            if exact_ties:
                idx = jnp.min(jnp.where(work == m, iota_k, float(nkeys)), axis=0, keepdims=True)
                sel = iota_k == idx
            else:
                sel = work == m
            rank = jnp.where(sel, float(a), rank)
            work = jnp.where(sel, REMOVED, work)
            vals.append(m)
        ranked = jnp.sum(jnp.where(rank < float(topk), 1.0, 0.0), axis=0, keepdims=True)
        return vals, rank, ranked

    def stack(vals):
        out = jnp.zeros((topk, tb), F32)
        for a in range(topk):
            out = jnp.where(iota_t == float(a), vals[a], out)
        return out

    def candidates(v1, v2):
        vs1 = stack(v1)
        vs2 = stack(v2)
        blocks = [v1[0] + vs2]
        for a in range(1, SUBLANES):
            blocks.append(v1[a] + vs2[:SUBLANES])
        blocks.append(vs1[SUBLANES:] + v2[0])
        return jnp.concatenate(blocks, axis=0) + invalid

    def route(s1, s2, exact_ties):
        v1, rank1, n1 = extract(s1, exact_ties)
        v2, rank2, n2 = extract(s2, exact_ties)
        cand = candidates(v1, v2)
        counts = jnp.zeros((topk, tb), F32)
        top = None
        zsum = None
        for kk in range(topk):
            m = jnp.max(cand, axis=0, keepdims=True)
            if exact_ties:
                f = jnp.min(jnp.where(cand == m, flat, 1e9), axis=0, keepdims=True)
                cand = jnp.where(flat == f, REMOVED, cand)
                counts = counts + jnp.where(iota_t == jnp.floor(f * (1.0 / topk)), 1.0, 0.0)
            else:
                cand = jnp.where(cand == m, REMOVED, cand)
            if kk == 0:
                top = m
                zsum = jnp.ones_like(m)
            else:
                zsum = zsum + jnp.exp(m - top)
        n3 = None
        if not exact_ties:
            gone = jnp.where(cand == REMOVED, 1.0, 0.0)
            n3 = jnp.sum(gone, axis=0, keepdims=True)
            per_rank = [jnp.sum(gone[:topk], axis=0, keepdims=True)]
            for a in range(1, SUBLANES):
                lo = topk + (a - 1) * SUBLANES
                per_rank.append(jnp.sum(gone[lo:lo + SUBLANES], axis=0, keepdims=True))
            counts = jnp.concatenate([stack(per_rank + [per_rank[0]] * (topk - SUBLANES))[:SUBLANES],
                                      gone[n_cand - SUBLANES:]], axis=0)
        c1 = jnp.zeros((nkeys, tb), F32)
        for a in range(topk):
            c1 = jnp.where(rank1 == float(a), counts[a:a + 1], c1)
        a1 = jnp.exp(s1 - v1[0]) * (1.0 / zsum)
        a2 = jnp.exp(s2 - v2[0])
        clean = None
        if not exact_ties:
            want = float(topk)
            bad = jnp.where(n1 != want, 1.0, 0.0) + jnp.where(n2 != want, 1.0, 0.0) + jnp.where(n3 != want, 1.0, 0.0)
            clean = jnp.max(bad) == 0.0
        return (rank2, c1, a1, a2), clean

    def store(hh, tables):
        r2_ref[hh], c1_ref[hh], a1_ref[hh], a2_ref[hh] = tables

    def pair(pp, carry):
        heads = (2 * pp, 2 * pp + 1)
        scores = []
        for hh in heads:
            scores.append((lax.dot_general(k1_ref[hh], q_ref[2 * hh], nt_dims, preferred_element_type=F32),
                           lax.dot_general(k2_ref[hh], q_ref[2 * hh + 1], nt_dims, preferred_element_type=F32)))
        quick = [route(s1, s2, False) for s1, s2 in scores]
        clean = jnp.logical_and(quick[0][1], quick[1][1])

        @pl.when(clean)
        def _no_ties():
            for hh, (tables, _) in zip(heads, quick):
                store(hh, tables)

        @pl.when(jnp.logical_not(clean))
        def _ties():
            for hh, (s1, s2) in zip(heads, scores):
                store(hh, route(s1, s2, True)[0])

        return carry

    lax.fori_loop(0, nh // 2, pair, 0)


def _peer_topk(q, k1, k2, topk):
    nh, nkeys, half = k1.shape
    t = q.shape[1]
    tb = LANES
    assert topk == 2 * SUBLANES and half == LANES and t % tb == 0
    rows = jnp.arange(topk + (SUBLANES - 1) * SUBLANES + SUBLANES)
    a_idx = jnp.where(rows < topk, 0, jnp.where(rows < topk + (SUBLANES - 1) * SUBLANES,
                                                1 + (rows - topk) // SUBLANES, SUBLANES + (rows - topk - (SUBLANES - 1) * SUBLANES)))
    b_idx = jnp.where(rows < topk, rows, jnp.where(rows < topk + (SUBLANES - 1) * SUBLANES, (rows - topk) % SUBLANES, 0))
    flat = jnp.broadcast_to((a_idx * topk + b_idx).astype(F32)[:, None], (rows.shape[0], tb))
    invalid = jnp.broadcast_to(jnp.where((a_idx + 1) * (b_idx + 1) <= topk, 0.0, EXCLUDED).astype(F32)[:, None],
                               (rows.shape[0], tb))
    out = lambda dt: jax.ShapeDtypeStruct((nh, nkeys, t), dt)
    ospec = pl.BlockSpec((nh, nkeys, tb), lambda i: (0, 0, i))
    return pl.pallas_call(
        functools.partial(_peer_topk_kernel, nh=nh, nkeys=nkeys, topk=topk),
        grid=(t // tb,),
        in_specs=[
            pl.BlockSpec((2 * nh, tb, LANES), lambda i: (0, i, 0)),
            pl.BlockSpec(k1.shape, lambda i: (0, 0, 0)),
            pl.BlockSpec(k2.shape, lambda i: (0, 0, 0)),
            pl.BlockSpec(flat.shape, lambda i: (0, 0)),
            pl.BlockSpec(invalid.shape, lambda i: (0, 0)),
        ],
        out_specs=[ospec] * 4,
        out_shape=[out(F32)] * 4,
        compiler_params=_params("parallel"),
        name="peer_topk",
    )(q, k1, k2, flat, invalid)


def _peer_mix_kernel(h_ref, u_ref, vt_ref, r2_ref, c1_ref, a1_ref, a2_ref, *rest, nh, nkeys, ni, ne, norm):
    if len(rest) == 4:
        o_ref, ht_scr, w_scr, acc_scr = rest
        x_ref = gt_ref = g_ref = None
    else:
        x_ref, gt_ref, g_ref, o_ref, ht_scr, w_scr, acc_scr = rest
    e = pl.program_id(1)
    tb = h_ref.shape[0]

    @pl.when(e == 0)
    def _init():
        acc_scr[...] = jnp.zeros(acc_scr.shape, F32)

    ht_scr[...] = lax.dot_general(u_ref[...], h_ref[...], (((1,), (1,)), ((), ())), preferred_element_type=F32)
    first_keys = pl.ds(pl.multiple_of(e * ni, SUBLANES), ni)
    for il in range(ni):
        rows = slice(il * nkeys, (il + 1) * nkeys)
        for lg in range(tb // LANES):
            sl = slice(lg * LANES, (lg + 1) * LANES)
            gate = jnp.zeros((nkeys, LANES), F32)
            for hh in range(nh):
                partners = c1_ref[hh, first_keys, sl][il:il + 1]
                first = a1_ref[hh, first_keys, sl][il:il + 1]
                gate = gate + jnp.where(r2_ref[hh, :, sl] < partners, a2_ref[hh, :, sl], 0.0) * first
            w_scr[rows, sl] = (gate * _gelu(ht_scr[rows, sl])).astype(BF16)
    acc_scr[...] += jnp.dot(vt_ref[...], w_scr[...], preferred_element_type=F32)

    @pl.when(e == ne - 1)
    def _done():
        mix = acc_scr[...].T
        if x_ref is None:
            o_ref[...] = mix
        else:
            y = x_ref[0] + gt_ref[0] * mix
            o_ref[0] = _rms(y, g_ref[...]) if norm else y


def _peer_mix(h2, u_tab, vt_tab, r2, c1, a1, a2, x1, gt, g, norm):
    b, l, _ = x1.shape
    t, d = h2.shape
    nh, nkeys, _ = r2.shape
    n_exp = u_tab.shape[0]
    tb = _tile(t, 512)
    ni = SUBLANES
    eb = ni * nkeys
    ne = n_exp // eb
    aux = pl.BlockSpec((nh, nkeys, tb), lambda i, e: (0, 0, i), pipeline_mode=pl.Buffered(1))
    in_specs = [
        pl.BlockSpec((tb, d), lambda i, e: (i, 0)),
        pl.BlockSpec((eb, d), lambda i, e: (e, 0)),
        pl.BlockSpec((d, eb), lambda i, e: (0, e)),
        aux, aux, aux, aux,
    ]
    args = [h2, u_tab, vt_tab, r2, c1, a1, a2]
    fused = l % tb == 0
    if fused:
        per_seq = l // tb
        in_specs += [
            pl.BlockSpec((1, tb, d), lambda i, e: (i // per_seq, i % per_seq, 0), pipeline_mode=pl.Buffered(1)),
            pl.BlockSpec((1, 1, d), lambda i, e: (i // per_seq, 0, 0)),
            pl.BlockSpec((1, d), lambda i, e: (0, 0)),
        ]
        args += [x1, gt, g.reshape(1, d)]
        out_spec = pl.BlockSpec((1, tb, d), lambda i, e: (i // per_seq, i % per_seq, 0))
        out_shape = jax.ShapeDtypeStruct((b, l, d), F32)
    else:
        out_spec = pl.BlockSpec((tb, d), lambda i, e: (i, 0))
        out_shape = jax.ShapeDtypeStruct((t, d), F32)
    out = pl.pallas_call(
        functools.partial(_peer_mix_kernel, nh=nh, nkeys=nkeys, ni=ni, ne=ne, norm=norm),
        grid=(t // tb, ne),
        in_specs=in_specs,
        out_specs=out_spec,
        out_shape=out_shape,
        scratch_shapes=[
            pltpu.VMEM((eb, tb), F32),
            pltpu.VMEM((eb, tb), BF16),
            pltpu.VMEM((d, tb), F32),
        ],
        compiler_params=_params("parallel", "arbitrary"),
        name="peer_mix",
    )(*args)
    return out if fused else _final(x1, out, gt, g, norm)


def _final_kernel(x_ref, p_ref, gt_ref, g_ref, y_ref, *, norm):
    y = x_ref[0] + gt_ref[0] * p_ref[0]
    y_ref[0] = _rms(y, g_ref[...]) if norm else y


def _final(x1, peer, gt, g, norm):
    b, l, d = x1.shape
    tm = _tile(l, 512)
    tok = lambda i, j: (i, j, 0)
    return pl.pallas_call(
        functools.partial(_final_kernel, norm=norm),
        grid=(b, l // tm),
        in_specs=[
            pl.BlockSpec((1, tm, d), tok),
            pl.BlockSpec((1, tm, d), tok),
            pl.BlockSpec((1, 1, d), lambda i, j: (i, 0, 0)),
            pl.BlockSpec((1, d), lambda i, j: (0, 0)),
        ],
        out_specs=pl.BlockSpec((1, tm, d), tok),
        out_shape=jax.ShapeDtypeStruct((b, l, d), F32),
        compiler_params=_params("parallel", "parallel"),
        name="final",
    )(x1, peer.reshape(b, l, d), gt, g.reshape(1, d))


def _rope_tables(pos, rope):
    half = rope // 2
    inv = jnp.power(ROPE_THETA, -jnp.arange(half, dtype=F32) / half)
    ang = pos.astype(F32)[:, None] * inv
    reps = LANES // half
    return jnp.tile(jnp.cos(ang), (1, reps)), jnp.tile(jnp.sin(ang), (1, reps))


def _prep_weights(p, dims):
    d, sw, q_lora, kv_lora, rope, heads, nope = (dims[k] for k in ("d", "sw", "q_lora", "kv_lora", "rope", "heads", "nope"))
    half = rope // 2
    w_in = p["w_in"]
    off_q = sw
    off_kv = off_q + q_lora
    off_kr = off_kv + kv_lora
    off_g = off_kr + rope
    kr_w = w_in[:, off_kr:off_g]
    kr_rot = jnp.concatenate([-kr_w[:, half:], kr_w[:, :half]], axis=1)
    pad = jnp.zeros((d, LANES - rope), F32)
    w_mla = jnp.concatenate([w_in[:, off_q:off_kr], kr_w, pad, kr_rot, pad], axis=1).astype(BF16)
    wq = p["w_qu"].reshape(q_lora, heads, nope + rope)
    wq_n = wq[:, :, :nope].reshape(q_lora, heads * nope)
    wq_r = wq[:, :, nope:]
    wq_rot = jnp.concatenate([-wq_r[:, :, half:], wq_r[:, :, :half]], axis=2)
    hpad = jnp.zeros((q_lora, heads, LANES - rope), F32)
    wq_a = jnp.concatenate([wq_r, hpad], axis=2).reshape(q_lora, heads * LANES)
    wq_b = jnp.concatenate([wq_rot, hpad], axis=2).reshape(q_lora, heads * LANES)
    return {
        "w_gates": w_in[:, off_g:].astype(BF16),
        "w_u": w_in[:, :sw].astype(BF16),
        "w_mla": w_mla,
        "w_q": jnp.concatenate([wq_n, wq_a, wq_b], axis=1).astype(BF16),
        "w_kv": jnp.concatenate([p["w_uk"].reshape(kv_lora, heads * nope),
                                 p["w_uv"].reshape(kv_lora, heads * dims["v_dim"])], axis=1).astype(BF16),
        "w_o": p["w_o"].astype(BF16),
        "w_glu": p["w_glu"].astype(BF16),
        "w_out": p["w_out"].astype(BF16),
        "peer_wq": p["peer_wq"].astype(BF16),
        "peer_k1": p["peer_k1"].astype(BF16),
        "peer_k2": p["peer_k2"].astype(BF16),
        "peer_u": p["peer_u"].astype(BF16),
        "peer_vt": p["peer_v"].T.astype(BF16),
        "s5": _s5_tables(p["ssm_a_re"], p["ssm_a_im"], p["ssm_log_dt"], p["ssm_b_re"], p["ssm_b_im"],
                         p["ssm_c_re"], p["ssm_c_im"], p["ssm_d"]),
    }


def _layer(x, mod, past_ckv, past_kr, s0, p, w, dims, g_final, last):
    b, l, d = x.shape
    t = b * l
    heads, rope, kv_lora, topk = dims["heads"], dims["rope"], dims["kv_lora"], dims["topk"]
    sh1, sc1, gt1, sh2, sc2, gt2 = mod
    past = 0 if past_ckv is None else past_ckv.shape[1]

    h = _normmod(x, p["g_norm1"], sc1, sh1).reshape(t, d)
    gates = _mm(h, w["w_gates"], BF16, act="sigmoid")
    u = _mm(h, w["w_u"], BF16)
    cos, sin = _rope_tables(past + jnp.arange(l, dtype=jnp.int32), rope)
    qn, qr, ckv, ckv_b, kr, krp = _mla_proj(
        h, w["w_mla"], p["g_q"], p["g_kv"], w["w_q"], cos, sin, l,
        heads=heads, q_lora=dims["q_lora"], kv_lora=kv_lora, rope=rope, scale=dims["scale"])

    ys, s_fin = _s5(u.reshape(b, l, -1), *w["s5"], s0)

    if past_ckv is None:
        lk, keys_c, keys_r = l, ckv_b, krp
        tq = tk = _tile(l, 512)
    else:
        n_keys = past + l
        lk = -(-n_keys // LANES) * LANES
        keys_c = jnp.concatenate([past_ckv.astype(BF16), ckv_b.reshape(b, l, kv_lora)], axis=1)
        keys_c = jnp.pad(keys_c, ((0, 0), (0, lk - n_keys), (0, 0))).reshape(b * lk, kv_lora)
        past_r = jnp.pad(past_kr.astype(BF16), ((0, 0), (0, 0), (0, LANES - rope)))
        keys_r = jnp.concatenate([past_r, krp.reshape(b, l, LANES)], axis=1)
        keys_r = jnp.pad(keys_r, ((0, 0), (0, lk - n_keys), (0, 0))).reshape(b * lk, LANES)
        tq, tk = l, lk
    kh, vh = _kvup(keys_c, w["w_kv"], heads)
    o = _flash(qn, qr, kh, keys_r, vh, batch=b, lq=l, lk=lk, q_pos0=past, n_keys=past + l,
               tq=tq, tk=tk, hb=heads)

    merged = _merge(ys.reshape(t, -1), o, w["w_glu"], w["w_o"], gates)
    x1, h2 = _resid(merged, w["w_out"], x, gt1, p["g_norm2"], sc2, sh2)

    h2 = h2.reshape(t, d)
    q = _mm_split(h2, w["peer_wq"], BF16)
    r2, c1, a1, a2 = _peer_topk(q, w["peer_k1"], w["peer_k2"], topk)
    x2 = _peer_mix(h2, w["peer_u"], w["peer_vt"], r2, c1, a1, a2, x1, gt2, g_final, last)
    return x2, ckv.reshape(b, l, kv_lora), kr.reshape(b, l, rope), s_fin


def kernel(x_prompt, x_sample, c_prompt, c_sample, cache_ckv, cache_krope, state_ssm_re, state_ssm_im, w_ada, b_ada, g_norm1, g_norm2, w_in, g_q, w_qu, g_kv, w_uk, w_uv, w_o, ssm_a_re, ssm_a_im, ssm_log_dt, ssm_b_re, ssm_b_im, ssm_c_re, ssm_c_im, ssm_d, w_glu, w_out, peer_wq, peer_k1, peer_k2, peer_u, peer_v, g_final):
    depth = w_in.shape[0]
    bp, lp, d = x_prompt.shape
    bs, ls, _ = x_sample.shape
    groups, states = ssm_a_re.shape[1:]
    heads, nope = w_uk.shape[2:]
    rope = cache_krope.shape[-1]
    dims = {
        "d": d, "sw": groups * ssm_b_re.shape[3], "q_lora": g_q.shape[1], "kv_lora": g_kv.shape[1],
        "rope": rope, "heads": heads, "nope": nope, "v_dim": w_uv.shape[3],
        "scale": math.log2(math.e) / math.sqrt(nope + rope), "topk": 16,
    }
    assert nope == LANES and dims["v_dim"] == LANES and rope <= LANES

    xp, xs = x_prompt, x_sample
    nb = bp + bs
    rows = -(-nb // 16) * 16
    c_all = jnp.pad(jnp.concatenate([c_prompt, c_sample], axis=0), ((0, rows - nb), (0, 0)))
    zeros = jnp.zeros((bp, 2, groups * states), F32)
    outs_p, outs_s = [], []
    for layer in range(depth):
        p = {
            "g_norm1": g_norm1[layer], "g_norm2": g_norm2[layer], "w_in": w_in[layer], "g_q": g_q[layer],
            "w_qu": w_qu[layer], "g_kv": g_kv[layer], "w_uk": w_uk[layer], "w_uv": w_uv[layer], "w_o": w_o[layer],
            "ssm_a_re": ssm_a_re[layer], "ssm_a_im": ssm_a_im[layer], "ssm_log_dt": ssm_log_dt[layer],
            "ssm_b_re": ssm_b_re[layer], "ssm_b_im": ssm_b_im[layer], "ssm_c_re": ssm_c_re[layer],
            "ssm_c_im": ssm_c_im[layer], "ssm_d": ssm_d[layer], "w_glu": w_glu[layer], "w_out": w_out[layer],
            "peer_wq": peer_wq[layer], "peer_k1": peer_k1[layer], "peer_k2": peer_k2[layer],
            "peer_u": peer_u[layer], "peer_v": peer_v[layer],
        }
        w = _prep_weights(p, dims)
        mod = _ada(c_all, w_ada[layer], b_ada[layer])
        mod_p = [m.reshape(bp, 1, d) for m in jnp.split(mod[:bp], 6, axis=-1)]
        mod_s = [m.reshape(bs, 1, d) for m in jnp.split(mod[bp:nb], 6, axis=-1)]
        s0_s = jnp.stack([state_ssm_re[layer].reshape(bs, -1), state_ssm_im[layer].reshape(bs, -1)], axis=1)
        last = layer == depth - 1
        xp, *res_p = _layer(xp, mod_p, None, None, zeros, p, w, dims, g_final, last)
        xs, *res_s = _layer(xs, mod_s, cache_ckv[layer], cache_krope[layer], s0_s, p, w, dims, g_final, last)
        for res, outs, g in ((res_p, outs_p, bp), (res_s, outs_s, bs)):
            ckv, kr, s_fin = res
            outs.append((ckv, kr, s_fin[:, 0].reshape(g, groups, states), s_fin[:, 1].reshape(g, groups, states)))
    stack = lambda outs, k: jnp.stack([o[k] for o in outs])
    return (xp, xs,
            stack(outs_p, 0), stack(outs_p, 1), stack(outs_p, 2), stack(outs_p, 3),
            stack(outs_s, 0), stack(outs_s, 1), stack(outs_s, 2), stack(outs_s, 3))
```

```python
import functools
import math

import jax
import jax.numpy as jnp
from jax import lax
from jax.experimental import pallas as pl
from jax.experimental.pallas import tpu as pltpu

F32 = jnp.float32
BF16 = jnp.bfloat16

EPS = 1e-6
CHUNK = 64
ROPE_THETA = 10000.0
LANES = 128
SUBLANES = 8
MXU_DIM = 256
VMEM_LIMIT_BYTES = 56 * 1024 * 1024
MASKED = -1e30
REMOVED = -3e38
EXCLUDED = -1e38


def _params(*semantics):
    return pltpu.CompilerParams(dimension_semantics=semantics, vmem_limit_bytes=VMEM_LIMIT_BYTES)


def _tile(n, pref):
    if n <= pref:
        return n
    t = pref
    while n % t:
        t //= 2
    assert t >= SUBLANES, (n, pref)
    return t


def _sigmoid(x):
    return 1.0 / (1.0 + jnp.exp(-x))


def _gelu(x):
    return 0.5 * x * (1.0 + jnp.tanh(math.sqrt(2.0 / math.pi) * (x + 0.044715 * (x * x * x))))


def _rms(x, g):
    return x * lax.rsqrt(jnp.mean(x * x, axis=-1, keepdims=True) + EPS) * g


def _ada_kernel(c_ref, w_ref, b_ref, o_ref):
    c = c_ref[...]
    a = (c * _sigmoid(c)).astype(BF16)
    o_ref[...] = jnp.dot(a, w_ref[...].astype(BF16), preferred_element_type=F32) + b_ref[...]


def _ada(c, w, b):
    rows, d = c.shape
    n = w.shape[1]
    tn = _tile(n, 1024)
    return pl.pallas_call(
        _ada_kernel,
        grid=(n // tn,),
        in_specs=[
            pl.BlockSpec((rows, d), lambda j: (0, 0)),
            pl.BlockSpec((d, tn), lambda j: (0, j)),
            pl.BlockSpec((1, tn), lambda j: (0, j)),
        ],
        out_specs=pl.BlockSpec((rows, tn), lambda j: (0, j)),
        out_shape=jax.ShapeDtypeStruct((rows, n), F32),
        compiler_params=_params("parallel"),
        name="ada",
    )(c, w, b.reshape(1, n))


def _normmod_kernel(x_ref, g_ref, sc_ref, sh_ref, o_ref):
    h = _rms(x_ref[0], g_ref[...]) * (1.0 + sc_ref[0]) + sh_ref[0]
    o_ref[0] = h.astype(BF16)


def _normmod(x, g, sc, sh):
    b, l, d = x.shape
    tm = _tile(l, 512)
    return pl.pallas_call(
        _normmod_kernel,
        grid=(b, l // tm),
        in_specs=[
            pl.BlockSpec((1, tm, d), lambda i, j: (i, j, 0)),
            pl.BlockSpec((1, d), lambda i, j: (0, 0)),
            pl.BlockSpec((1, 1, d), lambda i, j: (i, 0, 0)),
            pl.BlockSpec((1, 1, d), lambda i, j: (i, 0, 0)),
        ],
        out_specs=pl.BlockSpec((1, tm, d), lambda i, j: (i, j, 0)),
        out_shape=jax.ShapeDtypeStruct((b, l, d), BF16),
        compiler_params=_params("parallel", "parallel"),
        name="normmod",
    )(x, g.reshape(1, d), sc, sh)


def _mm_kernel(a_ref, w_ref, o_ref, *, act):
    z = jnp.dot(a_ref[...], w_ref[...], preferred_element_type=F32)
    if act == "sigmoid":
        z = _sigmoid(z)
    o_ref[...] = z.astype(o_ref.dtype)


def _mm(a, w, out_dtype, act=None, tn_pref=1024):
    t, k = a.shape
    n = w.shape[1]
    tm = _tile(t, 1024)
    tn = _tile(n, tn_pref)
    return pl.pallas_call(
        functools.partial(_mm_kernel, act=act),
        grid=(t // tm, n // tn),
        in_specs=[
            pl.BlockSpec((tm, k), lambda i, j: (i, 0)),
            pl.BlockSpec((k, tn), lambda i, j: (0, j)),
        ],
        out_specs=pl.BlockSpec((tm, tn), lambda i, j: (i, j)),
        out_shape=jax.ShapeDtypeStruct((t, n), out_dtype),
        compiler_params=_params("parallel", "parallel"),
        name="mm",
    )(a, w)


def _mm_split_kernel(a_ref, w_ref, o_ref, *, parts):
    z = jnp.dot(a_ref[...], w_ref[...], preferred_element_type=F32)
    for p in range(parts):
        o_ref[p] = z[:, p * LANES:(p + 1) * LANES].astype(o_ref.dtype)


def _mm_split(a, w, out_dtype):
    t, k = a.shape
    n = w.shape[1]
    tm = _tile(t, 512)
    tn = _tile(n, 1024)
    parts = tn // LANES
    return pl.pallas_call(
        functools.partial(_mm_split_kernel, parts=parts),
        grid=(t // tm, n // tn),
        in_specs=[
            pl.BlockSpec((tm, k), lambda i, j: (i, 0)),
            pl.BlockSpec((k, tn), lambda i, j: (0, j)),
        ],
        out_specs=pl.BlockSpec((parts, tm, LANES), lambda i, j: (j, i, 0)),
        out_shape=jax.ShapeDtypeStruct((n // LANES, t, LANES), out_dtype),
        compiler_params=_params("parallel", "parallel"),
        name="mm_split",
    )(a, w)


def _mla_proj_kernel(h_ref, wm_ref, gq_ref, gkv_ref, wq_ref, cos_ref, sin_ref,
                     qn_ref, qr_ref, ckv_ref, ckvb_ref, kr_ref, krp_ref,
                     *, heads, q_lora, kv_lora, rope, scale):
    z = jnp.dot(h_ref[...], wm_ref[...], preferred_element_type=F32)
    cos = cos_ref[...]
    sin = sin_ref[...]
    off = q_lora + kv_lora
    krp = z[:, off:off + LANES] * cos + z[:, off + LANES:off + 2 * LANES] * sin
    kr_ref[...] = krp[:, :rope]
    krp_ref[...] = krp.astype(BF16)
    ckv = _rms(z[:, q_lora:off], gkv_ref[...])
    ckv_ref[...] = ckv
    ckvb_ref[...] = ckv.astype(BF16)
    qd = _rms(z[:, :q_lora], gq_ref[...]).astype(BF16)
    zq = jnp.dot(qd, wq_ref[...], preferred_element_type=F32)
    hn = heads * LANES
    for hh in range(heads):
        lo = hh * LANES
        qn_ref[hh] = (zq[:, lo:lo + LANES] * scale).astype(BF16)
        qr = zq[:, hn + lo:hn + lo + LANES] * cos + zq[:, 2 * hn + lo:2 * hn + lo + LANES] * sin
        qr_ref[hh] = (qr * scale).astype(BF16)


def _mla_proj(h, wm, gq, gkv, wq, cos, sin, seq, *, heads, q_lora, kv_lora, rope, scale):
    t, d = h.shape
    tm = _tile(t, 256)
    if seq % tm == 0:
        nrep = seq // tm
        tab_map = lambda i: (i % nrep, 0)
    else:
        assert tm % seq == 0
        cos = jnp.tile(cos, (tm // seq, 1))
        sin = jnp.tile(sin, (tm // seq, 1))
        tab_map = lambda i: (0, 0)
    nm = wm.shape[1]
    nq = wq.shape[1]
    row = lambda i: (i, 0)
    const = lambda i: (0, 0)
    return pl.pallas_call(
        functools.partial(_mla_proj_kernel, heads=heads, q_lora=q_lora, kv_lora=kv_lora, rope=rope, scale=scale),
        grid=(t // tm,),
        in_specs=[
            pl.BlockSpec((tm, d), row),
            pl.BlockSpec((d, nm), const),
            pl.BlockSpec((1, q_lora), const),
            pl.BlockSpec((1, kv_lora), const),
            pl.BlockSpec((q_lora, nq), const),
            pl.BlockSpec((tm, LANES), tab_map),
            pl.BlockSpec((tm, LANES), tab_map),
        ],
        out_specs=[
            pl.BlockSpec((heads, tm, LANES), lambda i: (0, i, 0)),
            pl.BlockSpec((heads, tm, LANES), lambda i: (0, i, 0)),
            pl.BlockSpec((tm, kv_lora), row),
            pl.BlockSpec((tm, kv_lora), row),
            pl.BlockSpec((tm, rope), row),
            pl.BlockSpec((tm, LANES), row),
        ],
        out_shape=[
            jax.ShapeDtypeStruct((heads, t, LANES), BF16),
            jax.ShapeDtypeStruct((heads, t, LANES), BF16),
            jax.ShapeDtypeStruct((t, kv_lora), F32),
            jax.ShapeDtypeStruct((t, kv_lora), BF16),
            jax.ShapeDtypeStruct((t, rope), F32),
            jax.ShapeDtypeStruct((t, LANES), BF16),
        ],
        compiler_params=_params("parallel"),
        name="mla_proj",
    )(h, wm, gq.reshape(1, q_lora), gkv.reshape(1, kv_lora), wq, cos, sin)


def _kvup_kernel(c_ref, w_ref, k_ref, v_ref, *, heads):
    z = jnp.dot(c_ref[...], w_ref[...], preferred_element_type=F32)
    for hh in range(heads):
        k_ref[hh] = z[:, hh * LANES:(hh + 1) * LANES].astype(BF16)
        v_ref[hh] = z[:, (heads + hh) * LANES:(heads + hh + 1) * LANES].astype(BF16)


def _kvup(ckv, w, heads):
    t, c = ckv.shape
    tm = _tile(t, 512)
    if t % tm:
        tm = t
    return pl.pallas_call(
        functools.partial(_kvup_kernel, heads=heads),
        grid=(t // tm,),
        in_specs=[
            pl.BlockSpec((tm, c), lambda i: (i, 0)),
            pl.BlockSpec(w.shape, lambda i: (0, 0)),
        ],
        out_specs=[
            pl.BlockSpec((heads, tm, LANES), lambda i: (0, i, 0)),
            pl.BlockSpec((heads, tm, LANES), lambda i: (0, i, 0)),
        ],
        out_shape=[jax.ShapeDtypeStruct((heads, t, LANES), BF16)] * 2,
        compiler_params=_params("parallel"),
        name="kvup",
    )(ckv, w)


def _flash_kernel(iq_tab, ik_tab, flag_tab, qn_ref, qr_ref, k_ref, kr_ref, v_ref, o_ref,
                  qc_scr, m_scr, acc_scr, *, hb, tq, tk, q_pos0, n_keys):
    pair = pl.program_id(2)
    iq = iq_tab[pair]
    ik = ik_tab[pair]
    flags = flag_tab[pair]
    first = (flags & 1) != 0
    last = (flags & 2) != 0
    full = (flags & 4) != 0

    @pl.when(first)
    def _init():
        m_scr[...] = jnp.full(m_scr.shape, MASKED, F32)
        acc_scr[...] = jnp.zeros(acc_scr.shape, F32)
        for hh in range(hb):
            qc_scr[hh] = jnp.concatenate([qn_ref[hh], qr_ref[hh]], axis=1)

    def step(masked):
        kr = kr_ref[...]
        ones = jnp.ones((tk, LANES), BF16)
        if masked:
            qp = q_pos0 + iq * tq + lax.broadcasted_iota(jnp.int32, (tq, tk), 0)
            kp = ik * tk + lax.broadcasted_iota(jnp.int32, (tq, tk), 1)
            allowed = jnp.logical_and(kp // CHUNK <= qp // CHUNK, kp < n_keys)

        def head(hh, carry):
            kc = jnp.concatenate([k_ref[hh], kr], axis=1)
            s = lax.dot_general(qc_scr[hh], kc, (((1,), (1,)), ((), ())), preferred_element_type=F32)
            if masked:
                s = jnp.where(allowed, s, MASKED)
            m_prev = m_scr[hh]
            m_new = jnp.maximum(m_prev, jnp.max(s, axis=1, keepdims=True))
            alpha = jnp.exp2(m_prev - m_new)
            p = jnp.exp2(s - pltpu.repeat(m_new, tk // LANES, axis=1)).astype(BF16)
            vc = jnp.concatenate([v_ref[hh], ones], axis=1)
            acc_scr[hh] = pltpu.repeat(alpha, 2, axis=1) * acc_scr[hh] + jnp.dot(p, vc, preferred_element_type=F32)
            m_scr[hh] = m_new
            return carry

        lax.fori_loop(0, hb, head, 0, unroll=True)

    @pl.when(full)
    def _full():
        step(False)

    @pl.when(jnp.logical_not(full))
    def _diag():
        step(True)

    @pl.when(last)
    def _done():
        for hh in range(hb):
            acc = acc_scr[hh]
            o_ref[hh] = (acc[:, :LANES] / acc[:, LANES:]).astype(BF16)


def _flash_pairs(lq, lk, tq, tk, q_pos0, n_keys):
    iqs, iks, flags = [], [], []
    for iq in range(lq // tq):
        q_first = q_pos0 + iq * tq
        last_key = min(n_keys - 1, ((q_first + tq - 1) // CHUNK) * CHUNK + CHUNK - 1)
        full_key = min(n_keys - 1, (q_first // CHUNK) * CHUNK + CHUNK - 1)
        n_blocks = last_key // tk + 1
        for ik in range(n_blocks):
            full = (ik + 1) * tk - 1 <= full_key
            iqs.append(iq)
            iks.append(ik)
            flags.append((1 if ik == 0 else 0) | (2 if ik == n_blocks - 1 else 0) | (4 if full else 0))
    as_i32 = lambda xs: jnp.asarray(xs, jnp.int32)
    return as_i32(iqs), as_i32(iks), as_i32(flags)


def _flash(qn, qr, k, krp, v, *, batch, lq, lk, q_pos0, n_keys, tq, tk, hb):
    heads = qn.shape[0]
    nq = lq // tq
    nk = lk // tk
    assert lq % tq == 0 and lk % tk == 0 and heads % hb == 0 and tk % LANES == 0
    iq_tab, ik_tab, flag_tab = _flash_pairs(lq, lk, tq, tk, q_pos0, n_keys)
    qmap = lambda b, h, p, iqt, ikt, ft: (h, b * nq + iqt[p], 0)
    kmap = lambda b, h, p, iqt, ikt, ft: (h, b * nk + ikt[p], 0)
    return pl.pallas_call(
        functools.partial(_flash_kernel, hb=hb, tq=tq, tk=tk, q_pos0=q_pos0, n_keys=n_keys),
        grid_spec=pltpu.PrefetchScalarGridSpec(
            num_scalar_prefetch=3,
            grid=(batch, heads // hb, iq_tab.shape[0]),
            in_specs=[
                pl.BlockSpec((hb, tq, LANES), qmap),
                pl.BlockSpec((hb, tq, LANES), qmap),
                pl.BlockSpec((hb, tk, LANES), kmap),
                pl.BlockSpec((tk, LANES), lambda b, h, p, iqt, ikt, ft: (b * nk + ikt[p], 0)),
                pl.BlockSpec((hb, tk, LANES), kmap),
            ],
            out_specs=pl.BlockSpec((hb, tq, LANES), qmap),
            scratch_shapes=[
                pltpu.VMEM((hb, tq, 2 * LANES), BF16),
                pltpu.VMEM((hb, tq, LANES), F32),
                pltpu.VMEM((hb, tq, 2 * LANES), F32),
            ],
        ),
        out_shape=jax.ShapeDtypeStruct((heads, batch * lq, LANES), BF16),
        compiler_params=_params("parallel", "parallel", "arbitrary"),
        name="flash",
    )(iq_tab, ik_tab, flag_tab, qn, qr, k, krp, v)


def _s5_kernel(u_ref, wb_ref, scn_ref, wc_ref, d_ref, s0_ref, y_ref, st_ref, xs_ref, car_ref,
               *, ts, nkt, kw, sw, nt):
    it = pl.program_id(1)

    @pl.when(it == 0)
    def _load_state():
        car_ref[...] = s0_ref[0]

    for kt in range(nkt):
        ukt = u_ref[0, :, kt * kw:(kt + 1) * kw]
        xs_ref[...] = jnp.dot(ukt, wb_ref[kt], preferred_element_type=F32)
        st_sl = slice(kt * sw, (kt + 1) * sw)

        def blk(k, carry, kt=kt):
            cr, ci = carry
            rows = pl.ds(pl.multiple_of(k * SUBLANES, SUBLANES), SUBLANES)
            hr = xs_ref[rows, :sw]
            hi = xs_ref[rows, sw:]
            for c0, dist in ((0, 1), (2, 2), (4, 4)):
                ar = scn_ref[kt, c0]
                ai = scn_ref[kt, c0 + 1]
                sr = pltpu.roll(hr, dist, 0)
                si = pltpu.roll(hi, dist, 0)
                hr, hi = hr + ar * sr - ai * si, hi + ar * si + ai * sr
            pr = scn_ref[kt, 6]
            pi = scn_ref[kt, 7]
            crb = jnp.broadcast_to(cr, (SUBLANES, sw))
            cib = jnp.broadcast_to(ci, (SUBLANES, sw))
            hr, hi = hr + pr * crb - pi * cib, hi + pr * cib + pi * crb
            xs_ref[rows, :sw] = hr
            xs_ref[rows, sw:] = hi
            return hr[SUBLANES - 1:SUBLANES], hi[SUBLANES - 1:SUBLANES]

        cr, ci = lax.fori_loop(0, ts // SUBLANES, blk, (car_ref[0:1, st_sl], car_ref[1:2, st_sl]),
                               unroll=2 if ts >= 2 * SUBLANES else 1)
        car_ref[0:1, st_sl] = cr
        car_ref[1:2, st_sl] = ci
        y = jnp.dot(xs_ref[...].astype(BF16), wc_ref[kt], preferred_element_type=F32)
        y = y + d_ref[:, kt * kw:(kt + 1) * kw] * ukt.astype(F32)
        y_ref[0, :, kt * kw:(kt + 1) * kw] = _gelu(y).astype(BF16)

    @pl.when(it == nt - 1)
    def _store_state():
        st_ref[0] = car_ref[...]


def _s5(u, wb, scn, wc, dsk, s0):
    b, l, w = u.shape
    nkt, kw, sw2 = wb.shape
    sw = sw2 // 2
    ns = s0.shape[2]
    ts = _tile(l, 512)
    nt = l // ts
    const3 = lambda i, j: (0, 0, 0)
    return pl.pallas_call(
        functools.partial(_s5_kernel, ts=ts, nkt=nkt, kw=kw, sw=sw, nt=nt),
        grid=(b, nt),
        in_specs=[
            pl.BlockSpec((1, ts, w), lambda i, j: (i, j, 0)),
            pl.BlockSpec(wb.shape, const3),
            pl.BlockSpec(scn.shape, lambda i, j: (0, 0, 0, 0)),
            pl.BlockSpec(wc.shape, const3),
            pl.BlockSpec((1, w), lambda i, j: (0, 0)),
            pl.BlockSpec((1, 2, ns), lambda i, j: (i, 0, 0)),
        ],
        out_specs=[
            pl.BlockSpec((1, ts, w), lambda i, j: (i, j, 0)),
            pl.BlockSpec((1, 2, ns), lambda i, j: (i, 0, 0)),
        ],
        out_shape=[
            jax.ShapeDtypeStruct((b, l, w), BF16),
            jax.ShapeDtypeStruct((b, 2, ns), F32),
        ],
        scratch_shapes=[
            pltpu.VMEM((ts, sw2), F32),
            pltpu.VMEM((2, ns), F32),
        ],
        compiler_params=_params("parallel", "arbitrary"),
        name="s5",
    )(u, wb, scn, wc, dsk, s0)


def _s5_tables(a_re, a_im, log_dt, b_re, b_im, c_re, c_im, d_skip):
    g, p = a_re.shape
    ch = b_re.shape[2]
    gpt = MXU_DIM // ch
    nkt = g // gpt
    dt = jnp.exp(log_dt.astype(F32))[:, None]
    mag = jnp.exp(dt * a_re)
    ab_re = mag * jnp.cos(dt * a_im)
    ab_im = mag * jnp.sin(dt * a_im)
    den = a_re * a_re + a_im * a_im
    nr = ab_re - 1.0
    f_re = (nr * a_re + ab_im * a_im) / den
    f_im = (ab_im * a_re - nr * a_im) / den
    bb_re = f_re[..., None] * b_re - f_im[..., None] * b_im
    bb_im = f_re[..., None] * b_im + f_im[..., None] * b_re
    eye = jnp.eye(gpt, dtype=F32)

    def in_blockdiag(bb):
        return jnp.einsum("kgpi,gh->kgihp", bb.reshape(nkt, gpt, p, ch), eye).reshape(nkt, gpt * ch, gpt * p)

    def out_blockdiag(cc):
        return jnp.einsum("kgjp,gh->kgphj", cc.reshape(nkt, gpt, ch, p), eye).reshape(nkt, gpt * p, gpt * ch)

    wb = jnp.concatenate([in_blockdiag(bb_re), in_blockdiag(bb_im)], axis=2).astype(BF16)
    wc = jnp.concatenate([out_blockdiag(c_re), out_blockdiag(-c_im)], axis=1).astype(BF16)

    def cmul(x, y):
        return x[0] * y[0] - x[1] * y[1], x[0] * y[1] + x[1] * y[0]

    lam1 = (ab_re.reshape(nkt, gpt * p), ab_im.reshape(nkt, gpt * p))
    lam2 = cmul(lam1, lam1)
    lam4 = cmul(lam2, lam2)
    rows = jnp.arange(SUBLANES)[None, :, None]

    def shifted(lam, dist):
        return [jnp.where(rows >= dist, c[:, None, :], 0.0) for c in lam]

    pw = [lam1]
    for _ in range(SUBLANES - 1):
        pw.append(cmul(pw[-1], lam1))
    p_re = jnp.stack([c[0] for c in pw], axis=1)
    p_im = jnp.stack([c[1] for c in pw], axis=1)
    scn = jnp.stack(shifted(lam1, 1) + shifted(lam2, 2) + shifted(lam4, 4) + [p_re, p_im], axis=1)
    return wb, scn.astype(F32), wc, d_skip.reshape(1, g * ch).astype(F32)


def _merge_kernel(ys_ref, o_ref, wga_ref, wgb_ref, wo_ref, ga_ref, gb_ref, out_ref, *, heads):
    ys = ys_ref[...]
    ya = jnp.dot(ys, wga_ref[...], preferred_element_type=F32)
    ya = ya * _sigmoid(jnp.dot(ys, wgb_ref[...], preferred_element_type=F32))
    oc = jnp.concatenate([o_ref[hh] for hh in range(heads)], axis=1)
    yb = jnp.dot(oc, wo_ref[...], preferred_element_type=F32)
    out_ref[...] = (ga_ref[...].astype(F32) * ya + gb_ref[...].astype(F32) * yb).astype(BF16)


def _merge(ys, o, w_glu, w_o, gates):
    t, sw = ys.shape
    heads = o.shape[0]
    d = w_o.shape[1]
    tm = _tile(t, 512)
    tn = _tile(d, 1024)
    nj = d // tn
    return pl.pallas_call(
        functools.partial(_merge_kernel, heads=heads),
        grid=(t // tm, nj),
        in_specs=[
            pl.BlockSpec((tm, sw), lambda i, j: (i, 0)),
            pl.BlockSpec((heads, tm, LANES), lambda i, j: (0, i, 0)),
            pl.BlockSpec((sw, tn), lambda i, j: (0, j)),
            pl.BlockSpec((sw, tn), lambda i, j: (0, nj + j)),
            pl.BlockSpec((heads * LANES, tn), lambda i, j: (0, j)),
            pl.BlockSpec((tm, tn), lambda i, j: (i, j)),
            pl.BlockSpec((tm, tn), lambda i, j: (i, nj + j)),
        ],
        out_specs=pl.BlockSpec((tm, tn), lambda i, j: (i, j)),
        out_shape=jax.ShapeDtypeStruct((t, d), BF16),
        compiler_params=_params("parallel", "parallel"),
        name="merge",
    )(ys, o, w_glu, w_glu, w_o, gates, gates)


def _resid_kernel(m_ref, w_ref, x_ref, gt_ref, g_ref, sc_ref, sh_ref, x1_ref, h2_ref):
    x1 = x_ref[0] + gt_ref[0] * jnp.dot(m_ref[0], w_ref[...], preferred_element_type=F32)
    x1_ref[0] = x1
    h2_ref[0] = (_rms(x1, g_ref[...]) * (1.0 + sc_ref[0]) + sh_ref[0]).astype(BF16)


def _resid(merged, w_out, x, gt, g2, sc, sh):
    b, l, d = x.shape
    tm = _tile(l, 512)
    tok = lambda i, j: (i, j, 0)
    per_b = lambda i, j: (i, 0, 0)
    return pl.pallas_call(
        _resid_kernel,
        grid=(b, l // tm),
        in_specs=[
            pl.BlockSpec((1, tm, d), tok),
            pl.BlockSpec((d, d), lambda i, j: (0, 0)),
            pl.BlockSpec((1, tm, d), tok),
            pl.BlockSpec((1, 1, d), per_b),
            pl.BlockSpec((1, d), lambda i, j: (0, 0)),
            pl.BlockSpec((1, 1, d), per_b),
            pl.BlockSpec((1, 1, d), per_b),
        ],
        out_specs=[pl.BlockSpec((1, tm, d), tok), pl.BlockSpec((1, tm, d), tok)],
        out_shape=[jax.ShapeDtypeStruct((b, l, d), F32), jax.ShapeDtypeStruct((b, l, d), BF16)],
        compiler_params=_params("parallel", "parallel"),
        name="resid",
    )(merged.reshape(b, l, d), w_out, x, gt, g2.reshape(1, d), sc, sh)


def _peer_topk_kernel(q_ref, k1_ref, k2_ref, flat_ref, invalid_ref, r2_ref, c1_ref, a1_ref, a2_ref,
                      *, nh, nkeys, topk):
    tb = q_ref.shape[1]
    iota_k = lax.broadcasted_iota(jnp.int32, (nkeys, tb), 0).astype(F32)
    iota_t = lax.broadcasted_iota(jnp.int32, (topk, tb), 0).astype(F32)
    flat = flat_ref[...]
    invalid = invalid_ref[...]
    nt_dims = (((1,), (1,)), ((), ()))

    n_cand = flat.shape[0]

    def extract(s, exact_ties):
        work = s
        rank = jnp.full((nkeys, tb), float(topk), F32)
        vals = []
        for a in range(topk):
            m = jnp.max(work, axis=0, keepdims=True)
            if exact_ties:
                idx = jnp.min(jnp.where(work == m, iota_k, float(nkeys)), axis=0, keepdims=True)
                sel = iota_k == idx
            else:
                sel = work == m
            rank = jnp.where(sel, float(a), rank)
            work = jnp.where(sel, REMOVED, work)
            vals.append(m)
        ranked = jnp.sum(jnp.where(rank < float(topk), 1.0, 0.0), axis=0, keepdims=True)
        return vals, rank, ranked

    def stack(vals):
        out = jnp.zeros((topk, tb), F32)
        for a in range(topk):
            out = jnp.where(iota_t == float(a), vals[a], out)
        return out

    def candidates(v1, v2):
        vs1 = stack(v1)
        vs2 = stack(v2)
        blocks = [v1[0] + vs2]
        for a in range(1, SUBLANES):
            blocks.append(v1[a] + vs2[:SUBLANES])
        blocks.append(vs1[SUBLANES:] + v2[0])
        return jnp.concatenate(blocks, axis=0) + invalid

    def route(s1, s2, exact_ties):
        v1, rank1, n1 = extract(s1, exact_ties)
        v2, rank2, n2 = extract(s2, exact_ties)
        cand = candidates(v1, v2)
        counts = jnp.zeros((topk, tb), F32)
        top = None
        zsum = None
        for kk in range(topk):
            m = jnp.max(cand, axis=0, keepdims=True)
            if exact_ties:
                f = jnp.min(jnp.where(cand == m, flat, 1e9), axis=0, keepdims=True)
                cand = jnp.where(flat == f, REMOVED, cand)
                counts = counts + jnp.where(iota_t == jnp.floor(f * (1.0 / topk)), 1.0, 0.0)
            else:
                cand = jnp.where(cand == m, REMOVED, cand)
            if kk == 0:
                top = m
                zsum = jnp.ones_like(m)
            else:
                zsum = zsum + jnp.exp(m - top)
        n3 = None
        if not exact_ties:
            gone = jnp.where(cand == REMOVED, 1.0, 0.0)
            n3 = jnp.sum(gone, axis=0, keepdims=True)
            per_rank = [jnp.sum(gone[:topk], axis=0, keepdims=True)]
            for a in range(1, SUBLANES):
                lo = topk + (a - 1) * SUBLANES
                per_rank.append(jnp.sum(gone[lo:lo + SUBLANES], axis=0, keepdims=True))
            counts = jnp.concatenate([stack(per_rank + [per_rank[0]] * (topk - SUBLANES))[:SUBLANES],
                                      gone[n_cand - SUBLANES:]], axis=0)
        c1 = jnp.zeros((nkeys, tb), F32)
        for a in range(topk):
            c1 = jnp.where(rank1 == float(a), counts[a:a + 1], c1)
        a1 = jnp.exp(s1 - v1[0]) * (1.0 / zsum)
        a2 = jnp.exp(s2 - v2[0])
        clean = None
        if not exact_ties:
            want = float(topk)
            bad = jnp.where(n1 != want, 1.0, 0.0) + jnp.where(n2 != want, 1.0, 0.0) + jnp.where(n3 != want, 1.0, 0.0)
            clean = jnp.max(bad) == 0.0
        return (rank2, c1, a1, a2), clean

    def store(hh, tables):
        r2_ref[hh], c1_ref[hh], a1_ref[hh], a2_ref[hh] = tables

    def pair(pp, carry):
        heads = (2 * pp, 2 * pp + 1)
        scores = []
        for hh in heads:
            scores.append((lax.dot_general(k1_ref[hh], q_ref[2 * hh], nt_dims, preferred_element_type=F32),
                           lax.dot_general(k2_ref[hh], q_ref[2 * hh + 1], nt_dims, preferred_element_type=F32)))
        quick = [route(s1, s2, False) for s1, s2 in scores]
        clean = jnp.logical_and(quick[0][1], quick[1][1])

        @pl.when(clean)
        def _no_ties():
            for hh, (tables, _) in zip(heads, quick):
                store(hh, tables)

        @pl.when(jnp.logical_not(clean))
        def _ties():
            for hh, (s1, s2) in zip(heads, scores):
                store(hh, route(s1, s2, True)[0])

        return carry

    lax.fori_loop(0, nh // 2, pair, 0)


def _peer_topk(q, k1, k2, topk):
    nh, nkeys, half = k1.shape
    t = q.shape[1]
    tb = LANES
    assert topk == 2 * SUBLANES and half == LANES and t % tb == 0
    rows = jnp.arange(topk + (SUBLANES - 1) * SUBLANES + SUBLANES)
    a_idx = jnp.where(rows < topk, 0, jnp.where(rows < topk + (SUBLANES - 1) * SUBLANES,
                                                1 + (rows - topk) // SUBLANES, SUBLANES + (rows - topk - (SUBLANES - 1) * SUBLANES)))
    b_idx = jnp.where(rows < topk, rows, jnp.where(rows < topk + (SUBLANES - 1) * SUBLANES, (rows - topk) % SUBLANES, 0))
    flat = jnp.broadcast_to((a_idx * topk + b_idx).astype(F32)[:, None], (rows.shape[0], tb))
    invalid = jnp.broadcast_to(jnp.where((a_idx + 1) * (b_idx + 1) <= topk, 0.0, EXCLUDED).astype(F32)[:, None],
                               (rows.shape[0], tb))
    out = lambda dt: jax.ShapeDtypeStruct((nh, nkeys, t), dt)
    ospec = pl.BlockSpec((nh, nkeys, tb), lambda i: (0, 0, i))
    return pl.pallas_call(
        functools.partial(_peer_topk_kernel, nh=nh, nkeys=nkeys, topk=topk),
        grid=(t // tb,),
        in_specs=[
            pl.BlockSpec((2 * nh, tb, LANES), lambda i: (0, i, 0)),
            pl.BlockSpec(k1.shape, lambda i: (0, 0, 0)),
            pl.BlockSpec(k2.shape, lambda i: (0, 0, 0)),
            pl.BlockSpec(flat.shape, lambda i: (0, 0)),
            pl.BlockSpec(invalid.shape, lambda i: (0, 0)),
        ],
        out_specs=[ospec] * 4,
        out_shape=[out(F32)] * 4,
        compiler_params=_params("parallel"),
        name="peer_topk",
    )(q, k1, k2, flat, invalid)


def _peer_mix_kernel(h_ref, u_ref, vt_ref, r2_ref, c1_ref, a1_ref, a2_ref, *rest, nh, nkeys, ni, ne, norm):
    if len(rest) == 5:
        o_ref, ht_scr, w_scr, acc_scr, xt_scr = rest
        x_ref = gt_ref = g_ref = None
    else:
        x_ref, gt_ref, g_ref, o_ref, ht_scr, w_scr, acc_scr, xt_scr = rest
    e = pl.program_id(1)
    tb = h_ref.shape[0]

    @pl.when(e == 0)
    def _init():
        acc_scr[...] = jnp.zeros(acc_scr.shape, F32)
        xt_scr[...] = h_ref[...].T

    ht_scr[...] = jnp.dot(u_ref[...], xt_scr[...], preferred_element_type=F32)
    first_keys = pl.ds(pl.multiple_of(e * ni, SUBLANES), ni)
    for il in range(ni):
        rows = slice(il * nkeys, (il + 1) * nkeys)
        for lg in range(tb // LANES):
            sl = slice(lg * LANES, (lg + 1) * LANES)
            gate = jnp.zeros((nkeys, LANES), F32)
            for hh in range(nh):
                partners = c1_ref[hh, first_keys, sl][il:il + 1]
                first = a1_ref[hh, first_keys, sl][il:il + 1]
                gate = gate + jnp.where(r2_ref[hh, :, sl] < partners, a2_ref[hh, :, sl], 0.0) * first
            w_scr[rows, sl] = (gate * _gelu(ht_scr[rows, sl])).astype(BF16)
    acc_scr[...] += jnp.dot(vt_ref[...], w_scr[...], preferred_element_type=F32)

    @pl.when(e == ne - 1)
    def _done():
        mix = acc_scr[...].T
        if x_ref is None:
            o_ref[...] = mix
        else:
            y = x_ref[0] + gt_ref[0] * mix
            o_ref[0] = _rms(y, g_ref[...]) if norm else y


def _peer_mix(h2, u_tab, vt_tab, r2, c1, a1, a2, x1, gt, g, norm):
    b, l, _ = x1.shape
    t, d = h2.shape
    nh, nkeys, _ = r2.shape
    n_exp = u_tab.shape[0]
    tb = _tile(t, 512)
    ni = SUBLANES
    eb = ni * nkeys
    ne = n_exp // eb
    aux = pl.BlockSpec((nh, nkeys, tb), lambda i, e: (0, 0, i), pipeline_mode=pl.Buffered(1))
    in_specs = [
        pl.BlockSpec((tb, d), lambda i, e: (i, 0), pipeline_mode=pl.Buffered(1)),
        pl.BlockSpec((eb, d), lambda i, e: (e, 0)),
        pl.BlockSpec((d, eb), lambda i, e: (0, e)),
        aux, aux, aux, aux,
    ]
    args = [h2, u_tab, vt_tab, r2, c1, a1, a2]
    fused = l % tb == 0
    if fused:
        per_seq = l // tb
        in_specs += [
            pl.BlockSpec((1, tb, d), lambda i, e: (i // per_seq, i % per_seq, 0), pipeline_mode=pl.Buffered(1)),
            pl.BlockSpec((1, 1, d), lambda i, e: (i // per_seq, 0, 0)),
            pl.BlockSpec((1, d), lambda i, e: (0, 0)),
        ]
        args += [x1, gt, g.reshape(1, d)]
        out_spec = pl.BlockSpec((1, tb, d), lambda i, e: (i // per_seq, i % per_seq, 0))
        out_shape = jax.ShapeDtypeStruct((b, l, d), F32)
    else:
        out_spec = pl.BlockSpec((tb, d), lambda i, e: (i, 0))
        out_shape = jax.ShapeDtypeStruct((t, d), F32)
    out = pl.pallas_call(
        functools.partial(_peer_mix_kernel, nh=nh, nkeys=nkeys, ni=ni, ne=ne, norm=norm),
        grid=(t // tb, ne),
        in_specs=in_specs,
        out_specs=out_spec,
        out_shape=out_shape,
        scratch_shapes=[
            pltpu.VMEM((eb, tb), F32),
            pltpu.VMEM((eb, tb), BF16),
            pltpu.VMEM((d, tb), F32),
            pltpu.VMEM((d, tb), BF16),
        ],
        compiler_params=_params("parallel", "arbitrary"),
        name="peer_mix",
    )(*args)
    return out if fused else _final(x1, out, gt, g, norm)


def _final_kernel(x_ref, p_ref, gt_ref, g_ref, y_ref, *, norm):
    y = x_ref[0] + gt_ref[0] * p_ref[0]
    y_ref[0] = _rms(y, g_ref[...]) if norm else y


def _final(x1, peer, gt, g, norm):
    b, l, d = x1.shape
    tm = _tile(l, 512)
    tok = lambda i, j: (i, j, 0)
    return pl.pallas_call(
        functools.partial(_final_kernel, norm=norm),
        grid=(b, l // tm),
        in_specs=[
            pl.BlockSpec((1, tm, d), tok),
            pl.BlockSpec((1, tm, d), tok),
            pl.BlockSpec((1, 1, d), lambda i, j: (i, 0, 0)),
            pl.BlockSpec((1, d), lambda i, j: (0, 0)),
        ],
        out_specs=pl.BlockSpec((1, tm, d), tok),
        out_shape=jax.ShapeDtypeStruct((b, l, d), F32),
        compiler_params=_params("parallel", "parallel"),
        name="final",
    )(x1, peer.reshape(b, l, d), gt, g.reshape(1, d))


def _rope_tables(pos, rope):
    half = rope // 2
    inv = jnp.power(ROPE_THETA, -jnp.arange(half, dtype=F32) / half)
    ang = pos.astype(F32)[:, None] * inv
    reps = LANES // half
    return jnp.tile(jnp.cos(ang), (1, reps)), jnp.tile(jnp.sin(ang), (1, reps))


def _prep_weights(p, dims):
    d, sw, q_lora, kv_lora, rope, heads, nope = (dims[k] for k in ("d", "sw", "q_lora", "kv_lora", "rope", "heads", "nope"))
    half = rope // 2
    w_in = p["w_in"]
    off_q = sw
    off_kv = off_q + q_lora
    off_kr = off_kv + kv_lora
    off_g = off_kr + rope
    kr_w = w_in[:, off_kr:off_g]
    kr_rot = jnp.concatenate([-kr_w[:, half:], kr_w[:, :half]], axis=1)
    pad = jnp.zeros((d, LANES - rope), F32)
    w_mla = jnp.concatenate([w_in[:, off_q:off_kr], kr_w, pad, kr_rot, pad], axis=1).astype(BF16)
    wq = p["w_qu"].reshape(q_lora, heads, nope + rope)
    wq_n = wq[:, :, :nope].reshape(q_lora, heads * nope)
    wq_r = wq[:, :, nope:]
    wq_rot = jnp.concatenate([-wq_r[:, :, half:], wq_r[:, :, :half]], axis=2)
    hpad = jnp.zeros((q_lora, heads, LANES - rope), F32)
    wq_a = jnp.concatenate([wq_r, hpad], axis=2).reshape(q_lora, heads * LANES)
    wq_b = jnp.concatenate([wq_rot, hpad], axis=2).reshape(q_lora, heads * LANES)
    return {
        "w_gates": w_in[:, off_g:].astype(BF16),
        "w_u": w_in[:, :sw].astype(BF16),
        "w_mla": w_mla,
        "w_q": jnp.concatenate([wq_n, wq_a, wq_b], axis=1).astype(BF16),
        "w_kv": jnp.concatenate([p["w_uk"].reshape(kv_lora, heads * nope),
                                 p["w_uv"].reshape(kv_lora, heads * dims["v_dim"])], axis=1).astype(BF16),
        "w_o": p["w_o"].astype(BF16),
        "w_glu": p["w_glu"].astype(BF16),
        "w_out": p["w_out"].astype(BF16),
        "peer_wq": p["peer_wq"].astype(BF16),
        "peer_k1": p["peer_k1"].astype(BF16),
        "peer_k2": p["peer_k2"].astype(BF16),
        "peer_u": p["peer_u"].astype(BF16),
        "peer_vt": p["peer_v"].T.astype(BF16),
        "s5": _s5_tables(p["ssm_a_re"], p["ssm_a_im"], p["ssm_log_dt"], p["ssm_b_re"], p["ssm_b_im"],
                         p["ssm_c_re"], p["ssm_c_im"], p["ssm_d"]),
    }


def _layer(x, mod, past_ckv, past_kr, s0, p, w, dims, g_final, last):
    b, l, d = x.shape
    t = b * l
    heads, rope, kv_lora, topk = dims["heads"], dims["rope"], dims["kv_lora"], dims["topk"]
    sh1, sc1, gt1, sh2, sc2, gt2 = mod
    past = 0 if past_ckv is None else past_ckv.shape[1]

    h = _normmod(x, p["g_norm1"], sc1, sh1).reshape(t, d)
    gates = _mm(h, w["w_gates"], BF16, act="sigmoid")
    u = _mm(h, w["w_u"], BF16)
    cos, sin = _rope_tables(past + jnp.arange(l, dtype=jnp.int32), rope)
    qn, qr, ckv, ckv_b, kr, krp = _mla_proj(
        h, w["w_mla"], p["g_q"], p["g_kv"], w["w_q"], cos, sin, l,
        heads=heads, q_lora=dims["q_lora"], kv_lora=kv_lora, rope=rope, scale=dims["scale"])

    ys, s_fin = _s5(u.reshape(b, l, -1), *w["s5"], s0)

    if past_ckv is None:
        lk, keys_c, keys_r = l, ckv_b, krp
        tq = tk = _tile(l, 512)
    else:
        n_keys = past + l
        lk = -(-n_keys // LANES) * LANES
        keys_c = jnp.concatenate([past_ckv.astype(BF16), ckv_b.reshape(b, l, kv_lora)], axis=1)
        keys_c = jnp.pad(keys_c, ((0, 0), (0, lk - n_keys), (0, 0))).reshape(b * lk, kv_lora)
        past_r = jnp.pad(past_kr.astype(BF16), ((0, 0), (0, 0), (0, LANES - rope)))
        keys_r = jnp.concatenate([past_r, krp.reshape(b, l, LANES)], axis=1)
        keys_r = jnp.pad(keys_r, ((0, 0), (0, lk - n_keys), (0, 0))).reshape(b * lk, LANES)
        tq, tk = l, lk
    kh, vh = _kvup(keys_c, w["w_kv"], heads)
    o = _flash(qn, qr, kh, keys_r, vh, batch=b, lq=l, lk=lk, q_pos0=past, n_keys=past + l,
               tq=tq, tk=tk, hb=heads)

    merged = _merge(ys.reshape(t, -1), o, w["w_glu"], w["w_o"], gates)
    x1, h2 = _resid(merged, w["w_out"], x, gt1, p["g_norm2"], sc2, sh2)

    h2 = h2.reshape(t, d)
    q = _mm_split(h2, w["peer_wq"], BF16)
    r2, c1, a1, a2 = _peer_topk(q, w["peer_k1"], w["peer_k2"], topk)
    x2 = _peer_mix(h2, w["peer_u"], w["peer_vt"], r2, c1, a1, a2, x1, gt2, g_final, last)
    return x2, ckv.reshape(b, l, kv_lora), kr.reshape(b, l, rope), s_fin


def kernel(x_prompt, x_sample, c_prompt, c_sample, cache_ckv, cache_krope, state_ssm_re, state_ssm_im, w_ada, b_ada, g_norm1, g_norm2, w_in, g_q, w_qu, g_kv, w_uk, w_uv, w_o, ssm_a_re, ssm_a_im, ssm_log_dt, ssm_b_re, ssm_b_im, ssm_c_re, ssm_c_im, ssm_d, w_glu, w_out, peer_wq, peer_k1, peer_k2, peer_u, peer_v, g_final):
    depth = w_in.shape[0]
    bp, lp, d = x_prompt.shape
    bs, ls, _ = x_sample.shape
    groups, states = ssm_a_re.shape[1:]
    heads, nope = w_uk.shape[2:]
    rope = cache_krope.shape[-1]
    dims = {
        "d": d, "sw": groups * ssm_b_re.shape[3], "q_lora": g_q.shape[1], "kv_lora": g_kv.shape[1],
        "rope": rope, "heads": heads, "nope": nope, "v_dim": w_uv.shape[3],
        "scale": math.log2(math.e) / math.sqrt(nope + rope), "topk": 16,
    }
    assert nope == LANES and dims["v_dim"] == LANES and rope <= LANES

    xp, xs = x_prompt, x_sample
    nb = bp + bs
    rows = -(-nb // 16) * 16
    c_all = jnp.pad(jnp.concatenate([c_prompt, c_sample], axis=0), ((0, rows - nb), (0, 0)))
    zeros = jnp.zeros((bp, 2, groups * states), F32)
    outs_p, outs_s = [], []
    for layer in range(depth):
        p = {
            "g_norm1": g_norm1[layer], "g_norm2": g_norm2[layer], "w_in": w_in[layer], "g_q": g_q[layer],
            "w_qu": w_qu[layer], "g_kv": g_kv[layer], "w_uk": w_uk[layer], "w_uv": w_uv[layer], "w_o": w_o[layer],
            "ssm_a_re": ssm_a_re[layer], "ssm_a_im": ssm_a_im[layer], "ssm_log_dt": ssm_log_dt[layer],
            "ssm_b_re": ssm_b_re[layer], "ssm_b_im": ssm_b_im[layer], "ssm_c_re": ssm_c_re[layer],
            "ssm_c_im": ssm_c_im[layer], "ssm_d": ssm_d[layer], "w_glu": w_glu[layer], "w_out": w_out[layer],
            "peer_wq": peer_wq[layer], "peer_k1": peer_k1[layer], "peer_k2": peer_k2[layer],
            "peer_u": peer_u[layer], "peer_v": peer_v[layer],
        }
        w = _prep_weights(p, dims)
        mod = _ada(c_all, w_ada[layer], b_ada[layer])
        mod_p = [m.reshape(bp, 1, d) for m in jnp.split(mod[:bp], 6, axis=-1)]
        mod_s = [m.reshape(bs, 1, d) for m in jnp.split(mod[bp:nb], 6, axis=-1)]
        s0_s = jnp.stack([state_ssm_re[layer].reshape(bs, -1), state_ssm_im[layer].reshape(bs, -1)], axis=1)
        last = layer == depth - 1
        xp, *res_p = _layer(xp, mod_p, None, None, zeros, p, w, dims, g_final, last)
        xs, *res_s = _layer(xs, mod_s, cache_ckv[layer], cache_krope[layer], s0_s, p, w, dims, g_final, last)
        for res, outs, g in ((res_p, outs_p, bp), (res_s, outs_s, bs)):
            ckv, kr, s_fin = res
            outs.append((ckv, kr, s_fin[:, 0].reshape(g, groups, states), s_fin[:, 1].reshape(g, groups, states)))
    stack = lambda outs, k: jnp.stack([o[k] for o in outs])
    return (xp, xs,
            stack(outs_p, 0), stack(outs_p, 1), stack(outs_p, 2), stack(outs_p, 3),
            stack(outs_s, 0), stack(outs_s, 1), stack(outs_s, 2), stack(outs_s, 3))
```

```python
import functools
import math

import jax
import jax.numpy as jnp
from jax import lax
from jax.experimental import pallas as pl
from jax.experimental.pallas import tpu as pltpu

F32 = jnp.float32
BF16 = jnp.bfloat16

EPS = 1e-6
CHUNK = 64
ROPE_THETA = 10000.0
LANES = 128
SUBLANES = 8
MXU_DIM = 256
VMEM_LIMIT_BYTES = 56 * 1024 * 1024
MASKED = -1e30
REMOVED = -3e38
EXCLUDED = -1e38


def _params(*semantics):
    return pltpu.CompilerParams(dimension_semantics=semantics, vmem_limit_bytes=VMEM_LIMIT_BYTES)


def _tile(n, pref):
    if n <= pref:
        return n
    t = pref
    while n % t:
        t //= 2
    assert t >= SUBLANES, (n, pref)
    return t


def _sigmoid(x):
    return 1.0 / (1.0 + jnp.exp(-x))


def _gelu(x):
    return 0.5 * x * (1.0 + jnp.tanh(math.sqrt(2.0 / math.pi) * (x + 0.044715 * (x * x * x))))


def _rms(x, g):
    return x * lax.rsqrt(jnp.mean(x * x, axis=-1, keepdims=True) + EPS) * g


def _ada_kernel(c_ref, w_ref, b_ref, o_ref):
    c = c_ref[...]
    a = (c * _sigmoid(c)).astype(BF16)
    o_ref[...] = jnp.dot(a, w_ref[...].astype(BF16), preferred_element_type=F32) + b_ref[...]


def _ada(c, w, b):
    rows, d = c.shape
    n = w.shape[1]
    tn = _tile(n, 1024)
    return pl.pallas_call(
        _ada_kernel,
        grid=(n // tn,),
        in_specs=[
            pl.BlockSpec((rows, d), lambda j: (0, 0)),
            pl.BlockSpec((d, tn), lambda j: (0, j)),
            pl.BlockSpec((1, tn), lambda j: (0, j)),
        ],
        out_specs=pl.BlockSpec((rows, tn), lambda j: (0, j)),
        out_shape=jax.ShapeDtypeStruct((rows, n), F32),
        compiler_params=_params("parallel"),
        name="ada",
    )(c, w, b.reshape(1, n))


def _normmod_kernel(x_ref, g_ref, sc_ref, sh_ref, o_ref):
    h = _rms(x_ref[0], g_ref[...]) * (1.0 + sc_ref[0]) + sh_ref[0]
    o_ref[0] = h.astype(BF16)


def _normmod(x, g, sc, sh):
    b, l, d = x.shape
    tm = _tile(l, 512)
    return pl.pallas_call(
        _normmod_kernel,
        grid=(b, l // tm),
        in_specs=[
            pl.BlockSpec((1, tm, d), lambda i, j: (i, j, 0)),
            pl.BlockSpec((1, d), lambda i, j: (0, 0)),
            pl.BlockSpec((1, 1, d), lambda i, j: (i, 0, 0)),
            pl.BlockSpec((1, 1, d), lambda i, j: (i, 0, 0)),
        ],
        out_specs=pl.BlockSpec((1, tm, d), lambda i, j: (i, j, 0)),
        out_shape=jax.ShapeDtypeStruct((b, l, d), BF16),
        compiler_params=_params("parallel", "parallel"),
        name="normmod",
    )(x, g.reshape(1, d), sc, sh)


def _mm_kernel(a_ref, w_ref, o_ref, *, act):
    z = jnp.dot(a_ref[...], w_ref[...], preferred_element_type=F32)
    if act == "sigmoid":
        z = _sigmoid(z)
    o_ref[...] = z.astype(o_ref.dtype)


def _mm(a, w, out_dtype, act=None, tn_pref=1024):
    t, k = a.shape
    n = w.shape[1]
    tm = _tile(t, 1024)
    tn = _tile(n, tn_pref)
    return pl.pallas_call(
        functools.partial(_mm_kernel, act=act),
        grid=(t // tm, n // tn),
        in_specs=[
            pl.BlockSpec((tm, k), lambda i, j: (i, 0)),
            pl.BlockSpec((k, tn), lambda i, j: (0, j)),
        ],
        out_specs=pl.BlockSpec((tm, tn), lambda i, j: (i, j)),
        out_shape=jax.ShapeDtypeStruct((t, n), out_dtype),
        compiler_params=_params("parallel", "parallel"),
        name="mm",
    )(a, w)


def _mm_split_kernel(a_ref, w_ref, o_ref, *, parts):
    z = jnp.dot(a_ref[...], w_ref[...], preferred_element_type=F32)
    for p in range(parts):
        o_ref[p] = z[:, p * LANES:(p + 1) * LANES].astype(o_ref.dtype)


def _mm_split(a, w, out_dtype):
    t, k = a.shape
    n = w.shape[1]
    tm = _tile(t, 512)
    tn = _tile(n, 1024)
    parts = tn // LANES
    return pl.pallas_call(
        functools.partial(_mm_split_kernel, parts=parts),
        grid=(t // tm, n // tn),
        in_specs=[
            pl.BlockSpec((tm, k), lambda i, j: (i, 0)),
            pl.BlockSpec((k, tn), lambda i, j: (0, j)),
        ],
        out_specs=pl.BlockSpec((parts, tm, LANES), lambda i, j: (j, i, 0)),
        out_shape=jax.ShapeDtypeStruct((n // LANES, t, LANES), out_dtype),
        compiler_params=_params("parallel", "parallel"),
        name="mm_split",
    )(a, w)


def _mla_proj_kernel(h_ref, wm_ref, gq_ref, gkv_ref, wq_ref, cos_ref, sin_ref,
                     qn_ref, qr_ref, ckv_ref, ckvb_ref, kr_ref, krp_ref,
                     *, heads, q_lora, kv_lora, rope, scale):
    z = jnp.dot(h_ref[...], wm_ref[...], preferred_element_type=F32)
    cos = cos_ref[...]
    sin = sin_ref[...]
    off = q_lora + kv_lora
    krp = z[:, off:off + LANES] * cos + z[:, off + LANES:off + 2 * LANES] * sin
    kr_ref[...] = krp[:, :rope]
    krp_ref[...] = krp.astype(BF16)
    ckv = _rms(z[:, q_lora:off], gkv_ref[...])
    ckv_ref[...] = ckv
    ckvb_ref[...] = ckv.astype(BF16)
    qd = _rms(z[:, :q_lora], gq_ref[...]).astype(BF16)
    zq = jnp.dot(qd, wq_ref[...], preferred_element_type=F32)
    hn = heads * LANES
    for hh in range(heads):
        lo = hh * LANES
        qn_ref[hh] = (zq[:, lo:lo + LANES] * scale).astype(BF16)
        qr = zq[:, hn + lo:hn + lo + LANES] * cos + zq[:, 2 * hn + lo:2 * hn + lo + LANES] * sin
        qr_ref[hh] = (qr * scale).astype(BF16)


def _mla_proj(h, wm, gq, gkv, wq, cos, sin, seq, *, heads, q_lora, kv_lora, rope, scale):
    t, d = h.shape
    tm = _tile(t, 256)
    if seq % tm == 0:
        nrep = seq // tm
        tab_map = lambda i: (i % nrep, 0)
    else:
        assert tm % seq == 0
        cos = jnp.tile(cos, (tm // seq, 1))
        sin = jnp.tile(sin, (tm // seq, 1))
        tab_map = lambda i: (0, 0)
    nm = wm.shape[1]
    nq = wq.shape[1]
    row = lambda i: (i, 0)
    const = lambda i: (0, 0)
    return pl.pallas_call(
        functools.partial(_mla_proj_kernel, heads=heads, q_lora=q_lora, kv_lora=kv_lora, rope=rope, scale=scale),
        grid=(t // tm,),
        in_specs=[
            pl.BlockSpec((tm, d), row),
            pl.BlockSpec((d, nm), const),
            pl.BlockSpec((1, q_lora), const),
            pl.BlockSpec((1, kv_lora), const),
            pl.BlockSpec((q_lora, nq), const),
            pl.BlockSpec((tm, LANES), tab_map),
            pl.BlockSpec((tm, LANES), tab_map),
        ],
        out_specs=[
            pl.BlockSpec((heads, tm, LANES), lambda i: (0, i, 0)),
            pl.BlockSpec((heads, tm, LANES), lambda i: (0, i, 0)),
            pl.BlockSpec((tm, kv_lora), row),
            pl.BlockSpec((tm, kv_lora), row),
            pl.BlockSpec((tm, rope), row),
            pl.BlockSpec((tm, LANES), row),
        ],
        out_shape=[
            jax.ShapeDtypeStruct((heads, t, LANES), BF16),
            jax.ShapeDtypeStruct((heads, t, LANES), BF16),
            jax.ShapeDtypeStruct((t, kv_lora), F32),
            jax.ShapeDtypeStruct((t, kv_lora), BF16),
            jax.ShapeDtypeStruct((t, rope), F32),
            jax.ShapeDtypeStruct((t, LANES), BF16),
        ],
        compiler_params=_params("parallel"),
        name="mla_proj",
    )(h, wm, gq.reshape(1, q_lora), gkv.reshape(1, kv_lora), wq, cos, sin)


def _kvup_kernel(c_ref, w_ref, k_ref, v_ref, *, heads):
    z = jnp.dot(c_ref[...], w_ref[...], preferred_element_type=F32)
    for hh in range(heads):
        k_ref[hh] = z[:, hh * LANES:(hh + 1) * LANES].astype(BF16)
        v_ref[hh] = z[:, (heads + hh) * LANES:(heads + hh + 1) * LANES].astype(BF16)


def _kvup(ckv, w, heads):
    t, c = ckv.shape
    tm = _tile(t, 512)
    if t % tm:
        tm = t
    return pl.pallas_call(
        functools.partial(_kvup_kernel, heads=heads),
        grid=(t // tm,),
        in_specs=[
            pl.BlockSpec((tm, c), lambda i: (i, 0)),
            pl.BlockSpec(w.shape, lambda i: (0, 0)),
        ],
        out_specs=[
            pl.BlockSpec((heads, tm, LANES), lambda i: (0, i, 0)),
            pl.BlockSpec((heads, tm, LANES), lambda i: (0, i, 0)),
        ],
        out_shape=[jax.ShapeDtypeStruct((heads, t, LANES), BF16)] * 2,
        compiler_params=_params("parallel"),
        name="kvup",
    )(ckv, w)


def _flash_kernel(iq_tab, ik_tab, flag_tab, qn_ref, qr_ref, k_ref, kr_ref, v_ref, o_ref,
                  qc_scr, m_scr, acc_scr, *, hb, tq, tk, q_pos0, n_keys):
    pair = pl.program_id(2)
    iq = iq_tab[pair]
    ik = ik_tab[pair]
    flags = flag_tab[pair]
    first = (flags & 1) != 0
    last = (flags & 2) != 0
    full = (flags & 4) != 0

    @pl.when(first)
    def _init():
        m_scr[...] = jnp.full(m_scr.shape, MASKED, F32)
        acc_scr[...] = jnp.zeros(acc_scr.shape, F32)
        for hh in range(hb):
            qc_scr[hh] = jnp.concatenate([qn_ref[hh], qr_ref[hh]], axis=1)

    def step(masked):
        kr = kr_ref[...]
        ones = jnp.ones((tk, LANES), BF16)
        if masked:
            qp = q_pos0 + iq * tq + lax.broadcasted_iota(jnp.int32, (tq, tk), 0)
            kp = ik * tk + lax.broadcasted_iota(jnp.int32, (tq, tk), 1)
            allowed = jnp.logical_and(kp // CHUNK <= qp // CHUNK, kp < n_keys)

        def head(hh, carry):
            kc = jnp.concatenate([k_ref[hh], kr], axis=1)
            s = lax.dot_general(qc_scr[hh], kc, (((1,), (1,)), ((), ())), preferred_element_type=F32)
            if masked:
                s = jnp.where(allowed, s, MASKED)
            m_prev = m_scr[hh]
            m_new = jnp.maximum(m_prev, jnp.max(s, axis=1, keepdims=True))
            alpha = jnp.exp2(m_prev - m_new)
            p = jnp.exp2(s - pltpu.repeat(m_new, tk // LANES, axis=1)).astype(BF16)
            vc = jnp.concatenate([v_ref[hh], ones], axis=1)
            acc_scr[hh] = pltpu.repeat(alpha, 2, axis=1) * acc_scr[hh] + jnp.dot(p, vc, preferred_element_type=F32)
            m_scr[hh] = m_new
            return carry

        lax.fori_loop(0, hb, head, 0, unroll=True)

    @pl.when(full)
    def _full():
        step(False)

    @pl.when(jnp.logical_not(full))
    def _diag():
        step(True)

    @pl.when(last)
    def _done():
        for hh in range(hb):
            acc = acc_scr[hh]
            o_ref[hh] = (acc[:, :LANES] / acc[:, LANES:]).astype(BF16)


def _flash_pairs(lq, lk, tq, tk, q_pos0, n_keys):
    iqs, iks, flags = [], [], []
    for iq in range(lq // tq):
        q_first = q_pos0 + iq * tq
        last_key = min(n_keys - 1, ((q_first + tq - 1) // CHUNK) * CHUNK + CHUNK - 1)
        full_key = min(n_keys - 1, (q_first // CHUNK) * CHUNK + CHUNK - 1)
        n_blocks = last_key // tk + 1
        for ik in range(n_blocks):
            full = (ik + 1) * tk - 1 <= full_key
            iqs.append(iq)
            iks.append(ik)
            flags.append((1 if ik == 0 else 0) | (2 if ik == n_blocks - 1 else 0) | (4 if full else 0))
    as_i32 = lambda xs: jnp.asarray(xs, jnp.int32)
    return as_i32(iqs), as_i32(iks), as_i32(flags)


def _flash(qn, qr, k, krp, v, *, batch, lq, lk, q_pos0, n_keys, tq, tk, hb):
    heads = qn.shape[0]
    nq = lq // tq
    nk = lk // tk
    assert lq % tq == 0 and lk % tk == 0 and heads % hb == 0 and tk % LANES == 0
    iq_tab, ik_tab, flag_tab = _flash_pairs(lq, lk, tq, tk, q_pos0, n_keys)
    qmap = lambda b, h, p, iqt, ikt, ft: (h, b * nq + iqt[p], 0)
    kmap = lambda b, h, p, iqt, ikt, ft: (h, b * nk + ikt[p], 0)
    return pl.pallas_call(
        functools.partial(_flash_kernel, hb=hb, tq=tq, tk=tk, q_pos0=q_pos0, n_keys=n_keys),
        grid_spec=pltpu.PrefetchScalarGridSpec(
            num_scalar_prefetch=3,
            grid=(batch, heads // hb, iq_tab.shape[0]),
            in_specs=[
                pl.BlockSpec((hb, tq, LANES), qmap),
                pl.BlockSpec((hb, tq, LANES), qmap),
                pl.BlockSpec((hb, tk, LANES), kmap),
                pl.BlockSpec((tk, LANES), lambda b, h, p, iqt, ikt, ft: (b * nk + ikt[p], 0)),
                pl.BlockSpec((hb, tk, LANES), kmap),
            ],
            out_specs=pl.BlockSpec((hb, tq, LANES), qmap),
            scratch_shapes=[
                pltpu.VMEM((hb, tq, 2 * LANES), BF16),
                pltpu.VMEM((hb, tq, LANES), F32),
                pltpu.VMEM((hb, tq, 2 * LANES), F32),
            ],
        ),
        out_shape=jax.ShapeDtypeStruct((heads, batch * lq, LANES), BF16),
        compiler_params=_params("parallel", "parallel", "arbitrary"),
        name="flash",
    )(iq_tab, ik_tab, flag_tab, qn, qr, k, krp, v)


def _s5_kernel(u_ref, wb_ref, scn_ref, wc_ref, d_ref, s0_ref, y_ref, st_ref, xs_ref, car_ref,
               *, ts, nkt, kw, sw, nt):
    it = pl.program_id(1)

    @pl.when(it == 0)
    def _load_state():
        car_ref[...] = s0_ref[0]

    for kt in range(nkt):
        ukt = u_ref[0, :, kt * kw:(kt + 1) * kw]
        xs_ref[...] = jnp.dot(ukt, wb_ref[kt], preferred_element_type=F32)
        st_sl = slice(kt * sw, (kt + 1) * sw)

        def blk(k, carry, kt=kt):
            cr, ci = carry
            rows = pl.ds(pl.multiple_of(k * SUBLANES, SUBLANES), SUBLANES)
            hr = xs_ref[rows, :sw]
            hi = xs_ref[rows, sw:]
            for c0, dist in ((0, 1), (2, 2), (4, 4)):
                ar = scn_ref[kt, c0]
                ai = scn_ref[kt, c0 + 1]
                sr = pltpu.roll(hr, dist, 0)
                si = pltpu.roll(hi, dist, 0)
                hr, hi = hr + ar * sr - ai * si, hi + ar * si + ai * sr
            pr = scn_ref[kt, 6]
            pi = scn_ref[kt, 7]
            crb = jnp.broadcast_to(cr, (SUBLANES, sw))
            cib = jnp.broadcast_to(ci, (SUBLANES, sw))
            hr, hi = hr + pr * crb - pi * cib, hi + pr * cib + pi * crb
            xs_ref[rows, :sw] = hr
            xs_ref[rows, sw:] = hi
            return hr[SUBLANES - 1:SUBLANES], hi[SUBLANES - 1:SUBLANES]

        cr, ci = lax.fori_loop(0, ts // SUBLANES, blk, (car_ref[0:1, st_sl], car_ref[1:2, st_sl]),
                               unroll=2 if ts >= 2 * SUBLANES else 1)
        car_ref[0:1, st_sl] = cr
        car_ref[1:2, st_sl] = ci
        y = jnp.dot(xs_ref[...].astype(BF16), wc_ref[kt], preferred_element_type=F32)
        y = y + d_ref[:, kt * kw:(kt + 1) * kw] * ukt.astype(F32)
        y_ref[0, :, kt * kw:(kt + 1) * kw] = _gelu(y).astype(BF16)

    @pl.when(it == nt - 1)
    def _store_state():
        st_ref[0] = car_ref[...]


def _s5(u, wb, scn, wc, dsk, s0):
    b, l, w = u.shape
    nkt, kw, sw2 = wb.shape
    sw = sw2 // 2
    ns = s0.shape[2]
    ts = _tile(l, 512)
    nt = l // ts
    const3 = lambda i, j: (0, 0, 0)
    return pl.pallas_call(
        functools.partial(_s5_kernel, ts=ts, nkt=nkt, kw=kw, sw=sw, nt=nt),
        grid=(b, nt),
        in_specs=[
            pl.BlockSpec((1, ts, w), lambda i, j: (i, j, 0)),
            pl.BlockSpec(wb.shape, const3),
            pl.BlockSpec(scn.shape, lambda i, j: (0, 0, 0, 0)),
            pl.BlockSpec(wc.shape, const3),
            pl.BlockSpec((1, w), lambda i, j: (0, 0)),
            pl.BlockSpec((1, 2, ns), lambda i, j: (i, 0, 0)),
        ],
        out_specs=[
            pl.BlockSpec((1, ts, w), lambda i, j: (i, j, 0)),
            pl.BlockSpec((1, 2, ns), lambda i, j: (i, 0, 0)),
        ],
        out_shape=[
            jax.ShapeDtypeStruct((b, l, w), BF16),
            jax.ShapeDtypeStruct((b, 2, ns), F32),
        ],
        scratch_shapes=[
            pltpu.VMEM((ts, sw2), F32),
            pltpu.VMEM((2, ns), F32),
        ],
        compiler_params=_params("parallel", "arbitrary"),
        name="s5",
    )(u, wb, scn, wc, dsk, s0)


def _s5_tables(a_re, a_im, log_dt, b_re, b_im, c_re, c_im, d_skip):
    g, p = a_re.shape
    ch = b_re.shape[2]
    gpt = MXU_DIM // ch
    nkt = g // gpt
    dt = jnp.exp(log_dt.astype(F32))[:, None]
    mag = jnp.exp(dt * a_re)
    ab_re = mag * jnp.cos(dt * a_im)
    ab_im = mag * jnp.sin(dt * a_im)
    den = a_re * a_re + a_im * a_im
    nr = ab_re - 1.0
    f_re = (nr * a_re + ab_im * a_im) / den
    f_im = (ab_im * a_re - nr * a_im) / den
    bb_re = f_re[..., None] * b_re - f_im[..., None] * b_im
    bb_im = f_re[..., None] * b_im + f_im[..., None] * b_re
    eye = jnp.eye(gpt, dtype=F32)

    def in_blockdiag(bb):
        return jnp.einsum("kgpi,gh->kgihp", bb.reshape(nkt, gpt, p, ch), eye).reshape(nkt, gpt * ch, gpt * p)

    def out_blockdiag(cc):
        return jnp.einsum("kgjp,gh->kgphj", cc.reshape(nkt, gpt, ch, p), eye).reshape(nkt, gpt * p, gpt * ch)

    wb = jnp.concatenate([in_blockdiag(bb_re), in_blockdiag(bb_im)], axis=2).astype(BF16)
    wc = jnp.concatenate([out_blockdiag(c_re), out_blockdiag(-c_im)], axis=1).astype(BF16)

    def cmul(x, y):
        return x[0] * y[0] - x[1] * y[1], x[0] * y[1] + x[1] * y[0]

    lam1 = (ab_re.reshape(nkt, gpt * p), ab_im.reshape(nkt, gpt * p))
    lam2 = cmul(lam1, lam1)
    lam4 = cmul(lam2, lam2)
    rows = jnp.arange(SUBLANES)[None, :, None]

    def shifted(lam, dist):
        return [jnp.where(rows >= dist, c[:, None, :], 0.0) for c in lam]

    pw = [lam1]
    for _ in range(SUBLANES - 1):
        pw.append(cmul(pw[-1], lam1))
    p_re = jnp.stack([c[0] for c in pw], axis=1)
    p_im = jnp.stack([c[1] for c in pw], axis=1)
    scn = jnp.stack(shifted(lam1, 1) + shifted(lam2, 2) + shifted(lam4, 4) + [p_re, p_im], axis=1)
    return wb, scn.astype(F32), wc, d_skip.reshape(1, g * ch).astype(F32)


def _merge_kernel(ys_ref, o_ref, wga_ref, wgb_ref, wo_ref, ga_ref, gb_ref, out_ref, *, heads):
    ys = ys_ref[...]
    ya = jnp.dot(ys, wga_ref[...], preferred_element_type=F32)
    ya = ya * _sigmoid(jnp.dot(ys, wgb_ref[...], preferred_element_type=F32))
    oc = jnp.concatenate([o_ref[hh] for hh in range(heads)], axis=1)
    yb = jnp.dot(oc, wo_ref[...], preferred_element_type=F32)
    out_ref[...] = (ga_ref[...].astype(F32) * ya + gb_ref[...].astype(F32) * yb).astype(BF16)


def _merge(ys, o, w_glu, w_o, gates):
    t, sw = ys.shape
    heads = o.shape[0]
    d = w_o.shape[1]
    tm = _tile(t, 512)
    tn = _tile(d, 1024)
    nj = d // tn
    return pl.pallas_call(
        functools.partial(_merge_kernel, heads=heads),
        grid=(t // tm, nj),
        in_specs=[
            pl.BlockSpec((tm, sw), lambda i, j: (i, 0)),
            pl.BlockSpec((heads, tm, LANES), lambda i, j: (0, i, 0)),
            pl.BlockSpec((sw, tn), lambda i, j: (0, j)),
            pl.BlockSpec((sw, tn), lambda i, j: (0, nj + j)),
            pl.BlockSpec((heads * LANES, tn), lambda i, j: (0, j)),
            pl.BlockSpec((tm, tn), lambda i, j: (i, j)),
            pl.BlockSpec((tm, tn), lambda i, j: (i, nj + j)),
        ],
        out_specs=pl.BlockSpec((tm, tn), lambda i, j: (i, j)),
        out_shape=jax.ShapeDtypeStruct((t, d), BF16),
        compiler_params=_params("parallel", "parallel"),
        name="merge",
    )(ys, o, w_glu, w_glu, w_o, gates, gates)


def _resid_kernel(m_ref, w_ref, x_ref, gt_ref, g_ref, sc_ref, sh_ref, x1_ref, h2_ref):
    x1 = x_ref[0] + gt_ref[0] * jnp.dot(m_ref[0], w_ref[...], preferred_element_type=F32)
    x1_ref[0] = x1
    h2_ref[0] = (_rms(x1, g_ref[...]) * (1.0 + sc_ref[0]) + sh_ref[0]).astype(BF16)


def _resid(merged, w_out, x, gt, g2, sc, sh):
    b, l, d = x.shape
    tm = _tile(l, 512)
    tok = lambda i, j: (i, j, 0)
    per_b = lambda i, j: (i, 0, 0)
    return pl.pallas_call(
        _resid_kernel,
        grid=(b, l // tm),
        in_specs=[
            pl.BlockSpec((1, tm, d), tok),
            pl.BlockSpec((d, d), lambda i, j: (0, 0)),
            pl.BlockSpec((1, tm, d), tok),
            pl.BlockSpec((1, 1, d), per_b),
            pl.BlockSpec((1, d), lambda i, j: (0, 0)),
            pl.BlockSpec((1, 1, d), per_b),
            pl.BlockSpec((1, 1, d), per_b),
        ],
        out_specs=[pl.BlockSpec((1, tm, d), tok), pl.BlockSpec((1, tm, d), tok)],
        out_shape=[jax.ShapeDtypeStruct((b, l, d), F32), jax.ShapeDtypeStruct((b, l, d), BF16)],
        compiler_params=_params("parallel", "parallel"),
        name="resid",
    )(merged.reshape(b, l, d), w_out, x, gt, g2.reshape(1, d), sc, sh)


def _peer_topk_kernel(q_ref, k1_ref, k2_ref, flat_ref, invalid_ref, r2_ref, c1_ref, a1_ref, a2_ref,
                      *, nh, nkeys, topk):
    tb = q_ref.shape[1]
    iota_k = lax.broadcasted_iota(jnp.int32, (nkeys, tb), 0).astype(F32)
    iota_t = lax.broadcasted_iota(jnp.int32, (topk, tb), 0).astype(F32)
    flat = flat_ref[...]
    invalid = invalid_ref[...]
    nt_dims = (((1,), (1,)), ((), ()))

    n_cand = flat.shape[0]

    def extract(s, exact_ties):
        work = s
        rank = jnp.full((nkeys, tb), float(topk), F32)
        vals = []
        for a in range(topk):
            m = jnp.max(work, axis=0, keepdims=True)
            if exact_ties:
                idx = jnp.min(jnp.where(work == m, iota_k, float(nkeys)), axis=0, keepdims=True)
                sel = iota_k == idx
            else:
                sel = work == m
            rank = jnp.where(sel, float(a), rank)
            work = jnp.where(sel, REMOVED, work)
            vals.append(m)
        ranked = jnp.sum(jnp.where(rank < float(topk), 1.0, 0.0), axis=0, keepdims=True)
        return vals, rank, ranked

    def stack(vals):
        out = jnp.zeros((topk, tb), F32)
        for a in range(topk):
            out = jnp.where(iota_t == float(a), vals[a], out)
        return out

    def candidates(v1, v2):
        vs1 = stack(v1)
        vs2 = stack(v2)
        blocks = [v1[0] + vs2]
        for a in range(1, SUBLANES):
            blocks.append(v1[a] + vs2[:SUBLANES])
        blocks.append(vs1[SUBLANES:] + v2[0])
        return jnp.concatenate(blocks, axis=0) + invalid

    def route(s1, s2, exact_ties):
        v1, rank1, n1 = extract(s1, exact_ties)
        v2, rank2, n2 = extract(s2, exact_ties)
        cand = candidates(v1, v2)
        counts = jnp.zeros((topk, tb), F32)
        top = None
        zsum = None
        for kk in range(topk):
            m = jnp.max(cand, axis=0, keepdims=True)
            if exact_ties:
                f = jnp.min(jnp.where(cand == m, flat, 1e9), axis=0, keepdims=True)
                cand = jnp.where(flat == f, REMOVED, cand)
                counts = counts + jnp.where(iota_t == jnp.floor(f * (1.0 / topk)), 1.0, 0.0)
            else:
                cand = jnp.where(cand == m, REMOVED, cand)
            if kk == 0:
                top = m
                zsum = jnp.ones_like(m)
            else:
                zsum = zsum + jnp.exp(m - top)
        n3 = None
        if not exact_ties:
            gone = jnp.where(cand == REMOVED, 1.0, 0.0)
            n3 = jnp.sum(gone, axis=0, keepdims=True)
            per_rank = [jnp.sum(gone[:topk], axis=0, keepdims=True)]
            for a in range(1, SUBLANES):
                lo = topk + (a - 1) * SUBLANES
                per_rank.append(jnp.sum(gone[lo:lo + SUBLANES], axis=0, keepdims=True))
            counts = jnp.concatenate([stack(per_rank + [per_rank[0]] * (topk - SUBLANES))[:SUBLANES],
                                      gone[n_cand - SUBLANES:]], axis=0)
        c1 = jnp.zeros((nkeys, tb), F32)
        for a in range(topk):
            c1 = jnp.where(rank1 == float(a), counts[a:a + 1], c1)
        a1 = jnp.exp(s1 - v1[0]) * (1.0 / zsum)
        a2 = jnp.exp(s2 - v2[0])
        clean = None
        if not exact_ties:
            want = float(topk)
            bad = jnp.where(n1 != want, 1.0, 0.0) + jnp.where(n2 != want, 1.0, 0.0) + jnp.where(n3 != want, 1.0, 0.0)
            clean = jnp.max(bad) == 0.0
        return (rank2, c1, a1, a2), clean

    def store(hh, tables):
        r2_ref[hh], c1_ref[hh], a1_ref[hh], a2_ref[hh] = tables

    def pair(pp, carry):
        heads = (2 * pp, 2 * pp + 1)
        scores = []
        for hh in heads:
            scores.append((lax.dot_general(k1_ref[hh], q_ref[2 * hh], nt_dims, preferred_element_type=F32),
                           lax.dot_general(k2_ref[hh], q_ref[2 * hh + 1], nt_dims, preferred_element_type=F32)))
        quick = [route(s1, s2, False) for s1, s2 in scores]
        clean = jnp.logical_and(quick[0][1], quick[1][1])

        @pl.when(clean)
        def _no_ties():
            for hh, (tables, _) in zip(heads, quick):
                store(hh, tables)

        @pl.when(jnp.logical_not(clean))
        def _ties():
            for hh, (s1, s2) in zip(heads, scores):
                store(hh, route(s1, s2, True)[0])

        return carry

    lax.fori_loop(0, nh // 2, pair, 0)


def _peer_topk(q, k1, k2, topk):
    nh, nkeys, half = k1.shape
    t = q.shape[1]
    tb = LANES
    assert topk == 2 * SUBLANES and half == LANES and t % tb == 0
    rows = jnp.arange(topk + (SUBLANES - 1) * SUBLANES + SUBLANES)
    a_idx = jnp.where(rows < topk, 0, jnp.where(rows < topk + (SUBLANES - 1) * SUBLANES,
                                                1 + (rows - topk) // SUBLANES, SUBLANES + (rows - topk - (SUBLANES - 1) * SUBLANES)))
    b_idx = jnp.where(rows < topk, rows, jnp.where(rows < topk + (SUBLANES - 1) * SUBLANES, (rows - topk) % SUBLANES, 0))
    flat = jnp.broadcast_to((a_idx * topk + b_idx).astype(F32)[:, None], (rows.shape[0], tb))
    invalid = jnp.broadcast_to(jnp.where((a_idx + 1) * (b_idx + 1) <= topk, 0.0, EXCLUDED).astype(F32)[:, None],
                               (rows.shape[0], tb))
    out = lambda dt: jax.ShapeDtypeStruct((nh, nkeys, t), dt)
    ospec = pl.BlockSpec((nh, nkeys, tb), lambda i: (0, 0, i))
    return pl.pallas_call(
        functools.partial(_peer_topk_kernel, nh=nh, nkeys=nkeys, topk=topk),
        grid=(t // tb,),
        in_specs=[
            pl.BlockSpec((2 * nh, tb, LANES), lambda i: (0, i, 0)),
            pl.BlockSpec(k1.shape, lambda i: (0, 0, 0)),
            pl.BlockSpec(k2.shape, lambda i: (0, 0, 0)),
            pl.BlockSpec(flat.shape, lambda i: (0, 0)),
            pl.BlockSpec(invalid.shape, lambda i: (0, 0)),
        ],
        out_specs=[ospec] * 4,
        out_shape=[out(F32)] * 4,
        compiler_params=_params("parallel"),
        name="peer_topk",
    )(q, k1, k2, flat, invalid)


def _peer_mix_kernel(h_ref, u_ref, vt_ref, r2_ref, c1_ref, a1_ref, a2_ref, *rest, nh, nkeys, ni, ne, norm):
    if len(rest) == 5:
        o_ref, ht_scr, w_scr, acc_scr, xt_scr = rest
        x_ref = gt_ref = g_ref = None
    else:
        x_ref, gt_ref, g_ref, o_ref, ht_scr, w_scr, acc_scr, xt_scr = rest
    e = pl.program_id(1)
    tb = h_ref.shape[0]

    @pl.when(e == 0)
    def _init():
        acc_scr[...] = jnp.zeros(acc_scr.shape, F32)
        xt_scr[...] = h_ref[...].T

    ht_scr[...] = jnp.dot(u_ref[...], xt_scr[...], preferred_element_type=F32)
    first_keys = pl.ds(pl.multiple_of(e * ni, SUBLANES), ni)
    for il in range(ni):
        rows = slice(il * nkeys, (il + 1) * nkeys)
        for lg in range(tb // LANES):
            sl = slice(lg * LANES, (lg + 1) * LANES)
            gate = jnp.zeros((nkeys, LANES), F32)
            for hh in range(nh):
                partners = c1_ref[hh, first_keys, sl][il:il + 1]
                first = a1_ref[hh, first_keys, sl][il:il + 1]
                gate = gate + jnp.where(r2_ref[hh, :, sl] < partners, a2_ref[hh, :, sl], 0.0) * first
            w_scr[rows, sl] = (gate * _gelu(ht_scr[rows, sl])).astype(BF16)
    acc_scr[...] += jnp.dot(vt_ref[0], w_scr[...], preferred_element_type=F32)

    @pl.when(e == ne - 1)
    def _done():
        mix = acc_scr[...].T
        if x_ref is None:
            o_ref[...] = mix
        else:
            y = x_ref[0] + gt_ref[0] * mix
            o_ref[0] = _rms(y, g_ref[...]) if norm else y


def _peer_mix(h2, u_tab, vt_tab, r2, c1, a1, a2, x1, gt, g, norm):
    b, l, _ = x1.shape
    t, d = h2.shape
    nh, nkeys, _ = r2.shape
    n_exp = u_tab.shape[0]
    tb = _tile(t, 512)
    ni = SUBLANES
    eb = ni * nkeys
    ne = n_exp // eb
    aux = pl.BlockSpec((nh, nkeys, tb), lambda i, e: (0, 0, i), pipeline_mode=pl.Buffered(1))
    in_specs = [
        pl.BlockSpec((tb, d), lambda i, e: (i, 0), pipeline_mode=pl.Buffered(1)),
        pl.BlockSpec((eb, d), lambda i, e: (e, 0)),
        pl.BlockSpec((1, d, eb), lambda i, e: (e, 0, 0)),
        aux, aux, aux, aux,
    ]
    args = [h2, u_tab, vt_tab, r2, c1, a1, a2]
    fused = l % tb == 0
    if fused:
        per_seq = l // tb
        in_specs += [
            pl.BlockSpec((1, tb, d), lambda i, e: (i // per_seq, i % per_seq, 0), pipeline_mode=pl.Buffered(1)),
            pl.BlockSpec((1, 1, d), lambda i, e: (i // per_seq, 0, 0)),
            pl.BlockSpec((1, d), lambda i, e: (0, 0)),
        ]
        args += [x1, gt, g.reshape(1, d)]
        out_spec = pl.BlockSpec((1, tb, d), lambda i, e: (i // per_seq, i % per_seq, 0))
        out_shape = jax.ShapeDtypeStruct((b, l, d), F32)
    else:
        out_spec = pl.BlockSpec((tb, d), lambda i, e: (i, 0))
        out_shape = jax.ShapeDtypeStruct((t, d), F32)
    out = pl.pallas_call(
        functools.partial(_peer_mix_kernel, nh=nh, nkeys=nkeys, ni=ni, ne=ne, norm=norm),
        grid=(t // tb, ne),
        in_specs=in_specs,
        out_specs=out_spec,
        out_shape=out_shape,
        scratch_shapes=[
            pltpu.VMEM((eb, tb), F32),
            pltpu.VMEM((eb, tb), BF16),
            pltpu.VMEM((d, tb), F32),
            pltpu.VMEM((d, tb), BF16),
        ],
        compiler_params=_params("parallel", "arbitrary"),
        name="peer_mix",
    )(*args)
    return out if fused else _final(x1, out, gt, g, norm)


def _final_kernel(x_ref, p_ref, gt_ref, g_ref, y_ref, *, norm):
    y = x_ref[0] + gt_ref[0] * p_ref[0]
    y_ref[0] = _rms(y, g_ref[...]) if norm else y


def _final(x1, peer, gt, g, norm):
    b, l, d = x1.shape
    tm = _tile(l, 512)
    tok = lambda i, j: (i, j, 0)
    return pl.pallas_call(
        functools.partial(_final_kernel, norm=norm),
        grid=(b, l // tm),
        in_specs=[
            pl.BlockSpec((1, tm, d), tok),
            pl.BlockSpec((1, tm, d), tok),
            pl.BlockSpec((1, 1, d), lambda i, j: (i, 0, 0)),
            pl.BlockSpec((1, d), lambda i, j: (0, 0)),
        ],
        out_specs=pl.BlockSpec((1, tm, d), tok),
        out_shape=jax.ShapeDtypeStruct((b, l, d), F32),
        compiler_params=_params("parallel", "parallel"),
        name="final",
    )(x1, peer.reshape(b, l, d), gt, g.reshape(1, d))


def _rope_tables(pos, rope):
    half = rope // 2
    inv = jnp.power(ROPE_THETA, -jnp.arange(half, dtype=F32) / half)
    ang = pos.astype(F32)[:, None] * inv
    reps = LANES // half
    return jnp.tile(jnp.cos(ang), (1, reps)), jnp.tile(jnp.sin(ang), (1, reps))


def _prep_weights(p, dims):
    d, sw, q_lora, kv_lora, rope, heads, nope = (dims[k] for k in ("d", "sw", "q_lora", "kv_lora", "rope", "heads", "nope"))
    half = rope // 2
    w_in = p["w_in"]
    off_q = sw
    off_kv = off_q + q_lora
    off_kr = off_kv + kv_lora
    off_g = off_kr + rope
    kr_w = w_in[:, off_kr:off_g]
    kr_rot = jnp.concatenate([-kr_w[:, half:], kr_w[:, :half]], axis=1)
    pad = jnp.zeros((d, LANES - rope), F32)
    w_mla = jnp.concatenate([w_in[:, off_q:off_kr], kr_w, pad, kr_rot, pad], axis=1).astype(BF16)
    wq = p["w_qu"].reshape(q_lora, heads, nope + rope)
    wq_n = wq[:, :, :nope].reshape(q_lora, heads * nope)
    wq_r = wq[:, :, nope:]
    wq_rot = jnp.concatenate([-wq_r[:, :, half:], wq_r[:, :, :half]], axis=2)
    hpad = jnp.zeros((q_lora, heads, LANES - rope), F32)
    wq_a = jnp.concatenate([wq_r, hpad], axis=2).reshape(q_lora, heads * LANES)
    wq_b = jnp.concatenate([wq_rot, hpad], axis=2).reshape(q_lora, heads * LANES)
    return {
        "w_gates": w_in[:, off_g:].astype(BF16),
        "w_u": w_in[:, :sw].astype(BF16),
        "w_mla": w_mla,
        "w_q": jnp.concatenate([wq_n, wq_a, wq_b], axis=1).astype(BF16),
        "w_kv": jnp.concatenate([p["w_uk"].reshape(kv_lora, heads * nope),
                                 p["w_uv"].reshape(kv_lora, heads * dims["v_dim"])], axis=1).astype(BF16),
        "w_o": p["w_o"].astype(BF16),
        "w_glu": p["w_glu"].astype(BF16),
        "w_out": p["w_out"].astype(BF16),
        "peer_wq": p["peer_wq"].astype(BF16),
        "peer_k1": p["peer_k1"].astype(BF16),
        "peer_k2": p["peer_k2"].astype(BF16),
        "peer_u": p["peer_u"].astype(BF16),
        "peer_vt": p["peer_v"].reshape(-1, SUBLANES * p["peer_k1"].shape[1], d).transpose(0, 2, 1).astype(BF16),
        "s5": _s5_tables(p["ssm_a_re"], p["ssm_a_im"], p["ssm_log_dt"], p["ssm_b_re"], p["ssm_b_im"],
                         p["ssm_c_re"], p["ssm_c_im"], p["ssm_d"]),
    }


def _layer(x, mod, past_ckv, past_kr, s0, p, w, dims, g_final, last):
    b, l, d = x.shape
    t = b * l
    heads, rope, kv_lora, topk = dims["heads"], dims["rope"], dims["kv_lora"], dims["topk"]
    sh1, sc1, gt1, sh2, sc2, gt2 = mod
    past = 0 if past_ckv is None else past_ckv.shape[1]

    h = _normmod(x, p["g_norm1"], sc1, sh1).reshape(t, d)
    gates = _mm(h, w["w_gates"], BF16, act="sigmoid")
    u = _mm(h, w["w_u"], BF16)
    cos, sin = _rope_tables(past + jnp.arange(l, dtype=jnp.int32), rope)
    qn, qr, ckv, ckv_b, kr, krp = _mla_proj(
        h, w["w_mla"], p["g_q"], p["g_kv"], w["w_q"], cos, sin, l,
        heads=heads, q_lora=dims["q_lora"], kv_lora=kv_lora, rope=rope, scale=dims["scale"])

    ys, s_fin = _s5(u.reshape(b, l, -1), *w["s5"], s0)

    if past_ckv is None:
        lk, keys_c, keys_r = l, ckv_b, krp
        tq = tk = _tile(l, 512)
    else:
        n_keys = past + l
        lk = -(-n_keys // LANES) * LANES
        keys_c = jnp.concatenate([past_ckv.astype(BF16), ckv_b.reshape(b, l, kv_lora)], axis=1)
        keys_c = jnp.pad(keys_c, ((0, 0), (0, lk - n_keys), (0, 0))).reshape(b * lk, kv_lora)
        past_r = jnp.pad(past_kr.astype(BF16), ((0, 0), (0, 0), (0, LANES - rope)))
        keys_r = jnp.concatenate([past_r, krp.reshape(b, l, LANES)], axis=1)
        keys_r = jnp.pad(keys_r, ((0, 0), (0, lk - n_keys), (0, 0))).reshape(b * lk, LANES)
        tq, tk = l, lk
    kh, vh = _kvup(keys_c, w["w_kv"], heads)
    o = _flash(qn, qr, kh, keys_r, vh, batch=b, lq=l, lk=lk, q_pos0=past, n_keys=past + l,
               tq=tq, tk=tk, hb=heads)

    merged = _merge(ys.reshape(t, -1), o, w["w_glu"], w["w_o"], gates)
    x1, h2 = _resid(merged, w["w_out"], x, gt1, p["g_norm2"], sc2, sh2)

    h2 = h2.reshape(t, d)
    q = _mm_split(h2, w["peer_wq"], BF16)
    r2, c1, a1, a2 = _peer_topk(q, w["peer_k1"], w["peer_k2"], topk)
    x2 = _peer_mix(h2, w["peer_u"], w["peer_vt"], r2, c1, a1, a2, x1, gt2, g_final, last)
    return x2, ckv.reshape(b, l, kv_lora), kr.reshape(b, l, rope), s_fin


def kernel(x_prompt, x_sample, c_prompt, c_sample, cache_ckv, cache_krope, state_ssm_re, state_ssm_im, w_ada, b_ada, g_norm1, g_norm2, w_in, g_q, w_qu, g_kv, w_uk, w_uv, w_o, ssm_a_re, ssm_a_im, ssm_log_dt, ssm_b_re, ssm_b_im, ssm_c_re, ssm_c_im, ssm_d, w_glu, w_out, peer_wq, peer_k1, peer_k2, peer_u, peer_v, g_final):
    depth = w_in.shape[0]
    bp, lp, d = x_prompt.shape
    bs, ls, _ = x_sample.shape
    groups, states = ssm_a_re.shape[1:]
    heads, nope = w_uk.shape[2:]
    rope = cache_krope.shape[-1]
    dims = {
        "d": d, "sw": groups * ssm_b_re.shape[3], "q_lora": g_q.shape[1], "kv_lora": g_kv.shape[1],
        "rope": rope, "heads": heads, "nope": nope, "v_dim": w_uv.shape[3],
        "scale": math.log2(math.e) / math.sqrt(nope + rope), "topk": 16,
    }
    assert nope == LANES and dims["v_dim"] == LANES and rope <= LANES

    xp, xs = x_prompt, x_sample
    nb = bp + bs
    rows = -(-nb // 16) * 16
    c_all = jnp.pad(jnp.concatenate([c_prompt, c_sample], axis=0), ((0, rows - nb), (0, 0)))
    zeros = jnp.zeros((bp, 2, groups * states), F32)
    outs_p, outs_s = [], []
    for layer in range(depth):
        p = {
            "g_norm1": g_norm1[layer], "g_norm2": g_norm2[layer], "w_in": w_in[layer], "g_q": g_q[layer],
            "w_qu": w_qu[layer], "g_kv": g_kv[layer], "w_uk": w_uk[layer], "w_uv": w_uv[layer], "w_o": w_o[layer],
            "ssm_a_re": ssm_a_re[layer], "ssm_a_im": ssm_a_im[layer], "ssm_log_dt": ssm_log_dt[layer],
            "ssm_b_re": ssm_b_re[layer], "ssm_b_im": ssm_b_im[layer], "ssm_c_re": ssm_c_re[layer],
            "ssm_c_im": ssm_c_im[layer], "ssm_d": ssm_d[layer], "w_glu": w_glu[layer], "w_out": w_out[layer],
            "peer_wq": peer_wq[layer], "peer_k1": peer_k1[layer], "peer_k2": peer_k2[layer],
            "peer_u": peer_u[layer], "peer_v": peer_v[layer],
        }
        w = _prep_weights(p, dims)
        mod = _ada(c_all, w_ada[layer], b_ada[layer])
        mod_p = [m.reshape(bp, 1, d) for m in jnp.split(mod[:bp], 6, axis=-1)]
        mod_s = [m.reshape(bs, 1, d) for m in jnp.split(mod[bp:nb], 6, axis=-1)]
        s0_s = jnp.stack([state_ssm_re[layer].reshape(bs, -1), state_ssm_im[layer].reshape(bs, -1)], axis=1)
        last = layer == depth - 1
        xp, *res_p = _layer(xp, mod_p, None, None, zeros, p, w, dims, g_final, last)
        xs, *res_s = _layer(xs, mod_s, cache_ckv[layer], cache_krope[layer], s0_s, p, w, dims, g_final, last)
        for res, outs, g in ((res_p, outs_p, bp), (res_s, outs_s, bs)):
            ckv, kr, s_fin = res
            outs.append((ckv, kr, s_fin[:, 0].reshape(g, groups, states), s_fin[:, 1].reshape(g, groups, states)))
    stack = lambda outs, k: jnp.stack([o[k] for o in outs])
    return (xp, xs,
            stack(outs_p, 0), stack(outs_p, 1), stack(outs_p, 2), stack(outs_p, 3),
            stack(outs_s, 0), stack(outs_s, 1), stack(outs_s, 2), stack(outs_s, 3))
```

```python
import functools
import math

import jax
import jax.numpy as jnp
from jax import lax
from jax.experimental import pallas as pl
from jax.experimental.pallas import tpu as pltpu

F32 = jnp.float32
BF16 = jnp.bfloat16

EPS = 1e-6
CHUNK = 64
ROPE_THETA = 10000.0
LANES = 128
SUBLANES = 8
MXU_DIM = 256
VMEM_LIMIT_BYTES = 56 * 1024 * 1024
MASKED = -1e30
REMOVED = -3e38
EXCLUDED = -1e38


def _params(*semantics):
    return pltpu.CompilerParams(dimension_semantics=semantics, vmem_limit_bytes=VMEM_LIMIT_BYTES)


def _tile(n, pref):
    if n <= pref:
        return n
    t = pref
    while n % t:
        t //= 2
    assert t >= SUBLANES, (n, pref)
    return t


def _sigmoid(x):
    return 1.0 / (1.0 + jnp.exp(-x))


def _gelu(x):
    return 0.5 * x * (1.0 + jnp.tanh(math.sqrt(2.0 / math.pi) * (x + 0.044715 * (x * x * x))))


def _rms(x, g):
    return x * lax.rsqrt(jnp.mean(x * x, axis=-1, keepdims=True) + EPS) * g


def _ada_kernel(c_ref, w_ref, b_ref, o_ref):
    c = c_ref[...]
    a = (c * _sigmoid(c)).astype(BF16)
    o_ref[...] = jnp.dot(a, w_ref[...].astype(BF16), preferred_element_type=F32) + b_ref[...]


def _ada(c, w, b):
    rows, d = c.shape
    n = w.shape[1]
    tn = _tile(n, 1024)
    return pl.pallas_call(
        _ada_kernel,
        grid=(n // tn,),
        in_specs=[
            pl.BlockSpec((rows, d), lambda j: (0, 0)),
            pl.BlockSpec((d, tn), lambda j: (0, j)),
            pl.BlockSpec((1, tn), lambda j: (0, j)),
        ],
        out_specs=pl.BlockSpec((rows, tn), lambda j: (0, j)),
        out_shape=jax.ShapeDtypeStruct((rows, n), F32),
        compiler_params=_params("parallel"),
        name="ada",
    )(c, w, b.reshape(1, n))


def _normmod_kernel(x_ref, g_ref, sc_ref, sh_ref, o_ref):
    h = _rms(x_ref[0], g_ref[...]) * (1.0 + sc_ref[0]) + sh_ref[0]
    o_ref[0] = h.astype(BF16)


def _normmod(x, g, sc, sh):
    b, l, d = x.shape
    tm = _tile(l, 512)
    return pl.pallas_call(
        _normmod_kernel,
        grid=(b, l // tm),
        in_specs=[
            pl.BlockSpec((1, tm, d), lambda i, j: (i, j, 0)),
            pl.BlockSpec((1, d), lambda i, j: (0, 0)),
            pl.BlockSpec((1, 1, d), lambda i, j: (i, 0, 0)),
            pl.BlockSpec((1, 1, d), lambda i, j: (i, 0, 0)),
        ],
        out_specs=pl.BlockSpec((1, tm, d), lambda i, j: (i, j, 0)),
        out_shape=jax.ShapeDtypeStruct((b, l, d), BF16),
        compiler_params=_params("parallel", "parallel"),
        name="normmod",
    )(x, g.reshape(1, d), sc, sh)


def _mm_kernel(a_ref, w_ref, o_ref, *, act):
    z = jnp.dot(a_ref[...], w_ref[...], preferred_element_type=F32)
    if act == "sigmoid":
        z = _sigmoid(z)
    o_ref[...] = z.astype(o_ref.dtype)


def _mm(a, w, out_dtype, act=None, tn_pref=1024):
    t, k = a.shape
    n = w.shape[1]
    tm = _tile(t, 1024)
    tn = _tile(n, tn_pref)
    return pl.pallas_call(
        functools.partial(_mm_kernel, act=act),
        grid=(t // tm, n // tn),
        in_specs=[
            pl.BlockSpec((tm, k), lambda i, j: (i, 0)),
            pl.BlockSpec((k, tn), lambda i, j: (0, j)),
        ],
        out_specs=pl.BlockSpec((tm, tn), lambda i, j: (i, j)),
        out_shape=jax.ShapeDtypeStruct((t, n), out_dtype),
        compiler_params=_params("parallel", "parallel"),
        name="mm",
    )(a, w)


def _mm_split_kernel(a_ref, w_ref, o_ref, *, parts):
    z = jnp.dot(a_ref[...], w_ref[...], preferred_element_type=F32)
    for p in range(parts):
        o_ref[p] = z[:, p * LANES:(p + 1) * LANES].astype(o_ref.dtype)


def _mm_split(a, w, out_dtype):
    t, k = a.shape
    n = w.shape[1]
    tm = _tile(t, 512)
    tn = _tile(n, 1024)
    parts = tn // LANES
    return pl.pallas_call(
        functools.partial(_mm_split_kernel, parts=parts),
        grid=(t // tm, n // tn),
        in_specs=[
            pl.BlockSpec((tm, k), lambda i, j: (i, 0)),
            pl.BlockSpec((k, tn), lambda i, j: (0, j)),
        ],
        out_specs=pl.BlockSpec((parts, tm, LANES), lambda i, j: (j, i, 0)),
        out_shape=jax.ShapeDtypeStruct((n // LANES, t, LANES), out_dtype),
        compiler_params=_params("parallel", "parallel"),
        name="mm_split",
    )(a, w)


def _mla_proj_kernel(h_ref, wm_ref, gq_ref, gkv_ref, wq_ref, cos_ref, sin_ref,
                     qn_ref, qr_ref, ckv_ref, ckvb_ref, kr_ref, krp_ref,
                     *, heads, q_lora, kv_lora, rope, scale):
    z = jnp.dot(h_ref[...], wm_ref[...], preferred_element_type=F32)
    cos = cos_ref[...]
    sin = sin_ref[...]
    off = q_lora + kv_lora
    krp = z[:, off:off + LANES] * cos + z[:, off + LANES:off + 2 * LANES] * sin
    kr_ref[...] = krp[:, :rope]
    krp_ref[...] = krp.astype(BF16)
    ckv = _rms(z[:, q_lora:off], gkv_ref[...])
    ckv_ref[...] = ckv
    ckvb_ref[...] = ckv.astype(BF16)
    qd = _rms(z[:, :q_lora], gq_ref[...]).astype(BF16)
    zq = jnp.dot(qd, wq_ref[...], preferred_element_type=F32)
    hn = heads * LANES
    for hh in range(heads):
        lo = hh * LANES
        qn_ref[hh] = (zq[:, lo:lo + LANES] * scale).astype(BF16)
        qr = zq[:, hn + lo:hn + lo + LANES] * cos + zq[:, 2 * hn + lo:2 * hn + lo + LANES] * sin
        qr_ref[hh] = (qr * scale).astype(BF16)


def _mla_proj(h, wm, gq, gkv, wq, cos, sin, seq, *, heads, q_lora, kv_lora, rope, scale):
    t, d = h.shape
    tm = _tile(t, 256)
    if seq % tm == 0:
        nrep = seq // tm
        tab_map = lambda i: (i % nrep, 0)
    else:
        assert tm % seq == 0
        cos = jnp.tile(cos, (tm // seq, 1))
        sin = jnp.tile(sin, (tm // seq, 1))
        tab_map = lambda i: (0, 0)
    nm = wm.shape[1]
    nq = wq.shape[1]
    row = lambda i: (i, 0)
    const = lambda i: (0, 0)
    return pl.pallas_call(
        functools.partial(_mla_proj_kernel, heads=heads, q_lora=q_lora, kv_lora=kv_lora, rope=rope, scale=scale),
        grid=(t // tm,),
        in_specs=[
            pl.BlockSpec((tm, d), row),
            pl.BlockSpec((d, nm), const),
            pl.BlockSpec((1, q_lora), const),
            pl.BlockSpec((1, kv_lora), const),
            pl.BlockSpec((q_lora, nq), const),
            pl.BlockSpec((tm, LANES), tab_map),
            pl.BlockSpec((tm, LANES), tab_map),
        ],
        out_specs=[
            pl.BlockSpec((heads, tm, LANES), lambda i: (0, i, 0)),
            pl.BlockSpec((heads, tm, LANES), lambda i: (0, i, 0)),
            pl.BlockSpec((tm, kv_lora), row),
            pl.BlockSpec((tm, kv_lora), row),
            pl.BlockSpec((tm, rope), row),
            pl.BlockSpec((tm, LANES), row),
        ],
        out_shape=[
            jax.ShapeDtypeStruct((heads, t, LANES), BF16),
            jax.ShapeDtypeStruct((heads, t, LANES), BF16),
            jax.ShapeDtypeStruct((t, kv_lora), F32),
            jax.ShapeDtypeStruct((t, kv_lora), BF16),
            jax.ShapeDtypeStruct((t, rope), F32),
            jax.ShapeDtypeStruct((t, LANES), BF16),
        ],
        compiler_params=_params("parallel"),
        name="mla_proj",
    )(h, wm, gq.reshape(1, q_lora), gkv.reshape(1, kv_lora), wq, cos, sin)


def _kvup_kernel(c_ref, w_ref, k_ref, v_ref, *, heads):
    z = jnp.dot(c_ref[...], w_ref[...], preferred_element_type=F32)
    for hh in range(heads):
        k_ref[hh] = z[:, hh * LANES:(hh + 1) * LANES].astype(BF16)
        v_ref[hh] = z[:, (heads + hh) * LANES:(heads + hh + 1) * LANES].astype(BF16)


def _kvup(ckv, w, heads):
    t, c = ckv.shape
    tm = _tile(t, 512)
    if t % tm:
        tm = t
    return pl.pallas_call(
        functools.partial(_kvup_kernel, heads=heads),
        grid=(t // tm,),
        in_specs=[
            pl.BlockSpec((tm, c), lambda i: (i, 0)),
            pl.BlockSpec(w.shape, lambda i: (0, 0)),
        ],
        out_specs=[
            pl.BlockSpec((heads, tm, LANES), lambda i: (0, i, 0)),
            pl.BlockSpec((heads, tm, LANES), lambda i: (0, i, 0)),
        ],
        out_shape=[jax.ShapeDtypeStruct((heads, t, LANES), BF16)] * 2,
        compiler_params=_params("parallel"),
        name="kvup",
    )(ckv, w)


def _flash_kernel(iq_tab, ik_tab, flag_tab, qn_ref, qr_ref, k_ref, kr_ref, v_ref, o_ref,
                  qc_scr, m_scr, acc_scr, *, hb, tq, tk, q_pos0, n_keys):
    pair = pl.program_id(2)
    iq = iq_tab[pair]
    ik = ik_tab[pair]
    flags = flag_tab[pair]
    first = (flags & 1) != 0
    last = (flags & 2) != 0
    full = (flags & 4) != 0

    @pl.when(first)
    def _init():
        m_scr[...] = jnp.full(m_scr.shape, MASKED, F32)
        acc_scr[...] = jnp.zeros(acc_scr.shape, F32)
        for hh in range(hb):
            qc_scr[hh] = jnp.concatenate([qn_ref[hh], qr_ref[hh]], axis=1)

    def step(masked):
        kr = kr_ref[...]
        ones = jnp.ones((tk, LANES), BF16)
        if masked:
            qp = q_pos0 + iq * tq + lax.broadcasted_iota(jnp.int32, (tq, tk), 0)
            kp = ik * tk + lax.broadcasted_iota(jnp.int32, (tq, tk), 1)
            allowed = jnp.logical_and(kp // CHUNK <= qp // CHUNK, kp < n_keys)

        def head(hh, carry):
            kc = jnp.concatenate([k_ref[hh], kr], axis=1)
            s = lax.dot_general(qc_scr[hh], kc, (((1,), (1,)), ((), ())), preferred_element_type=F32)
            if masked:
                s = jnp.where(allowed, s, MASKED)
            m_prev = m_scr[hh]
            m_new = jnp.maximum(m_prev, jnp.max(s, axis=1, keepdims=True))
            alpha = jnp.exp2(m_prev - m_new)
            p = jnp.exp2(s - pltpu.repeat(m_new, tk // LANES, axis=1)).astype(BF16)
            vc = jnp.concatenate([v_ref[hh], ones], axis=1)
            acc_scr[hh] = pltpu.repeat(alpha, 2, axis=1) * acc_scr[hh] + jnp.dot(p, vc, preferred_element_type=F32)
            m_scr[hh] = m_new
            return carry

        lax.fori_loop(0, hb, head, 0, unroll=True)

    @pl.when(full)
    def _full():
        step(False)

    @pl.when(jnp.logical_not(full))
    def _diag():
        step(True)

    @pl.when(last)
    def _done():
        for hh in range(hb):
            acc = acc_scr[hh]
            o_ref[hh] = (acc[:, :LANES] / acc[:, LANES:]).astype(BF16)


def _flash_pairs(lq, lk, tq, tk, q_pos0, n_keys):
    iqs, iks, flags = [], [], []
    for iq in range(lq // tq):
        q_first = q_pos0 + iq * tq
        last_key = min(n_keys - 1, ((q_first + tq - 1) // CHUNK) * CHUNK + CHUNK - 1)
        full_key = min(n_keys - 1, (q_first // CHUNK) * CHUNK + CHUNK - 1)
        n_blocks = last_key // tk + 1
        for ik in range(n_blocks):
            full = (ik + 1) * tk - 1 <= full_key
            iqs.append(iq)
            iks.append(ik)
            flags.append((1 if ik == 0 else 0) | (2 if ik == n_blocks - 1 else 0) | (4 if full else 0))
    as_i32 = lambda xs: jnp.asarray(xs, jnp.int32)
    return as_i32(iqs), as_i32(iks), as_i32(flags)


def _flash(qn, qr, k, krp, v, *, batch, lq, lk, q_pos0, n_keys, tq, tk, hb):
    heads = qn.shape[0]
    nq = lq // tq
    nk = lk // tk
    assert lq % tq == 0 and lk % tk == 0 and heads % hb == 0 and tk % LANES == 0
    iq_tab, ik_tab, flag_tab = _flash_pairs(lq, lk, tq, tk, q_pos0, n_keys)
    qmap = lambda b, h, p, iqt, ikt, ft: (h, b * nq + iqt[p], 0)
    kmap = lambda b, h, p, iqt, ikt, ft: (h, b * nk + ikt[p], 0)
    return pl.pallas_call(
        functools.partial(_flash_kernel, hb=hb, tq=tq, tk=tk, q_pos0=q_pos0, n_keys=n_keys),
        grid_spec=pltpu.PrefetchScalarGridSpec(
            num_scalar_prefetch=3,
            grid=(batch, heads // hb, iq_tab.shape[0]),
            in_specs=[
                pl.BlockSpec((hb, tq, LANES), qmap),
                pl.BlockSpec((hb, tq, LANES), qmap),
                pl.BlockSpec((hb, tk, LANES), kmap),
                pl.BlockSpec((tk, LANES), lambda b, h, p, iqt, ikt, ft: (b * nk + ikt[p], 0)),
                pl.BlockSpec((hb, tk, LANES), kmap),
            ],
            out_specs=pl.BlockSpec((hb, tq, LANES), qmap),
            scratch_shapes=[
                pltpu.VMEM((hb, tq, 2 * LANES), BF16),
                pltpu.VMEM((hb, tq, LANES), F32),
                pltpu.VMEM((hb, tq, 2 * LANES), F32),
            ],
        ),
        out_shape=jax.ShapeDtypeStruct((heads, batch * lq, LANES), BF16),
        compiler_params=_params("parallel", "parallel", "arbitrary"),
        name="flash",
    )(iq_tab, ik_tab, flag_tab, qn, qr, k, krp, v)


def _s5_kernel(u_ref, wb_ref, scn_ref, wc_ref, d_ref, s0_ref, y_ref, st_ref, xs_ref, car_ref,
               *, ts, nkt, kw, sw, nt):
    it = pl.program_id(1)

    @pl.when(it == 0)
    def _load_state():
        car_ref[...] = s0_ref[0]

    for kt in range(nkt):
        ukt = u_ref[0, :, kt * kw:(kt + 1) * kw]
        xs_ref[...] = jnp.dot(ukt, wb_ref[kt], preferred_element_type=F32)
        st_sl = slice(kt * sw, (kt + 1) * sw)

        def blk(k, carry, kt=kt):
            cr, ci = carry
            rows = pl.ds(pl.multiple_of(k * SUBLANES, SUBLANES), SUBLANES)
            hr = xs_ref[rows, :sw]
            hi = xs_ref[rows, sw:]
            for c0, dist in ((0, 1), (2, 2), (4, 4)):
                ar = scn_ref[kt, c0]
                ai = scn_ref[kt, c0 + 1]
                sr = pltpu.roll(hr, dist, 0)
                si = pltpu.roll(hi, dist, 0)
                hr, hi = hr + ar * sr - ai * si, hi + ar * si + ai * sr
            pr = scn_ref[kt, 6]
            pi = scn_ref[kt, 7]
            crb = jnp.broadcast_to(cr, (SUBLANES, sw))
            cib = jnp.broadcast_to(ci, (SUBLANES, sw))
            hr, hi = hr + pr * crb - pi * cib, hi + pr * cib + pi * crb
            xs_ref[rows, :sw] = hr
            xs_ref[rows, sw:] = hi
            return hr[SUBLANES - 1:SUBLANES], hi[SUBLANES - 1:SUBLANES]

        cr, ci = lax.fori_loop(0, ts // SUBLANES, blk, (car_ref[0:1, st_sl], car_ref[1:2, st_sl]),
                               unroll=2 if ts >= 2 * SUBLANES else 1)
        car_ref[0:1, st_sl] = cr
        car_ref[1:2, st_sl] = ci
        y = jnp.dot(xs_ref[...].astype(BF16), wc_ref[kt], preferred_element_type=F32)
        y = y + d_ref[:, kt * kw:(kt + 1) * kw] * ukt.astype(F32)
        y_ref[0, :, kt * kw:(kt + 1) * kw] = _gelu(y).astype(BF16)

    @pl.when(it == nt - 1)
    def _store_state():
        st_ref[0] = car_ref[...]


def _s5(u, wb, scn, wc, dsk, s0):
    b, l, w = u.shape
    nkt, kw, sw2 = wb.shape
    sw = sw2 // 2
    ns = s0.shape[2]
    ts = _tile(l, 512)
    nt = l // ts
    const3 = lambda i, j: (0, 0, 0)
    return pl.pallas_call(
        functools.partial(_s5_kernel, ts=ts, nkt=nkt, kw=kw, sw=sw, nt=nt),
        grid=(b, nt),
        in_specs=[
            pl.BlockSpec((1, ts, w), lambda i, j: (i, j, 0)),
            pl.BlockSpec(wb.shape, const3),
            pl.BlockSpec(scn.shape, lambda i, j: (0, 0, 0, 0)),
            pl.BlockSpec(wc.shape, const3),
            pl.BlockSpec((1, w), lambda i, j: (0, 0)),
            pl.BlockSpec((1, 2, ns), lambda i, j: (i, 0, 0)),
        ],
        out_specs=[
            pl.BlockSpec((1, ts, w), lambda i, j: (i, j, 0)),
            pl.BlockSpec((1, 2, ns), lambda i, j: (i, 0, 0)),
        ],
        out_shape=[
            jax.ShapeDtypeStruct((b, l, w), BF16),
            jax.ShapeDtypeStruct((b, 2, ns), F32),
        ],
        scratch_shapes=[
            pltpu.VMEM((ts, sw2), F32),
            pltpu.VMEM((2, ns), F32),
        ],
        compiler_params=_params("parallel", "arbitrary"),
        name="s5",
    )(u, wb, scn, wc, dsk, s0)


def _s5_tables(a_re, a_im, log_dt, b_re, b_im, c_re, c_im, d_skip):
    g, p = a_re.shape
    ch = b_re.shape[2]
    gpt = MXU_DIM // ch
    nkt = g // gpt
    dt = jnp.exp(log_dt.astype(F32))[:, None]
    mag = jnp.exp(dt * a_re)
    ab_re = mag * jnp.cos(dt * a_im)
    ab_im = mag * jnp.sin(dt * a_im)
    den = a_re * a_re + a_im * a_im
    nr = ab_re - 1.0
    f_re = (nr * a_re + ab_im * a_im) / den
    f_im = (ab_im * a_re - nr * a_im) / den
    bb_re = f_re[..., None] * b_re - f_im[..., None] * b_im
    bb_im = f_re[..., None] * b_im + f_im[..., None] * b_re
    eye = jnp.eye(gpt, dtype=F32)

    def in_blockdiag(bb):
        return jnp.einsum("kgpi,gh->kgihp", bb.reshape(nkt, gpt, p, ch), eye).reshape(nkt, gpt * ch, gpt * p)

    def out_blockdiag(cc):
        return jnp.einsum("kgjp,gh->kgphj", cc.reshape(nkt, gpt, ch, p), eye).reshape(nkt, gpt * p, gpt * ch)

    wb = jnp.concatenate([in_blockdiag(bb_re), in_blockdiag(bb_im)], axis=2).astype(BF16)
    wc = jnp.concatenate([out_blockdiag(c_re), out_blockdiag(-c_im)], axis=1).astype(BF16)

    def cmul(x, y):
        return x[0] * y[0] - x[1] * y[1], x[0] * y[1] + x[1] * y[0]

    lam1 = (ab_re.reshape(nkt, gpt * p), ab_im.reshape(nkt, gpt * p))
    lam2 = cmul(lam1, lam1)
    lam4 = cmul(lam2, lam2)
    rows = jnp.arange(SUBLANES)[None, :, None]

    def shifted(lam, dist):
        return [jnp.where(rows >= dist, c[:, None, :], 0.0) for c in lam]

    pw = [lam1]
    for _ in range(SUBLANES - 1):
        pw.append(cmul(pw[-1], lam1))
    p_re = jnp.stack([c[0] for c in pw], axis=1)
    p_im = jnp.stack([c[1] for c in pw], axis=1)
    scn = jnp.stack(shifted(lam1, 1) + shifted(lam2, 2) + shifted(lam4, 4) + [p_re, p_im], axis=1)
    return wb, scn.astype(F32), wc, d_skip.reshape(1, g * ch).astype(F32)


def _merge_kernel(ys_ref, o_ref, wga_ref, wgb_ref, wo_ref, ga_ref, gb_ref, out_ref, *, heads):
    ys = ys_ref[...]
    ya = jnp.dot(ys, wga_ref[...], preferred_element_type=F32)
    ya = ya * _sigmoid(jnp.dot(ys, wgb_ref[...], preferred_element_type=F32))
    oc = jnp.concatenate([o_ref[hh] for hh in range(heads)], axis=1)
    yb = jnp.dot(oc, wo_ref[...], preferred_element_type=F32)
    out_ref[...] = (ga_ref[...].astype(F32) * ya + gb_ref[...].astype(F32) * yb).astype(BF16)


def _merge(ys, o, w_glu, w_o, gates):
    t, sw = ys.shape
    heads = o.shape[0]
    d = w_o.shape[1]
    tm = _tile(t, 512)
    tn = _tile(d, 1024)
    nj = d // tn
    return pl.pallas_call(
        functools.partial(_merge_kernel, heads=heads),
        grid=(t // tm, nj),
        in_specs=[
            pl.BlockSpec((tm, sw), lambda i, j: (i, 0)),
            pl.BlockSpec((heads, tm, LANES), lambda i, j: (0, i, 0)),
            pl.BlockSpec((sw, tn), lambda i, j: (0, j)),
            pl.BlockSpec((sw, tn), lambda i, j: (0, nj + j)),
            pl.BlockSpec((heads * LANES, tn), lambda i, j: (0, j)),
            pl.BlockSpec((tm, tn), lambda i, j: (i, j)),
            pl.BlockSpec((tm, tn), lambda i, j: (i, nj + j)),
        ],
        out_specs=pl.BlockSpec((tm, tn), lambda i, j: (i, j)),
        out_shape=jax.ShapeDtypeStruct((t, d), BF16),
        compiler_params=_params("parallel", "parallel"),
        name="merge",
    )(ys, o, w_glu, w_glu, w_o, gates, gates)


def _resid_kernel(m_ref, w_ref, x_ref, gt_ref, g_ref, sc_ref, sh_ref, x1_ref, h2_ref):
    x1 = x_ref[0] + gt_ref[0] * jnp.dot(m_ref[0], w_ref[...], preferred_element_type=F32)
    x1_ref[0] = x1
    h2_ref[0] = (_rms(x1, g_ref[...]) * (1.0 + sc_ref[0]) + sh_ref[0]).astype(BF16)


def _resid(merged, w_out, x, gt, g2, sc, sh):
    b, l, d = x.shape
    tm = _tile(l, 512)
    tok = lambda i, j: (i, j, 0)
    per_b = lambda i, j: (i, 0, 0)
    return pl.pallas_call(
        _resid_kernel,
        grid=(b, l // tm),
        in_specs=[
            pl.BlockSpec((1, tm, d), tok),
            pl.BlockSpec((d, d), lambda i, j: (0, 0)),
            pl.BlockSpec((1, tm, d), tok),
            pl.BlockSpec((1, 1, d), per_b),
            pl.BlockSpec((1, d), lambda i, j: (0, 0)),
            pl.BlockSpec((1, 1, d), per_b),
            pl.BlockSpec((1, 1, d), per_b),
        ],
        out_specs=[pl.BlockSpec((1, tm, d), tok), pl.BlockSpec((1, tm, d), tok)],
        out_shape=[jax.ShapeDtypeStruct((b, l, d), F32), jax.ShapeDtypeStruct((b, l, d), BF16)],
        compiler_params=_params("parallel", "parallel"),
        name="resid",
    )(merged.reshape(b, l, d), w_out, x, gt, g2.reshape(1, d), sc, sh)


def _peer_topk_kernel(q_ref, k1_ref, k2_ref, flat_ref, invalid_ref, r2_ref, c1_ref, a1_ref, a2_ref,
                      *, nh, nkeys, topk):
    tb = q_ref.shape[1]
    iota_k = lax.broadcasted_iota(jnp.int32, (nkeys, tb), 0).astype(F32)
    iota_t = lax.broadcasted_iota(jnp.int32, (topk, tb), 0).astype(F32)
    flat = flat_ref[...]
    invalid = invalid_ref[...]
    nt_dims = (((1,), (1,)), ((), ()))

    n_cand = flat.shape[0]

    def extract(s, exact_ties):
        work = s
        rank = jnp.full((nkeys, tb), float(topk), F32)
        vals = []
        for a in range(topk):
            m = jnp.max(work, axis=0, keepdims=True)
            if exact_ties:
                idx = jnp.min(jnp.where(work == m, iota_k, float(nkeys)), axis=0, keepdims=True)
                sel = iota_k == idx
            else:
                sel = work == m
            rank = jnp.where(sel, float(a), rank)
            work = jnp.where(sel, REMOVED, work)
            vals.append(m)
        ranked = jnp.sum(jnp.where(rank < float(topk), 1.0, 0.0), axis=0, keepdims=True)
        return vals, rank, ranked

    def stack(vals):
        out = jnp.zeros((topk, tb), F32)
        for a in range(topk):
            out = jnp.where(iota_t == float(a), vals[a], out)
        return out

    def candidates(v1, v2):
        vs1 = stack(v1)
        vs2 = stack(v2)
        blocks = [v1[0] + vs2]
        for a in range(1, SUBLANES):
            blocks.append(v1[a] + vs2[:SUBLANES])
        blocks.append(vs1[SUBLANES:] + v2[0])
        return jnp.concatenate(blocks, axis=0) + invalid

    def route(s1, s2, exact_ties):
        v1, rank1, n1 = extract(s1, exact_ties)
        v2, rank2, n2 = extract(s2, exact_ties)
        cand = candidates(v1, v2)
        counts = jnp.zeros((topk, tb), F32)
        top = None
        zsum = None
        for kk in range(topk):
            m = jnp.max(cand, axis=0, keepdims=True)
            if exact_ties:
                f = jnp.min(jnp.where(cand == m, flat, 1e9), axis=0, keepdims=True)
                cand = jnp.where(flat == f, REMOVED, cand)
                counts = counts + jnp.where(iota_t == jnp.floor(f * (1.0 / topk)), 1.0, 0.0)
            else:
                cand = jnp.where(cand == m, REMOVED, cand)
            if kk == 0:
                top = m
                zsum = jnp.ones_like(m)
            else:
                zsum = zsum + jnp.exp(m - top)
        n3 = None
        if not exact_ties:
            gone = jnp.where(cand == REMOVED, 1.0, 0.0)
            n3 = jnp.sum(gone, axis=0, keepdims=True)
            per_rank = [jnp.sum(gone[:topk], axis=0, keepdims=True)]
            for a in range(1, SUBLANES):
                lo = topk + (a - 1) * SUBLANES
                per_rank.append(jnp.sum(gone[lo:lo + SUBLANES], axis=0, keepdims=True))
            counts = jnp.concatenate([stack(per_rank + [per_rank[0]] * (topk - SUBLANES))[:SUBLANES],
                                      gone[n_cand - SUBLANES:]], axis=0)
        c1 = jnp.zeros((nkeys, tb), F32)
        for a in range(topk):
            c1 = jnp.where(rank1 == float(a), counts[a:a + 1], c1)
        a1 = jnp.exp(s1 - v1[0]) * (1.0 / zsum)
        a2 = jnp.exp(s2 - v2[0])
        clean = None
        if not exact_ties:
            want = float(topk)
            bad = jnp.where(n1 != want, 1.0, 0.0) + jnp.where(n2 != want, 1.0, 0.0) + jnp.where(n3 != want, 1.0, 0.0)
            clean = jnp.max(bad) == 0.0
        return (rank2, c1, a1, a2), clean

    def store(hh, tables):
        r2_ref[hh], c1_ref[hh], a1_ref[hh], a2_ref[hh] = tables

    per_trip = 4 if nh % 4 == 0 else 2

    def group(gg, carry):
        heads = [per_trip * gg + r for r in range(per_trip)]
        scores = []
        for hh in heads:
            scores.append((lax.dot_general(k1_ref[hh], q_ref[2 * hh], nt_dims, preferred_element_type=F32),
                           lax.dot_general(k2_ref[hh], q_ref[2 * hh + 1], nt_dims, preferred_element_type=F32)))
        quick = [route(s1, s2, False) for s1, s2 in scores]
        clean = quick[0][1]
        for _, ok in quick[1:]:
            clean = jnp.logical_and(clean, ok)

        @pl.when(clean)
        def _no_ties():
            for hh, (tables, _) in zip(heads, quick):
                store(hh, tables)

        @pl.when(jnp.logical_not(clean))
        def _ties():
            for hh, (s1, s2) in zip(heads, scores):
                store(hh, route(s1, s2, True)[0])

        return carry

    lax.fori_loop(0, nh // per_trip, group, 0)


def _peer_topk(q, k1, k2, topk):
    nh, nkeys, half = k1.shape
    t = q.shape[1]
    tb = LANES
    assert topk == 2 * SUBLANES and half == LANES and t % tb == 0
    rows = jnp.arange(topk + (SUBLANES - 1) * SUBLANES + SUBLANES)
    a_idx = jnp.where(rows < topk, 0, jnp.where(rows < topk + (SUBLANES - 1) * SUBLANES,
                                                1 + (rows - topk) // SUBLANES, SUBLANES + (rows - topk - (SUBLANES - 1) * SUBLANES)))
    b_idx = jnp.where(rows < topk, rows, jnp.where(rows < topk + (SUBLANES - 1) * SUBLANES, (rows - topk) % SUBLANES, 0))
    flat = jnp.broadcast_to((a_idx * topk + b_idx).astype(F32)[:, None], (rows.shape[0], tb))
    invalid = jnp.broadcast_to(jnp.where((a_idx + 1) * (b_idx + 1) <= topk, 0.0, EXCLUDED).astype(F32)[:, None],
                               (rows.shape[0], tb))
    out = lambda dt: jax.ShapeDtypeStruct((nh, nkeys, t), dt)
    ospec = pl.BlockSpec((nh, nkeys, tb), lambda i: (0, 0, i))
    return pl.pallas_call(
        functools.partial(_peer_topk_kernel, nh=nh, nkeys=nkeys, topk=topk),
        grid=(t // tb,),
        in_specs=[
            pl.BlockSpec((2 * nh, tb, LANES), lambda i: (0, i, 0)),
            pl.BlockSpec(k1.shape, lambda i: (0, 0, 0)),
            pl.BlockSpec(k2.shape, lambda i: (0, 0, 0)),
            pl.BlockSpec(flat.shape, lambda i: (0, 0)),
            pl.BlockSpec(invalid.shape, lambda i: (0, 0)),
        ],
        out_specs=[ospec] * 4,
        out_shape=[out(F32)] * 4,
        compiler_params=_params("parallel"),
        name="peer_topk",
    )(q, k1, k2, flat, invalid)


def _peer_mix_kernel(h_ref, u_ref, vt_ref, r2_ref, c1_ref, a1_ref, a2_ref, *rest, nh, nkeys, ni, ne, norm):
    if len(rest) == 5:
        o_ref, ht_scr, w_scr, acc_scr, xt_scr = rest
        x_ref = gt_ref = g_ref = None
    else:
        x_ref, gt_ref, g_ref, o_ref, ht_scr, w_scr, acc_scr, xt_scr = rest
    e = pl.program_id(1)
    tb = h_ref.shape[0]

    @pl.when(e == 0)
    def _init():
        acc_scr[...] = jnp.zeros(acc_scr.shape, F32)
        xt_scr[...] = h_ref[...].T

    ht_scr[...] = jnp.dot(u_ref[...], xt_scr[...], preferred_element_type=F32)
    first_keys = pl.ds(pl.multiple_of(e * ni, SUBLANES), ni)
    for il in range(ni):
        rows = slice(il * nkeys, (il + 1) * nkeys)
        for lg in range(tb // LANES):
            sl = slice(lg * LANES, (lg + 1) * LANES)
            gate = jnp.zeros((nkeys, LANES), F32)
            for hh in range(nh):
                partners = c1_ref[hh, first_keys, sl][il:il + 1]
                first = a1_ref[hh, first_keys, sl][il:il + 1]
                gate = gate + jnp.where(r2_ref[hh, :, sl] < partners, a2_ref[hh, :, sl], 0.0) * first
            w_scr[rows, sl] = (gate * _gelu(ht_scr[rows, sl])).astype(BF16)
    acc_scr[...] += jnp.dot(vt_ref[0], w_scr[...], preferred_element_type=F32)

    @pl.when(e == ne - 1)
    def _done():
        mix = acc_scr[...].T
        if x_ref is None:
            o_ref[...] = mix
        else:
            y = x_ref[0] + gt_ref[0] * mix
            o_ref[0] = _rms(y, g_ref[...]) if norm else y


def _peer_mix(h2, u_tab, vt_tab, r2, c1, a1, a2, x1, gt, g, norm):
    b, l, _ = x1.shape
    t, d = h2.shape
    nh, nkeys, _ = r2.shape
    n_exp = u_tab.shape[0]
    tb = _tile(t, 512)
    ni = SUBLANES
    eb = ni * nkeys
    ne = n_exp // eb
    aux = pl.BlockSpec((nh, nkeys, tb), lambda i, e: (0, 0, i), pipeline_mode=pl.Buffered(1))
    in_specs = [
        pl.BlockSpec((tb, d), lambda i, e: (i, 0), pipeline_mode=pl.Buffered(1)),
        pl.BlockSpec((eb, d), lambda i, e: (e, 0)),
        pl.BlockSpec((1, d, eb), lambda i, e: (e, 0, 0)),
        aux, aux, aux, aux,
    ]
    args = [h2, u_tab, vt_tab, r2, c1, a1, a2]
    fused = l % tb == 0
    if fused:
        per_seq = l // tb
        in_specs += [
            pl.BlockSpec((1, tb, d), lambda i, e: (i // per_seq, i % per_seq, 0), pipeline_mode=pl.Buffered(1)),
            pl.BlockSpec((1, 1, d), lambda i, e: (i // per_seq, 0, 0)),
            pl.BlockSpec((1, d), lambda i, e: (0, 0)),
        ]
        args += [x1, gt, g.reshape(1, d)]
        out_spec = pl.BlockSpec((1, tb, d), lambda i, e: (i // per_seq, i % per_seq, 0))
        out_shape = jax.ShapeDtypeStruct((b, l, d), F32)
    else:
        out_spec = pl.BlockSpec((tb, d), lambda i, e: (i, 0))
        out_shape = jax.ShapeDtypeStruct((t, d), F32)
    out = pl.pallas_call(
        functools.partial(_peer_mix_kernel, nh=nh, nkeys=nkeys, ni=ni, ne=ne, norm=norm),
        grid=(t // tb, ne),
        in_specs=in_specs,
        out_specs=out_spec,
        out_shape=out_shape,
        scratch_shapes=[
            pltpu.VMEM((eb, tb), F32),
            pltpu.VMEM((eb, tb), BF16),
            pltpu.VMEM((d, tb), F32),
            pltpu.VMEM((d, tb), BF16),
        ],
        compiler_params=_params("parallel", "arbitrary"),
        name="peer_mix",
    )(*args)
    return out if fused else _final(x1, out, gt, g, norm)


def _final_kernel(x_ref, p_ref, gt_ref, g_ref, y_ref, *, norm):
    y = x_ref[0] + gt_ref[0] * p_ref[0]
    y_ref[0] = _rms(y, g_ref[...]) if norm else y


def _final(x1, peer, gt, g, norm):
    b, l, d = x1.shape
    tm = _tile(l, 512)
    tok = lambda i, j: (i, j, 0)
    return pl.pallas_call(
        functools.partial(_final_kernel, norm=norm),
        grid=(b, l // tm),
        in_specs=[
            pl.BlockSpec((1, tm, d), tok),
            pl.BlockSpec((1, tm, d), tok),
            pl.BlockSpec((1, 1, d), lambda i, j: (i, 0, 0)),
            pl.BlockSpec((1, d), lambda i, j: (0, 0)),
        ],
        out_specs=pl.BlockSpec((1, tm, d), tok),
        out_shape=jax.ShapeDtypeStruct((b, l, d), F32),
        compiler_params=_params("parallel", "parallel"),
        name="final",
    )(x1, peer.reshape(b, l, d), gt, g.reshape(1, d))


def _rope_tables(pos, rope):
    half = rope // 2
    inv = jnp.power(ROPE_THETA, -jnp.arange(half, dtype=F32) / half)
    ang = pos.astype(F32)[:, None] * inv
    reps = LANES // half
    return jnp.tile(jnp.cos(ang), (1, reps)), jnp.tile(jnp.sin(ang), (1, reps))


def _prep_weights(p, dims):
    d, sw, q_lora, kv_lora, rope, heads, nope = (dims[k] for k in ("d", "sw", "q_lora", "kv_lora", "rope", "heads", "nope"))
    half = rope // 2
    w_in = p["w_in"]
    off_q = sw
    off_kv = off_q + q_lora
    off_kr = off_kv + kv_lora
    off_g = off_kr + rope
    kr_w = w_in[:, off_kr:off_g]
    kr_rot = jnp.concatenate([-kr_w[:, half:], kr_w[:, :half]], axis=1)
    pad = jnp.zeros((d, LANES - rope), F32)
    w_mla = jnp.concatenate([w_in[:, off_q:off_kr], kr_w, pad, kr_rot, pad], axis=1).astype(BF16)
    wq = p["w_qu"].reshape(q_lora, heads, nope + rope)
    wq_n = wq[:, :, :nope].reshape(q_lora, heads * nope)
    wq_r = wq[:, :, nope:]
    wq_rot = jnp.concatenate([-wq_r[:, :, half:], wq_r[:, :, :half]], axis=2)
    hpad = jnp.zeros((q_lora, heads, LANES - rope), F32)
    wq_a = jnp.concatenate([wq_r, hpad], axis=2).reshape(q_lora, heads * LANES)
    wq_b = jnp.concatenate([wq_rot, hpad], axis=2).reshape(q_lora, heads * LANES)
    return {
        "w_gates": w_in[:, off_g:].astype(BF16),
        "w_u": w_in[:, :sw].astype(BF16),
        "w_mla": w_mla,
        "w_q": jnp.concatenate([wq_n, wq_a, wq_b], axis=1).astype(BF16),
        "w_kv": jnp.concatenate([p["w_uk"].reshape(kv_lora, heads * nope),
                                 p["w_uv"].reshape(kv_lora, heads * dims["v_dim"])], axis=1).astype(BF16),
        "w_o": p["w_o"].astype(BF16),
        "w_glu": p["w_glu"].astype(BF16),
        "w_out": p["w_out"].astype(BF16),
        "peer_wq": p["peer_wq"].astype(BF16),
        "peer_k1": p["peer_k1"].astype(BF16),
        "peer_k2": p["peer_k2"].astype(BF16),
        "peer_u": p["peer_u"].astype(BF16),
        "peer_vt": p["peer_v"].reshape(-1, SUBLANES * p["peer_k1"].shape[1], d).transpose(0, 2, 1).astype(BF16),
        "s5": _s5_tables(p["ssm_a_re"], p["ssm_a_im"], p["ssm_log_dt"], p["ssm_b_re"], p["ssm_b_im"],
                         p["ssm_c_re"], p["ssm_c_im"], p["ssm_d"]),
    }


def _layer(x, mod, past_ckv, past_kr, s0, p, w, dims, g_final, last):
    b, l, d = x.shape
    t = b * l
    heads, rope, kv_lora, topk = dims["heads"], dims["rope"], dims["kv_lora"], dims["topk"]
    sh1, sc1, gt1, sh2, sc2, gt2 = mod
    past = 0 if past_ckv is None else past_ckv.shape[1]

    h = _normmod(x, p["g_norm1"], sc1, sh1).reshape(t, d)
    gates = _mm(h, w["w_gates"], BF16, act="sigmoid")
    u = _mm(h, w["w_u"], BF16)
    cos, sin = _rope_tables(past + jnp.arange(l, dtype=jnp.int32), rope)
    qn, qr, ckv, ckv_b, kr, krp = _mla_proj(
        h, w["w_mla"], p["g_q"], p["g_kv"], w["w_q"], cos, sin, l,
        heads=heads, q_lora=dims["q_lora"], kv_lora=kv_lora, rope=rope, scale=dims["scale"])

    ys, s_fin = _s5(u.reshape(b, l, -1), *w["s5"], s0)

    if past_ckv is None:
        lk, keys_c, keys_r = l, ckv_b, krp
        tq = tk = _tile(l, 512)
    else:
        n_keys = past + l
        lk = -(-n_keys // LANES) * LANES
        keys_c = jnp.concatenate([past_ckv.astype(BF16), ckv_b.reshape(b, l, kv_lora)], axis=1)
        keys_c = jnp.pad(keys_c, ((0, 0), (0, lk - n_keys), (0, 0))).reshape(b * lk, kv_lora)
        past_r = jnp.pad(past_kr.astype(BF16), ((0, 0), (0, 0), (0, LANES - rope)))
        keys_r = jnp.concatenate([past_r, krp.reshape(b, l, LANES)], axis=1)
        keys_r = jnp.pad(keys_r, ((0, 0), (0, lk - n_keys), (0, 0))).reshape(b * lk, LANES)
        tq, tk = l, lk
    kh, vh = _kvup(keys_c, w["w_kv"], heads)
    o = _flash(qn, qr, kh, keys_r, vh, batch=b, lq=l, lk=lk, q_pos0=past, n_keys=past + l,
               tq=tq, tk=tk, hb=heads)

    merged = _merge(ys.reshape(t, -1), o, w["w_glu"], w["w_o"], gates)
    x1, h2 = _resid(merged, w["w_out"], x, gt1, p["g_norm2"], sc2, sh2)

    h2 = h2.reshape(t, d)
    q = _mm_split(h2, w["peer_wq"], BF16)
    r2, c1, a1, a2 = _peer_topk(q, w["peer_k1"], w["peer_k2"], topk)
    x2 = _peer_mix(h2, w["peer_u"], w["peer_vt"], r2, c1, a1, a2, x1, gt2, g_final, last)
    return x2, ckv.reshape(b, l, kv_lora), kr.reshape(b, l, rope), s_fin


def kernel(x_prompt, x_sample, c_prompt, c_sample, cache_ckv, cache_krope, state_ssm_re, state_ssm_im, w_ada, b_ada, g_norm1, g_norm2, w_in, g_q, w_qu, g_kv, w_uk, w_uv, w_o, ssm_a_re, ssm_a_im, ssm_log_dt, ssm_b_re, ssm_b_im, ssm_c_re, ssm_c_im, ssm_d, w_glu, w_out, peer_wq, peer_k1, peer_k2, peer_u, peer_v, g_final):
    depth = w_in.shape[0]
    bp, lp, d = x_prompt.shape
    bs, ls, _ = x_sample.shape
    groups, states = ssm_a_re.shape[1:]
    heads, nope = w_uk.shape[2:]
    rope = cache_krope.shape[-1]
    dims = {
        "d": d, "sw": groups * ssm_b_re.shape[3], "q_lora": g_q.shape[1], "kv_lora": g_kv.shape[1],
        "rope": rope, "heads": heads, "nope": nope, "v_dim": w_uv.shape[3],
        "scale": math.log2(math.e) / math.sqrt(nope + rope), "topk": 16,
    }
    assert nope == LANES and dims["v_dim"] == LANES and rope <= LANES

    xp, xs = x_prompt, x_sample
    nb = bp + bs
    rows = -(-nb // 16) * 16
    c_all = jnp.pad(jnp.concatenate([c_prompt, c_sample], axis=0), ((0, rows - nb), (0, 0)))
    zeros = jnp.zeros((bp, 2, groups * states), F32)
    outs_p, outs_s = [], []
    for layer in range(depth):
        p = {
            "g_norm1": g_norm1[layer], "g_norm2": g_norm2[layer], "w_in": w_in[layer], "g_q": g_q[layer],
            "w_qu": w_qu[layer], "g_kv": g_kv[layer], "w_uk": w_uk[layer], "w_uv": w_uv[layer], "w_o": w_o[layer],
            "ssm_a_re": ssm_a_re[layer], "ssm_a_im": ssm_a_im[layer], "ssm_log_dt": ssm_log_dt[layer],
            "ssm_b_re": ssm_b_re[layer], "ssm_b_im": ssm_b_im[layer], "ssm_c_re": ssm_c_re[layer],
            "ssm_c_im": ssm_c_im[layer], "ssm_d": ssm_d[layer], "w_glu": w_glu[layer], "w_out": w_out[layer],
            "peer_wq": peer_wq[layer], "peer_k1": peer_k1[layer], "peer_k2": peer_k2[layer],
            "peer_u": peer_u[layer], "peer_v": peer_v[layer],
        }
        w = _prep_weights(p, dims)
        mod = _ada(c_all, w_ada[layer], b_ada[layer])
        mod_p = [m.reshape(bp, 1, d) for m in jnp.split(mod[:bp], 6, axis=-1)]
        mod_s = [m.reshape(bs, 1, d) for m in jnp.split(mod[bp:nb], 6, axis=-1)]
        s0_s = jnp.stack([state_ssm_re[layer].reshape(bs, -1), state_ssm_im[layer].reshape(bs, -1)], axis=1)
        last = layer == depth - 1
        xp, *res_p = _layer(xp, mod_p, None, None, zeros, p, w, dims, g_final, last)
        xs, *res_s = _layer(xs, mod_s, cache_ckv[layer], cache_krope[layer], s0_s, p, w, dims, g_final, last)
        for res, outs, g in ((res_p, outs_p, bp), (res_s, outs_s, bs)):
            ckv, kr, s_fin = res
            outs.append((ckv, kr, s_fin[:, 0].reshape(g, groups, states), s_fin[:, 1].reshape(g, groups, states)))
    stack = lambda outs, k: jnp.stack([o[k] for o in outs])
    return (xp, xs,
            stack(outs_p, 0), stack(outs_p, 1), stack(outs_p, 2), stack(outs_p, 3),
            stack(outs_s, 0), stack(outs_s, 1), stack(outs_s, 2), stack(outs_s, 3))
```

```python
import functools
import math

import jax
import jax.numpy as jnp
from jax import lax
from jax.experimental import pallas as pl
from jax.experimental.pallas import tpu as pltpu

F32 = jnp.float32
BF16 = jnp.bfloat16

EPS = 1e-6
CHUNK = 64
ROPE_THETA = 10000.0
LANES = 128
SUBLANES = 8
MXU_DIM = 256
VMEM_LIMIT_BYTES = 56 * 1024 * 1024
MASKED = -1e30
REMOVED = -3e38
EXCLUDED = -1e38


def _params(*semantics):
    return pltpu.CompilerParams(dimension_semantics=semantics, vmem_limit_bytes=VMEM_LIMIT_BYTES)


def _tile(n, pref):
    if n <= pref:
        return n
    t = pref
    while n % t:
        t //= 2
    assert t >= SUBLANES, (n, pref)
    return t


def _sigmoid(x):
    return 1.0 / (1.0 + jnp.exp(-x))


def _gelu(x):
    return 0.5 * x * (1.0 + jnp.tanh(math.sqrt(2.0 / math.pi) * (x + 0.044715 * (x * x * x))))


def _rms(x, g):
    return x * lax.rsqrt(jnp.mean(x * x, axis=-1, keepdims=True) + EPS) * g


def _ada_kernel(c_ref, w_ref, b_ref, o_ref):
    c = c_ref[...]
    a = (c * _sigmoid(c)).astype(BF16)
    o_ref[...] = jnp.dot(a, w_ref[...].astype(BF16), preferred_element_type=F32) + b_ref[...]


def _ada(c, w, b):
    rows, d = c.shape
    n = w.shape[1]
    tn = _tile(n, 1024)
    return pl.pallas_call(
        _ada_kernel,
        grid=(n // tn,),
        in_specs=[
            pl.BlockSpec((rows, d), lambda j: (0, 0)),
            pl.BlockSpec((d, tn), lambda j: (0, j)),
            pl.BlockSpec((1, tn), lambda j: (0, j)),
        ],
        out_specs=pl.BlockSpec((rows, tn), lambda j: (0, j)),
        out_shape=jax.ShapeDtypeStruct((rows, n), F32),
        compiler_params=_params("parallel"),
        name="ada",
    )(c, w, b.reshape(1, n))


def _normmod_kernel(x_ref, g_ref, sc_ref, sh_ref, o_ref):
    h = _rms(x_ref[0], g_ref[...]) * (1.0 + sc_ref[0]) + sh_ref[0]
    o_ref[0] = h.astype(BF16)


def _normmod(x, g, sc, sh):
    b, l, d = x.shape
    tm = _tile(l, 512)
    return pl.pallas_call(
        _normmod_kernel,
        grid=(b, l // tm),
        in_specs=[
            pl.BlockSpec((1, tm, d), lambda i, j: (i, j, 0)),
            pl.BlockSpec((1, d), lambda i, j: (0, 0)),
            pl.BlockSpec((1, 1, d), lambda i, j: (i, 0, 0)),
            pl.BlockSpec((1, 1, d), lambda i, j: (i, 0, 0)),
        ],
        out_specs=pl.BlockSpec((1, tm, d), lambda i, j: (i, j, 0)),
        out_shape=jax.ShapeDtypeStruct((b, l, d), BF16),
        compiler_params=_params("parallel", "parallel"),
        name="normmod",
    )(x, g.reshape(1, d), sc, sh)


def _mm_kernel(a_ref, w_ref, o_ref, *, act):
    z = jnp.dot(a_ref[...], w_ref[...], preferred_element_type=F32)
    if act == "sigmoid":
        z = _sigmoid(z)
    o_ref[...] = z.astype(o_ref.dtype)


def _mm(a, w, out_dtype, act=None, tn_pref=1024):
    t, k = a.shape
    n = w.shape[1]
    tm = _tile(t, 1024)
    tn = _tile(n, tn_pref)
    return pl.pallas_call(
        functools.partial(_mm_kernel, act=act),
        grid=(t // tm, n // tn),
        in_specs=[
            pl.BlockSpec((tm, k), lambda i, j: (i, 0)),
            pl.BlockSpec((k, tn), lambda i, j: (0, j)),
        ],
        out_specs=pl.BlockSpec((tm, tn), lambda i, j: (i, j)),
        out_shape=jax.ShapeDtypeStruct((t, n), out_dtype),
        compiler_params=_params("parallel", "parallel"),
        name="mm",
    )(a, w)


def _mm_split_kernel(a_ref, w_ref, o_ref, *, parts):
    z = jnp.dot(a_ref[...], w_ref[...], preferred_element_type=F32)
    for p in range(parts):
        o_ref[p] = z[:, p * LANES:(p + 1) * LANES].astype(o_ref.dtype)


def _mm_split(a, w, out_dtype):
    t, k = a.shape
    n = w.shape[1]
    tm = _tile(t, 512)
    tn = _tile(n, 1024)
    parts = tn // LANES
    return pl.pallas_call(
        functools.partial(_mm_split_kernel, parts=parts),
        grid=(t // tm, n // tn),
        in_specs=[
            pl.BlockSpec((tm, k), lambda i, j: (i, 0)),
            pl.BlockSpec((k, tn), lambda i, j: (0, j)),
        ],
        out_specs=pl.BlockSpec((parts, tm, LANES), lambda i, j: (j, i, 0)),
        out_shape=jax.ShapeDtypeStruct((n // LANES, t, LANES), out_dtype),
        compiler_params=_params("parallel", "parallel"),
        name="mm_split",
    )(a, w)


def _mla_proj_kernel(h_ref, wm_ref, gq_ref, gkv_ref, wq_ref, cos_ref, sin_ref,
                     qn_ref, qr_ref, ckv_ref, ckvb_ref, kr_ref, krp_ref,
                     *, heads, q_lora, kv_lora, rope, scale):
    z = jnp.dot(h_ref[...], wm_ref[...], preferred_element_type=F32)
    cos = cos_ref[...]
    sin = sin_ref[...]
    off = q_lora + kv_lora
    krp = z[:, off:off + LANES] * cos + z[:, off + LANES:off + 2 * LANES] * sin
    kr_ref[...] = krp[:, :rope]
    krp_ref[...] = krp.astype(BF16)
    ckv = _rms(z[:, q_lora:off], gkv_ref[...])
    ckv_ref[...] = ckv
    ckvb_ref[...] = ckv.astype(BF16)
    qd = _rms(z[:, :q_lora], gq_ref[...]).astype(BF16)
    zq = jnp.dot(qd, wq_ref[...], preferred_element_type=F32)
    hn = heads * LANES
    for hh in range(heads):
        lo = hh * LANES
        qn_ref[hh] = (zq[:, lo:lo + LANES] * scale).astype(BF16)
        qr = zq[:, hn + lo:hn + lo + LANES] * cos + zq[:, 2 * hn + lo:2 * hn + lo + LANES] * sin
        qr_ref[hh] = (qr * scale).astype(BF16)


def _mla_proj(h, wm, gq, gkv, wq, cos, sin, seq, *, heads, q_lora, kv_lora, rope, scale):
    t, d = h.shape
    tm = _tile(t, 256)
    if seq % tm == 0:
        nrep = seq // tm
        tab_map = lambda i: (i % nrep, 0)
    else:
        assert tm % seq == 0
        cos = jnp.tile(cos, (tm // seq, 1))
        sin = jnp.tile(sin, (tm // seq, 1))
        tab_map = lambda i: (0, 0)
    nm = wm.shape[1]
    nq = wq.shape[1]
    row = lambda i: (i, 0)
    const = lambda i: (0, 0)
    return pl.pallas_call(
        functools.partial(_mla_proj_kernel, heads=heads, q_lora=q_lora, kv_lora=kv_lora, rope=rope, scale=scale),
        grid=(t // tm,),
        in_specs=[
            pl.BlockSpec((tm, d), row),
            pl.BlockSpec((d, nm), const),
            pl.BlockSpec((1, q_lora), const),
            pl.BlockSpec((1, kv_lora), const),
            pl.BlockSpec((q_lora, nq), const),
            pl.BlockSpec((tm, LANES), tab_map),
            pl.BlockSpec((tm, LANES), tab_map),
        ],
        out_specs=[
            pl.BlockSpec((heads, tm, LANES), lambda i: (0, i, 0)),
            pl.BlockSpec((heads, tm, LANES), lambda i: (0, i, 0)),
            pl.BlockSpec((tm, kv_lora), row),
            pl.BlockSpec((tm, kv_lora), row),
            pl.BlockSpec((tm, rope), row),
            pl.BlockSpec((tm, LANES), row),
        ],
        out_shape=[
            jax.ShapeDtypeStruct((heads, t, LANES), BF16),
            jax.ShapeDtypeStruct((heads, t, LANES), BF16),
            jax.ShapeDtypeStruct((t, kv_lora), F32),
            jax.ShapeDtypeStruct((t, kv_lora), BF16),
            jax.ShapeDtypeStruct((t, rope), F32),
            jax.ShapeDtypeStruct((t, LANES), BF16),
        ],
        compiler_params=_params("parallel"),
        name="mla_proj",
    )(h, wm, gq.reshape(1, q_lora), gkv.reshape(1, kv_lora), wq, cos, sin)


def _kvup_kernel(c_ref, w_ref, k_ref, v_ref, *, heads):
    z = jnp.dot(c_ref[...], w_ref[...], preferred_element_type=F32)
    for hh in range(heads):
        k_ref[hh] = z[:, hh * LANES:(hh + 1) * LANES].astype(BF16)
        v_ref[hh] = z[:, (heads + hh) * LANES:(heads + hh + 1) * LANES].astype(BF16)


def _kvup(ckv, w, heads):
    t, c = ckv.shape
    tm = _tile(t, 512)
    if t % tm:
        tm = t
    return pl.pallas_call(
        functools.partial(_kvup_kernel, heads=heads),
        grid=(t // tm,),
        in_specs=[
            pl.BlockSpec((tm, c), lambda i: (i, 0)),
            pl.BlockSpec(w.shape, lambda i: (0, 0)),
        ],
        out_specs=[
            pl.BlockSpec((heads, tm, LANES), lambda i: (0, i, 0)),
            pl.BlockSpec((heads, tm, LANES), lambda i: (0, i, 0)),
        ],
        out_shape=[jax.ShapeDtypeStruct((heads, t, LANES), BF16)] * 2,
        compiler_params=_params("parallel"),
        name="kvup",
    )(ckv, w)


def _flash_kernel(iq_tab, ik_tab, flag_tab, qn_ref, qr_ref, k_ref, kr_ref, v_ref, o_ref,
                  qc_scr, m_scr, acc_scr, *, hb, tq, tk, q_pos0, n_keys):
    pair = pl.program_id(2)
    iq = iq_tab[pair]
    ik = ik_tab[pair]
    flags = flag_tab[pair]
    first = (flags & 1) != 0
    last = (flags & 2) != 0
    full = (flags & 4) != 0

    @pl.when(first)
    def _init():
        m_scr[...] = jnp.full(m_scr.shape, MASKED, F32)
        acc_scr[...] = jnp.zeros(acc_scr.shape, F32)
        for hh in range(hb):
            qc_scr[hh] = jnp.concatenate([qn_ref[hh], qr_ref[hh]], axis=1)

    def step(masked):
        kr = kr_ref[...]
        ones = jnp.ones((tk, LANES), BF16)
        if masked:
            qp = q_pos0 + iq * tq + lax.broadcasted_iota(jnp.int32, (tq, tk), 0)
            kp = ik * tk + lax.broadcasted_iota(jnp.int32, (tq, tk), 1)
            allowed = jnp.logical_and(kp // CHUNK <= qp // CHUNK, kp < n_keys)

        def head(hh, carry):
            kc = jnp.concatenate([k_ref[hh], kr], axis=1)
            s = lax.dot_general(qc_scr[hh], kc, (((1,), (1,)), ((), ())), preferred_element_type=F32)
            if masked:
                s = jnp.where(allowed, s, MASKED)
            m_prev = m_scr[hh]
            m_new = jnp.maximum(m_prev, jnp.max(s, axis=1, keepdims=True))
            alpha = jnp.exp2(m_prev - m_new)
            p = jnp.exp2(s - pltpu.repeat(m_new, tk // LANES, axis=1)).astype(BF16)
            vc = jnp.concatenate([v_ref[hh], ones], axis=1)
            acc_scr[hh] = pltpu.repeat(alpha, 2, axis=1) * acc_scr[hh] + jnp.dot(p, vc, preferred_element_type=F32)
            m_scr[hh] = m_new
            return carry

        lax.fori_loop(0, hb, head, 0, unroll=True)

    @pl.when(full)
    def _full():
        step(False)

    @pl.when(jnp.logical_not(full))
    def _diag():
        step(True)

    @pl.when(last)
    def _done():
        for hh in range(hb):
            acc = acc_scr[hh]
            o_ref[hh] = (acc[:, :LANES] / acc[:, LANES:]).astype(BF16)


def _flash_pairs(lq, lk, tq, tk, q_pos0, n_keys):
    iqs, iks, flags = [], [], []
    for iq in range(lq // tq):
        q_first = q_pos0 + iq * tq
        last_key = min(n_keys - 1, ((q_first + tq - 1) // CHUNK) * CHUNK + CHUNK - 1)
        full_key = min(n_keys - 1, (q_first // CHUNK) * CHUNK + CHUNK - 1)
        n_blocks = last_key // tk + 1
        for ik in range(n_blocks):
            full = (ik + 1) * tk - 1 <= full_key
            iqs.append(iq)
            iks.append(ik)
            flags.append((1 if ik == 0 else 0) | (2 if ik == n_blocks - 1 else 0) | (4 if full else 0))
    as_i32 = lambda xs: jnp.asarray(xs, jnp.int32)
    return as_i32(iqs), as_i32(iks), as_i32(flags)


def _flash(qn, qr, k, krp, v, *, batch, lq, lk, q_pos0, n_keys, tq, tk, hb):
    heads = qn.shape[0]
    nq = lq // tq
    nk = lk // tk
    assert lq % tq == 0 and lk % tk == 0 and heads % hb == 0 and tk % LANES == 0
    iq_tab, ik_tab, flag_tab = _flash_pairs(lq, lk, tq, tk, q_pos0, n_keys)
    qmap = lambda b, h, p, iqt, ikt, ft: (h, b * nq + iqt[p], 0)
    kmap = lambda b, h, p, iqt, ikt, ft: (h, b * nk + ikt[p], 0)
    return pl.pallas_call(
        functools.partial(_flash_kernel, hb=hb, tq=tq, tk=tk, q_pos0=q_pos0, n_keys=n_keys),
        grid_spec=pltpu.PrefetchScalarGridSpec(
            num_scalar_prefetch=3,
            grid=(batch, heads // hb, iq_tab.shape[0]),
            in_specs=[
                pl.BlockSpec((hb, tq, LANES), qmap),
                pl.BlockSpec((hb, tq, LANES), qmap),
                pl.BlockSpec((hb, tk, LANES), kmap),
                pl.BlockSpec((tk, LANES), lambda b, h, p, iqt, ikt, ft: (b * nk + ikt[p], 0)),
                pl.BlockSpec((hb, tk, LANES), kmap),
            ],
            out_specs=pl.BlockSpec((hb, tq, LANES), qmap),
            scratch_shapes=[
                pltpu.VMEM((hb, tq, 2 * LANES), BF16),
                pltpu.VMEM((hb, tq, LANES), F32),
                pltpu.VMEM((hb, tq, 2 * LANES), F32),
            ],
        ),
        out_shape=jax.ShapeDtypeStruct((heads, batch * lq, LANES), BF16),
        compiler_params=_params("parallel", "parallel", "arbitrary"),
        name="flash",
    )(iq_tab, ik_tab, flag_tab, qn, qr, k, krp, v)


def _s5_kernel(u_ref, pm_ref, pt_ref, wb_ref, scn_ref, wc_ref, d_ref, s0_ref, y_ref, st_ref,
               xs_ref, up_ref, yp_ref, car_ref, *, ts, nkt, kw, sw, nt):
    it = pl.program_id(1)
    steps = ts // SUBLANES

    @pl.when(it == 0)
    def _load_state():
        car_ref[...] = s0_ref[0]

    up_ref[...] = jnp.dot(pm_ref[...], u_ref[0], preferred_element_type=F32).astype(BF16)
    first_row = lax.broadcasted_iota(jnp.int32, (SUBLANES, sw), 0) == 0
    for kt in range(nkt):
        ukt = up_ref[:, kt * kw:(kt + 1) * kw]
        xs_ref[...] = jnp.dot(ukt, wb_ref[kt], preferred_element_type=F32)
        st_sl = slice(kt * sw, (kt + 1) * sw)
        lr = scn_ref[kt, 0]
        li = scn_ref[kt, 1]

        def advance(k, carry, store, lr=lr, li=li):
            hr, hi = carry
            rows = pl.ds(pl.multiple_of(k * SUBLANES, SUBLANES), SUBLANES)
            nr = lr * hr - li * hi + xs_ref[rows, :sw]
            ni = lr * hi + li * hr + xs_ref[rows, sw:]
            if store:
                xs_ref[rows, :sw] = nr
                xs_ref[rows, sw:] = ni
            return nr, ni

        zero = jnp.zeros((SUBLANES, sw), F32)
        er, ei = lax.fori_loop(0, steps, functools.partial(advance, store=False), (zero, zero))
        br = jnp.where(first_row, jnp.broadcast_to(car_ref[0:1, st_sl], (SUBLANES, sw)), pltpu.roll(er, 1, 0))
        bi = jnp.where(first_row, jnp.broadcast_to(car_ref[1:2, st_sl], (SUBLANES, sw)), pltpu.roll(ei, 1, 0))
        for c0, dist in ((2, 1), (4, 2), (6, 4)):
            ar = scn_ref[kt, c0]
            ai = scn_ref[kt, c0 + 1]
            sr = pltpu.roll(br, dist, 0)
            si = pltpu.roll(bi, dist, 0)
            br, bi = br + ar * sr - ai * si, bi + ar * si + ai * sr
        hr, hi = lax.fori_loop(0, steps, functools.partial(advance, store=True), (br, bi))
        car_ref[0:1, st_sl] = hr[SUBLANES - 1:SUBLANES]
        car_ref[1:2, st_sl] = hi[SUBLANES - 1:SUBLANES]
        y = jnp.dot(xs_ref[...].astype(BF16), wc_ref[kt], preferred_element_type=F32)
        y = y + d_ref[:, kt * kw:(kt + 1) * kw] * ukt.astype(F32)
        yp_ref[:, kt * kw:(kt + 1) * kw] = y.astype(BF16)
    y_ref[0] = _gelu(jnp.dot(pt_ref[...], yp_ref[...], preferred_element_type=F32)).astype(BF16)

    @pl.when(it == nt - 1)
    def _store_state():
        st_ref[0] = car_ref[...]


def _s5(u, wb, lam, wc, dsk, s0):
    b, l, w = u.shape
    nkt, kw, sw2 = wb.shape
    sw = sw2 // 2
    ns = s0.shape[2]
    ts = _tile(l, 512)
    nt = l // ts
    steps = ts // SUBLANES

    def cmul(x, y):
        return x[0] * y[0] - x[1] * y[1], x[0] * y[1] + x[1] * y[0]

    mu = (lam[0], lam[1])
    for _ in range(steps - 1):
        mu = cmul(mu, (lam[0], lam[1]))
    mu2 = cmul(mu, mu)
    mu4 = cmul(mu2, mu2)
    rows = jnp.arange(SUBLANES)[None, :, None]
    shifted = lambda c, dist: [jnp.where(rows >= dist, x[:, None, :], 0.0) for x in c]
    rep = [jnp.broadcast_to(x[:, None, :], (nkt, SUBLANES, sw)) for x in (lam[0], lam[1])]
    scn = jnp.stack(rep + shifted(mu, 1) + shifted(mu2, 2) + shifted(mu4, 4), axis=1).astype(F32)
    r = jnp.arange(ts)
    pm = jax.nn.one_hot((r % SUBLANES) * steps + r // SUBLANES, ts, dtype=BF16)
    const2 = lambda i, j: (0, 0)
    const3 = lambda i, j: (0, 0, 0)
    return pl.pallas_call(
        functools.partial(_s5_kernel, ts=ts, nkt=nkt, kw=kw, sw=sw, nt=nt),
        grid=(b, nt),
        in_specs=[
            pl.BlockSpec((1, ts, w), lambda i, j: (i, j, 0)),
            pl.BlockSpec((ts, ts), const2),
            pl.BlockSpec((ts, ts), const2),
            pl.BlockSpec(wb.shape, const3),
            pl.BlockSpec(scn.shape, lambda i, j: (0, 0, 0, 0)),
            pl.BlockSpec(wc.shape, const3),
            pl.BlockSpec((1, w), const2),
            pl.BlockSpec((1, 2, ns), lambda i, j: (i, 0, 0)),
        ],
        out_specs=[
            pl.BlockSpec((1, ts, w), lambda i, j: (i, j, 0)),
            pl.BlockSpec((1, 2, ns), lambda i, j: (i, 0, 0)),
        ],
        out_shape=[
            jax.ShapeDtypeStruct((b, l, w), BF16),
            jax.ShapeDtypeStruct((b, 2, ns), F32),
        ],
        scratch_shapes=[
            pltpu.VMEM((ts, sw2), F32),
            pltpu.VMEM((ts, w), BF16),
            pltpu.VMEM((ts, w), BF16),
            pltpu.VMEM((2, ns), F32),
        ],
        compiler_params=_params("parallel", "arbitrary"),
        name="s5",
    )(u, pm, pm.T, wb, scn, wc, dsk, s0)


def _s5_tables(a_re, a_im, log_dt, b_re, b_im, c_re, c_im, d_skip):
    g, p = a_re.shape
    ch = b_re.shape[2]
    gpt = MXU_DIM // ch
    nkt = g // gpt
    dt = jnp.exp(log_dt.astype(F32))[:, None]
    mag = jnp.exp(dt * a_re)
    ab_re = mag * jnp.cos(dt * a_im)
    ab_im = mag * jnp.sin(dt * a_im)
    den = a_re * a_re + a_im * a_im
    nr = ab_re - 1.0
    f_re = (nr * a_re + ab_im * a_im) / den
    f_im = (ab_im * a_re - nr * a_im) / den
    bb_re = f_re[..., None] * b_re - f_im[..., None] * b_im
    bb_im = f_re[..., None] * b_im + f_im[..., None] * b_re
    eye = jnp.eye(gpt, dtype=F32)

    def in_blockdiag(bb):
        return jnp.einsum("kgpi,gh->kgihp", bb.reshape(nkt, gpt, p, ch), eye).reshape(nkt, gpt * ch, gpt * p)

    def out_blockdiag(cc):
        return jnp.einsum("kgjp,gh->kgphj", cc.reshape(nkt, gpt, ch, p), eye).reshape(nkt, gpt * p, gpt * ch)

    wb = jnp.concatenate([in_blockdiag(bb_re), in_blockdiag(bb_im)], axis=2).astype(BF16)
    wc = jnp.concatenate([out_blockdiag(c_re), out_blockdiag(-c_im)], axis=1).astype(BF16)
    lam = jnp.stack([ab_re.reshape(nkt, gpt * p), ab_im.reshape(nkt, gpt * p)]).astype(F32)
    return wb, lam, wc, d_skip.reshape(1, g * ch).astype(F32)


def _merge_kernel(ys_ref, o_ref, wga_ref, wgb_ref, wo_ref, ga_ref, gb_ref, out_ref, *, heads):
    ys = ys_ref[...]
    ya = jnp.dot(ys, wga_ref[...], preferred_element_type=F32)
    ya = ya * _sigmoid(jnp.dot(ys, wgb_ref[...], preferred_element_type=F32))
    oc = jnp.concatenate([o_ref[hh] for hh in range(heads)], axis=1)
    yb = jnp.dot(oc, wo_ref[...], preferred_element_type=F32)
    out_ref[...] = (ga_ref[...].astype(F32) * ya + gb_ref[...].astype(F32) * yb).astype(BF16)


def _merge(ys, o, w_glu, w_o, gates):
    t, sw = ys.shape
    heads = o.shape[0]
    d = w_o.shape[1]
    tm = _tile(t, 512)
    tn = _tile(d, 1024)
    nj = d // tn
    return pl.pallas_call(
        functools.partial(_merge_kernel, heads=heads),
        grid=(t // tm, nj),
        in_specs=[
            pl.BlockSpec((tm, sw), lambda i, j: (i, 0)),
            pl.BlockSpec((heads, tm, LANES), lambda i, j: (0, i, 0)),
            pl.BlockSpec((sw, tn), lambda i, j: (0, j)),
            pl.BlockSpec((sw, tn), lambda i, j: (0, nj + j)),
            pl.BlockSpec((heads * LANES, tn), lambda i, j: (0, j)),
            pl.BlockSpec((tm, tn), lambda i, j: (i, j)),
            pl.BlockSpec((tm, tn), lambda i, j: (i, nj + j)),
        ],
        out_specs=pl.BlockSpec((tm, tn), lambda i, j: (i, j)),
        out_shape=jax.ShapeDtypeStruct((t, d), BF16),
        compiler_params=_params("parallel", "parallel"),
        name="merge",
    )(ys, o, w_glu, w_glu, w_o, gates, gates)


def _resid_kernel(m_ref, w_ref, x_ref, gt_ref, g_ref, sc_ref, sh_ref, x1_ref, h2_ref):
    x1 = x_ref[0] + gt_ref[0] * jnp.dot(m_ref[0], w_ref[...], preferred_element_type=F32)
    x1_ref[0] = x1
    h2_ref[0] = (_rms(x1, g_ref[...]) * (1.0 + sc_ref[0]) + sh_ref[0]).astype(BF16)


def _resid(merged, w_out, x, gt, g2, sc, sh):
    b, l, d = x.shape
    tm = _tile(l, 512)
    tok = lambda i, j: (i, j, 0)
    per_b = lambda i, j: (i, 0, 0)
    return pl.pallas_call(
        _resid_kernel,
        grid=(b, l // tm),
        in_specs=[
            pl.BlockSpec((1, tm, d), tok),
            pl.BlockSpec((d, d), lambda i, j: (0, 0)),
            pl.BlockSpec((1, tm, d), tok),
            pl.BlockSpec((1, 1, d), per_b),
            pl.BlockSpec((1, d), lambda i, j: (0, 0)),
            pl.BlockSpec((1, 1, d), per_b),
            pl.BlockSpec((1, 1, d), per_b),
        ],
        out_specs=[pl.BlockSpec((1, tm, d), tok), pl.BlockSpec((1, tm, d), tok)],
        out_shape=[jax.ShapeDtypeStruct((b, l, d), F32), jax.ShapeDtypeStruct((b, l, d), BF16)],
        compiler_params=_params("parallel", "parallel"),
        name="resid",
    )(merged.reshape(b, l, d), w_out, x, gt, g2.reshape(1, d), sc, sh)


def _peer_topk_kernel(q_ref, k1_ref, k2_ref, flat_ref, invalid_ref, r2_ref, c1_ref, a1_ref, a2_ref,
                      *, nh, nkeys, topk):
    tb = q_ref.shape[1]
    iota_k = lax.broadcasted_iota(jnp.int32, (nkeys, tb), 0).astype(F32)
    iota_t = lax.broadcasted_iota(jnp.int32, (topk, tb), 0).astype(F32)
    flat = flat_ref[...]
    invalid = invalid_ref[...]
    nt_dims = (((1,), (1,)), ((), ()))

    n_cand = flat.shape[0]

    def extract(s, exact_ties):
        work = s
        rank = jnp.full((nkeys, tb), float(topk), F32)
        vals = []
        for a in range(topk):
            m = jnp.max(work, axis=0, keepdims=True)
            if exact_ties:
                idx = jnp.min(jnp.where(work == m, iota_k, float(nkeys)), axis=0, keepdims=True)
                sel = iota_k == idx
            else:
                sel = work == m
            rank = jnp.where(sel, float(a), rank)
            work = jnp.where(sel, REMOVED, work)
            vals.append(m)
        ranked = jnp.sum(jnp.where(rank < float(topk), 1.0, 0.0), axis=0, keepdims=True)
        return vals, rank, ranked

    def stack(vals):
        out = jnp.zeros((topk, tb), F32)
        for a in range(topk):
            out = jnp.where(iota_t == float(a), vals[a], out)
        return out

    def candidates(v1, v2):
        vs1 = stack(v1)
        vs2 = stack(v2)
        blocks = [v1[0] + vs2]
        for a in range(1, SUBLANES):
            blocks.append(v1[a] + vs2[:SUBLANES])
        blocks.append(vs1[SUBLANES:] + v2[0])
        return jnp.concatenate(blocks, axis=0) + invalid

    def route(s1, s2, exact_ties):
        v1, rank1, n1 = extract(s1, exact_ties)
        v2, rank2, n2 = extract(s2, exact_ties)
        cand = candidates(v1, v2)
        counts = jnp.zeros((topk, tb), F32)
        top = None
        zsum = None
        for kk in range(topk):
            m = jnp.max(cand, axis=0, keepdims=True)
            if exact_ties:
                f = jnp.min(jnp.where(cand == m, flat, 1e9), axis=0, keepdims=True)
                cand = jnp.where(flat == f, REMOVED, cand)
                counts = counts + jnp.where(iota_t == jnp.floor(f * (1.0 / topk)), 1.0, 0.0)
            else:
                cand = jnp.where(cand == m, REMOVED, cand)
            if kk == 0:
                top = m
                zsum = jnp.ones_like(m)
            else:
                zsum = zsum + jnp.exp(m - top)
        n3 = None
        if not exact_ties:
            gone = jnp.where(cand == REMOVED, 1.0, 0.0)
            n3 = jnp.sum(gone, axis=0, keepdims=True)
            per_rank = [jnp.sum(gone[:topk], axis=0, keepdims=True)]
            for a in range(1, SUBLANES):
                lo = topk + (a - 1) * SUBLANES
                per_rank.append(jnp.sum(gone[lo:lo + SUBLANES], axis=0, keepdims=True))
            counts = jnp.concatenate([stack(per_rank + [per_rank[0]] * (topk - SUBLANES))[:SUBLANES],
                                      gone[n_cand - SUBLANES:]], axis=0)
        c1 = jnp.zeros((nkeys, tb), F32)
        for a in range(topk):
            c1 = jnp.where(rank1 == float(a), counts[a:a + 1], c1)
        a1 = jnp.exp(s1 - v1[0]) * (1.0 / zsum)
        a2 = jnp.exp(s2 - v2[0])
        clean = None
        if not exact_ties:
            want = float(topk)
            bad = jnp.where(n1 != want, 1.0, 0.0) + jnp.where(n2 != want, 1.0, 0.0) + jnp.where(n3 != want, 1.0, 0.0)
            clean = jnp.max(bad) == 0.0
        return (rank2, c1, a1, a2), clean

    def store(hh, tables):
        r2_ref[hh], c1_ref[hh], a1_ref[hh], a2_ref[hh] = tables

    per_trip = 4 if nh % 4 == 0 else 2

    def group(gg, carry):
        heads = [per_trip * gg + r for r in range(per_trip)]
        scores = []
        for hh in heads:
            scores.append((lax.dot_general(k1_ref[hh], q_ref[2 * hh], nt_dims, preferred_element_type=F32),
                           lax.dot_general(k2_ref[hh], q_ref[2 * hh + 1], nt_dims, preferred_element_type=F32)))
        quick = [route(s1, s2, False) for s1, s2 in scores]
        clean = quick[0][1]
        for _, ok in quick[1:]:
            clean = jnp.logical_and(clean, ok)

        @pl.when(clean)
        def _no_ties():
            for hh, (tables, _) in zip(heads, quick):
                store(hh, tables)

        @pl.when(jnp.logical_not(clean))
        def _ties():
            for hh, (s1, s2) in zip(heads, scores):
                store(hh, route(s1, s2, True)[0])

        return carry

    lax.fori_loop(0, nh // per_trip, group, 0)


def _peer_topk(q, k1, k2, topk):
    nh, nkeys, half = k1.shape
    t = q.shape[1]
    tb = LANES
    assert topk == 2 * SUBLANES and half == LANES and t % tb == 0
    rows = jnp.arange(topk + (SUBLANES - 1) * SUBLANES + SUBLANES)
    a_idx = jnp.where(rows < topk, 0, jnp.where(rows < topk + (SUBLANES - 1) * SUBLANES,
                                                1 + (rows - topk) // SUBLANES, SUBLANES + (rows - topk - (SUBLANES - 1) * SUBLANES)))
    b_idx = jnp.where(rows < topk, rows, jnp.where(rows < topk + (SUBLANES - 1) * SUBLANES, (rows - topk) % SUBLANES, 0))
    flat = jnp.broadcast_to((a_idx * topk + b_idx).astype(F32)[:, None], (rows.shape[0], tb))
    invalid = jnp.broadcast_to(jnp.where((a_idx + 1) * (b_idx + 1) <= topk, 0.0, EXCLUDED).astype(F32)[:, None],
                               (rows.shape[0], tb))
    out = lambda dt: jax.ShapeDtypeStruct((nh, nkeys, t), dt)
    ospec = pl.BlockSpec((nh, nkeys, tb), lambda i: (0, 0, i))
    return pl.pallas_call(
        functools.partial(_peer_topk_kernel, nh=nh, nkeys=nkeys, topk=topk),
        grid=(t // tb,),
        in_specs=[
            pl.BlockSpec((2 * nh, tb, LANES), lambda i: (0, i, 0)),
            pl.BlockSpec(k1.shape, lambda i: (0, 0, 0)),
            pl.BlockSpec(k2.shape, lambda i: (0, 0, 0)),
            pl.BlockSpec(flat.shape, lambda i: (0, 0)),
            pl.BlockSpec(invalid.shape, lambda i: (0, 0)),
        ],
        out_specs=[ospec] * 4,
        out_shape=[out(F32)] * 4,
        compiler_params=_params("parallel"),
        name="peer_topk",
    )(q, k1, k2, flat, invalid)


def _peer_mix_kernel(h_ref, u_ref, vt_ref, r2_ref, c1_ref, a1_ref, a2_ref, *rest, nh, nkeys, ni, ne, norm):
    if len(rest) == 5:
        o_ref, ht_scr, w_scr, acc_scr, xt_scr = rest
        x_ref = gt_ref = g_ref = None
    else:
        x_ref, gt_ref, g_ref, o_ref, ht_scr, w_scr, acc_scr, xt_scr = rest
    e = pl.program_id(1)
    tb = h_ref.shape[0]

    @pl.when(e == 0)
    def _init():
        acc_scr[...] = jnp.zeros(acc_scr.shape, F32)
        xt_scr[...] = h_ref[...].T

    ht_scr[...] = jnp.dot(u_ref[...], xt_scr[...], preferred_element_type=F32)
    first_keys = pl.ds(pl.multiple_of(e * ni, SUBLANES), ni)
    for il in range(ni):
        rows = slice(il * nkeys, (il + 1) * nkeys)
        for lg in range(tb // LANES):
            sl = slice(lg * LANES, (lg + 1) * LANES)
            gate = jnp.zeros((nkeys, LANES), F32)
            for hh in range(nh):
                partners = c1_ref[hh, first_keys, sl][il:il + 1]
                first = a1_ref[hh, first_keys, sl][il:il + 1]
                gate = gate + jnp.where(r2_ref[hh, :, sl] < partners, a2_ref[hh, :, sl], 0.0) * first
            w_scr[rows, sl] = (gate * _gelu(ht_scr[rows, sl])).astype(BF16)
    acc_scr[...] += jnp.dot(vt_ref[0], w_scr[...], preferred_element_type=F32)

    @pl.when(e == ne - 1)
    def _done():
        mix = acc_scr[...].T
        if x_ref is None:
            o_ref[...] = mix
        else:
            y = x_ref[0] + gt_ref[0] * mix
            o_ref[0] = _rms(y, g_ref[...]) if norm else y


def _peer_mix(h2, u_tab, vt_tab, r2, c1, a1, a2, x1, gt, g, norm):
    b, l, _ = x1.shape
    t, d = h2.shape
    nh, nkeys, _ = r2.shape
    n_exp = u_tab.shape[0]
    tb = _tile(t, 512)
    ni = SUBLANES
    eb = ni * nkeys
    ne = n_exp // eb
    aux = pl.BlockSpec((nh, nkeys, tb), lambda i, e: (0, 0, i), pipeline_mode=pl.Buffered(1))
    in_specs = [
        pl.BlockSpec((tb, d), lambda i, e: (i, 0), pipeline_mode=pl.Buffered(1)),
        pl.BlockSpec((eb, d), lambda i, e: (e, 0)),
        pl.BlockSpec((1, d, eb), lambda i, e: (e, 0, 0)),
        aux, aux, aux, aux,
    ]
    args = [h2, u_tab, vt_tab, r2, c1, a1, a2]
    fused = l % tb == 0
    if fused:
        per_seq = l // tb
        in_specs += [
            pl.BlockSpec((1, tb, d), lambda i, e: (i // per_seq, i % per_seq, 0), pipeline_mode=pl.Buffered(1)),
            pl.BlockSpec((1, 1, d), lambda i, e: (i // per_seq, 0, 0)),
            pl.BlockSpec((1, d), lambda i, e: (0, 0)),
        ]
        args += [x1, gt, g.reshape(1, d)]
        out_spec = pl.BlockSpec((1, tb, d), lambda i, e: (i // per_seq, i % per_seq, 0))
        out_shape = jax.ShapeDtypeStruct((b, l, d), F32)
    else:
        out_spec = pl.BlockSpec((tb, d), lambda i, e: (i, 0))
        out_shape = jax.ShapeDtypeStruct((t, d), F32)
    out = pl.pallas_call(
        functools.partial(_peer_mix_kernel, nh=nh, nkeys=nkeys, ni=ni, ne=ne, norm=norm),
        grid=(t // tb, ne),
        in_specs=in_specs,
        out_specs=out_spec,
        out_shape=out_shape,
        scratch_shapes=[
            pltpu.VMEM((eb, tb), F32),
            pltpu.VMEM((eb, tb), BF16),
            pltpu.VMEM((d, tb), F32),
            pltpu.VMEM((d, tb), BF16),
        ],
        compiler_params=_params("parallel", "arbitrary"),
        name="peer_mix",
    )(*args)
    return out if fused else _final(x1, out, gt, g, norm)


def _final_kernel(x_ref, p_ref, gt_ref, g_ref, y_ref, *, norm):
    y = x_ref[0] + gt_ref[0] * p_ref[0]
    y_ref[0] = _rms(y, g_ref[...]) if norm else y


def _final(x1, peer, gt, g, norm):
    b, l, d = x1.shape
    tm = _tile(l, 512)
    tok = lambda i, j: (i, j, 0)
    return pl.pallas_call(
        functools.partial(_final_kernel, norm=norm),
        grid=(b, l // tm),
        in_specs=[
            pl.BlockSpec((1, tm, d), tok),
            pl.BlockSpec((1, tm, d), tok),
            pl.BlockSpec((1, 1, d), lambda i, j: (i, 0, 0)),
            pl.BlockSpec((1, d), lambda i, j: (0, 0)),
        ],
        out_specs=pl.BlockSpec((1, tm, d), tok),
        out_shape=jax.ShapeDtypeStruct((b, l, d), F32),
        compiler_params=_params("parallel", "parallel"),
        name="final",
    )(x1, peer.reshape(b, l, d), gt, g.reshape(1, d))


def _rope_tables(pos, rope):
    half = rope // 2
    inv = jnp.power(ROPE_THETA, -jnp.arange(half, dtype=F32) / half)
    ang = pos.astype(F32)[:, None] * inv
    reps = LANES // half
    return jnp.tile(jnp.cos(ang), (1, reps)), jnp.tile(jnp.sin(ang), (1, reps))


def _prep_weights(p, dims):
    d, sw, q_lora, kv_lora, rope, heads, nope = (dims[k] for k in ("d", "sw", "q_lora", "kv_lora", "rope", "heads", "nope"))
    half = rope // 2
    w_in = p["w_in"]
    off_q = sw
    off_kv = off_q + q_lora
    off_kr = off_kv + kv_lora
    off_g = off_kr + rope
    kr_w = w_in[:, off_kr:off_g]
    kr_rot = jnp.concatenate([-kr_w[:, half:], kr_w[:, :half]], axis=1)
    pad = jnp.zeros((d, LANES - rope), F32)
    w_mla = jnp.concatenate([w_in[:, off_q:off_kr], kr_w, pad, kr_rot, pad], axis=1).astype(BF16)
    wq = p["w_qu"].reshape(q_lora, heads, nope + rope)
    wq_n = wq[:, :, :nope].reshape(q_lora, heads * nope)
    wq_r = wq[:, :, nope:]
    wq_rot = jnp.concatenate([-wq_r[:, :, half:], wq_r[:, :, :half]], axis=2)
    hpad = jnp.zeros((q_lora, heads, LANES - rope), F32)
    wq_a = jnp.concatenate([wq_r, hpad], axis=2).reshape(q_lora, heads * LANES)
    wq_b = jnp.concatenate([wq_rot, hpad], axis=2).reshape(q_lora, heads * LANES)
    return {
        "w_gates": w_in[:, off_g:].astype(BF16),
        "w_u": w_in[:, :sw].astype(BF16),
        "w_mla": w_mla,
        "w_q": jnp.concatenate([wq_n, wq_a, wq_b], axis=1).astype(BF16),
        "w_kv": jnp.concatenate([p["w_uk"].reshape(kv_lora, heads * nope),
                                 p["w_uv"].reshape(kv_lora, heads * dims["v_dim"])], axis=1).astype(BF16),
        "w_o": p["w_o"].astype(BF16),
        "w_glu": p["w_glu"].astype(BF16),
        "w_out": p["w_out"].astype(BF16),
        "peer_wq": p["peer_wq"].astype(BF16),
        "peer_k1": p["peer_k1"].astype(BF16),
        "peer_k2": p["peer_k2"].astype(BF16),
        "peer_u": p["peer_u"].astype(BF16),
        "peer_vt": p["peer_v"].reshape(-1, SUBLANES * p["peer_k1"].shape[1], d).transpose(0, 2, 1).astype(BF16),
        "s5": _s5_tables(p["ssm_a_re"], p["ssm_a_im"], p["ssm_log_dt"], p["ssm_b_re"], p["ssm_b_im"],
                         p["ssm_c_re"], p["ssm_c_im"], p["ssm_d"]),
    }


def _layer(x, mod, past_ckv, past_kr, s0, p, w, dims, g_final, last):
    b, l, d = x.shape
    t = b * l
    heads, rope, kv_lora, topk = dims["heads"], dims["rope"], dims["kv_lora"], dims["topk"]
    sh1, sc1, gt1, sh2, sc2, gt2 = mod
    past = 0 if past_ckv is None else past_ckv.shape[1]

    h = _normmod(x, p["g_norm1"], sc1, sh1).reshape(t, d)
    gates = _mm(h, w["w_gates"], BF16, act="sigmoid")
    u = _mm(h, w["w_u"], BF16)
    cos, sin = _rope_tables(past + jnp.arange(l, dtype=jnp.int32), rope)
    qn, qr, ckv, ckv_b, kr, krp = _mla_proj(
        h, w["w_mla"], p["g_q"], p["g_kv"], w["w_q"], cos, sin, l,
        heads=heads, q_lora=dims["q_lora"], kv_lora=kv_lora, rope=rope, scale=dims["scale"])

    ys, s_fin = _s5(u.reshape(b, l, -1), *w["s5"], s0)

    if past_ckv is None:
        lk, keys_c, keys_r = l, ckv_b, krp
        tq = tk = _tile(l, 512)
    else:
        n_keys = past + l
        lk = -(-n_keys // LANES) * LANES
        keys_c = jnp.concatenate([past_ckv.astype(BF16), ckv_b.reshape(b, l, kv_lora)], axis=1)
        keys_c = jnp.pad(keys_c, ((0, 0), (0, lk - n_keys), (0, 0))).reshape(b * lk, kv_lora)
        past_r = jnp.pad(past_kr.astype(BF16), ((0, 0), (0, 0), (0, LANES - rope)))
        keys_r = jnp.concatenate([past_r, krp.reshape(b, l, LANES)], axis=1)
        keys_r = jnp.pad(keys_r, ((0, 0), (0, lk - n_keys), (0, 0))).reshape(b * lk, LANES)
        tq, tk = l, lk
    kh, vh = _kvup(keys_c, w["w_kv"], heads)
    o = _flash(qn, qr, kh, keys_r, vh, batch=b, lq=l, lk=lk, q_pos0=past, n_keys=past + l,
               tq=tq, tk=tk, hb=heads)

    merged = _merge(ys.reshape(t, -1), o, w["w_glu"], w["w_o"], gates)
    x1, h2 = _resid(merged, w["w_out"], x, gt1, p["g_norm2"], sc2, sh2)

    h2 = h2.reshape(t, d)
    q = _mm_split(h2, w["peer_wq"], BF16)
    r2, c1, a1, a2 = _peer_topk(q, w["peer_k1"], w["peer_k2"], topk)
    x2 = _peer_mix(h2, w["peer_u"], w["peer_vt"], r2, c1, a1, a2, x1, gt2, g_final, last)
    return x2, ckv.reshape(b, l, kv_lora), kr.reshape(b, l, rope), s_fin


def kernel(x_prompt, x_sample, c_prompt, c_sample, cache_ckv, cache_krope, state_ssm_re, state_ssm_im, w_ada, b_ada, g_norm1, g_norm2, w_in, g_q, w_qu, g_kv, w_uk, w_uv, w_o, ssm_a_re, ssm_a_im, ssm_log_dt, ssm_b_re, ssm_b_im, ssm_c_re, ssm_c_im, ssm_d, w_glu, w_out, peer_wq, peer_k1, peer_k2, peer_u, peer_v, g_final):
    depth = w_in.shape[0]
    bp, lp, d = x_prompt.shape
    bs, ls, _ = x_sample.shape
    groups, states = ssm_a_re.shape[1:]
    heads, nope = w_uk.shape[2:]
    rope = cache_krope.shape[-1]
    dims = {
        "d": d, "sw": groups * ssm_b_re.shape[3], "q_lora": g_q.shape[1], "kv_lora": g_kv.shape[1],
        "rope": rope, "heads": heads, "nope": nope, "v_dim": w_uv.shape[3],
        "scale": math.log2(math.e) / math.sqrt(nope + rope), "topk": 16,
    }
    assert nope == LANES and dims["v_dim"] == LANES and rope <= LANES

    xp, xs = x_prompt, x_sample
    nb = bp + bs
    rows = -(-nb // 16) * 16
    c_all = jnp.pad(jnp.concatenate([c_prompt, c_sample], axis=0), ((0, rows - nb), (0, 0)))
    zeros = jnp.zeros((bp, 2, groups * states), F32)
    outs_p, outs_s = [], []
    for layer in range(depth):
        p = {
            "g_norm1": g_norm1[layer], "g_norm2": g_norm2[layer], "w_in": w_in[layer], "g_q": g_q[layer],
            "w_qu": w_qu[layer], "g_kv": g_kv[layer], "w_uk": w_uk[layer], "w_uv": w_uv[layer], "w_o": w_o[layer],
            "ssm_a_re": ssm_a_re[layer], "ssm_a_im": ssm_a_im[layer], "ssm_log_dt": ssm_log_dt[layer],
            "ssm_b_re": ssm_b_re[layer], "ssm_b_im": ssm_b_im[layer], "ssm_c_re": ssm_c_re[layer],
            "ssm_c_im": ssm_c_im[layer], "ssm_d": ssm_d[layer], "w_glu": w_glu[layer], "w_out": w_out[layer],
            "peer_wq": peer_wq[layer], "peer_k1": peer_k1[layer], "peer_k2": peer_k2[layer],
            "peer_u": peer_u[layer], "peer_v": peer_v[layer],
        }
        w = _prep_weights(p, dims)
        mod = _ada(c_all, w_ada[layer], b_ada[layer])
        mod_p = [m.reshape(bp, 1, d) for m in jnp.split(mod[:bp], 6, axis=-1)]
        mod_s = [m.reshape(bs, 1, d) for m in jnp.split(mod[bp:nb], 6, axis=-1)]
        s0_s = jnp.stack([state_ssm_re[layer].reshape(bs, -1), state_ssm_im[layer].reshape(bs, -1)], axis=1)
        last = layer == depth - 1
        xp, *res_p = _layer(xp, mod_p, None, None, zeros, p, w, dims, g_final, last)
        xs, *res_s = _layer(xs, mod_s, cache_ckv[layer], cache_krope[layer], s0_s, p, w, dims, g_final, last)
        for res, outs, g in ((res_p, outs_p, bp), (res_s, outs_s, bs)):
            ckv, kr, s_fin = res
            outs.append((ckv, kr, s_fin[:, 0].reshape(g, groups, states), s_fin[:, 1].reshape(g, groups, states)))
    stack = lambda outs, k: jnp.stack([o[k] for o in outs])
    return (xp, xs,
            stack(outs_p, 0), stack(outs_p, 1), stack(outs_p, 2), stack(outs_p, 3),
            stack(outs_s, 0), stack(outs_s, 1), stack(outs_s, 2), stack(outs_s, 3))
```

```python
import functools
import math

import jax
import jax.numpy as jnp
from jax import lax
from jax.experimental import pallas as pl
from jax.experimental.pallas import tpu as pltpu

F32 = jnp.float32
BF16 = jnp.bfloat16

EPS = 1e-6
CHUNK = 64
ROPE_THETA = 10000.0
LANES = 128
SUBLANES = 8
MXU_DIM = 256
BF16_ROWS = 16
TOKEN_TILE = 512
WIDE_TILE = 1024
RESIDENT_WEIGHT_TOKEN_TILE = 256
VMEM_LIMIT_BYTES = 56 * 1024 * 1024
MASKED = -1e30
REMOVED = -3e38
EXCLUDED = -1e38


def _params(*semantics):
    return pltpu.CompilerParams(dimension_semantics=semantics, vmem_limit_bytes=VMEM_LIMIT_BYTES)


def _tile(n, pref):
    if n <= pref:
        return n
    t = pref
    while n % t:
        t //= 2
    assert t >= SUBLANES, (n, pref)
    return t


def _sigmoid(x):
    return 1.0 / (1.0 + jnp.exp(-x))


def _gelu(x):
    return 0.5 * x * (1.0 + jnp.tanh(math.sqrt(2.0 / math.pi) * (x + 0.044715 * (x * x * x))))


def _rms(x, g):
    return x * lax.rsqrt(jnp.mean(x * x, axis=-1, keepdims=True) + EPS) * g


def _ada_kernel(c_ref, w_ref, b_ref, o_ref):
    c = c_ref[...]
    a = (c * _sigmoid(c)).astype(BF16)
    o_ref[...] = jnp.dot(a, w_ref[...].astype(BF16), preferred_element_type=F32) + b_ref[...]


def _ada(c, w, b):
    rows, d = c.shape
    n = w.shape[1]
    tn = _tile(n, WIDE_TILE)
    return pl.pallas_call(
        _ada_kernel,
        grid=(n // tn,),
        in_specs=[
            pl.BlockSpec((rows, d), lambda j: (0, 0)),
            pl.BlockSpec((d, tn), lambda j: (0, j)),
            pl.BlockSpec((1, tn), lambda j: (0, j)),
        ],
        out_specs=pl.BlockSpec((rows, tn), lambda j: (0, j)),
        out_shape=jax.ShapeDtypeStruct((rows, n), F32),
        compiler_params=_params("parallel"),
        name="ada",
    )(c, w, b.reshape(1, n))


def _normmod_kernel(x_ref, g_ref, sc_ref, sh_ref, o_ref):
    h = _rms(x_ref[0], g_ref[...]) * (1.0 + sc_ref[0]) + sh_ref[0]
    o_ref[0] = h.astype(BF16)


def _normmod(x, g, sc, sh):
    b, l, d = x.shape
    tm = _tile(l, TOKEN_TILE)
    return pl.pallas_call(
        _normmod_kernel,
        grid=(b, l // tm),
        in_specs=[
            pl.BlockSpec((1, tm, d), lambda i, j: (i, j, 0)),
            pl.BlockSpec((1, d), lambda i, j: (0, 0)),
            pl.BlockSpec((1, 1, d), lambda i, j: (i, 0, 0)),
            pl.BlockSpec((1, 1, d), lambda i, j: (i, 0, 0)),
        ],
        out_specs=pl.BlockSpec((1, tm, d), lambda i, j: (i, j, 0)),
        out_shape=jax.ShapeDtypeStruct((b, l, d), BF16),
        compiler_params=_params("parallel", "parallel"),
        name="normmod",
    )(x, g.reshape(1, d), sc, sh)


def _mm_kernel(a_ref, w_ref, o_ref, *, act):
    z = jnp.dot(a_ref[...], w_ref[...], preferred_element_type=F32)
    if act == "sigmoid":
        z = _sigmoid(z)
    o_ref[...] = z.astype(o_ref.dtype)


def _mm(a, w, out_dtype, act=None, tn_pref=WIDE_TILE):
    t, k = a.shape
    n = w.shape[1]
    tm = _tile(t, WIDE_TILE)
    tn = _tile(n, tn_pref)
    return pl.pallas_call(
        functools.partial(_mm_kernel, act=act),
        grid=(t // tm, n // tn),
        in_specs=[
            pl.BlockSpec((tm, k), lambda i, j: (i, 0)),
            pl.BlockSpec((k, tn), lambda i, j: (0, j)),
        ],
        out_specs=pl.BlockSpec((tm, tn), lambda i, j: (i, j)),
        out_shape=jax.ShapeDtypeStruct((t, n), out_dtype),
        compiler_params=_params("parallel", "parallel"),
        name="mm",
    )(a, w)


def _mm_split_kernel(a_ref, w_ref, o_ref, *, parts):
    z = jnp.dot(a_ref[...], w_ref[...], preferred_element_type=F32)
    for p in range(parts):
        o_ref[p] = z[:, p * LANES:(p + 1) * LANES].astype(o_ref.dtype)


def _mm_split(a, w, out_dtype):
    t, k = a.shape
    n = w.shape[1]
    tm = _tile(t, TOKEN_TILE)
    tn = _tile(n, WIDE_TILE)
    parts = tn // LANES
    return pl.pallas_call(
        functools.partial(_mm_split_kernel, parts=parts),
        grid=(t // tm, n // tn),
        in_specs=[
            pl.BlockSpec((tm, k), lambda i, j: (i, 0)),
            pl.BlockSpec((k, tn), lambda i, j: (0, j)),
        ],
        out_specs=pl.BlockSpec((parts, tm, LANES), lambda i, j: (j, i, 0)),
        out_shape=jax.ShapeDtypeStruct((n // LANES, t, LANES), out_dtype),
        compiler_params=_params("parallel", "parallel"),
        name="mm_split",
    )(a, w)


def _mla_proj_kernel(h_ref, wm_ref, gq_ref, gkv_ref, wq_ref, cos_ref, sin_ref,
                     qn_ref, qr_ref, ckv_ref, ckvb_ref, kr_ref, krp_ref,
                     *, heads, q_lora, kv_lora, rope, scale):
    z = jnp.dot(h_ref[...], wm_ref[...], preferred_element_type=F32)
    cos = cos_ref[...]
    sin = sin_ref[...]
    off = q_lora + kv_lora
    krp = z[:, off:off + LANES] * cos + z[:, off + LANES:off + 2 * LANES] * sin
    kr_ref[...] = krp[:, :rope]
    krp_ref[...] = krp.astype(BF16)
    ckv = _rms(z[:, q_lora:off], gkv_ref[...])
    ckv_ref[...] = ckv
    ckvb_ref[...] = ckv.astype(BF16)
    qd = _rms(z[:, :q_lora], gq_ref[...]).astype(BF16)
    zq = jnp.dot(qd, wq_ref[...], preferred_element_type=F32)
    hn = heads * LANES
    for hh in range(heads):
        lo = hh * LANES
        qn_ref[hh] = (zq[:, lo:lo + LANES] * scale).astype(BF16)
        qr = zq[:, hn + lo:hn + lo + LANES] * cos + zq[:, 2 * hn + lo:2 * hn + lo + LANES] * sin
        qr_ref[hh] = (qr * scale).astype(BF16)


def _mla_proj(h, wm, gq, gkv, wq, cos, sin, seq, *, heads, q_lora, kv_lora, rope, scale):
    t, d = h.shape
    tm = _tile(t, RESIDENT_WEIGHT_TOKEN_TILE)
    if seq % tm == 0:
        nrep = seq // tm
        tab_map = lambda i: (i % nrep, 0)
    else:
        assert tm % seq == 0
        cos = jnp.tile(cos, (tm // seq, 1))
        sin = jnp.tile(sin, (tm // seq, 1))
        tab_map = lambda i: (0, 0)
    nm = wm.shape[1]
    nq = wq.shape[1]
    row = lambda i: (i, 0)
    const = lambda i: (0, 0)
    return pl.pallas_call(
        functools.partial(_mla_proj_kernel, heads=heads, q_lora=q_lora, kv_lora=kv_lora, rope=rope, scale=scale),
        grid=(t // tm,),
        in_specs=[
            pl.BlockSpec((tm, d), row),
            pl.BlockSpec((d, nm), const),
            pl.BlockSpec((1, q_lora), const),
            pl.BlockSpec((1, kv_lora), const),
            pl.BlockSpec((q_lora, nq), const),
            pl.BlockSpec((tm, LANES), tab_map),
            pl.BlockSpec((tm, LANES), tab_map),
        ],
        out_specs=[
            pl.BlockSpec((heads, tm, LANES), lambda i: (0, i, 0)),
            pl.BlockSpec((heads, tm, LANES), lambda i: (0, i, 0)),
            pl.BlockSpec((tm, kv_lora), row),
            pl.BlockSpec((tm, kv_lora), row),
            pl.BlockSpec((tm, rope), row),
            pl.BlockSpec((tm, LANES), row),
        ],
        out_shape=[
            jax.ShapeDtypeStruct((heads, t, LANES), BF16),
            jax.ShapeDtypeStruct((heads, t, LANES), BF16),
            jax.ShapeDtypeStruct((t, kv_lora), F32),
            jax.ShapeDtypeStruct((t, kv_lora), BF16),
            jax.ShapeDtypeStruct((t, rope), F32),
            jax.ShapeDtypeStruct((t, LANES), BF16),
        ],
        compiler_params=_params("parallel"),
        name="mla_proj",
    )(h, wm, gq.reshape(1, q_lora), gkv.reshape(1, kv_lora), wq, cos, sin)


def _kvup_kernel(c_ref, w_ref, k_ref, v_ref, *, heads):
    z = jnp.dot(c_ref[...], w_ref[...], preferred_element_type=F32)
    for hh in range(heads):
        k_ref[hh] = z[:, hh * LANES:(hh + 1) * LANES].astype(BF16)
        v_ref[hh] = z[:, (heads + hh) * LANES:(heads + hh + 1) * LANES].astype(BF16)


def _kvup(ckv, w, heads):
    t, c = ckv.shape
    tm = _tile(t, TOKEN_TILE)
    if t % tm:
        tm = t
    return pl.pallas_call(
        functools.partial(_kvup_kernel, heads=heads),
        grid=(t // tm,),
        in_specs=[
            pl.BlockSpec((tm, c), lambda i: (i, 0)),
            pl.BlockSpec(w.shape, lambda i: (0, 0)),
        ],
        out_specs=[
            pl.BlockSpec((heads, tm, LANES), lambda i: (0, i, 0)),
            pl.BlockSpec((heads, tm, LANES), lambda i: (0, i, 0)),
        ],
        out_shape=[jax.ShapeDtypeStruct((heads, t, LANES), BF16)] * 2,
        compiler_params=_params("parallel"),
        name="kvup",
    )(ckv, w)


def _flash_kernel(iq_tab, ik_tab, flag_tab, qn_ref, qr_ref, k_ref, kr_ref, v_ref, o_ref,
                  qc_scr, m_scr, acc_scr, *, hb, tq, tk, q_pos0, n_keys):
    pair = pl.program_id(2)
    iq = iq_tab[pair]
    ik = ik_tab[pair]
    flags = flag_tab[pair]
    first = (flags & 1) != 0
    last = (flags & 2) != 0
    full = (flags & 4) != 0

    @pl.when(first)
    def _init():
        m_scr[...] = jnp.full(m_scr.shape, MASKED, F32)
        acc_scr[...] = jnp.zeros(acc_scr.shape, F32)
        for hh in range(hb):
            qc_scr[hh] = jnp.concatenate([qn_ref[hh], qr_ref[hh]], axis=1)

    def step(masked):
        kr = kr_ref[...]
        ones = jnp.ones((tk, LANES), BF16)
        if masked:
            qp = q_pos0 + iq * tq + lax.broadcasted_iota(jnp.int32, (tq, tk), 0)
            kp = ik * tk + lax.broadcasted_iota(jnp.int32, (tq, tk), 1)
            allowed = jnp.logical_and(kp // CHUNK <= qp // CHUNK, kp < n_keys)

        def head(hh, carry):
            kc = jnp.concatenate([k_ref[hh], kr], axis=1)
            s = lax.dot_general(qc_scr[hh], kc, (((1,), (1,)), ((), ())), preferred_element_type=F32)
            if masked:
                s = jnp.where(allowed, s, MASKED)
            m_prev = m_scr[hh]
            m_new = jnp.maximum(m_prev, jnp.max(s, axis=1, keepdims=True))
            alpha = jnp.exp2(m_prev - m_new)
            p = jnp.exp2(s - pltpu.repeat(m_new, tk // LANES, axis=1)).astype(BF16)
            vc = jnp.concatenate([v_ref[hh], ones], axis=1)
            acc_scr[hh] = pltpu.repeat(alpha, 2, axis=1) * acc_scr[hh] + jnp.dot(p, vc, preferred_element_type=F32)
            m_scr[hh] = m_new
            return carry

        lax.fori_loop(0, hb, head, 0, unroll=True)

    @pl.when(full)
    def _full():
        step(False)

    @pl.when(jnp.logical_not(full))
    def _diag():
        step(True)

    @pl.when(last)
    def _done():
        for hh in range(hb):
            acc = acc_scr[hh]
            o_ref[hh] = (acc[:, :LANES] / acc[:, LANES:]).astype(BF16)


def _flash_pairs(lq, lk, tq, tk, q_pos0, n_keys):
    iqs, iks, flags = [], [], []
    for iq in range(lq // tq):
        q_first = q_pos0 + iq * tq
        last_key = min(n_keys - 1, ((q_first + tq - 1) // CHUNK) * CHUNK + CHUNK - 1)
        full_key = min(n_keys - 1, (q_first // CHUNK) * CHUNK + CHUNK - 1)
        n_blocks = last_key // tk + 1
        for ik in range(n_blocks):
            full = (ik + 1) * tk - 1 <= full_key
            iqs.append(iq)
            iks.append(ik)
            flags.append((1 if ik == 0 else 0) | (2 if ik == n_blocks - 1 else 0) | (4 if full else 0))
    as_i32 = lambda xs: jnp.asarray(xs, jnp.int32)
    return as_i32(iqs), as_i32(iks), as_i32(flags)


def _flash(qn, qr, k, krp, v, *, batch, lq, lk, q_pos0, n_keys, tq, tk, hb):
    heads = qn.shape[0]
    nq = lq // tq
    nk = lk // tk
    assert lq % tq == 0 and lk % tk == 0 and heads % hb == 0 and tk % LANES == 0
    iq_tab, ik_tab, flag_tab = _flash_pairs(lq, lk, tq, tk, q_pos0, n_keys)
    qmap = lambda b, h, p, iqt, ikt, ft: (h, b * nq + iqt[p], 0)
    kmap = lambda b, h, p, iqt, ikt, ft: (h, b * nk + ikt[p], 0)
    return pl.pallas_call(
        functools.partial(_flash_kernel, hb=hb, tq=tq, tk=tk, q_pos0=q_pos0, n_keys=n_keys),
        grid_spec=pltpu.PrefetchScalarGridSpec(
            num_scalar_prefetch=3,
            grid=(batch, heads // hb, iq_tab.shape[0]),
            in_specs=[
                pl.BlockSpec((hb, tq, LANES), qmap),
                pl.BlockSpec((hb, tq, LANES), qmap),
                pl.BlockSpec((hb, tk, LANES), kmap),
                pl.BlockSpec((tk, LANES), lambda b, h, p, iqt, ikt, ft: (b * nk + ikt[p], 0)),
                pl.BlockSpec((hb, tk, LANES), kmap),
            ],
            out_specs=pl.BlockSpec((hb, tq, LANES), qmap),
            scratch_shapes=[
                pltpu.VMEM((hb, tq, 2 * LANES), BF16),
                pltpu.VMEM((hb, tq, LANES), F32),
                pltpu.VMEM((hb, tq, 2 * LANES), F32),
            ],
        ),
        out_shape=jax.ShapeDtypeStruct((heads, batch * lq, LANES), BF16),
        compiler_params=_params("parallel", "parallel", "arbitrary"),
        name="flash",
    )(iq_tab, ik_tab, flag_tab, qn, qr, k, krp, v)


def _s5_kernel(u_ref, pm_ref, pt_ref, wb_ref, scn_ref, wc_ref, d_ref, s0_ref, y_ref, st_ref,
               xs_ref, up_ref, yp_ref, car_ref, *, ts, nkt, kw, sw, nt):
    it = pl.program_id(1)
    steps = ts // SUBLANES

    @pl.when(it == 0)
    def _load_state():
        car_ref[...] = s0_ref[0]

    up_ref[...] = jnp.dot(pm_ref[...], u_ref[0], preferred_element_type=F32).astype(BF16)
    first_row = lax.broadcasted_iota(jnp.int32, (SUBLANES, sw), 0) == 0
    for kt in range(nkt):
        ukt = up_ref[:, kt * kw:(kt + 1) * kw]
        xs_ref[...] = jnp.dot(ukt, wb_ref[kt], preferred_element_type=F32)
        st_sl = slice(kt * sw, (kt + 1) * sw)
        lr = scn_ref[kt, 0]
        li = scn_ref[kt, 1]

        def advance(k, carry, store, lr=lr, li=li):
            hr, hi = carry
            rows = pl.ds(pl.multiple_of(k * SUBLANES, SUBLANES), SUBLANES)
            nr = lr * hr - li * hi + xs_ref[rows, :sw]
            ni = lr * hi + li * hr + xs_ref[rows, sw:]
            if store:
                xs_ref[rows, :sw] = nr
                xs_ref[rows, sw:] = ni
            return nr, ni

        zero = jnp.zeros((SUBLANES, sw), F32)
        er, ei = lax.fori_loop(0, steps, functools.partial(advance, store=False), (zero, zero))
        br = jnp.where(first_row, jnp.broadcast_to(car_ref[0:1, st_sl], (SUBLANES, sw)), pltpu.roll(er, 1, 0))
        bi = jnp.where(first_row, jnp.broadcast_to(car_ref[1:2, st_sl], (SUBLANES, sw)), pltpu.roll(ei, 1, 0))
        for c0, dist in ((2, 1), (4, 2), (6, 4)):
            ar = scn_ref[kt, c0]
            ai = scn_ref[kt, c0 + 1]
            sr = pltpu.roll(br, dist, 0)
            si = pltpu.roll(bi, dist, 0)
            br, bi = br + ar * sr - ai * si, bi + ar * si + ai * sr
        hr, hi = lax.fori_loop(0, steps, functools.partial(advance, store=True), (br, bi))
        car_ref[0:1, st_sl] = hr[SUBLANES - 1:SUBLANES]
        car_ref[1:2, st_sl] = hi[SUBLANES - 1:SUBLANES]
        y = jnp.dot(xs_ref[...].astype(BF16), wc_ref[kt], preferred_element_type=F32)
        y = y + d_ref[:, kt * kw:(kt + 1) * kw] * ukt.astype(F32)
        yp_ref[:, kt * kw:(kt + 1) * kw] = y.astype(BF16)
    y_ref[0] = _gelu(jnp.dot(pt_ref[...], yp_ref[...], preferred_element_type=F32)).astype(BF16)

    @pl.when(it == nt - 1)
    def _store_state():
        st_ref[0] = car_ref[...]


def _s5(u, wb, lam, wc, dsk, s0):
    b, l, w = u.shape
    nkt, kw, sw2 = wb.shape
    sw = sw2 // 2
    ns = s0.shape[2]
    ts = _tile(l, TOKEN_TILE)
    nt = l // ts
    steps = ts // SUBLANES

    def cmul(x, y):
        return x[0] * y[0] - x[1] * y[1], x[0] * y[1] + x[1] * y[0]

    mu = (lam[0], lam[1])
    for _ in range(steps - 1):
        mu = cmul(mu, (lam[0], lam[1]))
    mu2 = cmul(mu, mu)
    mu4 = cmul(mu2, mu2)
    rows = jnp.arange(SUBLANES)[None, :, None]
    shifted = lambda c, dist: [jnp.where(rows >= dist, x[:, None, :], 0.0) for x in c]
    rep = [jnp.broadcast_to(x[:, None, :], (nkt, SUBLANES, sw)) for x in (lam[0], lam[1])]
    scn = jnp.stack(rep + shifted(mu, 1) + shifted(mu2, 2) + shifted(mu4, 4), axis=1).astype(F32)
    r = jnp.arange(ts)
    pm = jax.nn.one_hot((r % SUBLANES) * steps + r // SUBLANES, ts, dtype=BF16)
    const2 = lambda i, j: (0, 0)
    const3 = lambda i, j: (0, 0, 0)
    return pl.pallas_call(
        functools.partial(_s5_kernel, ts=ts, nkt=nkt, kw=kw, sw=sw, nt=nt),
        grid=(b, nt),
        in_specs=[
            pl.BlockSpec((1, ts, w), lambda i, j: (i, j, 0)),
            pl.BlockSpec((ts, ts), const2),
            pl.BlockSpec((ts, ts), const2),
            pl.BlockSpec(wb.shape, const3),
            pl.BlockSpec(scn.shape, lambda i, j: (0, 0, 0, 0)),
            pl.BlockSpec(wc.shape, const3),
            pl.BlockSpec((1, w), const2),
            pl.BlockSpec((1, 2, ns), lambda i, j: (i, 0, 0)),
        ],
        out_specs=[
            pl.BlockSpec((1, ts, w), lambda i, j: (i, j, 0)),
            pl.BlockSpec((1, 2, ns), lambda i, j: (i, 0, 0)),
        ],
        out_shape=[
            jax.ShapeDtypeStruct((b, l, w), BF16),
            jax.ShapeDtypeStruct((b, 2, ns), F32),
        ],
        scratch_shapes=[
            pltpu.VMEM((ts, sw2), F32),
            pltpu.VMEM((ts, w), BF16),
            pltpu.VMEM((ts, w), BF16),
            pltpu.VMEM((2, ns), F32),
        ],
        compiler_params=_params("parallel", "arbitrary"),
        name="s5",
    )(u, pm, pm.T, wb, scn, wc, dsk, s0)


def _s5_tables(a_re, a_im, log_dt, b_re, b_im, c_re, c_im, d_skip):
    g, p = a_re.shape
    ch = b_re.shape[2]
    gpt = MXU_DIM // ch
    nkt = g // gpt
    dt = jnp.exp(log_dt.astype(F32))[:, None]
    mag = jnp.exp(dt * a_re)
    ab_re = mag * jnp.cos(dt * a_im)
    ab_im = mag * jnp.sin(dt * a_im)
    den = a_re * a_re + a_im * a_im
    nr = ab_re - 1.0
    f_re = (nr * a_re + ab_im * a_im) / den
    f_im = (ab_im * a_re - nr * a_im) / den
    bb_re = f_re[..., None] * b_re - f_im[..., None] * b_im
    bb_im = f_re[..., None] * b_im + f_im[..., None] * b_re
    eye = jnp.eye(gpt, dtype=F32)

    def in_blockdiag(bb):
        return jnp.einsum("kgpi,gh->kgihp", bb.reshape(nkt, gpt, p, ch), eye).reshape(nkt, gpt * ch, gpt * p)

    def out_blockdiag(cc):
        return jnp.einsum("kgjp,gh->kgphj", cc.reshape(nkt, gpt, ch, p), eye).reshape(nkt, gpt * p, gpt * ch)

    wb = jnp.concatenate([in_blockdiag(bb_re), in_blockdiag(bb_im)], axis=2).astype(BF16)
    wc = jnp.concatenate([out_blockdiag(c_re), out_blockdiag(-c_im)], axis=1).astype(BF16)
    lam = jnp.stack([ab_re.reshape(nkt, gpt * p), ab_im.reshape(nkt, gpt * p)]).astype(F32)
    return wb, lam, wc, d_skip.reshape(1, g * ch).astype(F32)


def _merge_kernel(ys_ref, o_ref, wga_ref, wgb_ref, wo_ref, ga_ref, gb_ref, out_ref, *, heads):
    ys = ys_ref[...]
    ya = jnp.dot(ys, wga_ref[...], preferred_element_type=F32)
    ya = ya * _sigmoid(jnp.dot(ys, wgb_ref[...], preferred_element_type=F32))
    oc = jnp.concatenate([o_ref[hh] for hh in range(heads)], axis=1)
    yb = jnp.dot(oc, wo_ref[...], preferred_element_type=F32)
    out_ref[...] = (ga_ref[...].astype(F32) * ya + gb_ref[...].astype(F32) * yb).astype(BF16)


def _merge(ys, o, w_glu, w_o, gates):
    t, sw = ys.shape
    heads = o.shape[0]
    d = w_o.shape[1]
    tm = _tile(t, TOKEN_TILE)
    tn = _tile(d, WIDE_TILE)
    nj = d // tn
    return pl.pallas_call(
        functools.partial(_merge_kernel, heads=heads),
        grid=(t // tm, nj),
        in_specs=[
            pl.BlockSpec((tm, sw), lambda i, j: (i, 0)),
            pl.BlockSpec((heads, tm, LANES), lambda i, j: (0, i, 0)),
            pl.BlockSpec((sw, tn), lambda i, j: (0, j)),
            pl.BlockSpec((sw, tn), lambda i, j: (0, nj + j)),
            pl.BlockSpec((heads * LANES, tn), lambda i, j: (0, j)),
            pl.BlockSpec((tm, tn), lambda i, j: (i, j)),
            pl.BlockSpec((tm, tn), lambda i, j: (i, nj + j)),
        ],
        out_specs=pl.BlockSpec((tm, tn), lambda i, j: (i, j)),
        out_shape=jax.ShapeDtypeStruct((t, d), BF16),
        compiler_params=_params("parallel", "parallel"),
        name="merge",
    )(ys, o, w_glu, w_glu, w_o, gates, gates)


def _resid_kernel(m_ref, w_ref, x_ref, gt_ref, g_ref, sc_ref, sh_ref, x1_ref, h2_ref):
    x1 = x_ref[0] + gt_ref[0] * jnp.dot(m_ref[0], w_ref[...], preferred_element_type=F32)
    x1_ref[0] = x1
    h2_ref[0] = (_rms(x1, g_ref[...]) * (1.0 + sc_ref[0]) + sh_ref[0]).astype(BF16)


def _resid(merged, w_out, x, gt, g2, sc, sh):
    b, l, d = x.shape
    tm = _tile(l, TOKEN_TILE)
    tok = lambda i, j: (i, j, 0)
    per_b = lambda i, j: (i, 0, 0)
    return pl.pallas_call(
        _resid_kernel,
        grid=(b, l // tm),
        in_specs=[
            pl.BlockSpec((1, tm, d), tok),
            pl.BlockSpec((d, d), lambda i, j: (0, 0)),
            pl.BlockSpec((1, tm, d), tok),
            pl.BlockSpec((1, 1, d), per_b),
            pl.BlockSpec((1, d), lambda i, j: (0, 0)),
            pl.BlockSpec((1, 1, d), per_b),
            pl.BlockSpec((1, 1, d), per_b),
        ],
        out_specs=[pl.BlockSpec((1, tm, d), tok), pl.BlockSpec((1, tm, d), tok)],
        out_shape=[jax.ShapeDtypeStruct((b, l, d), F32), jax.ShapeDtypeStruct((b, l, d), BF16)],
        compiler_params=_params("parallel", "parallel"),
        name="resid",
    )(merged.reshape(b, l, d), w_out, x, gt, g2.reshape(1, d), sc, sh)


def _peer_topk_kernel(q_ref, k1_ref, k2_ref, flat_ref, invalid_ref, r2_ref, c1_ref, a1_ref, a2_ref,
                      *, nh, nkeys, topk):
    tb = q_ref.shape[1]
    iota_k = lax.broadcasted_iota(jnp.int32, (nkeys, tb), 0).astype(F32)
    iota_t = lax.broadcasted_iota(jnp.int32, (topk, tb), 0).astype(F32)
    flat = flat_ref[...]
    invalid = invalid_ref[...]
    nt_dims = (((1,), (1,)), ((), ()))

    n_cand = flat.shape[0]

    def extract(s, exact_ties):
        work = s
        rank = jnp.full((nkeys, tb), float(topk), F32)
        vals = []
        for a in range(topk):
            m = jnp.max(work, axis=0, keepdims=True)
            if exact_ties:
                idx = jnp.min(jnp.where(work == m, iota_k, float(nkeys)), axis=0, keepdims=True)
                sel = iota_k == idx
            else:
                sel = work == m
            rank = jnp.where(sel, float(a), rank)
            work = jnp.where(sel, REMOVED, work)
            vals.append(m)
        ranked = jnp.sum(jnp.where(rank < float(topk), 1.0, 0.0), axis=0, keepdims=True)
        return vals, rank, ranked

    def stack(vals):
        out = jnp.zeros((topk, tb), F32)
        for a in range(topk):
            out = jnp.where(iota_t == float(a), vals[a], out)
        return out

    def candidates(v1, v2):
        vs1 = stack(v1)
        vs2 = stack(v2)
        blocks = [v1[0] + vs2]
        for a in range(1, SUBLANES):
            blocks.append(v1[a] + vs2[:SUBLANES])
        blocks.append(vs1[SUBLANES:] + v2[0])
        return jnp.concatenate(blocks, axis=0) + invalid

    def route(s1, s2, exact_ties):
        v1, rank1, n1 = extract(s1, exact_ties)
        v2, rank2, n2 = extract(s2, exact_ties)
        cand = candidates(v1, v2)
        counts = jnp.zeros((topk, tb), F32)
        top = None
        zsum = None
        for kk in range(topk):
            m = jnp.max(cand, axis=0, keepdims=True)
            if exact_ties:
                f = jnp.min(jnp.where(cand == m, flat, 1e9), axis=0, keepdims=True)
                cand = jnp.where(flat == f, REMOVED, cand)
                counts = counts + jnp.where(iota_t == jnp.floor(f * (1.0 / topk)), 1.0, 0.0)
            else:
                cand = jnp.where(cand == m, REMOVED, cand)
            if kk == 0:
                top = m
                zsum = jnp.ones_like(m)
            else:
                zsum = zsum + jnp.exp(m - top)
        n3 = None
        if not exact_ties:
            gone = jnp.where(cand == REMOVED, 1.0, 0.0)
            n3 = jnp.sum(gone, axis=0, keepdims=True)
            per_rank = [jnp.sum(gone[:topk], axis=0, keepdims=True)]
            for a in range(1, SUBLANES):
                lo = topk + (a - 1) * SUBLANES
                per_rank.append(jnp.sum(gone[lo:lo + SUBLANES], axis=0, keepdims=True))
            counts = jnp.concatenate([stack(per_rank + [per_rank[0]] * (topk - SUBLANES))[:SUBLANES],
                                      gone[n_cand - SUBLANES:]], axis=0)
        c1 = jnp.zeros((nkeys, tb), F32)
        for a in range(topk):
            c1 = jnp.where(rank1 == float(a), counts[a:a + 1], c1)
        a1 = jnp.exp(s1 - v1[0]) * (1.0 / zsum)
        a2 = jnp.exp(s2 - v2[0])
        clean = None
        if not exact_ties:
            want = float(topk)
            bad = jnp.where(n1 != want, 1.0, 0.0) + jnp.where(n2 != want, 1.0, 0.0) + jnp.where(n3 != want, 1.0, 0.0)
            clean = jnp.max(bad) == 0.0
        return (rank2, c1, a1, a2), clean

    def store(hh, tables):
        r2_ref[hh], c1_ref[hh], a1_ref[hh], a2_ref[hh] = tables

    per_trip = 4 if nh % 4 == 0 else 2

    def group(gg, carry):
        heads = [per_trip * gg + r for r in range(per_trip)]
        scores = []
        for hh in heads:
            scores.append((lax.dot_general(k1_ref[hh], q_ref[2 * hh], nt_dims, preferred_element_type=F32),
                           lax.dot_general(k2_ref[hh], q_ref[2 * hh + 1], nt_dims, preferred_element_type=F32)))
        quick = [route(s1, s2, False) for s1, s2 in scores]
        clean = quick[0][1]
        for _, ok in quick[1:]:
            clean = jnp.logical_and(clean, ok)

        @pl.when(clean)
        def _no_ties():
            for hh, (tables, _) in zip(heads, quick):
                store(hh, tables)

        @pl.when(jnp.logical_not(clean))
        def _ties():
            for hh, (s1, s2) in zip(heads, scores):
                store(hh, route(s1, s2, True)[0])

        return carry

    lax.fori_loop(0, nh // per_trip, group, 0)


def _peer_topk(q, k1, k2, topk):
    nh, nkeys, half = k1.shape
    t = q.shape[1]
    tb = LANES
    assert topk == 2 * SUBLANES and half == LANES and t % tb == 0
    rows = jnp.arange(topk + (SUBLANES - 1) * SUBLANES + SUBLANES)
    a_idx = jnp.where(rows < topk, 0, jnp.where(rows < topk + (SUBLANES - 1) * SUBLANES,
                                                1 + (rows - topk) // SUBLANES, SUBLANES + (rows - topk - (SUBLANES - 1) * SUBLANES)))
    b_idx = jnp.where(rows < topk, rows, jnp.where(rows < topk + (SUBLANES - 1) * SUBLANES, (rows - topk) % SUBLANES, 0))
    flat = jnp.broadcast_to((a_idx * topk + b_idx).astype(F32)[:, None], (rows.shape[0], tb))
    invalid = jnp.broadcast_to(jnp.where((a_idx + 1) * (b_idx + 1) <= topk, 0.0, EXCLUDED).astype(F32)[:, None],
                               (rows.shape[0], tb))
    out = lambda dt: jax.ShapeDtypeStruct((nh, nkeys, t), dt)
    ospec = pl.BlockSpec((nh, nkeys, tb), lambda i: (0, 0, i))
    return pl.pallas_call(
        functools.partial(_peer_topk_kernel, nh=nh, nkeys=nkeys, topk=topk),
        grid=(t // tb,),
        in_specs=[
            pl.BlockSpec((2 * nh, tb, LANES), lambda i: (0, i, 0)),
            pl.BlockSpec(k1.shape, lambda i: (0, 0, 0)),
            pl.BlockSpec(k2.shape, lambda i: (0, 0, 0)),
            pl.BlockSpec(flat.shape, lambda i: (0, 0)),
            pl.BlockSpec(invalid.shape, lambda i: (0, 0)),
        ],
        out_specs=[ospec] * 4,
        out_shape=[out(F32)] * 4,
        compiler_params=_params("parallel"),
        name="peer_topk",
    )(q, k1, k2, flat, invalid)


def _peer_mix_kernel(h_ref, u_ref, vt_ref, r2_ref, c1_ref, a1_ref, a2_ref, *rest, nh, nkeys, ni, ne, norm):
    if len(rest) == 5:
        o_ref, ht_scr, w_scr, acc_scr, xt_scr = rest
        x_ref = gt_ref = g_ref = None
    else:
        x_ref, gt_ref, g_ref, o_ref, ht_scr, w_scr, acc_scr, xt_scr = rest
    e = pl.program_id(1)
    tb = h_ref.shape[0]

    @pl.when(e == 0)
    def _init():
        acc_scr[...] = jnp.zeros(acc_scr.shape, F32)
        xt_scr[...] = h_ref[...].T

    ht_scr[...] = jnp.dot(u_ref[...], xt_scr[...], preferred_element_type=F32)
    first_keys = pl.ds(pl.multiple_of(e * ni, SUBLANES), ni)
    for il in range(ni):
        rows = slice(il * nkeys, (il + 1) * nkeys)
        for lg in range(tb // LANES):
            sl = slice(lg * LANES, (lg + 1) * LANES)
            gate = jnp.zeros((nkeys, LANES), F32)
            for hh in range(nh):
                partners = c1_ref[hh, first_keys, sl][il:il + 1]
                first = a1_ref[hh, first_keys, sl][il:il + 1]
                gate = gate + jnp.where(r2_ref[hh, :, sl] < partners, a2_ref[hh, :, sl], 0.0) * first
            w_scr[rows, sl] = (gate * _gelu(ht_scr[rows, sl])).astype(BF16)
    acc_scr[...] += jnp.dot(vt_ref[0], w_scr[...], preferred_element_type=F32)

    @pl.when(e == ne - 1)
    def _done():
        mix = acc_scr[...].T
        if x_ref is None:
            o_ref[...] = mix
        else:
            y = x_ref[0] + gt_ref[0] * mix
            o_ref[0] = _rms(y, g_ref[...]) if norm else y


def _peer_mix(h2, u_tab, vt_tab, r2, c1, a1, a2, x1, gt, g, norm):
    b, l, _ = x1.shape
    t, d = h2.shape
    nh, nkeys, _ = r2.shape
    n_exp = u_tab.shape[0]
    tb = _tile(t, TOKEN_TILE)
    ni = SUBLANES
    eb = ni * nkeys
    ne = n_exp // eb
    aux = pl.BlockSpec((nh, nkeys, tb), lambda i, e: (0, 0, i), pipeline_mode=pl.Buffered(1))
    in_specs = [
        pl.BlockSpec((tb, d), lambda i, e: (i, 0), pipeline_mode=pl.Buffered(1)),
        pl.BlockSpec((eb, d), lambda i, e: (e, 0)),
        pl.BlockSpec((1, d, eb), lambda i, e: (e, 0, 0)),
        aux, aux, aux, aux,
    ]
    args = [h2, u_tab, vt_tab, r2, c1, a1, a2]
    fused = l % tb == 0
    if fused:
        per_seq = l // tb
        in_specs += [
            pl.BlockSpec((1, tb, d), lambda i, e: (i // per_seq, i % per_seq, 0), pipeline_mode=pl.Buffered(1)),
            pl.BlockSpec((1, 1, d), lambda i, e: (i // per_seq, 0, 0)),
            pl.BlockSpec((1, d), lambda i, e: (0, 0)),
        ]
        args += [x1, gt, g.reshape(1, d)]
        out_spec = pl.BlockSpec((1, tb, d), lambda i, e: (i // per_seq, i % per_seq, 0))
        out_shape = jax.ShapeDtypeStruct((b, l, d), F32)
    else:
        out_spec = pl.BlockSpec((tb, d), lambda i, e: (i, 0))
        out_shape = jax.ShapeDtypeStruct((t, d), F32)
    out = pl.pallas_call(
        functools.partial(_peer_mix_kernel, nh=nh, nkeys=nkeys, ni=ni, ne=ne, norm=norm),
        grid=(t // tb, ne),
        in_specs=in_specs,
        out_specs=out_spec,
        out_shape=out_shape,
        scratch_shapes=[
            pltpu.VMEM((eb, tb), F32),
            pltpu.VMEM((eb, tb), BF16),
            pltpu.VMEM((d, tb), F32),
            pltpu.VMEM((d, tb), BF16),
        ],
        compiler_params=_params("parallel", "arbitrary"),
        name="peer_mix",
    )(*args)
    return out if fused else _final(x1, out, gt, g, norm)


def _final_kernel(x_ref, p_ref, gt_ref, g_ref, y_ref, *, norm):
    y = x_ref[0] + gt_ref[0] * p_ref[0]
    y_ref[0] = _rms(y, g_ref[...]) if norm else y


def _final(x1, peer, gt, g, norm):
    b, l, d = x1.shape
    tm = _tile(l, TOKEN_TILE)
    tok = lambda i, j: (i, j, 0)
    return pl.pallas_call(
        functools.partial(_final_kernel, norm=norm),
        grid=(b, l // tm),
        in_specs=[
            pl.BlockSpec((1, tm, d), tok),
            pl.BlockSpec((1, tm, d), tok),
            pl.BlockSpec((1, 1, d), lambda i, j: (i, 0, 0)),
            pl.BlockSpec((1, d), lambda i, j: (0, 0)),
        ],
        out_specs=pl.BlockSpec((1, tm, d), tok),
        out_shape=jax.ShapeDtypeStruct((b, l, d), F32),
        compiler_params=_params("parallel", "parallel"),
        name="final",
    )(x1, peer.reshape(b, l, d), gt, g.reshape(1, d))


def _rope_tables(pos, rope):
    half = rope // 2
    inv = jnp.power(ROPE_THETA, -jnp.arange(half, dtype=F32) / half)
    ang = pos.astype(F32)[:, None] * inv
    reps = LANES // half
    return jnp.tile(jnp.cos(ang), (1, reps)), jnp.tile(jnp.sin(ang), (1, reps))


def _prep_weights(p, dims):
    d, sw, q_lora, kv_lora, rope, heads, nope = (dims[k] for k in ("d", "sw", "q_lora", "kv_lora", "rope", "heads", "nope"))
    half = rope // 2
    w_in = p["w_in"]
    off_q = sw
    off_kv = off_q + q_lora
    off_kr = off_kv + kv_lora
    off_g = off_kr + rope
    kr_w = w_in[:, off_kr:off_g]
    kr_rot = jnp.concatenate([-kr_w[:, half:], kr_w[:, :half]], axis=1)
    pad = jnp.zeros((d, LANES - rope), F32)
    w_mla = jnp.concatenate([w_in[:, off_q:off_kr], kr_w, pad, kr_rot, pad], axis=1).astype(BF16)
    wq = p["w_qu"].reshape(q_lora, heads, nope + rope)
    wq_n = wq[:, :, :nope].reshape(q_lora, heads * nope)
    wq_r = wq[:, :, nope:]
    wq_rot = jnp.concatenate([-wq_r[:, :, half:], wq_r[:, :, :half]], axis=2)
    hpad = jnp.zeros((q_lora, heads, LANES - rope), F32)
    wq_a = jnp.concatenate([wq_r, hpad], axis=2).reshape(q_lora, heads * LANES)
    wq_b = jnp.concatenate([wq_rot, hpad], axis=2).reshape(q_lora, heads * LANES)
    return {
        "w_gates": w_in[:, off_g:].astype(BF16),
        "w_u": w_in[:, :sw].astype(BF16),
        "w_mla": w_mla,
        "w_q": jnp.concatenate([wq_n, wq_a, wq_b], axis=1).astype(BF16),
        "w_kv": jnp.concatenate([p["w_uk"].reshape(kv_lora, heads * nope),
                                 p["w_uv"].reshape(kv_lora, heads * dims["v_dim"])], axis=1).astype(BF16),
        "w_o": p["w_o"].astype(BF16),
        "w_glu": p["w_glu"].astype(BF16),
        "w_out": p["w_out"].astype(BF16),
        "peer_wq": p["peer_wq"].astype(BF16),
        "peer_k1": p["peer_k1"].astype(BF16),
        "peer_k2": p["peer_k2"].astype(BF16),
        "peer_u": p["peer_u"].astype(BF16),
        "peer_vt": p["peer_v"].reshape(-1, SUBLANES * p["peer_k1"].shape[1], d).transpose(0, 2, 1).astype(BF16),
        "s5": _s5_tables(p["ssm_a_re"], p["ssm_a_im"], p["ssm_log_dt"], p["ssm_b_re"], p["ssm_b_im"],
                         p["ssm_c_re"], p["ssm_c_im"], p["ssm_d"]),
    }


def _layer(x, mod, past_ckv, past_kr, s0, p, w, dims, g_final, last):
    b, l, d = x.shape
    t = b * l
    heads, rope, kv_lora, topk = dims["heads"], dims["rope"], dims["kv_lora"], dims["topk"]
    sh1, sc1, gt1, sh2, sc2, gt2 = mod
    past = 0 if past_ckv is None else past_ckv.shape[1]

    h = _normmod(x, p["g_norm1"], sc1, sh1).reshape(t, d)
    gates = _mm(h, w["w_gates"], BF16, act="sigmoid")
    u = _mm(h, w["w_u"], BF16)
    cos, sin = _rope_tables(past + jnp.arange(l, dtype=jnp.int32), rope)
    qn, qr, ckv, ckv_b, kr, krp = _mla_proj(
        h, w["w_mla"], p["g_q"], p["g_kv"], w["w_q"], cos, sin, l,
        heads=heads, q_lora=dims["q_lora"], kv_lora=kv_lora, rope=rope, scale=dims["scale"])

    ys, s_fin = _s5(u.reshape(b, l, -1), *w["s5"], s0)

    if past_ckv is None:
        lk, keys_c, keys_r = l, ckv_b, krp
        tq = tk = _tile(l, TOKEN_TILE)
    else:
        n_keys = past + l
        lk = -(-n_keys // LANES) * LANES
        keys_c = jnp.concatenate([past_ckv.astype(BF16), ckv_b.reshape(b, l, kv_lora)], axis=1)
        keys_c = jnp.pad(keys_c, ((0, 0), (0, lk - n_keys), (0, 0))).reshape(b * lk, kv_lora)
        past_r = jnp.pad(past_kr.astype(BF16), ((0, 0), (0, 0), (0, LANES - rope)))
        keys_r = jnp.concatenate([past_r, krp.reshape(b, l, LANES)], axis=1)
        keys_r = jnp.pad(keys_r, ((0, 0), (0, lk - n_keys), (0, 0))).reshape(b * lk, LANES)
        tq, tk = l, lk
    kh, vh = _kvup(keys_c, w["w_kv"], heads)
    o = _flash(qn, qr, kh, keys_r, vh, batch=b, lq=l, lk=lk, q_pos0=past, n_keys=past + l,
               tq=tq, tk=tk, hb=heads)

    merged = _merge(ys.reshape(t, -1), o, w["w_glu"], w["w_o"], gates)
    x1, h2 = _resid(merged, w["w_out"], x, gt1, p["g_norm2"], sc2, sh2)

    h2 = h2.reshape(t, d)
    q = _mm_split(h2, w["peer_wq"], BF16)
    r2, c1, a1, a2 = _peer_topk(q, w["peer_k1"], w["peer_k2"], topk)
    x2 = _peer_mix(h2, w["peer_u"], w["peer_vt"], r2, c1, a1, a2, x1, gt2, g_final, last)
    return x2, ckv.reshape(b, l, kv_lora), kr.reshape(b, l, rope), s_fin


def kernel(x_prompt, x_sample, c_prompt, c_sample, cache_ckv, cache_krope, state_ssm_re, state_ssm_im, w_ada, b_ada, g_norm1, g_norm2, w_in, g_q, w_qu, g_kv, w_uk, w_uv, w_o, ssm_a_re, ssm_a_im, ssm_log_dt, ssm_b_re, ssm_b_im, ssm_c_re, ssm_c_im, ssm_d, w_glu, w_out, peer_wq, peer_k1, peer_k2, peer_u, peer_v, g_final):
    depth = w_in.shape[0]
    bp, lp, d = x_prompt.shape
    bs, ls, _ = x_sample.shape
    groups, states = ssm_a_re.shape[1:]
    heads, nope = w_uk.shape[2:]
    rope = cache_krope.shape[-1]
    dims = {
        "d": d, "sw": groups * ssm_b_re.shape[3], "q_lora": g_q.shape[1], "kv_lora": g_kv.shape[1],
        "rope": rope, "heads": heads, "nope": nope, "v_dim": w_uv.shape[3],
        "scale": math.log2(math.e) / math.sqrt(nope + rope), "topk": 16,
    }
    assert nope == LANES and dims["v_dim"] == LANES and rope <= LANES

    xp, xs = x_prompt, x_sample
    nb = bp + bs
    rows = -(-nb // BF16_ROWS) * BF16_ROWS
    c_all = jnp.pad(jnp.concatenate([c_prompt, c_sample], axis=0), ((0, rows - nb), (0, 0)))
    zeros = jnp.zeros((bp, 2, groups * states), F32)
    outs_p, outs_s = [], []
    for layer in range(depth):
        p = {
            "g_norm1": g_norm1[layer], "g_norm2": g_norm2[layer], "w_in": w_in[layer], "g_q": g_q[layer],
            "w_qu": w_qu[layer], "g_kv": g_kv[layer], "w_uk": w_uk[layer], "w_uv": w_uv[layer], "w_o": w_o[layer],
            "ssm_a_re": ssm_a_re[layer], "ssm_a_im": ssm_a_im[layer], "ssm_log_dt": ssm_log_dt[layer],
            "ssm_b_re": ssm_b_re[layer], "ssm_b_im": ssm_b_im[layer], "ssm_c_re": ssm_c_re[layer],
            "ssm_c_im": ssm_c_im[layer], "ssm_d": ssm_d[layer], "w_glu": w_glu[layer], "w_out": w_out[layer],
            "peer_wq": peer_wq[layer], "peer_k1": peer_k1[layer], "peer_k2": peer_k2[layer],
            "peer_u": peer_u[layer], "peer_v": peer_v[layer],
        }
        w = _prep_weights(p, dims)
        mod = _ada(c_all, w_ada[layer], b_ada[layer])
        mod_p = [m.reshape(bp, 1, d) for m in jnp.split(mod[:bp], 6, axis=-1)]
        mod_s = [m.reshape(bs, 1, d) for m in jnp.split(mod[bp:nb], 6, axis=-1)]
        s0_s = jnp.stack([state_ssm_re[layer].reshape(bs, -1), state_ssm_im[layer].reshape(bs, -1)], axis=1)
        last = layer == depth - 1
        xp, *res_p = _layer(xp, mod_p, None, None, zeros, p, w, dims, g_final, last)
        xs, *res_s = _layer(xs, mod_s, cache_ckv[layer], cache_krope[layer], s0_s, p, w, dims, g_final, last)
        for res, outs, g in ((res_p, outs_p, bp), (res_s, outs_s, bs)):
            ckv, kr, s_fin = res
            outs.append((ckv, kr, s_fin[:, 0].reshape(g, groups, states), s_fin[:, 1].reshape(g, groups, states)))
    stack = lambda outs, k: jnp.stack([o[k] for o in outs])
    return (xp, xs,
            stack(outs_p, 0), stack(outs_p, 1), stack(outs_p, 2), stack(outs_p, 3),
            stack(outs_s, 0), stack(outs_s, 1), stack(outs_s, 2), stack(outs_s, 3))
```

```python
import functools
import math

import jax
import jax.numpy as jnp
from jax import lax
from jax.experimental import pallas as pl
from jax.experimental.pallas import tpu as pltpu

F32 = jnp.float32
BF16 = jnp.bfloat16

EPS = 1e-6
CHUNK = 64
ROPE_THETA = 10000.0
LANES = 128
SUBLANES = 8
MXU_DIM = 256
BF16_ROWS = 16
TOKEN_TILE = 512
WIDE_TILE = 1024
RESIDENT_WEIGHT_TOKEN_TILE = 256
VMEM_LIMIT_BYTES = 56 * 1024 * 1024
MASKED = -1e30
REMOVED = -3e38
EXCLUDED = -1e38


def _params(*semantics):
    return pltpu.CompilerParams(dimension_semantics=semantics, vmem_limit_bytes=VMEM_LIMIT_BYTES)


def _tile(n, pref):
    if n <= pref:
        return n
    t = pref
    while n % t:
        t //= 2
    assert t >= SUBLANES, (n, pref)
    return t


def _sigmoid(x):
    return 1.0 / (1.0 + jnp.exp(-x))


def _gelu(x):
    return 0.5 * x * (1.0 + jnp.tanh(math.sqrt(2.0 / math.pi) * (x + 0.044715 * (x * x * x))))


def _rms(x, g):
    return x * lax.rsqrt(jnp.mean(x * x, axis=-1, keepdims=True) + EPS) * g


def _ada_kernel(c_ref, w_ref, b_ref, o_ref):
    c = c_ref[...]
    a = (c * _sigmoid(c)).astype(BF16)
    o_ref[...] = jnp.dot(a, w_ref[...].astype(BF16), preferred_element_type=F32) + b_ref[...]


def _ada(c, w, b):
    rows, d = c.shape
    n = w.shape[1]
    tn = _tile(n, WIDE_TILE)
    return pl.pallas_call(
        _ada_kernel,
        grid=(n // tn,),
        in_specs=[
            pl.BlockSpec((rows, d), lambda j: (0, 0)),
            pl.BlockSpec((d, tn), lambda j: (0, j)),
            pl.BlockSpec((1, tn), lambda j: (0, j)),
        ],
        out_specs=pl.BlockSpec((rows, tn), lambda j: (0, j)),
        out_shape=jax.ShapeDtypeStruct((rows, n), F32),
        compiler_params=_params("parallel"),
        name="ada",
    )(c, w, b.reshape(1, n))


def _normmod_kernel(x_ref, g_ref, sc_ref, sh_ref, o_ref):
    h = _rms(x_ref[0], g_ref[...]) * (1.0 + sc_ref[0]) + sh_ref[0]
    o_ref[0] = h.astype(BF16)


def _normmod(x, g, sc, sh):
    b, l, d = x.shape
    tm = _tile(l, TOKEN_TILE)
    return pl.pallas_call(
        _normmod_kernel,
        grid=(b, l // tm),
        in_specs=[
            pl.BlockSpec((1, tm, d), lambda i, j: (i, j, 0)),
            pl.BlockSpec((1, d), lambda i, j: (0, 0)),
            pl.BlockSpec((1, 1, d), lambda i, j: (i, 0, 0)),
            pl.BlockSpec((1, 1, d), lambda i, j: (i, 0, 0)),
        ],
        out_specs=pl.BlockSpec((1, tm, d), lambda i, j: (i, j, 0)),
        out_shape=jax.ShapeDtypeStruct((b, l, d), BF16),
        compiler_params=_params("parallel", "parallel"),
        name="normmod",
    )(x, g.reshape(1, d), sc, sh)


def _mm_kernel(a_ref, w_ref, o_ref, *, act):
    z = jnp.dot(a_ref[...], w_ref[...], preferred_element_type=F32)
    if act == "sigmoid":
        z = _sigmoid(z)
    o_ref[...] = z.astype(o_ref.dtype)


def _mm(a, w, out_dtype, act=None, tn_pref=WIDE_TILE):
    t, k = a.shape
    n = w.shape[1]
    tm = _tile(t, WIDE_TILE)
    tn = _tile(n, tn_pref)
    return pl.pallas_call(
        functools.partial(_mm_kernel, act=act),
        grid=(t // tm, n // tn),
        in_specs=[
            pl.BlockSpec((tm, k), lambda i, j: (i, 0)),
            pl.BlockSpec((k, tn), lambda i, j: (0, j)),
        ],
        out_specs=pl.BlockSpec((tm, tn), lambda i, j: (i, j)),
        out_shape=jax.ShapeDtypeStruct((t, n), out_dtype),
        compiler_params=_params("parallel", "parallel"),
        name="mm",
    )(a, w)


def _mm_split_kernel(a_ref, w_ref, o_ref, *, parts):
    z = jnp.dot(a_ref[...], w_ref[...], preferred_element_type=F32)
    for p in range(parts):
        o_ref[p] = z[:, p * LANES:(p + 1) * LANES].astype(o_ref.dtype)


def _mm_split(a, w, out_dtype):
    t, k = a.shape
    n = w.shape[1]
    tm = _tile(t, TOKEN_TILE)
    tn = _tile(n, WIDE_TILE)
    parts = tn // LANES
    return pl.pallas_call(
        functools.partial(_mm_split_kernel, parts=parts),
        grid=(t // tm, n // tn),
        in_specs=[
            pl.BlockSpec((tm, k), lambda i, j: (i, 0)),
            pl.BlockSpec((k, tn), lambda i, j: (0, j)),
        ],
        out_specs=pl.BlockSpec((parts, tm, LANES), lambda i, j: (j, i, 0)),
        out_shape=jax.ShapeDtypeStruct((n // LANES, t, LANES), out_dtype),
        compiler_params=_params("parallel", "parallel"),
        name="mm_split",
    )(a, w)


def _mla_proj_kernel(h_ref, wm_ref, gq_ref, gkv_ref, wq_ref, cos_ref, sin_ref,
                     qn_ref, qr_ref, ckv_ref, ckvb_ref, kr_ref, krp_ref,
                     *, heads, q_lora, kv_lora, rope, scale):
    z = jnp.dot(h_ref[...], wm_ref[...], preferred_element_type=F32)
    cos = cos_ref[...]
    sin = sin_ref[...]
    off = q_lora + kv_lora
    krp = z[:, off:off + LANES] * cos + z[:, off + LANES:off + 2 * LANES] * sin
    kr_ref[...] = krp[:, :rope]
    krp_ref[...] = krp.astype(BF16)
    ckv = _rms(z[:, q_lora:off], gkv_ref[...])
    ckv_ref[...] = ckv
    ckvb_ref[...] = ckv.astype(BF16)
    qd = _rms(z[:, :q_lora], gq_ref[...]).astype(BF16)
    zq = jnp.dot(qd, wq_ref[...], preferred_element_type=F32)
    hn = heads * LANES
    for hh in range(heads):
        lo = hh * LANES
        qn_ref[hh] = (zq[:, lo:lo + LANES] * scale).astype(BF16)
        qr = zq[:, hn + lo:hn + lo + LANES] * cos + zq[:, 2 * hn + lo:2 * hn + lo + LANES] * sin
        qr_ref[hh] = (qr * scale).astype(BF16)


def _mla_proj(h, wm, gq, gkv, wq, cos, sin, seq, *, heads, q_lora, kv_lora, rope, scale):
    t, d = h.shape
    tm = _tile(t, RESIDENT_WEIGHT_TOKEN_TILE)
    if seq % tm == 0:
        nrep = seq // tm
        tab_map = lambda i: (i % nrep, 0)
    else:
        assert tm % seq == 0
        cos = jnp.tile(cos, (tm // seq, 1))
        sin = jnp.tile(sin, (tm // seq, 1))
        tab_map = lambda i: (0, 0)
    nm = wm.shape[1]
    nq = wq.shape[1]
    row = lambda i: (i, 0)
    const = lambda i: (0, 0)
    return pl.pallas_call(
        functools.partial(_mla_proj_kernel, heads=heads, q_lora=q_lora, kv_lora=kv_lora, rope=rope, scale=scale),
        grid=(t // tm,),
        in_specs=[
            pl.BlockSpec((tm, d), row),
            pl.BlockSpec((d, nm), const),
            pl.BlockSpec((1, q_lora), const),
            pl.BlockSpec((1, kv_lora), const),
            pl.BlockSpec((q_lora, nq), const),
            pl.BlockSpec((tm, LANES), tab_map),
            pl.BlockSpec((tm, LANES), tab_map),
        ],
        out_specs=[
            pl.BlockSpec((heads, tm, LANES), lambda i: (0, i, 0)),
            pl.BlockSpec((heads, tm, LANES), lambda i: (0, i, 0)),
            pl.BlockSpec((tm, kv_lora), row),
            pl.BlockSpec((tm, kv_lora), row),
            pl.BlockSpec((tm, rope), row),
            pl.BlockSpec((tm, LANES), row),
        ],
        out_shape=[
            jax.ShapeDtypeStruct((heads, t, LANES), BF16),
            jax.ShapeDtypeStruct((heads, t, LANES), BF16),
            jax.ShapeDtypeStruct((t, kv_lora), F32),
            jax.ShapeDtypeStruct((t, kv_lora), BF16),
            jax.ShapeDtypeStruct((t, rope), F32),
            jax.ShapeDtypeStruct((t, LANES), BF16),
        ],
        compiler_params=_params("parallel"),
        name="mla_proj",
    )(h, wm, gq.reshape(1, q_lora), gkv.reshape(1, kv_lora), wq, cos, sin)


def _kvup_kernel(c_ref, w_ref, k_ref, v_ref, *, heads):
    z = jnp.dot(c_ref[...], w_ref[...], preferred_element_type=F32)
    for hh in range(heads):
        k_ref[hh] = z[:, hh * LANES:(hh + 1) * LANES].astype(BF16)
        v_ref[hh] = z[:, (heads + hh) * LANES:(heads + hh + 1) * LANES].astype(BF16)


def _kvup(ckv, w, heads):
    t, c = ckv.shape
    tm = _tile(t, TOKEN_TILE)
    if t % tm:
        tm = t
    return pl.pallas_call(
        functools.partial(_kvup_kernel, heads=heads),
        grid=(t // tm,),
        in_specs=[
            pl.BlockSpec((tm, c), lambda i: (i, 0)),
            pl.BlockSpec(w.shape, lambda i: (0, 0)),
        ],
        out_specs=[
            pl.BlockSpec((heads, tm, LANES), lambda i: (0, i, 0)),
            pl.BlockSpec((heads, tm, LANES), lambda i: (0, i, 0)),
        ],
        out_shape=[jax.ShapeDtypeStruct((heads, t, LANES), BF16)] * 2,
        compiler_params=_params("parallel"),
        name="kvup",
    )(ckv, w)


def _flash_kernel(iq_tab, ik_tab, flag_tab, qn_ref, qr_ref, k_ref, kr_ref, v_ref, o_ref,
                  qc_scr, m_scr, acc_scr, *, hb, tq, tk, q_pos0, n_keys):
    pair = pl.program_id(2)
    iq = iq_tab[pair]
    ik = ik_tab[pair]
    flags = flag_tab[pair]
    first = (flags & 1) != 0
    last = (flags & 2) != 0
    full = (flags & 4) != 0

    @pl.when(first)
    def _init():
        m_scr[...] = jnp.full(m_scr.shape, MASKED, F32)
        acc_scr[...] = jnp.zeros(acc_scr.shape, F32)
        for hh in range(hb):
            qc_scr[hh] = jnp.concatenate([qn_ref[hh], qr_ref[hh]], axis=1)

    def step(masked):
        kr = kr_ref[...]
        ones = jnp.ones((tk, LANES), BF16)
        if masked:
            qp = q_pos0 + iq * tq + lax.broadcasted_iota(jnp.int32, (tq, tk), 0)
            kp = ik * tk + lax.broadcasted_iota(jnp.int32, (tq, tk), 1)
            allowed = jnp.logical_and(kp // CHUNK <= qp // CHUNK, kp < n_keys)

        def head(hh, carry):
            kc = jnp.concatenate([k_ref[hh], kr], axis=1)
            s = lax.dot_general(qc_scr[hh], kc, (((1,), (1,)), ((), ())), preferred_element_type=F32)
            if masked:
                s = jnp.where(allowed, s, MASKED)
            m_prev = m_scr[hh]
            m_new = jnp.maximum(m_prev, jnp.max(s, axis=1, keepdims=True))
            alpha = jnp.exp2(m_prev - m_new)
            p = jnp.exp2(s - pltpu.repeat(m_new, tk // LANES, axis=1)).astype(BF16)
            vc = jnp.concatenate([v_ref[hh], ones], axis=1)
            acc_scr[hh] = pltpu.repeat(alpha, 2, axis=1) * acc_scr[hh] + jnp.dot(p, vc, preferred_element_type=F32)
            m_scr[hh] = m_new
            return carry

        lax.fori_loop(0, hb, head, 0, unroll=True)

    @pl.when(full)
    def _full():
        step(False)

    @pl.when(jnp.logical_not(full))
    def _diag():
        step(True)

    @pl.when(last)
    def _done():
        for hh in range(hb):
            acc = acc_scr[hh]
            o_ref[hh] = (acc[:, :LANES] / acc[:, LANES:]).astype(BF16)


def _flash_pairs(lq, lk, tq, tk, q_pos0, n_keys):
    iqs, iks, flags = [], [], []
    for iq in range(lq // tq):
        q_first = q_pos0 + iq * tq
        last_key = min(n_keys - 1, ((q_first + tq - 1) // CHUNK) * CHUNK + CHUNK - 1)
        full_key = min(n_keys - 1, (q_first // CHUNK) * CHUNK + CHUNK - 1)
        n_blocks = last_key // tk + 1
        for ik in range(n_blocks):
            full = (ik + 1) * tk - 1 <= full_key
            iqs.append(iq)
            iks.append(ik)
            flags.append((1 if ik == 0 else 0) | (2 if ik == n_blocks - 1 else 0) | (4 if full else 0))
    as_i32 = lambda xs: jnp.asarray(xs, jnp.int32)
    return as_i32(iqs), as_i32(iks), as_i32(flags)


def _flash(qn, qr, k, krp, v, *, batch, lq, lk, q_pos0, n_keys, tq, tk, hb):
    heads = qn.shape[0]
    nq = lq // tq
    nk = lk // tk
    assert lq % tq == 0 and lk % tk == 0 and heads % hb == 0 and tk % LANES == 0
    iq_tab, ik_tab, flag_tab = _flash_pairs(lq, lk, tq, tk, q_pos0, n_keys)
    qmap = lambda b, h, p, iqt, ikt, ft: (h, b * nq + iqt[p], 0)
    kmap = lambda b, h, p, iqt, ikt, ft: (h, b * nk + ikt[p], 0)
    return pl.pallas_call(
        functools.partial(_flash_kernel, hb=hb, tq=tq, tk=tk, q_pos0=q_pos0, n_keys=n_keys),
        grid_spec=pltpu.PrefetchScalarGridSpec(
            num_scalar_prefetch=3,
            grid=(batch, heads // hb, iq_tab.shape[0]),
            in_specs=[
                pl.BlockSpec((hb, tq, LANES), qmap),
                pl.BlockSpec((hb, tq, LANES), qmap),
                pl.BlockSpec((hb, tk, LANES), kmap),
                pl.BlockSpec((tk, LANES), lambda b, h, p, iqt, ikt, ft: (b * nk + ikt[p], 0)),
                pl.BlockSpec((hb, tk, LANES), kmap),
            ],
            out_specs=pl.BlockSpec((hb, tq, LANES), qmap),
            scratch_shapes=[
                pltpu.VMEM((hb, tq, 2 * LANES), BF16),
                pltpu.VMEM((hb, tq, LANES), F32),
                pltpu.VMEM((hb, tq, 2 * LANES), F32),
            ],
        ),
        out_shape=jax.ShapeDtypeStruct((heads, batch * lq, LANES), BF16),
        compiler_params=_params("parallel", "parallel", "arbitrary"),
        name="flash",
    )(iq_tab, ik_tab, flag_tab, qn, qr, k, krp, v)


def _s5_kernel(u_ref, pm_ref, pt_ref, wb_ref, scn_ref, wc_ref, d_ref, s0_ref, y_ref, st_ref,
               xs_ref, up_ref, yp_ref, car_ref, *, ts, nkt, kw, sw, nt):
    it = pl.program_id(1)
    steps = ts // SUBLANES

    @pl.when(it == 0)
    def _load_state():
        car_ref[...] = s0_ref[0]

    up_ref[...] = jnp.dot(pm_ref[...], u_ref[0], preferred_element_type=F32).astype(BF16)
    first_row = lax.broadcasted_iota(jnp.int32, (SUBLANES, sw), 0) == 0
    for kt in range(nkt):
        ukt = up_ref[:, kt * kw:(kt + 1) * kw]
        xs_ref[...] = jnp.dot(ukt, wb_ref[kt], preferred_element_type=F32)
        st_sl = slice(kt * sw, (kt + 1) * sw)
        lr = scn_ref[kt, 0]
        li = scn_ref[kt, 1]

        def advance(k, carry, store, lr=lr, li=li):
            hr, hi = carry
            rows = pl.ds(pl.multiple_of(k * SUBLANES, SUBLANES), SUBLANES)
            nr = lr * hr - li * hi + xs_ref[rows, :sw]
            ni = lr * hi + li * hr + xs_ref[rows, sw:]
            if store:
                xs_ref[rows, :sw] = nr
                xs_ref[rows, sw:] = ni
            return nr, ni

        zero = jnp.zeros((SUBLANES, sw), F32)
        er, ei = lax.fori_loop(0, steps, functools.partial(advance, store=False), (zero, zero))
        br = jnp.where(first_row, jnp.broadcast_to(car_ref[0:1, st_sl], (SUBLANES, sw)), pltpu.roll(er, 1, 0))
        bi = jnp.where(first_row, jnp.broadcast_to(car_ref[1:2, st_sl], (SUBLANES, sw)), pltpu.roll(ei, 1, 0))
        for c0, dist in ((2, 1), (4, 2), (6, 4)):
            ar = scn_ref[kt, c0]
            ai = scn_ref[kt, c0 + 1]
            sr = pltpu.roll(br, dist, 0)
            si = pltpu.roll(bi, dist, 0)
            br, bi = br + ar * sr - ai * si, bi + ar * si + ai * sr
        hr, hi = lax.fori_loop(0, steps, functools.partial(advance, store=True), (br, bi))
        car_ref[0:1, st_sl] = hr[SUBLANES - 1:SUBLANES]
        car_ref[1:2, st_sl] = hi[SUBLANES - 1:SUBLANES]
        y = jnp.dot(xs_ref[...].astype(BF16), wc_ref[kt], preferred_element_type=F32)
        y = y + d_ref[:, kt * kw:(kt + 1) * kw] * ukt.astype(F32)
        yp_ref[:, kt * kw:(kt + 1) * kw] = y.astype(BF16)
    y_ref[0] = _gelu(jnp.dot(pt_ref[...], yp_ref[...], preferred_element_type=F32)).astype(BF16)

    @pl.when(it == nt - 1)
    def _store_state():
        st_ref[0] = car_ref[...]


def _s5(u, wb, lam, wc, dsk, s0):
    b, l, w = u.shape
    nkt, kw, sw2 = wb.shape
    sw = sw2 // 2
    ns = s0.shape[2]
    ts = _tile(l, TOKEN_TILE)
    nt = l // ts
    steps = ts // SUBLANES

    def cmul(x, y):
        return x[0] * y[0] - x[1] * y[1], x[0] * y[1] + x[1] * y[0]

    mu = (lam[0], lam[1])
    for _ in range(steps - 1):
        mu = cmul(mu, (lam[0], lam[1]))
    mu2 = cmul(mu, mu)
    mu4 = cmul(mu2, mu2)
    rows = jnp.arange(SUBLANES)[None, :, None]
    shifted = lambda c, dist: [jnp.where(rows >= dist, x[:, None, :], 0.0) for x in c]
    rep = [jnp.broadcast_to(x[:, None, :], (nkt, SUBLANES, sw)) for x in (lam[0], lam[1])]
    scn = jnp.stack(rep + shifted(mu, 1) + shifted(mu2, 2) + shifted(mu4, 4), axis=1).astype(F32)
    r = jnp.arange(ts)
    pm = jax.nn.one_hot((r % SUBLANES) * steps + r // SUBLANES, ts, dtype=BF16)
    const2 = lambda i, j: (0, 0)
    const3 = lambda i, j: (0, 0, 0)
    return pl.pallas_call(
        functools.partial(_s5_kernel, ts=ts, nkt=nkt, kw=kw, sw=sw, nt=nt),
        grid=(b, nt),
        in_specs=[
            pl.BlockSpec((1, ts, w), lambda i, j: (i, j, 0)),
            pl.BlockSpec((ts, ts), const2),
            pl.BlockSpec((ts, ts), const2),
            pl.BlockSpec(wb.shape, const3),
            pl.BlockSpec(scn.shape, lambda i, j: (0, 0, 0, 0)),
            pl.BlockSpec(wc.shape, const3),
            pl.BlockSpec((1, w), const2),
            pl.BlockSpec((1, 2, ns), lambda i, j: (i, 0, 0)),
        ],
        out_specs=[
            pl.BlockSpec((1, ts, w), lambda i, j: (i, j, 0)),
            pl.BlockSpec((1, 2, ns), lambda i, j: (i, 0, 0)),
        ],
        out_shape=[
            jax.ShapeDtypeStruct((b, l, w), BF16),
            jax.ShapeDtypeStruct((b, 2, ns), F32),
        ],
        scratch_shapes=[
            pltpu.VMEM((ts, sw2), F32),
            pltpu.VMEM((ts, w), BF16),
            pltpu.VMEM((ts, w), BF16),
            pltpu.VMEM((2, ns), F32),
        ],
        compiler_params=_params("parallel", "arbitrary"),
        name="s5",
    )(u, pm, pm.T, wb, scn, wc, dsk, s0)


def _s5_tables(a_re, a_im, log_dt, b_re, b_im, c_re, c_im, d_skip):
    g, p = a_re.shape
    ch = b_re.shape[2]
    gpt = MXU_DIM // ch
    nkt = g // gpt
    dt = jnp.exp(log_dt.astype(F32))[:, None]
    mag = jnp.exp(dt * a_re)
    ab_re = mag * jnp.cos(dt * a_im)
    ab_im = mag * jnp.sin(dt * a_im)
    den = a_re * a_re + a_im * a_im
    nr = ab_re - 1.0
    f_re = (nr * a_re + ab_im * a_im) / den
    f_im = (ab_im * a_re - nr * a_im) / den
    bb_re = f_re[..., None] * b_re - f_im[..., None] * b_im
    bb_im = f_re[..., None] * b_im + f_im[..., None] * b_re
    eye = jnp.eye(gpt, dtype=F32)

    def in_blockdiag(bb):
        return jnp.einsum("kgpi,gh->kgihp", bb.reshape(nkt, gpt, p, ch), eye).reshape(nkt, gpt * ch, gpt * p)

    def out_blockdiag(cc):
        return jnp.einsum("kgjp,gh->kgphj", cc.reshape(nkt, gpt, ch, p), eye).reshape(nkt, gpt * p, gpt * ch)

    wb = jnp.concatenate([in_blockdiag(bb_re), in_blockdiag(bb_im)], axis=2).astype(BF16)
    wc = jnp.concatenate([out_blockdiag(c_re), out_blockdiag(-c_im)], axis=1).astype(BF16)
    lam = jnp.stack([ab_re.reshape(nkt, gpt * p), ab_im.reshape(nkt, gpt * p)]).astype(F32)
    return wb, lam, wc, d_skip.reshape(1, g * ch).astype(F32)


def _merge_kernel(ys_ref, o_ref, wga_ref, wgb_ref, wo_ref, ga_ref, gb_ref, out_ref, *, heads):
    ys = ys_ref[...]
    ya = jnp.dot(ys, wga_ref[...], preferred_element_type=F32)
    ya = ya * _sigmoid(jnp.dot(ys, wgb_ref[...], preferred_element_type=F32))
    oc = jnp.concatenate([o_ref[hh] for hh in range(heads)], axis=1)
    yb = jnp.dot(oc, wo_ref[...], preferred_element_type=F32)
    out_ref[...] = (ga_ref[...].astype(F32) * ya + gb_ref[...].astype(F32) * yb).astype(BF16)


def _merge(ys, o, w_glu, w_o, gates):
    t, sw = ys.shape
    heads = o.shape[0]
    d = w_o.shape[1]
    tm = _tile(t, TOKEN_TILE)
    tn = _tile(d, WIDE_TILE)
    nj = d // tn
    return pl.pallas_call(
        functools.partial(_merge_kernel, heads=heads),
        grid=(t // tm, nj),
        in_specs=[
            pl.BlockSpec((tm, sw), lambda i, j: (i, 0)),
            pl.BlockSpec((heads, tm, LANES), lambda i, j: (0, i, 0)),
            pl.BlockSpec((sw, tn), lambda i, j: (0, j)),
            pl.BlockSpec((sw, tn), lambda i, j: (0, nj + j)),
            pl.BlockSpec((heads * LANES, tn), lambda i, j: (0, j)),
            pl.BlockSpec((tm, tn), lambda i, j: (i, j)),
            pl.BlockSpec((tm, tn), lambda i, j: (i, nj + j)),
        ],
        out_specs=pl.BlockSpec((tm, tn), lambda i, j: (i, j)),
        out_shape=jax.ShapeDtypeStruct((t, d), BF16),
        compiler_params=_params("parallel", "parallel"),
        name="merge",
    )(ys, o, w_glu, w_glu, w_o, gates, gates)


def _resid_kernel(m_ref, w_ref, x_ref, gt_ref, g_ref, sc_ref, sh_ref, x1_ref, h2_ref):
    x1 = x_ref[0] + gt_ref[0] * jnp.dot(m_ref[0], w_ref[...], preferred_element_type=F32)
    x1_ref[0] = x1
    h2_ref[0] = (_rms(x1, g_ref[...]) * (1.0 + sc_ref[0]) + sh_ref[0]).astype(BF16)


def _resid(merged, w_out, x, gt, g2, sc, sh):
    b, l, d = x.shape
    tm = _tile(l, TOKEN_TILE)
    tok = lambda i, j: (i, j, 0)
    per_b = lambda i, j: (i, 0, 0)
    return pl.pallas_call(
        _resid_kernel,
        grid=(b, l // tm),
        in_specs=[
            pl.BlockSpec((1, tm, d), tok),
            pl.BlockSpec((d, d), lambda i, j: (0, 0)),
            pl.BlockSpec((1, tm, d), tok),
            pl.BlockSpec((1, 1, d), per_b),
            pl.BlockSpec((1, d), lambda i, j: (0, 0)),
            pl.BlockSpec((1, 1, d), per_b),
            pl.BlockSpec((1, 1, d), per_b),
        ],
        out_specs=[pl.BlockSpec((1, tm, d), tok), pl.BlockSpec((1, tm, d), tok)],
        out_shape=[jax.ShapeDtypeStruct((b, l, d), F32), jax.ShapeDtypeStruct((b, l, d), BF16)],
        compiler_params=_params("parallel", "parallel"),
        name="resid",
    )(merged.reshape(b, l, d), w_out, x, gt, g2.reshape(1, d), sc, sh)


def _peer_topk_kernel(q_ref, k1_ref, k2_ref, flat_ref, invalid_ref, r2_ref, c1_ref, a1_ref, a2_ref,
                      *, nh, nkeys, topk):
    tb = q_ref.shape[1]
    iota_k = lax.broadcasted_iota(jnp.int32, (nkeys, tb), 0).astype(F32)
    iota_t = lax.broadcasted_iota(jnp.int32, (topk, tb), 0).astype(F32)
    flat = flat_ref[...]
    invalid = invalid_ref[...]
    nt_dims = (((1,), (1,)), ((), ()))

    n_cand = flat.shape[0]

    def extract(s, exact_ties):
        work = s
        rank = jnp.full((nkeys, tb), float(topk), F32)
        vals = []
        for a in range(topk):
            m = jnp.max(work, axis=0, keepdims=True)
            if exact_ties:
                idx = jnp.min(jnp.where(work == m, iota_k, float(nkeys)), axis=0, keepdims=True)
                sel = iota_k == idx
            else:
                sel = work == m
            rank = jnp.where(sel, float(a), rank)
            work = jnp.where(sel, REMOVED, work)
            vals.append(m)
        ranked = jnp.sum(jnp.where(rank < float(topk), 1.0, 0.0), axis=0, keepdims=True)
        return vals, rank, ranked

    def stack(vals):
        out = jnp.zeros((topk, tb), F32)
        for a in range(topk):
            out = jnp.where(iota_t == float(a), vals[a], out)
        return out

    def candidates(v1, v2):
        vs1 = stack(v1)
        vs2 = stack(v2)
        blocks = [v1[0] + vs2]
        for a in range(1, SUBLANES):
            blocks.append(v1[a] + vs2[:SUBLANES])
        blocks.append(vs1[SUBLANES:] + v2[0])
        return jnp.concatenate(blocks, axis=0) + invalid

    def route(s1, s2, exact_ties):
        v1, rank1, n1 = extract(s1, exact_ties)
        v2, rank2, n2 = extract(s2, exact_ties)
        cand = candidates(v1, v2)
        counts = jnp.zeros((topk, tb), F32)
        top = None
        zsum = None
        for kk in range(topk):
            m = jnp.max(cand, axis=0, keepdims=True)
            if exact_ties:
                f = jnp.min(jnp.where(cand == m, flat, 1e9), axis=0, keepdims=True)
                cand = jnp.where(flat == f, REMOVED, cand)
                counts = counts + jnp.where(iota_t == jnp.floor(f * (1.0 / topk)), 1.0, 0.0)
            else:
                cand = jnp.where(cand == m, REMOVED, cand)
            if kk == 0:
                top = m
                zsum = jnp.ones_like(m)
            else:
                zsum = zsum + jnp.exp(m - top)
        n3 = None
        if not exact_ties:
            gone = jnp.where(cand == REMOVED, 1.0, 0.0)
            n3 = jnp.sum(gone, axis=0, keepdims=True)
            per_rank = [jnp.sum(gone[:topk], axis=0, keepdims=True)]
            for a in range(1, SUBLANES):
                lo = topk + (a - 1) * SUBLANES
                per_rank.append(jnp.sum(gone[lo:lo + SUBLANES], axis=0, keepdims=True))
            counts = jnp.concatenate([stack(per_rank + [per_rank[0]] * (topk - SUBLANES))[:SUBLANES],
                                      gone[n_cand - SUBLANES:]], axis=0)
        c1 = jnp.zeros((nkeys, tb), F32)
        for a in range(topk):
            c1 = jnp.where(rank1 == float(a), counts[a:a + 1], c1)
        a1 = jnp.exp(s1 - v1[0]) * (1.0 / zsum)
        a2 = jnp.exp(s2 - v2[0])
        clean = None
        if not exact_ties:
            want = float(topk)
            bad = jnp.where(n1 != want, 1.0, 0.0) + jnp.where(n2 != want, 1.0, 0.0) + jnp.where(n3 != want, 1.0, 0.0)
            clean = jnp.max(bad) == 0.0
        return (rank2, c1, a1, a2), clean

    def store(hh, tables):
        r2_ref[hh], c1_ref[hh], a1_ref[hh], a2_ref[hh] = tables

    per_trip = 4 if nh % 4 == 0 else 2

    def group(gg, carry):
        heads = [per_trip * gg + r for r in range(per_trip)]
        scores = []
        for hh in heads:
            scores.append((lax.dot_general(k1_ref[hh], q_ref[2 * hh], nt_dims, preferred_element_type=F32),
                           lax.dot_general(k2_ref[hh], q_ref[2 * hh + 1], nt_dims, preferred_element_type=F32)))
        quick = [route(s1, s2, False) for s1, s2 in scores]
        clean = quick[0][1]
        for _, ok in quick[1:]:
            clean = jnp.logical_and(clean, ok)

        @pl.when(clean)
        def _no_ties():
            for hh, (tables, _) in zip(heads, quick):
                store(hh, tables)

        @pl.when(jnp.logical_not(clean))
        def _ties():
            for hh, (s1, s2) in zip(heads, scores):
                store(hh, route(s1, s2, True)[0])

        return carry

    lax.fori_loop(0, nh // per_trip, group, 0)


def _peer_topk(q, k1, k2, topk):
    nh, nkeys, half = k1.shape
    t = q.shape[1]
    tb = LANES
    assert topk == 2 * SUBLANES and half == LANES and t % tb == 0
    rows = jnp.arange(topk + (SUBLANES - 1) * SUBLANES + SUBLANES)
    a_idx = jnp.where(rows < topk, 0, jnp.where(rows < topk + (SUBLANES - 1) * SUBLANES,
                                                1 + (rows - topk) // SUBLANES, SUBLANES + (rows - topk - (SUBLANES - 1) * SUBLANES)))
    b_idx = jnp.where(rows < topk, rows, jnp.where(rows < topk + (SUBLANES - 1) * SUBLANES, (rows - topk) % SUBLANES, 0))
    flat = jnp.broadcast_to((a_idx * topk + b_idx).astype(F32)[:, None], (rows.shape[0], tb))
    invalid = jnp.broadcast_to(jnp.where((a_idx + 1) * (b_idx + 1) <= topk, 0.0, EXCLUDED).astype(F32)[:, None],
                               (rows.shape[0], tb))
    out = lambda dt: jax.ShapeDtypeStruct((nh, nkeys, t), dt)
    ospec = pl.BlockSpec((nh, nkeys, tb), lambda i: (0, 0, i))
    return pl.pallas_call(
        functools.partial(_peer_topk_kernel, nh=nh, nkeys=nkeys, topk=topk),
        grid=(t // tb,),
        in_specs=[
            pl.BlockSpec((2 * nh, tb, LANES), lambda i: (0, i, 0)),
            pl.BlockSpec(k1.shape, lambda i: (0, 0, 0)),
            pl.BlockSpec(k2.shape, lambda i: (0, 0, 0)),
            pl.BlockSpec(flat.shape, lambda i: (0, 0)),
            pl.BlockSpec(invalid.shape, lambda i: (0, 0)),
        ],
        out_specs=[ospec] * 4,
        out_shape=[out(F32)] * 4,
        compiler_params=_params("parallel"),
        name="peer_topk",
    )(q, k1, k2, flat, invalid)


def _peer_mix_kernel(h_ref, u_ref, vt_ref, r2_ref, c1_ref, a1_ref, a2_ref, *rest, nh, nkeys, ni, ne, norm):
    if len(rest) == 5:
        o_ref, ht_scr, w_scr, acc_scr, xt_scr = rest
        x_ref = gt_ref = g_ref = None
    else:
        x_ref, gt_ref, g_ref, o_ref, ht_scr, w_scr, acc_scr, xt_scr = rest
    e = pl.program_id(1)
    tb = h_ref.shape[0]

    @pl.when(e == 0)
    def _init():
        acc_scr[...] = jnp.zeros(acc_scr.shape, F32)
        xt_scr[...] = h_ref[...].T

    ht_scr[...] = jnp.dot(u_ref[...], xt_scr[...], preferred_element_type=F32)
    for il in range(ni):
        rows = slice(il * nkeys, (il + 1) * nkeys)
        for lg in range(tb // LANES):
            sl = slice(lg * LANES, (lg + 1) * LANES)
            gate = jnp.zeros((nkeys, LANES), F32)
            for hh in range(nh):
                partners = c1_ref[hh, :, sl][il:il + 1]
                first = a1_ref[hh, :, sl][il:il + 1]
                gate = gate + jnp.where(r2_ref[hh, :, sl] < partners, a2_ref[hh, :, sl], 0.0) * first
            w_scr[rows, sl] = (gate * _gelu(ht_scr[rows, sl])).astype(BF16)
    acc_scr[...] += jnp.dot(vt_ref[0], w_scr[...], preferred_element_type=F32)

    @pl.when(e == ne - 1)
    def _done():
        mix = acc_scr[...].T
        if x_ref is None:
            o_ref[...] = mix
        else:
            y = x_ref[0] + gt_ref[0] * mix
            o_ref[0] = _rms(y, g_ref[...]) if norm else y


def _peer_mix(h2, u_tab, vt_tab, r2, c1, a1, a2, x1, gt, g, norm):
    b, l, _ = x1.shape
    t, d = h2.shape
    nh, nkeys, _ = r2.shape
    n_exp = u_tab.shape[0]
    tb = _tile(t, TOKEN_TILE)
    ni = SUBLANES
    eb = ni * nkeys
    ne = n_exp // eb
    second = pl.BlockSpec((nh, nkeys, tb), lambda i, e: (0, 0, i))
    first = pl.BlockSpec((nh, ni, tb), lambda i, e: (0, e, i))
    in_specs = [
        pl.BlockSpec((tb, d), lambda i, e: (i, 0), pipeline_mode=pl.Buffered(1)),
        pl.BlockSpec((eb, d), lambda i, e: (e, 0)),
        pl.BlockSpec((1, d, eb), lambda i, e: (e, 0, 0)),
        second, first, first, second,
    ]
    args = [h2, u_tab, vt_tab, r2, c1, a1, a2]
    fused = l % tb == 0
    if fused:
        per_seq = l // tb
        in_specs += [
            pl.BlockSpec((1, tb, d), lambda i, e: (i // per_seq, i % per_seq, 0), pipeline_mode=pl.Buffered(1)),
            pl.BlockSpec((1, 1, d), lambda i, e: (i // per_seq, 0, 0)),
            pl.BlockSpec((1, d), lambda i, e: (0, 0)),
        ]
        args += [x1, gt, g.reshape(1, d)]
        out_spec = pl.BlockSpec((1, tb, d), lambda i, e: (i // per_seq, i % per_seq, 0))
        out_shape = jax.ShapeDtypeStruct((b, l, d), F32)
    else:
        out_spec = pl.BlockSpec((tb, d), lambda i, e: (i, 0))
        out_shape = jax.ShapeDtypeStruct((t, d), F32)
    out = pl.pallas_call(
        functools.partial(_peer_mix_kernel, nh=nh, nkeys=nkeys, ni=ni, ne=ne, norm=norm),
        grid=(t // tb, ne),
        in_specs=in_specs,
        out_specs=out_spec,
        out_shape=out_shape,
        scratch_shapes=[
            pltpu.VMEM((eb, tb), F32),
            pltpu.VMEM((eb, tb), BF16),
            pltpu.VMEM((d, tb), F32),
            pltpu.VMEM((d, tb), BF16),
        ],
        compiler_params=_params("parallel", "arbitrary"),
        name="peer_mix",
    )(*args)
    return out if fused else _final(x1, out, gt, g, norm)


def _final_kernel(x_ref, p_ref, gt_ref, g_ref, y_ref, *, norm):
    y = x_ref[0] + gt_ref[0] * p_ref[0]
    y_ref[0] = _rms(y, g_ref[...]) if norm else y


def _final(x1, peer, gt, g, norm):
    b, l, d = x1.shape
    tm = _tile(l, TOKEN_TILE)
    tok = lambda i, j: (i, j, 0)
    return pl.pallas_call(
        functools.partial(_final_kernel, norm=norm),
        grid=(b, l // tm),
        in_specs=[
            pl.BlockSpec((1, tm, d), tok),
            pl.BlockSpec((1, tm, d), tok),
            pl.BlockSpec((1, 1, d), lambda i, j: (i, 0, 0)),
            pl.BlockSpec((1, d), lambda i, j: (0, 0)),
        ],
        out_specs=pl.BlockSpec((1, tm, d), tok),
        out_shape=jax.ShapeDtypeStruct((b, l, d), F32),
        compiler_params=_params("parallel", "parallel"),
        name="final",
    )(x1, peer.reshape(b, l, d), gt, g.reshape(1, d))


def _rope_tables(pos, rope):
    half = rope // 2
    inv = jnp.power(ROPE_THETA, -jnp.arange(half, dtype=F32) / half)
    ang = pos.astype(F32)[:, None] * inv
    reps = LANES // half
    return jnp.tile(jnp.cos(ang), (1, reps)), jnp.tile(jnp.sin(ang), (1, reps))


def _prep_weights(p, dims):
    d, sw, q_lora, kv_lora, rope, heads, nope = (dims[k] for k in ("d", "sw", "q_lora", "kv_lora", "rope", "heads", "nope"))
    half = rope // 2
    w_in = p["w_in"]
    off_q = sw
    off_kv = off_q + q_lora
    off_kr = off_kv + kv_lora
    off_g = off_kr + rope
    kr_w = w_in[:, off_kr:off_g]
    kr_rot = jnp.concatenate([-kr_w[:, half:], kr_w[:, :half]], axis=1)
    pad = jnp.zeros((d, LANES - rope), F32)
    w_mla = jnp.concatenate([w_in[:, off_q:off_kr], kr_w, pad, kr_rot, pad], axis=1).astype(BF16)
    wq = p["w_qu"].reshape(q_lora, heads, nope + rope)
    wq_n = wq[:, :, :nope].reshape(q_lora, heads * nope)
    wq_r = wq[:, :, nope:]
    wq_rot = jnp.concatenate([-wq_r[:, :, half:], wq_r[:, :, :half]], axis=2)
    hpad = jnp.zeros((q_lora, heads, LANES - rope), F32)
    wq_a = jnp.concatenate([wq_r, hpad], axis=2).reshape(q_lora, heads * LANES)
    wq_b = jnp.concatenate([wq_rot, hpad], axis=2).reshape(q_lora, heads * LANES)
    return {
        "w_gates": w_in[:, off_g:].astype(BF16),
        "w_u": w_in[:, :sw].astype(BF16),
        "w_mla": w_mla,
        "w_q": jnp.concatenate([wq_n, wq_a, wq_b], axis=1).astype(BF16),
        "w_kv": jnp.concatenate([p["w_uk"].reshape(kv_lora, heads * nope),
                                 p["w_uv"].reshape(kv_lora, heads * dims["v_dim"])], axis=1).astype(BF16),
        "w_o": p["w_o"].astype(BF16),
        "w_glu": p["w_glu"].astype(BF16),
        "w_out": p["w_out"].astype(BF16),
        "peer_wq": p["peer_wq"].astype(BF16),
        "peer_k1": p["peer_k1"].astype(BF16),
        "peer_k2": p["peer_k2"].astype(BF16),
        "peer_u": p["peer_u"].astype(BF16),
        "peer_vt": p["peer_v"].reshape(-1, SUBLANES * p["peer_k1"].shape[1], d).transpose(0, 2, 1).astype(BF16),
        "s5": _s5_tables(p["ssm_a_re"], p["ssm_a_im"], p["ssm_log_dt"], p["ssm_b_re"], p["ssm_b_im"],
                         p["ssm_c_re"], p["ssm_c_im"], p["ssm_d"]),
    }


def _layer(x, mod, past_ckv, past_kr, s0, p, w, dims, g_final, last):
    b, l, d = x.shape
    t = b * l
    heads, rope, kv_lora, topk = dims["heads"], dims["rope"], dims["kv_lora"], dims["topk"]
    sh1, sc1, gt1, sh2, sc2, gt2 = mod
    past = 0 if past_ckv is None else past_ckv.shape[1]

    h = _normmod(x, p["g_norm1"], sc1, sh1).reshape(t, d)
    gates = _mm(h, w["w_gates"], BF16, act="sigmoid")
    u = _mm(h, w["w_u"], BF16)
    cos, sin = _rope_tables(past + jnp.arange(l, dtype=jnp.int32), rope)
    qn, qr, ckv, ckv_b, kr, krp = _mla_proj(
        h, w["w_mla"], p["g_q"], p["g_kv"], w["w_q"], cos, sin, l,
        heads=heads, q_lora=dims["q_lora"], kv_lora=kv_lora, rope=rope, scale=dims["scale"])

    ys, s_fin = _s5(u.reshape(b, l, -1), *w["s5"], s0)

    if past_ckv is None:
        lk, keys_c, keys_r = l, ckv_b, krp
        tq = tk = _tile(l, TOKEN_TILE)
    else:
        n_keys = past + l
        lk = -(-n_keys // LANES) * LANES
        keys_c = jnp.concatenate([past_ckv.astype(BF16), ckv_b.reshape(b, l, kv_lora)], axis=1)
        keys_c = jnp.pad(keys_c, ((0, 0), (0, lk - n_keys), (0, 0))).reshape(b * lk, kv_lora)
        past_r = jnp.pad(past_kr.astype(BF16), ((0, 0), (0, 0), (0, LANES - rope)))
        keys_r = jnp.concatenate([past_r, krp.reshape(b, l, LANES)], axis=1)
        keys_r = jnp.pad(keys_r, ((0, 0), (0, lk - n_keys), (0, 0))).reshape(b * lk, LANES)
        tq, tk = l, lk
    kh, vh = _kvup(keys_c, w["w_kv"], heads)
    o = _flash(qn, qr, kh, keys_r, vh, batch=b, lq=l, lk=lk, q_pos0=past, n_keys=past + l,
               tq=tq, tk=tk, hb=heads)

    merged = _merge(ys.reshape(t, -1), o, w["w_glu"], w["w_o"], gates)
    x1, h2 = _resid(merged, w["w_out"], x, gt1, p["g_norm2"], sc2, sh2)

    h2 = h2.reshape(t, d)
    q = _mm_split(h2, w["peer_wq"], BF16)
    r2, c1, a1, a2 = _peer_topk(q, w["peer_k1"], w["peer_k2"], topk)
    x2 = _peer_mix(h2, w["peer_u"], w["peer_vt"], r2, c1, a1, a2, x1, gt2, g_final, last)
    return x2, ckv.reshape(b, l, kv_lora), kr.reshape(b, l, rope), s_fin


def kernel(x_prompt, x_sample, c_prompt, c_sample, cache_ckv, cache_krope, state_ssm_re, state_ssm_im, w_ada, b_ada, g_norm1, g_norm2, w_in, g_q, w_qu, g_kv, w_uk, w_uv, w_o, ssm_a_re, ssm_a_im, ssm_log_dt, ssm_b_re, ssm_b_im, ssm_c_re, ssm_c_im, ssm_d, w_glu, w_out, peer_wq, peer_k1, peer_k2, peer_u, peer_v, g_final):
    depth = w_in.shape[0]
    bp, lp, d = x_prompt.shape
    bs, ls, _ = x_sample.shape
    groups, states = ssm_a_re.shape[1:]
    heads, nope = w_uk.shape[2:]
    rope = cache_krope.shape[-1]
    dims = {
        "d": d, "sw": groups * ssm_b_re.shape[3], "q_lora": g_q.shape[1], "kv_lora": g_kv.shape[1],
        "rope": rope, "heads": heads, "nope": nope, "v_dim": w_uv.shape[3],
        "scale": math.log2(math.e) / math.sqrt(nope + rope), "topk": 16,
    }
    assert nope == LANES and dims["v_dim"] == LANES and rope <= LANES

    xp, xs = x_prompt, x_sample
    nb = bp + bs
    rows = -(-nb // BF16_ROWS) * BF16_ROWS
    c_all = jnp.pad(jnp.concatenate([c_prompt, c_sample], axis=0), ((0, rows - nb), (0, 0)))
    zeros = jnp.zeros((bp, 2, groups * states), F32)
    outs_p, outs_s = [], []
    for layer in range(depth):
        p = {
            "g_norm1": g_norm1[layer], "g_norm2": g_norm2[layer], "w_in": w_in[layer], "g_q": g_q[layer],
            "w_qu": w_qu[layer], "g_kv": g_kv[layer], "w_uk": w_uk[layer], "w_uv": w_uv[layer], "w_o": w_o[layer],
            "ssm_a_re": ssm_a_re[layer], "ssm_a_im": ssm_a_im[layer], "ssm_log_dt": ssm_log_dt[layer],
            "ssm_b_re": ssm_b_re[layer], "ssm_b_im": ssm_b_im[layer], "ssm_c_re": ssm_c_re[layer],
            "ssm_c_im": ssm_c_im[layer], "ssm_d": ssm_d[layer], "w_glu": w_glu[layer], "w_out": w_out[layer],
            "peer_wq": peer_wq[layer], "peer_k1": peer_k1[layer], "peer_k2": peer_k2[layer],
            "peer_u": peer_u[layer], "peer_v": peer_v[layer],
        }
        w = _prep_weights(p, dims)
        mod = _ada(c_all, w_ada[layer], b_ada[layer])
        mod_p = [m.reshape(bp, 1, d) for m in jnp.split(mod[:bp], 6, axis=-1)]
        mod_s = [m.reshape(bs, 1, d) for m in jnp.split(mod[bp:nb], 6, axis=-1)]
        s0_s = jnp.stack([state_ssm_re[layer].reshape(bs, -1), state_ssm_im[layer].reshape(bs, -1)], axis=1)
        last = layer == depth - 1
        xp, *res_p = _layer(xp, mod_p, None, None, zeros, p, w, dims, g_final, last)
        xs, *res_s = _layer(xs, mod_s, cache_ckv[layer], cache_krope[layer], s0_s, p, w, dims, g_final, last)
        for res, outs, g in ((res_p, outs_p, bp), (res_s, outs_s, bs)):
            ckv, kr, s_fin = res
            outs.append((ckv, kr, s_fin[:, 0].reshape(g, groups, states), s_fin[:, 1].reshape(g, groups, states)))
    stack = lambda outs, k: jnp.stack([o[k] for o in outs])
    return (xp, xs,
            stack(outs_p, 0), stack(outs_p, 1), stack(outs_p, 2), stack(outs_p, 3),
            stack(outs_s, 0), stack(outs_s, 1), stack(outs_s, 2), stack(outs_s, 3))
```

```python
import functools
import math

import jax
import jax.numpy as jnp
from jax import lax
from jax.experimental import pallas as pl
from jax.experimental.pallas import tpu as pltpu

F32 = jnp.float32
BF16 = jnp.bfloat16

EPS = 1e-6
CHUNK = 64
ROPE_THETA = 10000.0
LANES = 128
SUBLANES = 8
MXU_DIM = 256
BF16_ROWS = 16
TOKEN_TILE = 512
WIDE_TILE = 1024
RESIDENT_WEIGHT_TOKEN_TILE = 256
VMEM_LIMIT_BYTES = 56 * 1024 * 1024
MASKED = -1e30
REMOVED = -3e38
EXCLUDED = -1e38


def _params(*semantics):
    return pltpu.CompilerParams(dimension_semantics=semantics, vmem_limit_bytes=VMEM_LIMIT_BYTES)


def _tile(n, pref):
    if n <= pref:
        return n
    t = pref
    while n % t:
        t //= 2
    assert t >= SUBLANES, (n, pref)
    return t


def _sigmoid(x):
    return 1.0 / (1.0 + jnp.exp(-x))


def _gelu(x):
    return 0.5 * x * (1.0 + jnp.tanh(math.sqrt(2.0 / math.pi) * (x + 0.044715 * (x * x * x))))


def _rms(x, g):
    return x * lax.rsqrt(jnp.mean(x * x, axis=-1, keepdims=True) + EPS) * g


def _ada_kernel(c_ref, w_ref, b_ref, o_ref):
    c = c_ref[...]
    a = (c * _sigmoid(c)).astype(BF16)
    o_ref[...] = jnp.dot(a, w_ref[...].astype(BF16), preferred_element_type=F32) + b_ref[...]


def _ada(c, w, b):
    rows, d = c.shape
    n = w.shape[1]
    tn = _tile(n, WIDE_TILE)
    return pl.pallas_call(
        _ada_kernel,
        grid=(n // tn,),
        in_specs=[
            pl.BlockSpec((rows, d), lambda j: (0, 0)),
            pl.BlockSpec((d, tn), lambda j: (0, j)),
            pl.BlockSpec((1, tn), lambda j: (0, j)),
        ],
        out_specs=pl.BlockSpec((rows, tn), lambda j: (0, j)),
        out_shape=jax.ShapeDtypeStruct((rows, n), F32),
        compiler_params=_params("parallel"),
        name="ada",
    )(c, w, b.reshape(1, n))


def _normmod_kernel(x_ref, g_ref, sc_ref, sh_ref, o_ref):
    h = _rms(x_ref[0], g_ref[...]) * (1.0 + sc_ref[0]) + sh_ref[0]
    o_ref[0] = h.astype(BF16)


def _normmod(x, g, sc, sh):
    b, l, d = x.shape
    tm = _tile(l, TOKEN_TILE)
    return pl.pallas_call(
        _normmod_kernel,
        grid=(b, l // tm),
        in_specs=[
            pl.BlockSpec((1, tm, d), lambda i, j: (i, j, 0)),
            pl.BlockSpec((1, d), lambda i, j: (0, 0)),
            pl.BlockSpec((1, 1, d), lambda i, j: (i, 0, 0)),
            pl.BlockSpec((1, 1, d), lambda i, j: (i, 0, 0)),
        ],
        out_specs=pl.BlockSpec((1, tm, d), lambda i, j: (i, j, 0)),
        out_shape=jax.ShapeDtypeStruct((b, l, d), BF16),
        compiler_params=_params("parallel", "parallel"),
        name="normmod",
    )(x, g.reshape(1, d), sc, sh)


def _mm_kernel(a_ref, w_ref, o_ref, *, act):
    z = jnp.dot(a_ref[...], w_ref[...], preferred_element_type=F32)
    if act == "sigmoid":
        z = _sigmoid(z)
    o_ref[...] = z.astype(o_ref.dtype)


def _mm(a, w, out_dtype, act=None, tn_pref=WIDE_TILE):
    t, k = a.shape
    n = w.shape[1]
    tm = _tile(t, WIDE_TILE)
    tn = _tile(n, tn_pref)
    return pl.pallas_call(
        functools.partial(_mm_kernel, act=act),
        grid=(t // tm, n // tn),
        in_specs=[
            pl.BlockSpec((tm, k), lambda i, j: (i, 0)),
            pl.BlockSpec((k, tn), lambda i, j: (0, j)),
        ],
        out_specs=pl.BlockSpec((tm, tn), lambda i, j: (i, j)),
        out_shape=jax.ShapeDtypeStruct((t, n), out_dtype),
        compiler_params=_params("parallel", "parallel"),
        name="mm",
    )(a, w)


def _mm_split_kernel(a_ref, w_ref, o_ref, *, parts):
    z = jnp.dot(a_ref[...], w_ref[...], preferred_element_type=F32)
    for p in range(parts):
        o_ref[p] = z[:, p * LANES:(p + 1) * LANES].astype(o_ref.dtype)


def _mm_split(a, w, out_dtype):
    t, k = a.shape
    n = w.shape[1]
    tm = _tile(t, TOKEN_TILE)
    tn = _tile(n, WIDE_TILE)
    parts = tn // LANES
    return pl.pallas_call(
        functools.partial(_mm_split_kernel, parts=parts),
        grid=(t // tm, n // tn),
        in_specs=[
            pl.BlockSpec((tm, k), lambda i, j: (i, 0)),
            pl.BlockSpec((k, tn), lambda i, j: (0, j)),
        ],
        out_specs=pl.BlockSpec((parts, tm, LANES), lambda i, j: (j, i, 0)),
        out_shape=jax.ShapeDtypeStruct((n // LANES, t, LANES), out_dtype),
        compiler_params=_params("parallel", "parallel"),
        name="mm_split",
    )(a, w)


def _mla_proj_kernel(h_ref, wm_ref, gq_ref, gkv_ref, wq_ref, cos_ref, sin_ref,
                     qn_ref, qr_ref, ckv_ref, ckvb_ref, kr_ref, krp_ref,
                     *, heads, q_lora, kv_lora, rope, scale):
    z = jnp.dot(h_ref[...], wm_ref[...], preferred_element_type=F32)
    cos = cos_ref[...]
    sin = sin_ref[...]
    off = q_lora + kv_lora
    krp = z[:, off:off + LANES] * cos + z[:, off + LANES:off + 2 * LANES] * sin
    kr_ref[...] = krp[:, :rope]
    krp_ref[...] = krp.astype(BF16)
    ckv = _rms(z[:, q_lora:off], gkv_ref[...])
    ckv_ref[...] = ckv
    ckvb_ref[...] = ckv.astype(BF16)
    qd = _rms(z[:, :q_lora], gq_ref[...]).astype(BF16)
    zq = jnp.dot(qd, wq_ref[...], preferred_element_type=F32)
    hn = heads * LANES
    for hh in range(heads):
        lo = hh * LANES
        qn_ref[hh] = (zq[:, lo:lo + LANES] * scale).astype(BF16)
        qr = zq[:, hn + lo:hn + lo + LANES] * cos + zq[:, 2 * hn + lo:2 * hn + lo + LANES] * sin
        qr_ref[hh] = (qr * scale).astype(BF16)


def _mla_proj(h, wm, gq, gkv, wq, cos, sin, seq, *, heads, q_lora, kv_lora, rope, scale):
    t, d = h.shape
    tm = _tile(t, RESIDENT_WEIGHT_TOKEN_TILE)
    if seq % tm == 0:
        nrep = seq // tm
        tab_map = lambda i: (i % nrep, 0)
    else:
        assert tm % seq == 0
        cos = jnp.tile(cos, (tm // seq, 1))
        sin = jnp.tile(sin, (tm // seq, 1))
        tab_map = lambda i: (0, 0)
    nm = wm.shape[1]
    nq = wq.shape[1]
    row = lambda i: (i, 0)
    const = lambda i: (0, 0)
    return pl.pallas_call(
        functools.partial(_mla_proj_kernel, heads=heads, q_lora=q_lora, kv_lora=kv_lora, rope=rope, scale=scale),
        grid=(t // tm,),
        in_specs=[
            pl.BlockSpec((tm, d), row),
            pl.BlockSpec((d, nm), const),
            pl.BlockSpec((1, q_lora), const),
            pl.BlockSpec((1, kv_lora), const),
            pl.BlockSpec((q_lora, nq), const),
            pl.BlockSpec((tm, LANES), tab_map),
            pl.BlockSpec((tm, LANES), tab_map),
        ],
        out_specs=[
            pl.BlockSpec((heads, tm, LANES), lambda i: (0, i, 0)),
            pl.BlockSpec((heads, tm, LANES), lambda i: (0, i, 0)),
            pl.BlockSpec((tm, kv_lora), row),
            pl.BlockSpec((tm, kv_lora), row),
            pl.BlockSpec((tm, rope), row),
            pl.BlockSpec((tm, LANES), row),
        ],
        out_shape=[
            jax.ShapeDtypeStruct((heads, t, LANES), BF16),
            jax.ShapeDtypeStruct((heads, t, LANES), BF16),
            jax.ShapeDtypeStruct((t, kv_lora), F32),
            jax.ShapeDtypeStruct((t, kv_lora), BF16),
            jax.ShapeDtypeStruct((t, rope), F32),
            jax.ShapeDtypeStruct((t, LANES), BF16),
        ],
        compiler_params=_params("parallel"),
        name="mla_proj",
    )(h, wm, gq.reshape(1, q_lora), gkv.reshape(1, kv_lora), wq, cos, sin)


def _kvup_kernel(c_ref, w_ref, k_ref, v_ref, *, heads):
    z = jnp.dot(c_ref[...], w_ref[...], preferred_element_type=F32)
    for hh in range(heads):
        k_ref[hh] = z[:, hh * LANES:(hh + 1) * LANES].astype(BF16)
        v_ref[hh] = z[:, (heads + hh) * LANES:(heads + hh + 1) * LANES].astype(BF16)


def _kvup(ckv, w, heads):
    t, c = ckv.shape
    tm = _tile(t, TOKEN_TILE)
    if t % tm:
        tm = t
    return pl.pallas_call(
        functools.partial(_kvup_kernel, heads=heads),
        grid=(t // tm,),
        in_specs=[
            pl.BlockSpec((tm, c), lambda i: (i, 0)),
            pl.BlockSpec(w.shape, lambda i: (0, 0)),
        ],
        out_specs=[
            pl.BlockSpec((heads, tm, LANES), lambda i: (0, i, 0)),
            pl.BlockSpec((heads, tm, LANES), lambda i: (0, i, 0)),
        ],
        out_shape=[jax.ShapeDtypeStruct((heads, t, LANES), BF16)] * 2,
        compiler_params=_params("parallel"),
        name="kvup",
    )(ckv, w)


def _flash_kernel(iq_tab, ik_tab, flag_tab, qn_ref, qr_ref, k_ref, kr_ref, v_ref, o_ref,
                  qc_scr, m_scr, acc_scr, *, hb, tq, tk, q_pos0, n_keys):
    pair = pl.program_id(2)
    iq = iq_tab[pair]
    ik = ik_tab[pair]
    flags = flag_tab[pair]
    first = (flags & 1) != 0
    last = (flags & 2) != 0
    full = (flags & 4) != 0

    @pl.when(first)
    def _init():
        m_scr[...] = jnp.full(m_scr.shape, MASKED, F32)
        acc_scr[...] = jnp.zeros(acc_scr.shape, F32)
        for hh in range(hb):
            qc_scr[hh] = jnp.concatenate([qn_ref[hh], qr_ref[hh]], axis=1)

    def step(masked):
        kr = kr_ref[...]
        ones = jnp.ones((tk, LANES), BF16)
        if masked:
            qp = q_pos0 + iq * tq + lax.broadcasted_iota(jnp.int32, (tq, tk), 0)
            kp = ik * tk + lax.broadcasted_iota(jnp.int32, (tq, tk), 1)
            allowed = jnp.logical_and(kp // CHUNK <= qp // CHUNK, kp < n_keys)

        def head(hh, carry):
            kc = jnp.concatenate([k_ref[hh], kr], axis=1)
            s = lax.dot_general(qc_scr[hh], kc, (((1,), (1,)), ((), ())), preferred_element_type=F32)
            if masked:
                s = jnp.where(allowed, s, MASKED)
            m_prev = m_scr[hh]
            m_new = jnp.maximum(m_prev, jnp.max(s, axis=1, keepdims=True))
            alpha = jnp.exp2(m_prev - m_new)
            p = jnp.exp2(s - pltpu.repeat(m_new, tk // LANES, axis=1)).astype(BF16)
            vc = jnp.concatenate([v_ref[hh], ones], axis=1)
            acc_scr[hh] = pltpu.repeat(alpha, 2, axis=1) * acc_scr[hh] + jnp.dot(p, vc, preferred_element_type=F32)
            m_scr[hh] = m_new
            return carry

        lax.fori_loop(0, hb, head, 0, unroll=True)

    @pl.when(full)
    def _full():
        step(False)

    @pl.when(jnp.logical_not(full))
    def _diag():
        step(True)

    @pl.when(last)
    def _done():
        for hh in range(hb):
            acc = acc_scr[hh]
            o_ref[hh] = (acc[:, :LANES] / acc[:, LANES:]).astype(BF16)


def _flash_pairs(lq, lk, tq, tk, q_pos0, n_keys):
    iqs, iks, flags = [], [], []
    for iq in range(lq // tq):
        q_first = q_pos0 + iq * tq
        last_key = min(n_keys - 1, ((q_first + tq - 1) // CHUNK) * CHUNK + CHUNK - 1)
        full_key = min(n_keys - 1, (q_first // CHUNK) * CHUNK + CHUNK - 1)
        n_blocks = last_key // tk + 1
        for ik in range(n_blocks):
            full = (ik + 1) * tk - 1 <= full_key
            iqs.append(iq)
            iks.append(ik)
            flags.append((1 if ik == 0 else 0) | (2 if ik == n_blocks - 1 else 0) | (4 if full else 0))
    as_i32 = lambda xs: jnp.asarray(xs, jnp.int32)
    return as_i32(iqs), as_i32(iks), as_i32(flags)


def _flash(qn, qr, k, krp, v, *, batch, lq, lk, q_pos0, n_keys, tq, tk, hb):
    heads = qn.shape[0]
    nq = lq // tq
    nk = lk // tk
    assert lq % tq == 0 and lk % tk == 0 and heads % hb == 0 and tk % LANES == 0
    iq_tab, ik_tab, flag_tab = _flash_pairs(lq, lk, tq, tk, q_pos0, n_keys)
    qmap = lambda b, h, p, iqt, ikt, ft: (h, b * nq + iqt[p], 0)
    kmap = lambda b, h, p, iqt, ikt, ft: (h, b * nk + ikt[p], 0)
    return pl.pallas_call(
        functools.partial(_flash_kernel, hb=hb, tq=tq, tk=tk, q_pos0=q_pos0, n_keys=n_keys),
        grid_spec=pltpu.PrefetchScalarGridSpec(
            num_scalar_prefetch=3,
            grid=(batch, heads // hb, iq_tab.shape[0]),
            in_specs=[
                pl.BlockSpec((hb, tq, LANES), qmap),
                pl.BlockSpec((hb, tq, LANES), qmap),
                pl.BlockSpec((hb, tk, LANES), kmap),
                pl.BlockSpec((tk, LANES), lambda b, h, p, iqt, ikt, ft: (b * nk + ikt[p], 0)),
                pl.BlockSpec((hb, tk, LANES), kmap),
            ],
            out_specs=pl.BlockSpec((hb, tq, LANES), qmap),
            scratch_shapes=[
                pltpu.VMEM((hb, tq, 2 * LANES), BF16),
                pltpu.VMEM((hb, tq, LANES), F32),
                pltpu.VMEM((hb, tq, 2 * LANES), F32),
            ],
        ),
        out_shape=jax.ShapeDtypeStruct((heads, batch * lq, LANES), BF16),
        compiler_params=_params("parallel", "parallel", "arbitrary"),
        name="flash",
    )(iq_tab, ik_tab, flag_tab, qn, qr, k, krp, v)


def _s5_kernel(u_ref, pm_ref, pt_ref, wb_ref, scn_ref, wc_ref, d_ref, s0_ref, y_ref, st_ref,
               xs_ref, up_ref, yp_ref, car_ref, *, ts, nkt, kw, sw, nt):
    it = pl.program_id(1)
    steps = ts // SUBLANES

    @pl.when(it == 0)
    def _load_state():
        car_ref[...] = s0_ref[0]

    up_ref[...] = jnp.dot(pm_ref[...], u_ref[0], preferred_element_type=F32).astype(BF16)
    first_row = lax.broadcasted_iota(jnp.int32, (SUBLANES, sw), 0) == 0
    for kt in range(nkt):
        ukt = up_ref[:, kt * kw:(kt + 1) * kw]
        xs_ref[...] = jnp.dot(ukt, wb_ref[kt], preferred_element_type=F32)
        st_sl = slice(kt * sw, (kt + 1) * sw)
        lr = scn_ref[kt, 0]
        li = scn_ref[kt, 1]

        def advance(k, carry, store, lr=lr, li=li):
            hr, hi = carry
            rows = pl.ds(pl.multiple_of(k * SUBLANES, SUBLANES), SUBLANES)
            nr = lr * hr - li * hi + xs_ref[rows, :sw]
            ni = lr * hi + li * hr + xs_ref[rows, sw:]
            if store:
                xs_ref[rows, :sw] = nr
                xs_ref[rows, sw:] = ni
            return nr, ni

        zero = jnp.zeros((SUBLANES, sw), F32)
        er, ei = lax.fori_loop(0, steps, functools.partial(advance, store=False), (zero, zero))
        br = jnp.where(first_row, jnp.broadcast_to(car_ref[0:1, st_sl], (SUBLANES, sw)), pltpu.roll(er, 1, 0))
        bi = jnp.where(first_row, jnp.broadcast_to(car_ref[1:2, st_sl], (SUBLANES, sw)), pltpu.roll(ei, 1, 0))
        for c0, dist in ((2, 1), (4, 2), (6, 4)):
            ar = scn_ref[kt, c0]
            ai = scn_ref[kt, c0 + 1]
            sr = pltpu.roll(br, dist, 0)
            si = pltpu.roll(bi, dist, 0)
            br, bi = br + ar * sr - ai * si, bi + ar * si + ai * sr
        hr, hi = lax.fori_loop(0, steps, functools.partial(advance, store=True), (br, bi))
        car_ref[0:1, st_sl] = hr[SUBLANES - 1:SUBLANES]
        car_ref[1:2, st_sl] = hi[SUBLANES - 1:SUBLANES]
        y = jnp.dot(xs_ref[...].astype(BF16), wc_ref[kt], preferred_element_type=F32)
        y = y + d_ref[:, kt * kw:(kt + 1) * kw] * ukt.astype(F32)
        yp_ref[:, kt * kw:(kt + 1) * kw] = y.astype(BF16)
    y_ref[0] = _gelu(jnp.dot(pt_ref[...], yp_ref[...], preferred_element_type=F32)).astype(BF16)

    @pl.when(it == nt - 1)
    def _store_state():
        st_ref[0] = car_ref[...]


def _s5(u, wb, lam, wc, dsk, s0):
    b, l, w = u.shape
    nkt, kw, sw2 = wb.shape
    sw = sw2 // 2
    ns = s0.shape[2]
    ts = _tile(l, TOKEN_TILE)
    nt = l // ts
    steps = ts // SUBLANES

    def cmul(x, y):
        return x[0] * y[0] - x[1] * y[1], x[0] * y[1] + x[1] * y[0]

    mu = (lam[0], lam[1])
    for _ in range(steps - 1):
        mu = cmul(mu, (lam[0], lam[1]))
    mu2 = cmul(mu, mu)
    mu4 = cmul(mu2, mu2)
    rows = jnp.arange(SUBLANES)[None, :, None]
    shifted = lambda c, dist: [jnp.where(rows >= dist, x[:, None, :], 0.0) for x in c]
    rep = [jnp.broadcast_to(x[:, None, :], (nkt, SUBLANES, sw)) for x in (lam[0], lam[1])]
    scn = jnp.stack(rep + shifted(mu, 1) + shifted(mu2, 2) + shifted(mu4, 4), axis=1).astype(F32)
    r = jnp.arange(ts)
    pm = jax.nn.one_hot((r % SUBLANES) * steps + r // SUBLANES, ts, dtype=BF16)
    const2 = lambda i, j: (0, 0)
    const3 = lambda i, j: (0, 0, 0)
    return pl.pallas_call(
        functools.partial(_s5_kernel, ts=ts, nkt=nkt, kw=kw, sw=sw, nt=nt),
        grid=(b, nt),
        in_specs=[
            pl.BlockSpec((1, ts, w), lambda i, j: (i, j, 0)),
            pl.BlockSpec((ts, ts), const2),
            pl.BlockSpec((ts, ts), const2),
            pl.BlockSpec(wb.shape, const3),
            pl.BlockSpec(scn.shape, lambda i, j: (0, 0, 0, 0)),
            pl.BlockSpec(wc.shape, const3),
            pl.BlockSpec((1, w), const2),
            pl.BlockSpec((1, 2, ns), lambda i, j: (i, 0, 0)),
        ],
        out_specs=[
            pl.BlockSpec((1, ts, w), lambda i, j: (i, j, 0)),
            pl.BlockSpec((1, 2, ns), lambda i, j: (i, 0, 0)),
        ],
        out_shape=[
            jax.ShapeDtypeStruct((b, l, w), BF16),
            jax.ShapeDtypeStruct((b, 2, ns), F32),
        ],
        scratch_shapes=[
            pltpu.VMEM((ts, sw2), F32),
            pltpu.VMEM((ts, w), BF16),
            pltpu.VMEM((ts, w), BF16),
            pltpu.VMEM((2, ns), F32),
        ],
        compiler_params=_params("parallel", "arbitrary"),
        name="s5",
    )(u, pm, pm.T, wb, scn, wc, dsk, s0)


def _s5_tables(a_re, a_im, log_dt, b_re, b_im, c_re, c_im, d_skip):
    g, p = a_re.shape
    ch = b_re.shape[2]
    gpt = MXU_DIM // ch
    nkt = g // gpt
    dt = jnp.exp(log_dt.astype(F32))[:, None]
    mag = jnp.exp(dt * a_re)
    ab_re = mag * jnp.cos(dt * a_im)
    ab_im = mag * jnp.sin(dt * a_im)
    den = a_re * a_re + a_im * a_im
    nr = ab_re - 1.0
    f_re = (nr * a_re + ab_im * a_im) / den
    f_im = (ab_im * a_re - nr * a_im) / den
    bb_re = f_re[..., None] * b_re - f_im[..., None] * b_im
    bb_im = f_re[..., None] * b_im + f_im[..., None] * b_re
    eye = jnp.eye(gpt, dtype=F32)

    def in_blockdiag(bb):
        return jnp.einsum("kgpi,gh->kgihp", bb.reshape(nkt, gpt, p, ch), eye).reshape(nkt, gpt * ch, gpt * p)

    def out_blockdiag(cc):
        return jnp.einsum("kgjp,gh->kgphj", cc.reshape(nkt, gpt, ch, p), eye).reshape(nkt, gpt * p, gpt * ch)

    wb = jnp.concatenate([in_blockdiag(bb_re), in_blockdiag(bb_im)], axis=2).astype(BF16)
    wc = jnp.concatenate([out_blockdiag(c_re), out_blockdiag(-c_im)], axis=1).astype(BF16)
    lam = jnp.stack([ab_re.reshape(nkt, gpt * p), ab_im.reshape(nkt, gpt * p)]).astype(F32)
    return wb, lam, wc, d_skip.reshape(1, g * ch).astype(F32)


def _merge_kernel(ys_ref, o_ref, wga_ref, wgb_ref, wo_ref, ga_ref, gb_ref, out_ref, *, heads):
    ys = ys_ref[...]
    ya = jnp.dot(ys, wga_ref[...], preferred_element_type=F32)
    ya = ya * _sigmoid(jnp.dot(ys, wgb_ref[...], preferred_element_type=F32))
    oc = jnp.concatenate([o_ref[hh] for hh in range(heads)], axis=1)
    yb = jnp.dot(oc, wo_ref[...], preferred_element_type=F32)
    out_ref[...] = (ga_ref[...].astype(F32) * ya + gb_ref[...].astype(F32) * yb).astype(BF16)


def _merge(ys, o, w_glu, w_o, gates):
    t, sw = ys.shape
    heads = o.shape[0]
    d = w_o.shape[1]
    tm = _tile(t, TOKEN_TILE)
    tn = _tile(d, WIDE_TILE)
    nj = d // tn
    return pl.pallas_call(
        functools.partial(_merge_kernel, heads=heads),
        grid=(t // tm, nj),
        in_specs=[
            pl.BlockSpec((tm, sw), lambda i, j: (i, 0)),
            pl.BlockSpec((heads, tm, LANES), lambda i, j: (0, i, 0)),
            pl.BlockSpec((sw, tn), lambda i, j: (0, j)),
            pl.BlockSpec((sw, tn), lambda i, j: (0, nj + j)),
            pl.BlockSpec((heads * LANES, tn), lambda i, j: (0, j)),
            pl.BlockSpec((tm, tn), lambda i, j: (i, j)),
            pl.BlockSpec((tm, tn), lambda i, j: (i, nj + j)),
        ],
        out_specs=pl.BlockSpec((tm, tn), lambda i, j: (i, j)),
        out_shape=jax.ShapeDtypeStruct((t, d), BF16),
        compiler_params=_params("parallel", "parallel"),
        name="merge",
    )(ys, o, w_glu, w_glu, w_o, gates, gates)


def _resid_kernel(m_ref, w_ref, x_ref, gt_ref, g_ref, sc_ref, sh_ref, x1_ref, h2_ref):
    x1 = x_ref[0] + gt_ref[0] * jnp.dot(m_ref[0], w_ref[...], preferred_element_type=F32)
    x1_ref[0] = x1
    h2_ref[0] = (_rms(x1, g_ref[...]) * (1.0 + sc_ref[0]) + sh_ref[0]).astype(BF16)


def _resid(merged, w_out, x, gt, g2, sc, sh):
    b, l, d = x.shape
    tm = _tile(l, TOKEN_TILE)
    tok = lambda i, j: (i, j, 0)
    per_b = lambda i, j: (i, 0, 0)
    return pl.pallas_call(
        _resid_kernel,
        grid=(b, l // tm),
        in_specs=[
            pl.BlockSpec((1, tm, d), tok),
            pl.BlockSpec((d, d), lambda i, j: (0, 0)),
            pl.BlockSpec((1, tm, d), tok),
            pl.BlockSpec((1, 1, d), per_b),
            pl.BlockSpec((1, d), lambda i, j: (0, 0)),
            pl.BlockSpec((1, 1, d), per_b),
            pl.BlockSpec((1, 1, d), per_b),
        ],
        out_specs=[pl.BlockSpec((1, tm, d), tok), pl.BlockSpec((1, tm, d), tok)],
        out_shape=[jax.ShapeDtypeStruct((b, l, d), F32), jax.ShapeDtypeStruct((b, l, d), BF16)],
        compiler_params=_params("parallel", "parallel"),
        name="resid",
    )(merged.reshape(b, l, d), w_out, x, gt, g2.reshape(1, d), sc, sh)


def _peer_topk_kernel(q_ref, k1_ref, k2_ref, flat_ref, invalid_ref, r2_ref, c1_ref, a1_ref, a2_ref,
                      *, nh, nkeys, topk):
    tb = q_ref.shape[1]
    iota_k = lax.broadcasted_iota(jnp.int32, (nkeys, tb), 0).astype(F32)
    iota_t = lax.broadcasted_iota(jnp.int32, (topk, tb), 0).astype(F32)
    flat = flat_ref[...]
    invalid = invalid_ref[...]
    nt_dims = (((1,), (1,)), ((), ()))

    n_cand = flat.shape[0]

    def extract(s, exact_ties):
        work = s
        rank = jnp.full((nkeys, tb), float(topk), F32)
        vals = []
        for a in range(topk):
            m = jnp.max(work, axis=0, keepdims=True)
            if exact_ties:
                idx = jnp.min(jnp.where(work == m, iota_k, float(nkeys)), axis=0, keepdims=True)
                sel = iota_k == idx
            else:
                sel = work == m
            rank = jnp.where(sel, float(a), rank)
            work = jnp.where(sel, REMOVED, work)
            vals.append(m)
        ranked = jnp.sum(jnp.where(rank < float(topk), 1.0, 0.0), axis=0, keepdims=True)
        return vals, rank, ranked

    def stack(vals):
        out = jnp.zeros((topk, tb), F32)
        for a in range(topk):
            out = jnp.where(iota_t == float(a), vals[a], out)
        return out

    def candidates(v1, v2):
        vs1 = stack(v1)
        vs2 = stack(v2)
        blocks = [v1[0] + vs2]
        for a in range(1, SUBLANES):
            blocks.append(v1[a] + vs2[:SUBLANES])
        blocks.append(vs1[SUBLANES:] + v2[0])
        return jnp.concatenate(blocks, axis=0) + invalid

    def route(s1, s2, exact_ties):
        v1, rank1, n1 = extract(s1, exact_ties)
        v2, rank2, n2 = extract(s2, exact_ties)
        cand = candidates(v1, v2)
        counts = jnp.zeros((topk, tb), F32)
        top = None
        zsum = None
        for kk in range(topk):
            m = jnp.max(cand, axis=0, keepdims=True)
            if exact_ties:
                f = jnp.min(jnp.where(cand == m, flat, 1e9), axis=0, keepdims=True)
                cand = jnp.where(flat == f, REMOVED, cand)
                counts = counts + jnp.where(iota_t == jnp.floor(f * (1.0 / topk)), 1.0, 0.0)
            else:
                cand = jnp.where(cand == m, REMOVED, cand)
            if kk == 0:
                top = m
                zsum = jnp.ones_like(m)
            else:
                zsum = zsum + jnp.exp(m - top)
        n3 = None
        if not exact_ties:
            gone = jnp.where(cand == REMOVED, 1.0, 0.0)
            n3 = jnp.sum(gone, axis=0, keepdims=True)
            per_rank = [jnp.sum(gone[:topk], axis=0, keepdims=True)]
            for a in range(1, SUBLANES):
                lo = topk + (a - 1) * SUBLANES
                per_rank.append(jnp.sum(gone[lo:lo + SUBLANES], axis=0, keepdims=True))
            counts = jnp.concatenate([stack(per_rank + [per_rank[0]] * (topk - SUBLANES))[:SUBLANES],
                                      gone[n_cand - SUBLANES:]], axis=0)
        c1 = jnp.zeros((nkeys, tb), F32)
        for a in range(topk):
            c1 = jnp.where(rank1 == float(a), counts[a:a + 1], c1)
        a1 = jnp.exp(s1 - v1[0]) * (1.0 / zsum)
        a2 = jnp.exp(s2 - v2[0])
        clean = None
        if not exact_ties:
            want = float(topk)
            bad = jnp.where(n1 != want, 1.0, 0.0) + jnp.where(n2 != want, 1.0, 0.0) + jnp.where(n3 != want, 1.0, 0.0)
            clean = jnp.max(bad) == 0.0
        return (rank2, c1, a1, a2), clean

    def store(hh, tables):
        r2_ref[hh], c1_ref[hh], a1_ref[hh], a2_ref[hh] = tables

    per_trip = 4 if nh % 4 == 0 else 2

    def group(gg, carry):
        heads = [per_trip * gg + r for r in range(per_trip)]
        scores = []
        for hh in heads:
            scores.append((lax.dot_general(k1_ref[hh], q_ref[2 * hh], nt_dims, preferred_element_type=F32),
                           lax.dot_general(k2_ref[hh], q_ref[2 * hh + 1], nt_dims, preferred_element_type=F32)))
        quick = [route(s1, s2, False) for s1, s2 in scores]
        clean = quick[0][1]
        for _, ok in quick[1:]:
            clean = jnp.logical_and(clean, ok)

        @pl.when(clean)
        def _no_ties():
            for hh, (tables, _) in zip(heads, quick):
                store(hh, tables)

        @pl.when(jnp.logical_not(clean))
        def _ties():
            for hh, (s1, s2) in zip(heads, scores):
                store(hh, route(s1, s2, True)[0])

        return carry

    lax.fori_loop(0, nh // per_trip, group, 0)


def _peer_topk(q, k1, k2, topk):
    nh, nkeys, half = k1.shape
    t = q.shape[1]
    tb = LANES
    assert topk == 2 * SUBLANES and half == LANES and t % tb == 0
    rows = jnp.arange(topk + (SUBLANES - 1) * SUBLANES + SUBLANES)
    a_idx = jnp.where(rows < topk, 0, jnp.where(rows < topk + (SUBLANES - 1) * SUBLANES,
                                                1 + (rows - topk) // SUBLANES, SUBLANES + (rows - topk - (SUBLANES - 1) * SUBLANES)))
    b_idx = jnp.where(rows < topk, rows, jnp.where(rows < topk + (SUBLANES - 1) * SUBLANES, (rows - topk) % SUBLANES, 0))
    flat = jnp.broadcast_to((a_idx * topk + b_idx).astype(F32)[:, None], (rows.shape[0], tb))
    invalid = jnp.broadcast_to(jnp.where((a_idx + 1) * (b_idx + 1) <= topk, 0.0, EXCLUDED).astype(F32)[:, None],
                               (rows.shape[0], tb))
    out = lambda dt: jax.ShapeDtypeStruct((nh, nkeys, t), dt)
    ospec = pl.BlockSpec((nh, nkeys, tb), lambda i: (0, 0, i))
    return pl.pallas_call(
        functools.partial(_peer_topk_kernel, nh=nh, nkeys=nkeys, topk=topk),
        grid=(t // tb,),
        in_specs=[
            pl.BlockSpec((2 * nh, tb, LANES), lambda i: (0, i, 0)),
            pl.BlockSpec(k1.shape, lambda i: (0, 0, 0)),
            pl.BlockSpec(k2.shape, lambda i: (0, 0, 0)),
            pl.BlockSpec(flat.shape, lambda i: (0, 0)),
            pl.BlockSpec(invalid.shape, lambda i: (0, 0)),
        ],
        out_specs=[ospec] * 4,
        out_shape=[out(F32)] * 4,
        compiler_params=_params("parallel"),
        name="peer_topk",
    )(q, k1, k2, flat, invalid)


def _peer_mix_kernel(h_ref, u_ref, vt_ref, r2_ref, c1_ref, a1_ref, a2_ref, *rest, nh, nkeys, ni, ne, norm):
    if len(rest) == 4:
        o_ref, w_scr, acc_scr, xt_scr = rest
        x_ref = gt_ref = g_ref = None
    else:
        x_ref, gt_ref, g_ref, o_ref, w_scr, acc_scr, xt_scr = rest
    e = pl.program_id(1)
    tb = h_ref.shape[0]

    @pl.when(e == 0)
    def _init():
        acc_scr[...] = jnp.zeros(acc_scr.shape, F32)
        xt_scr[...] = h_ref[...].T

    first_keys = pl.ds(pl.multiple_of(e * ni, SUBLANES), ni)
    per_chunk = 2
    for c in range(ni // per_chunk):
        crows = slice(c * per_chunk * nkeys, (c + 1) * per_chunk * nkeys)
        act = _gelu(jnp.dot(u_ref[crows, :], xt_scr[...], preferred_element_type=F32))
        for il2 in range(per_chunk):
            il = c * per_chunk + il2
            rows = slice(il * nkeys, (il + 1) * nkeys)
            for lg in range(tb // LANES):
                sl = slice(lg * LANES, (lg + 1) * LANES)
                gate = jnp.zeros((nkeys, LANES), F32)
                for hh in range(nh):
                    partners = c1_ref[hh, first_keys, sl][il:il + 1]
                    first = a1_ref[hh, first_keys, sl][il:il + 1]
                    gate = gate + jnp.where(r2_ref[hh, :, sl] < partners, a2_ref[hh, :, sl], 0.0) * first
                w_scr[rows, sl] = (gate * act[il2 * nkeys:(il2 + 1) * nkeys, sl]).astype(BF16)
    acc_scr[...] += jnp.dot(vt_ref[0], w_scr[...], preferred_element_type=F32)

    @pl.when(e == ne - 1)
    def _done():
        mix = acc_scr[...].T
        if x_ref is None:
            o_ref[...] = mix
        else:
            y = x_ref[0] + gt_ref[0] * mix
            o_ref[0] = _rms(y, g_ref[...]) if norm else y


def _peer_mix(h2, u_tab, vt_tab, r2, c1, a1, a2, x1, gt, g, norm):
    b, l, _ = x1.shape
    t, d = h2.shape
    nh, nkeys, _ = r2.shape
    n_exp = u_tab.shape[0]
    tb = _tile(t, TOKEN_TILE)
    ni = SUBLANES
    eb = ni * nkeys
    ne = n_exp // eb
    aux = pl.BlockSpec((nh, nkeys, tb), lambda i, e: (0, 0, i), pipeline_mode=pl.Buffered(1))
    in_specs = [
        pl.BlockSpec((tb, d), lambda i, e: (i, 0), pipeline_mode=pl.Buffered(1)),
        pl.BlockSpec((eb, d), lambda i, e: (e, 0)),
        pl.BlockSpec((1, d, eb), lambda i, e: (e, 0, 0)),
        aux, aux, aux, aux,
    ]
    args = [h2, u_tab, vt_tab, r2, c1, a1, a2]
    fused = l % tb == 0
    if fused:
        per_seq = l // tb
        in_specs += [
            pl.BlockSpec((1, tb, d), lambda i, e: (i // per_seq, i % per_seq, 0), pipeline_mode=pl.Buffered(1)),
            pl.BlockSpec((1, 1, d), lambda i, e: (i // per_seq, 0, 0)),
            pl.BlockSpec((1, d), lambda i, e: (0, 0)),
        ]
        args += [x1, gt, g.reshape(1, d)]
        out_spec = pl.BlockSpec((1, tb, d), lambda i, e: (i // per_seq, i % per_seq, 0))
        out_shape = jax.ShapeDtypeStruct((b, l, d), F32)
    else:
        out_spec = pl.BlockSpec((tb, d), lambda i, e: (i, 0))
        out_shape = jax.ShapeDtypeStruct((t, d), F32)
    out = pl.pallas_call(
        functools.partial(_peer_mix_kernel, nh=nh, nkeys=nkeys, ni=ni, ne=ne, norm=norm),
        grid=(t // tb, ne),
        in_specs=in_specs,
        out_specs=out_spec,
        out_shape=out_shape,
        scratch_shapes=[
            pltpu.VMEM((eb, tb), BF16),
            pltpu.VMEM((d, tb), F32),
            pltpu.VMEM((d, tb), BF16),
        ],
        compiler_params=_params("parallel", "arbitrary"),
        name="peer_mix",
    )(*args)
    return out if fused else _final(x1, out, gt, g, norm)


def _final_kernel(x_ref, p_ref, gt_ref, g_ref, y_ref, *, norm):
    y = x_ref[0] + gt_ref[0] * p_ref[0]
    y_ref[0] = _rms(y, g_ref[...]) if norm else y


def _final(x1, peer, gt, g, norm):
    b, l, d = x1.shape
    tm = _tile(l, TOKEN_TILE)
    tok = lambda i, j: (i, j, 0)
    return pl.pallas_call(
        functools.partial(_final_kernel, norm=norm),
        grid=(b, l // tm),
        in_specs=[
            pl.BlockSpec((1, tm, d), tok),
            pl.BlockSpec((1, tm, d), tok),
            pl.BlockSpec((1, 1, d), lambda i, j: (i, 0, 0)),
            pl.BlockSpec((1, d), lambda i, j: (0, 0)),
        ],
        out_specs=pl.BlockSpec((1, tm, d), tok),
        out_shape=jax.ShapeDtypeStruct((b, l, d), F32),
        compiler_params=_params("parallel", "parallel"),
        name="final",
    )(x1, peer.reshape(b, l, d), gt, g.reshape(1, d))


def _rope_tables(pos, rope):
    half = rope // 2
    inv = jnp.power(ROPE_THETA, -jnp.arange(half, dtype=F32) / half)
    ang = pos.astype(F32)[:, None] * inv
    reps = LANES // half
    return jnp.tile(jnp.cos(ang), (1, reps)), jnp.tile(jnp.sin(ang), (1, reps))


def _prep_weights(p, dims):
    d, sw, q_lora, kv_lora, rope, heads, nope = (dims[k] for k in ("d", "sw", "q_lora", "kv_lora", "rope", "heads", "nope"))
    half = rope // 2
    w_in = p["w_in"]
    off_q = sw
    off_kv = off_q + q_lora
    off_kr = off_kv + kv_lora
    off_g = off_kr + rope
    kr_w = w_in[:, off_kr:off_g]
    kr_rot = jnp.concatenate([-kr_w[:, half:], kr_w[:, :half]], axis=1)
    pad = jnp.zeros((d, LANES - rope), F32)
    w_mla = jnp.concatenate([w_in[:, off_q:off_kr], kr_w, pad, kr_rot, pad], axis=1).astype(BF16)
    wq = p["w_qu"].reshape(q_lora, heads, nope + rope)
    wq_n = wq[:, :, :nope].reshape(q_lora, heads * nope)
    wq_r = wq[:, :, nope:]
    wq_rot = jnp.concatenate([-wq_r[:, :, half:], wq_r[:, :, :half]], axis=2)
    hpad = jnp.zeros((q_lora, heads, LANES - rope), F32)
    wq_a = jnp.concatenate([wq_r, hpad], axis=2).reshape(q_lora, heads * LANES)
    wq_b = jnp.concatenate([wq_rot, hpad], axis=2).reshape(q_lora, heads * LANES)
    return {
        "w_gates": w_in[:, off_g:].astype(BF16),
        "w_u": w_in[:, :sw].astype(BF16),
        "w_mla": w_mla,
        "w_q": jnp.concatenate([wq_n, wq_a, wq_b], axis=1).astype(BF16),
        "w_kv": jnp.concatenate([p["w_uk"].reshape(kv_lora, heads * nope),
                                 p["w_uv"].reshape(kv_lora, heads * dims["v_dim"])], axis=1).astype(BF16),
        "w_o": p["w_o"].astype(BF16),
        "w_glu": p["w_glu"].astype(BF16),
        "w_out": p["w_out"].astype(BF16),
        "peer_wq": p["peer_wq"].astype(BF16),
        "peer_k1": p["peer_k1"].astype(BF16),
        "peer_k2": p["peer_k2"].astype(BF16),
        "peer_u": p["peer_u"].astype(BF16),
        "peer_vt": p["peer_v"].reshape(-1, SUBLANES * p["peer_k1"].shape[1], d).transpose(0, 2, 1).astype(BF16),
        "s5": _s5_tables(p["ssm_a_re"], p["ssm_a_im"], p["ssm_log_dt"], p["ssm_b_re"], p["ssm_b_im"],
                         p["ssm_c_re"], p["ssm_c_im"], p["ssm_d"]),
    }


def _layer(x, mod, past_ckv, past_kr, s0, p, w, dims, g_final, last):
    b, l, d = x.shape
    t = b * l
    heads, rope, kv_lora, topk = dims["heads"], dims["rope"], dims["kv_lora"], dims["topk"]
    sh1, sc1, gt1, sh2, sc2, gt2 = mod
    past = 0 if past_ckv is None else past_ckv.shape[1]

    h = _normmod(x, p["g_norm1"], sc1, sh1).reshape(t, d)
    gates = _mm(h, w["w_gates"], BF16, act="sigmoid")
    u = _mm(h, w["w_u"], BF16)
    cos, sin = _rope_tables(past + jnp.arange(l, dtype=jnp.int32), rope)
    qn, qr, ckv, ckv_b, kr, krp = _mla_proj(
        h, w["w_mla"], p["g_q"], p["g_kv"], w["w_q"], cos, sin, l,
        heads=heads, q_lora=dims["q_lora"], kv_lora=kv_lora, rope=rope, scale=dims["scale"])

    ys, s_fin = _s5(u.reshape(b, l, -1), *w["s5"], s0)

    if past_ckv is None:
        lk, keys_c, keys_r = l, ckv_b, krp
        tq = tk = _tile(l, TOKEN_TILE)
    else:
        n_keys = past + l
        lk = -(-n_keys // LANES) * LANES
        keys_c = jnp.concatenate([past_ckv.astype(BF16), ckv_b.reshape(b, l, kv_lora)], axis=1)
        keys_c = jnp.pad(keys_c, ((0, 0), (0, lk - n_keys), (0, 0))).reshape(b * lk, kv_lora)
        past_r = jnp.pad(past_kr.astype(BF16), ((0, 0), (0, 0), (0, LANES - rope)))
        keys_r = jnp.concatenate([past_r, krp.reshape(b, l, LANES)], axis=1)
        keys_r = jnp.pad(keys_r, ((0, 0), (0, lk - n_keys), (0, 0))).reshape(b * lk, LANES)
        tq, tk = l, lk
    kh, vh = _kvup(keys_c, w["w_kv"], heads)
    o = _flash(qn, qr, kh, keys_r, vh, batch=b, lq=l, lk=lk, q_pos0=past, n_keys=past + l,
               tq=tq, tk=tk, hb=heads)

    merged = _merge(ys.reshape(t, -1), o, w["w_glu"], w["w_o"], gates)
    x1, h2 = _resid(merged, w["w_out"], x, gt1, p["g_norm2"], sc2, sh2)

    h2 = h2.reshape(t, d)
    q = _mm_split(h2, w["peer_wq"], BF16)
    r2, c1, a1, a2 = _peer_topk(q, w["peer_k1"], w["peer_k2"], topk)
    x2 = _peer_mix(h2, w["peer_u"], w["peer_vt"], r2, c1, a1, a2, x1, gt2, g_final, last)
    return x2, ckv.reshape(b, l, kv_lora), kr.reshape(b, l, rope), s_fin


def kernel(x_prompt, x_sample, c_prompt, c_sample, cache_ckv, cache_krope, state_ssm_re, state_ssm_im, w_ada, b_ada, g_norm1, g_norm2, w_in, g_q, w_qu, g_kv, w_uk, w_uv, w_o, ssm_a_re, ssm_a_im, ssm_log_dt, ssm_b_re, ssm_b_im, ssm_c_re, ssm_c_im, ssm_d, w_glu, w_out, peer_wq, peer_k1, peer_k2, peer_u, peer_v, g_final):
    depth = w_in.shape[0]
    bp, lp, d = x_prompt.shape
    bs, ls, _ = x_sample.shape
    groups, states = ssm_a_re.shape[1:]
    heads, nope = w_uk.shape[2:]
    rope = cache_krope.shape[-1]
    dims = {
        "d": d, "sw": groups * ssm_b_re.shape[3], "q_lora": g_q.shape[1], "kv_lora": g_kv.shape[1],
        "rope": rope, "heads": heads, "nope": nope, "v_dim": w_uv.shape[3],
        "scale": math.log2(math.e) / math.sqrt(nope + rope), "topk": 16,
    }
    assert nope == LANES and dims["v_dim"] == LANES and rope <= LANES

    xp, xs = x_prompt, x_sample
    nb = bp + bs
    rows = -(-nb // BF16_ROWS) * BF16_ROWS
    c_all = jnp.pad(jnp.concatenate([c_prompt, c_sample], axis=0), ((0, rows - nb), (0, 0)))
    zeros = jnp.zeros((bp, 2, groups * states), F32)
    outs_p, outs_s = [], []
    for layer in range(depth):
        p = {
            "g_norm1": g_norm1[layer], "g_norm2": g_norm2[layer], "w_in": w_in[layer], "g_q": g_q[layer],
            "w_qu": w_qu[layer], "g_kv": g_kv[layer], "w_uk": w_uk[layer], "w_uv": w_uv[layer], "w_o": w_o[layer],
            "ssm_a_re": ssm_a_re[layer], "ssm_a_im": ssm_a_im[layer], "ssm_log_dt": ssm_log_dt[layer],
            "ssm_b_re": ssm_b_re[layer], "ssm_b_im": ssm_b_im[layer], "ssm_c_re": ssm_c_re[layer],
            "ssm_c_im": ssm_c_im[layer], "ssm_d": ssm_d[layer], "w_glu": w_glu[layer], "w_out": w_out[layer],
            "peer_wq": peer_wq[layer], "peer_k1": peer_k1[layer], "peer_k2": peer_k2[layer],
            "peer_u": peer_u[layer], "peer_v": peer_v[layer],
        }
        w = _prep_weights(p, dims)
        mod = _ada(c_all, w_ada[layer], b_ada[layer])
        mod_p = [m.reshape(bp, 1, d) for m in jnp.split(mod[:bp], 6, axis=-1)]
        mod_s = [m.reshape(bs, 1, d) for m in jnp.split(mod[bp:nb], 6, axis=-1)]
        s0_s = jnp.stack([state_ssm_re[layer].reshape(bs, -1), state_ssm_im[layer].reshape(bs, -1)], axis=1)
        last = layer == depth - 1
        xp, *res_p = _layer(xp, mod_p, None, None, zeros, p, w, dims, g_final, last)
        xs, *res_s = _layer(xs, mod_s, cache_ckv[layer], cache_krope[layer], s0_s, p, w, dims, g_final, last)
        for res, outs, g in ((res_p, outs_p, bp), (res_s, outs_s, bs)):
            ckv, kr, s_fin = res
            outs.append((ckv, kr, s_fin[:, 0].reshape(g, groups, states), s_fin[:, 1].reshape(g, groups, states)))
    stack = lambda outs, k: jnp.stack([o[k] for o in outs])
    return (xp, xs,
            stack(outs_p, 0), stack(outs_p, 1), stack(outs_p, 2), stack(outs_p, 3),
            stack(outs_s, 0), stack(outs_s, 1), stack(outs_s, 2), stack(outs_s, 3))
```
